```python
import math
import jax
import jax.numpy as jnp
from jax import lax
import numpy as np

D_MODEL = 1024
BATCH = 8
SEQ = 16384
DEPTH = 2

D_MIX = D_MODEL
HG_HEADS = 4
HG_DK = 64
HG_DV = 64
HG_QK = HG_HEADS * HG_DK
HG_W = HG_HEADS * HG_DV
GD_HEADS = 4
GD_DK = 128
GD_DV = 128
GD_QK = GD_HEADS * GD_DK
GD_W = GD_HEADS * GD_DV
RG_W = D_MIX - HG_W - GD_W
RG_BLOCKS = 4
RG_BD = RG_W // RG_BLOCKS
RG_C = 8.0
CONV_W = 4
D_FF = 2816
CHUNK = 64
EPS = 1e-6
SPLIT_SIZES = (HG_QK, HG_QK, HG_W, HG_W, GD_QK, GD_QK, GD_W, GD_W, GD_HEADS, GD_HEADS, RG_W, RG_W)
SPLIT_POINTS = tuple(int(p) for p in np.cumsum(SPLIT_SIZES)[:-1])
N_IN = int(sum(SPLIT_SIZES))

kernel_name = 'hybrid_hgrn2_gdn_rglru_macaron'


def rmsnorm(x, w):
    xf = x.astype(jnp.float32)
    y = xf * lax.rsqrt(jnp.mean(xf * xf, axis=-1, keepdims=True) + EPS)
    return (y * w.astype(jnp.float32)).astype(x.dtype)


def head_rmsnorm(o, w):
    return o * lax.rsqrt(jnp.mean(o * o, axis=-1, keepdims=True) + EPS) * w.astype(jnp.float32)


def l2norm(t):
    return t * lax.rsqrt(jnp.sum(t * t, axis=-1, keepdims=True) + EPS)


def causal_depthwise_conv(x, w):
    return lax.conv_general_dilated(
        x, w[:, None, :].astype(x.dtype), window_strides=(1,), padding=[(CONV_W - 1, 0)],
        dimension_numbers=('NWC', 'WIO', 'NWC'), feature_group_count=x.shape[-1])


def to_chunks(t):
    b, l, h, d = t.shape
    return t.reshape(b, l // CHUNK, CHUNK, h, d).transpose(1, 0, 3, 2, 4)


def from_chunks(t):
    n, b, h, c, d = t.shape
    return t.transpose(1, 0, 3, 2, 4).reshape(b, n * c, h, d)


def swiglu(h, w_gate, w_up, w_down):
    return (jax.nn.silu(h @ w_gate) * (h @ w_up)) @ w_down


def hgrn2_chunked(q, k, v, logf):
    qc, kc, vc, gc = to_chunks(q), to_chunks(k), to_chunks(v), to_chunks(logf)
    _, b, h, _, dk = qc.shape
    dv = vc.shape[-1]
    causal = jnp.tril(jnp.ones((CHUNK, CHUNK), dtype=bool))[:, :, None]

    def step(state, inp):
        qi, ki, vi, gi = inp
        cum = jnp.cumsum(gi, axis=2)
        diff = cum[:, :, :, None, :] - cum[:, :, None, :, :]
        decay = jnp.exp(jnp.where(causal, diff, -jnp.inf))
        scores = jnp.einsum('bhtd,bhsd,bhtsd->bhts', qi, ki, decay)
        o = (jnp.einsum('bhts,bhsv->bhtv', scores, vi)
             + jnp.einsum('bhtd,bhdv->bhtv', qi * jnp.exp(cum), state))
        last = cum[:, :, -1:, :]
        state = (state * jnp.exp(last[:, :, 0, :, None])
                 + jnp.einsum('bhsd,bhsv->bhdv', ki * jnp.exp(last - cum), vi))
        return state, o

    s0 = jnp.zeros((b, h, dk, dv), jnp.float32)
    _, o = lax.scan(step, s0, (qc, kc, vc, gc))
    return from_chunks(o)


def gated_delta_chunked(q, k, v, g, beta):
    qc, kc, vc = to_chunks(q), to_chunks(k), to_chunks(v)
    gc = to_chunks(g[..., None])[..., 0]
    bc = to_chunks(beta[..., None])[..., 0]
    _, b, h, _, dk = qc.shape
    dv = vc.shape[-1]
    incl = jnp.tril(jnp.ones((CHUNK, CHUNK), dtype=bool))
    strict = jnp.tril(jnp.ones((CHUNK, CHUNK), dtype=bool), k=-1)
    eye = jnp.eye(CHUNK, dtype=jnp.float32)

    def step(state, inp):
        qi, ki, vi, gi, bi = inp
        cum = jnp.cumsum(gi, axis=-1)
        decay = jnp.exp(jnp.where(incl, cum[..., :, None] - cum[..., None, :], -jnp.inf))
        kb = ki * bi[..., None]
        lower = jnp.where(strict, jnp.einsum('bhtd,bhsd->bhts', kb, ki) * decay, 0.0)
        rhs = jnp.concatenate([vi * bi[..., None], kb * jnp.exp(cum)[..., None]], axis=-1)
        sol = lax.linalg.triangular_solve(eye + lower, rhs, left_side=True, lower=True,
                                          unit_diagonal=True)
        u, w = sol[..., :dv], sol[..., dv:]
        v_new = u - jnp.einsum('bhtd,bhdv->bhtv', w, state)
        scores = jnp.einsum('bhtd,bhsd->bhts', qi, ki) * decay
        o = (jnp.einsum('bhtd,bhdv->bhtv', qi * jnp.exp(cum)[..., None], state)
             + jnp.einsum('bhts,bhsv->bhtv', scores, v_new))
        last = cum[..., -1:]
        state = (state * jnp.exp(last)[..., None]
                 + jnp.einsum('bhsd,bhsv->bhdv', ki * jnp.exp(last - cum)[..., None], v_new))
        return state, o

    s0 = jnp.zeros((b, h, dk, dv), jnp.float32)
    _, o = lax.scan(step, s0, (qc, kc, vc, gc, bc))
    return from_chunks(o)


def rg_lru(xc, wr, br, wi, bi, lam):
    b, l, _ = xc.shape
    xb = xc.reshape(b, l, RG_BLOCKS, RG_BD)
    r = jax.nn.sigmoid(jnp.einsum('blni,nio->blno', xb, wr.astype(jnp.float32)).reshape(b, l, RG_W)
                       + br.astype(jnp.float32))
    ig = jax.nn.sigmoid(jnp.einsum('blni,nio->blno', xb, wi.astype(jnp.float32)).reshape(b, l, RG_W)
                        + bi.astype(jnp.float32))
    log_a = -RG_C * r * jax.nn.softplus(-lam.astype(jnp.float32))
    a = jnp.exp(log_a)
    bx = jnp.sqrt(-jnp.expm1(2.0 * log_a)) * (ig * xc)

    def combine(c1, c2):
        a1, b1 = c1
        a2, b2 = c2
        return a1 * a2, a2 * b1 + b2

    _, hs = lax.associative_scan(combine, (a, bx), axis=1)
    return hs


def hybrid_mixer(h, lb, w_in, hg_norm_w, gd_conv_w, gd_a_log, gd_dt_bias, gd_norm_w,
                 rg_conv_w, rg_conv_b, rg_wr, rg_br, rg_wi, rg_bi, rg_lambda, w_out):
    b, l, _ = h.shape
    f32 = jnp.float32
    proj = (h @ w_in).astype(f32)
    (hg_q, hg_f, hg_i, hg_g, gd_q, gd_k, gd_v, gd_z, gd_b, gd_a,
     rg_x, rg_gate) = jnp.split(proj, SPLIT_POINTS, axis=-1)

    lbf = lb.reshape(HG_HEADS, HG_DK)
    q = jax.nn.silu(hg_q).reshape(b, l, HG_HEADS, HG_DK) * (HG_DK ** -0.5)
    f = lbf + (1.0 - lbf) * jax.nn.sigmoid(hg_f.reshape(b, l, HG_HEADS, HG_DK))
    o_hg = hgrn2_chunked(q, 1.0 - f, hg_i.reshape(b, l, HG_HEADS, HG_DV), jnp.log(f))
    o_hg = head_rmsnorm(o_hg, hg_norm_w) * jax.nn.silu(hg_g.reshape(b, l, HG_HEADS, HG_DV))

    qkv = jax.nn.silu(causal_depthwise_conv(jnp.concatenate([gd_q, gd_k, gd_v], axis=-1), gd_conv_w))
    cq, ck, cv = jnp.split(qkv, (GD_QK, 2 * GD_QK), axis=-1)
    cq = l2norm(cq.reshape(b, l, GD_HEADS, GD_DK)) * (GD_DK ** -0.5)
    ck = l2norm(ck.reshape(b, l, GD_HEADS, GD_DK))
    beta = jax.nn.sigmoid(gd_b)
    g = -jnp.exp(gd_a_log.astype(f32)) * jax.nn.softplus(gd_a + gd_dt_bias.astype(f32))
    o_gd = gated_delta_chunked(cq, ck, cv.reshape(b, l, GD_HEADS, GD_DV), g, beta)
    o_gd = head_rmsnorm(o_gd, gd_norm_w) * jax.nn.silu(gd_z.reshape(b, l, GD_HEADS, GD_DV))

    xc = causal_depthwise_conv(rg_x, rg_conv_w) + rg_conv_b.astype(f32)
    o_rg = rg_lru(xc, rg_wr, rg_br, rg_wi, rg_bi, rg_lambda) * jax.nn.gelu(rg_gate, approximate=True)

    y = jnp.concatenate([o_hg.reshape(b, l, HG_W), o_gd.reshape(b, l, GD_W), o_rg], axis=-1)
    return y.astype(h.dtype) @ w_out


def _fwd_setup_inputs(seed: int = 0) -> dict:
    key = jax.random.key(seed)
    ks = jax.random.split(key, 32)
    f32 = jnp.float32

    def nrm(k, shape, scale):
        return jax.random.normal(k, shape, f32) * scale

    def gain(k, shape):
        return 1.0 + 0.02 * jax.random.normal(k, shape, f32)

    dt = jnp.exp(jax.random.uniform(ks[12], (DEPTH, GD_HEADS), f32, math.log(1e-3), math.log(1e-1)))
    a0 = jax.random.uniform(ks[19], (DEPTH, RG_W), f32, 0.9, 0.999) ** (1.0 / RG_C)
    return {
        'x': nrm(ks[0], (BATCH, SEQ, D_MODEL), 1.0),
        'norm_ffn1': gain(ks[1], (DEPTH, D_MODEL)),
        'ffn1_gate': nrm(ks[2], (DEPTH, D_MODEL, D_FF), D_MODEL ** -0.5),
        'ffn1_up': nrm(ks[3], (DEPTH, D_MODEL, D_FF), D_MODEL ** -0.5),
        'ffn1_down': nrm(ks[4], (DEPTH, D_FF, D_MODEL), D_FF ** -0.5),
        'norm_mix': gain(ks[5], (DEPTH, D_MODEL)),
        'w_in': nrm(ks[6], (DEPTH, D_MODEL, N_IN), D_MODEL ** -0.5),
        'hg_lb': nrm(ks[7], (DEPTH, HG_QK), 1.0),
        'hg_norm_w': gain(ks[8], (DEPTH, HG_DV)),
        'gd_conv_w': nrm(ks[9], (DEPTH, CONV_W, 2 * GD_QK + GD_W), CONV_W ** -0.5),
        'gd_a_log': jnp.log(jax.random.uniform(ks[10], (DEPTH, GD_HEADS), f32, 1.0, 16.0)),
        'gd_dt_bias': dt + jnp.log(-jnp.expm1(-dt)),
        'gd_norm_w': gain(ks[11], (DEPTH, GD_DV)),
        'rg_conv_w': nrm(ks[13], (DEPTH, CONV_W, RG_W), CONV_W ** -0.5),
        'rg_conv_b': nrm(ks[14], (DEPTH, RG_W), 0.02),
        'rg_wr': nrm(ks[15], (DEPTH, RG_BLOCKS, RG_BD, RG_BD), RG_BD ** -0.5),
        'rg_br': nrm(ks[16], (DEPTH, RG_W), 0.1),
        'rg_wi': nrm(ks[17], (DEPTH, RG_BLOCKS, RG_BD, RG_BD), RG_BD ** -0.5),
        'rg_bi': nrm(ks[18], (DEPTH, RG_W), 0.1),
        'rg_lambda': jnp.log(a0) - jnp.log1p(-a0),
        'w_out': nrm(ks[20], (DEPTH, D_MIX, D_MODEL), D_MIX ** -0.5),
        'norm_ffn2': gain(ks[21], (DEPTH, D_MODEL)),
        'ffn2_gate': nrm(ks[22], (DEPTH, D_MODEL, D_FF), D_MODEL ** -0.5),
        'ffn2_up': nrm(ks[23], (DEPTH, D_MODEL, D_FF), D_MODEL ** -0.5),
        'ffn2_down': nrm(ks[24], (DEPTH, D_FF, D_MODEL), D_FF ** -0.5),
        'norm_final': gain(ks[25], (D_MODEL,)),
    }


def _fwd_reference(x, norm_ffn1, ffn1_gate, ffn1_up, ffn1_down, norm_mix, w_in, hg_lb, hg_norm_w,
              gd_conv_w, gd_a_log, gd_dt_bias, gd_norm_w, rg_conv_w, rg_conv_b, rg_wr, rg_br,
              rg_wi, rg_bi, rg_lambda, w_out, norm_ffn2, ffn2_gate, ffn2_up, ffn2_down, norm_final):
    lb_all = jnp.cumsum(jax.nn.softmax(hg_lb.astype(jnp.float32), axis=0), axis=0)
    lb_all = lb_all - lb_all[0]
    for layer in range(DEPTH):
        x = x + 0.5 * swiglu(rmsnorm(x, norm_ffn1[layer]), ffn1_gate[layer], ffn1_up[layer],
                             ffn1_down[layer])
        x = x + hybrid_mixer(rmsnorm(x, norm_mix[layer]), lb_all[layer], w_in[layer], hg_norm_w[layer],
                             gd_conv_w[layer], gd_a_log[layer], gd_dt_bias[layer], gd_norm_w[layer],
                             rg_conv_w[layer], rg_conv_b[layer], rg_wr[layer], rg_br[layer],
                             rg_wi[layer], rg_bi[layer], rg_lambda[layer], w_out[layer])
        x = x + 0.5 * swiglu(rmsnorm(x, norm_ffn2[layer]), ffn2_gate[layer], ffn2_up[layer],
                             ffn2_down[layer])
    return rmsnorm(x, norm_final)


import jax as _jax
import jax.numpy as _jnp

TWIN_FORMAT = 'train_step'
FWD_PARAMS = ['x', 'norm_ffn1', 'ffn1_gate', 'ffn1_up', 'ffn1_down', 'norm_mix', 'w_in', 'hg_lb', 'hg_norm_w', 'gd_conv_w', 'gd_a_log', 'gd_dt_bias', 'gd_norm_w', 'rg_conv_w', 'rg_conv_b', 'rg_wr', 'rg_br', 'rg_wi', 'rg_bi', 'rg_lambda', 'w_out', 'norm_ffn2', 'ffn2_gate', 'ffn2_up', 'ffn2_down', 'norm_final']
TWIN_WEIGHTS = ['norm_ffn1', 'ffn1_gate', 'ffn1_up', 'ffn1_down', 'norm_mix', 'w_in', 'hg_lb', 'hg_norm_w', 'gd_conv_w', 'gd_a_log', 'gd_dt_bias', 'gd_norm_w', 'rg_conv_w', 'rg_conv_b', 'rg_wr', 'rg_br', 'rg_wi', 'rg_bi', 'rg_lambda', 'w_out', 'norm_ffn2', 'ffn2_gate', 'ffn2_up', 'ffn2_down', 'norm_final']
TWIN_DIFF_INPUT = 'x'
TWIN_INPUTS = ['x', 'norm_ffn1', 'ffn1_gate', 'ffn1_up', 'ffn1_down', 'norm_mix', 'w_in', 'hg_lb', 'hg_norm_w', 'gd_conv_w', 'gd_a_log', 'gd_dt_bias', 'gd_norm_w', 'rg_conv_w', 'rg_conv_b', 'rg_wr', 'rg_br', 'rg_wi', 'rg_bi', 'rg_lambda', 'w_out', 'norm_ffn2', 'ffn2_gate', 'ffn2_up', 'ffn2_down', 'norm_final', 'loss_target', 'm_norm_ffn1', 'm_ffn1_gate', 'm_ffn1_up', 'm_ffn1_down', 'm_norm_mix', 'm_w_in', 'm_hg_lb', 'm_hg_norm_w', 'm_gd_conv_w', 'm_gd_a_log', 'm_gd_dt_bias', 'm_gd_norm_w', 'm_rg_conv_w', 'm_rg_conv_b', 'm_rg_wr', 'm_rg_br', 'm_rg_wi', 'm_rg_bi', 'm_rg_lambda', 'm_w_out', 'm_norm_ffn2', 'm_ffn2_gate', 'm_ffn2_up', 'm_ffn2_down', 'm_norm_final', 'v_norm_ffn1', 'v_ffn1_gate', 'v_ffn1_up', 'v_ffn1_down', 'v_norm_mix', 'v_w_in', 'v_hg_lb', 'v_hg_norm_w', 'v_gd_conv_w', 'v_gd_a_log', 'v_gd_dt_bias', 'v_gd_norm_w', 'v_rg_conv_w', 'v_rg_conv_b', 'v_rg_wr', 'v_rg_br', 'v_rg_wi', 'v_rg_bi', 'v_rg_lambda', 'v_w_out', 'v_norm_ffn2', 'v_ffn2_gate', 'v_ffn2_up', 'v_ffn2_down', 'v_norm_final']
TWIN_OUTPUTS = ['loss', 'grad_x', 'grad_norm_ffn1', 'grad_ffn1_gate', 'grad_ffn1_up', 'grad_ffn1_down', 'grad_norm_mix', 'grad_w_in', 'grad_hg_lb', 'grad_hg_norm_w', 'grad_gd_conv_w', 'grad_gd_a_log', 'grad_gd_dt_bias', 'grad_gd_norm_w', 'grad_rg_conv_w', 'grad_rg_conv_b', 'grad_rg_wr', 'grad_rg_br', 'grad_rg_wi', 'grad_rg_bi', 'grad_rg_lambda', 'grad_w_out', 'grad_norm_ffn2', 'grad_ffn2_gate', 'grad_ffn2_up', 'grad_ffn2_down', 'grad_norm_final', 'delta_norm_ffn1', 'delta_ffn1_gate', 'delta_ffn1_up', 'delta_ffn1_down', 'delta_norm_mix', 'delta_w_in', 'delta_hg_lb', 'delta_hg_norm_w', 'delta_gd_conv_w', 'delta_gd_a_log', 'delta_gd_dt_bias', 'delta_gd_norm_w', 'delta_rg_conv_w', 'delta_rg_conv_b', 'delta_rg_wr', 'delta_rg_br', 'delta_rg_wi', 'delta_rg_bi', 'delta_rg_lambda', 'delta_w_out', 'delta_norm_ffn2', 'delta_ffn2_gate', 'delta_ffn2_up', 'delta_ffn2_down', 'delta_norm_final', 'new_m_norm_ffn1', 'new_m_ffn1_gate', 'new_m_ffn1_up', 'new_m_ffn1_down', 'new_m_norm_mix', 'new_m_w_in', 'new_m_hg_lb', 'new_m_hg_norm_w', 'new_m_gd_conv_w', 'new_m_gd_a_log', 'new_m_gd_dt_bias', 'new_m_gd_norm_w', 'new_m_rg_conv_w', 'new_m_rg_conv_b', 'new_m_rg_wr', 'new_m_rg_br', 'new_m_rg_wi', 'new_m_rg_bi', 'new_m_rg_lambda', 'new_m_w_out', 'new_m_norm_ffn2', 'new_m_ffn2_gate', 'new_m_ffn2_up', 'new_m_ffn2_down', 'new_m_norm_final', 'new_v_norm_ffn1', 'new_v_ffn1_gate', 'new_v_ffn1_up', 'new_v_ffn1_down', 'new_v_norm_mix', 'new_v_w_in', 'new_v_hg_lb', 'new_v_hg_norm_w', 'new_v_gd_conv_w', 'new_v_gd_a_log', 'new_v_gd_dt_bias', 'new_v_gd_norm_w', 'new_v_rg_conv_w', 'new_v_rg_conv_b', 'new_v_rg_wr', 'new_v_rg_br', 'new_v_rg_wi', 'new_v_rg_bi', 'new_v_rg_lambda', 'new_v_w_out', 'new_v_norm_ffn2', 'new_v_ffn2_gate', 'new_v_ffn2_up', 'new_v_ffn2_down', 'new_v_norm_final']
TWIN_LEAF_KINDS = {'loss': 'loss', 'grad_x': 'grad_x', 'grad_norm_ffn1': 'grad_w', 'grad_ffn1_gate': 'grad_w', 'grad_ffn1_up': 'grad_w', 'grad_ffn1_down': 'grad_w', 'grad_norm_mix': 'grad_w', 'grad_w_in': 'grad_w', 'grad_hg_lb': 'grad_w', 'grad_hg_norm_w': 'grad_w', 'grad_gd_conv_w': 'grad_w', 'grad_gd_a_log': 'grad_w', 'grad_gd_dt_bias': 'grad_w', 'grad_gd_norm_w': 'grad_w', 'grad_rg_conv_w': 'grad_w', 'grad_rg_conv_b': 'grad_w', 'grad_rg_wr': 'grad_w', 'grad_rg_br': 'grad_w', 'grad_rg_wi': 'grad_w', 'grad_rg_bi': 'grad_w', 'grad_rg_lambda': 'grad_w', 'grad_w_out': 'grad_w', 'grad_norm_ffn2': 'grad_w', 'grad_ffn2_gate': 'grad_w', 'grad_ffn2_up': 'grad_w', 'grad_ffn2_down': 'grad_w', 'grad_norm_final': 'grad_w', 'delta_norm_ffn1': 'delta_w', 'delta_ffn1_gate': 'delta_w', 'delta_ffn1_up': 'delta_w', 'delta_ffn1_down': 'delta_w', 'delta_norm_mix': 'delta_w', 'delta_w_in': 'delta_w', 'delta_hg_lb': 'delta_w', 'delta_hg_norm_w': 'delta_w', 'delta_gd_conv_w': 'delta_w', 'delta_gd_a_log': 'delta_w', 'delta_gd_dt_bias': 'delta_w', 'delta_gd_norm_w': 'delta_w', 'delta_rg_conv_w': 'delta_w', 'delta_rg_conv_b': 'delta_w', 'delta_rg_wr': 'delta_w', 'delta_rg_br': 'delta_w', 'delta_rg_wi': 'delta_w', 'delta_rg_bi': 'delta_w', 'delta_rg_lambda': 'delta_w', 'delta_w_out': 'delta_w', 'delta_norm_ffn2': 'delta_w', 'delta_ffn2_gate': 'delta_w', 'delta_ffn2_up': 'delta_w', 'delta_ffn2_down': 'delta_w', 'delta_norm_final': 'delta_w', 'new_m_norm_ffn1': 'new_m', 'new_m_ffn1_gate': 'new_m', 'new_m_ffn1_up': 'new_m', 'new_m_ffn1_down': 'new_m', 'new_m_norm_mix': 'new_m', 'new_m_w_in': 'new_m', 'new_m_hg_lb': 'new_m', 'new_m_hg_norm_w': 'new_m', 'new_m_gd_conv_w': 'new_m', 'new_m_gd_a_log': 'new_m', 'new_m_gd_dt_bias': 'new_m', 'new_m_gd_norm_w': 'new_m', 'new_m_rg_conv_w': 'new_m', 'new_m_rg_conv_b': 'new_m', 'new_m_rg_wr': 'new_m', 'new_m_rg_br': 'new_m', 'new_m_rg_wi': 'new_m', 'new_m_rg_bi': 'new_m', 'new_m_rg_lambda': 'new_m', 'new_m_w_out': 'new_m', 'new_m_norm_ffn2': 'new_m', 'new_m_ffn2_gate': 'new_m', 'new_m_ffn2_up': 'new_m', 'new_m_ffn2_down': 'new_m', 'new_m_norm_final': 'new_m', 'new_v_norm_ffn1': 'new_v', 'new_v_ffn1_gate': 'new_v', 'new_v_ffn1_up': 'new_v', 'new_v_ffn1_down': 'new_v', 'new_v_norm_mix': 'new_v', 'new_v_w_in': 'new_v', 'new_v_hg_lb': 'new_v', 'new_v_hg_norm_w': 'new_v', 'new_v_gd_conv_w': 'new_v', 'new_v_gd_a_log': 'new_v', 'new_v_gd_dt_bias': 'new_v', 'new_v_gd_norm_w': 'new_v', 'new_v_rg_conv_w': 'new_v', 'new_v_rg_conv_b': 'new_v', 'new_v_rg_wr': 'new_v', 'new_v_rg_br': 'new_v', 'new_v_rg_wi': 'new_v', 'new_v_rg_bi': 'new_v', 'new_v_rg_lambda': 'new_v', 'new_v_w_out': 'new_v', 'new_v_norm_ffn2': 'new_v', 'new_v_ffn2_gate': 'new_v', 'new_v_ffn2_up': 'new_v', 'new_v_ffn2_down': 'new_v', 'new_v_norm_final': 'new_v'}


def _forward(args):
    return _fwd_reference(*[args[k] for k in FWD_PARAMS])


def _output_shape():
    def fwd():
        inp = _fwd_setup_inputs(0)
        return _fwd_reference(*[inp[k] for k in FWD_PARAMS])
    out = _jax.eval_shape(fwd)
    return out.shape, out.dtype

N_MICROBATCH = 1
ADAM_LR = 0.001
ADAM_B1 = 0.9
ADAM_B2 = 0.999
ADAM_EPS = 1e-08
ADAM_WD = 0.01
ADAM_STEP = 10
PER_EXAMPLE_BATCH_AXIS = {'x': 0, 'loss_target': 0}
SHARED_INPUTS = []
_WEIGHT_DTYPES = {'norm_ffn1': _jnp.float32, 'ffn1_gate': _jnp.float32, 'ffn1_up': _jnp.float32, 'ffn1_down': _jnp.float32, 'norm_mix': _jnp.float32, 'w_in': _jnp.float32, 'hg_lb': _jnp.float32, 'hg_norm_w': _jnp.float32, 'gd_conv_w': _jnp.float32, 'gd_a_log': _jnp.float32, 'gd_dt_bias': _jnp.float32, 'gd_norm_w': _jnp.float32, 'rg_conv_w': _jnp.float32, 'rg_conv_b': _jnp.float32, 'rg_wr': _jnp.float32, 'rg_br': _jnp.float32, 'rg_wi': _jnp.float32, 'rg_bi': _jnp.float32, 'rg_lambda': _jnp.float32, 'w_out': _jnp.float32, 'norm_ffn2': _jnp.float32, 'ffn2_gate': _jnp.float32, 'ffn2_up': _jnp.float32, 'ffn2_down': _jnp.float32, 'norm_final': _jnp.float32}
MOMENT_SCALE = {'norm_ffn1': 1.812598e-01, 'ffn1_gate': 7.528675e-02, 'ffn1_up': 7.289863e-02, 'ffn1_down': 1.209192e-01, 'norm_mix': 2.800808e-01, 'w_in': 1.522523e-01, 'hg_lb': 2.545254e-02, 'hg_norm_w': 5.074859e-01, 'gd_conv_w': 1.423349e-01, 'gd_a_log': 7.936473e-01, 'gd_dt_bias': 8.346125e-01, 'gd_norm_w': 3.854116e-01, 'rg_conv_w': 2.149254e-01, 'rg_conv_b': 2.138206e+00, 'rg_wr': 6.621047e-02, 'rg_br': 5.583009e-02, 'rg_wi': 1.199590e-01, 'rg_bi': 1.071352e-01, 'rg_lambda': 1.023386e-01, 'w_out': 1.919854e-01, 'norm_ffn2': 1.283631e-01, 'ffn2_gate': 5.602305e-02, 'ffn2_up': 5.428294e-02, 'ffn2_down': 9.006140e-02, 'norm_final': 1.279908e+02}


def _to_microbatches(a, axis):
    t = _jnp.moveaxis(a, axis, 0)
    t = t.reshape((N_MICROBATCH, t.shape[0] // N_MICROBATCH) + t.shape[1:])
    return _jnp.moveaxis(t, 1, axis + 1)


def setup_inputs(seed: int = 0) -> dict:
    inp = _fwd_setup_inputs(seed)
    key = _jax.random.fold_in(_jax.random.key(seed), 7919)
    shape, _ = _output_shape()
    out = dict(inp)
    out["loss_target"] = _jax.random.normal(_jax.random.fold_in(key, 0), shape, _jnp.float32)
    for i, name in enumerate(TWIN_WEIGHTS):
        w = inp[name].astype(_jnp.float32)
        if MOMENT_SCALE is None:
            s = _jnp.sqrt(_jnp.mean(_jnp.square(w)) + 1e-30)
        else:
            s = MOMENT_SCALE[name]
        km, kv = _jax.random.split(_jax.random.fold_in(key, i + 1))
        out[name] = w
        out["m_" + name] = s * _jax.random.normal(km, w.shape, _jnp.float32)
        out["v_" + name] = (s * s) * _jax.random.uniform(kv, w.shape, _jnp.float32, 0.5, 1.5)
    if N_MICROBATCH > 1:
        for name, axis in PER_EXAMPLE_BATCH_AXIS.items():
            out[name] = _to_microbatches(out[name], axis)
    return {'x': out['x'], 'norm_ffn1': out['norm_ffn1'], 'ffn1_gate': out['ffn1_gate'], 'ffn1_up': out['ffn1_up'], 'ffn1_down': out['ffn1_down'], 'norm_mix': out['norm_mix'], 'w_in': out['w_in'], 'hg_lb': out['hg_lb'], 'hg_norm_w': out['hg_norm_w'], 'gd_conv_w': out['gd_conv_w'], 'gd_a_log': out['gd_a_log'], 'gd_dt_bias': out['gd_dt_bias'], 'gd_norm_w': out['gd_norm_w'], 'rg_conv_w': out['rg_conv_w'], 'rg_conv_b': out['rg_conv_b'], 'rg_wr': out['rg_wr'], 'rg_br': out['rg_br'], 'rg_wi': out['rg_wi'], 'rg_bi': out['rg_bi'], 'rg_lambda': out['rg_lambda'], 'w_out': out['w_out'], 'norm_ffn2': out['norm_ffn2'], 'ffn2_gate': out['ffn2_gate'], 'ffn2_up': out['ffn2_up'], 'ffn2_down': out['ffn2_down'], 'norm_final': out['norm_final'], 'loss_target': out['loss_target'], 'm_norm_ffn1': out['m_norm_ffn1'], 'm_ffn1_gate': out['m_ffn1_gate'], 'm_ffn1_up': out['m_ffn1_up'], 'm_ffn1_down': out['m_ffn1_down'], 'm_norm_mix': out['m_norm_mix'], 'm_w_in': out['m_w_in'], 'm_hg_lb': out['m_hg_lb'], 'm_hg_norm_w': out['m_hg_norm_w'], 'm_gd_conv_w': out['m_gd_conv_w'], 'm_gd_a_log': out['m_gd_a_log'], 'm_gd_dt_bias': out['m_gd_dt_bias'], 'm_gd_norm_w': out['m_gd_norm_w'], 'm_rg_conv_w': out['m_rg_conv_w'], 'm_rg_conv_b': out['m_rg_conv_b'], 'm_rg_wr': out['m_rg_wr'], 'm_rg_br': out['m_rg_br'], 'm_rg_wi': out['m_rg_wi'], 'm_rg_bi': out['m_rg_bi'], 'm_rg_lambda': out['m_rg_lambda'], 'm_w_out': out['m_w_out'], 'm_norm_ffn2': out['m_norm_ffn2'], 'm_ffn2_gate': out['m_ffn2_gate'], 'm_ffn2_up': out['m_ffn2_up'], 'm_ffn2_down': out['m_ffn2_down'], 'm_norm_final': out['m_norm_final'], 'v_norm_ffn1': out['v_norm_ffn1'], 'v_ffn1_gate': out['v_ffn1_gate'], 'v_ffn1_up': out['v_ffn1_up'], 'v_ffn1_down': out['v_ffn1_down'], 'v_norm_mix': out['v_norm_mix'], 'v_w_in': out['v_w_in'], 'v_hg_lb': out['v_hg_lb'], 'v_hg_norm_w': out['v_hg_norm_w'], 'v_gd_conv_w': out['v_gd_conv_w'], 'v_gd_a_log': out['v_gd_a_log'], 'v_gd_dt_bias': out['v_gd_dt_bias'], 'v_gd_norm_w': out['v_gd_norm_w'], 'v_rg_conv_w': out['v_rg_conv_w'], 'v_rg_conv_b': out['v_rg_conv_b'], 'v_rg_wr': out['v_rg_wr'], 'v_rg_br': out['v_rg_br'], 'v_rg_wi': out['v_rg_wi'], 'v_rg_bi': out['v_rg_bi'], 'v_rg_lambda': out['v_rg_lambda'], 'v_w_out': out['v_w_out'], 'v_norm_ffn2': out['v_norm_ffn2'], 'v_ffn2_gate': out['v_ffn2_gate'], 'v_ffn2_up': out['v_ffn2_up'], 'v_ffn2_down': out['v_ffn2_down'], 'v_norm_final': out['v_norm_final']}


def _loss(weights, diff, rest, loss_target):
    with _jax.named_scope("forward"):
        args = {**rest, TWIN_DIFF_INPUT: diff, **{k: w.astype(_WEIGHT_DTYPES[k]) for k, w in weights.items()}}
        y = _forward(args)
    with _jax.named_scope("loss_head"):
        err = _jnp.square(y.astype(_jnp.float32) - loss_target)
        return 0.5 * _jnp.sum(_jnp.mean(err, axis=-1)) if err.ndim else 0.5 * err


def _adamw(w, g, m, v):
    m = ADAM_B1 * m + (1.0 - ADAM_B1) * g
    v = ADAM_B2 * v + (1.0 - ADAM_B2) * _jnp.square(g)
    m_hat = m / (1.0 - ADAM_B1 ** ADAM_STEP)
    v_hat = v / (1.0 - ADAM_B2 ** ADAM_STEP)
    delta = -ADAM_LR * (m_hat / (_jnp.sqrt(v_hat) + ADAM_EPS) + ADAM_WD * w)
    return delta, m, v


def reference(x, norm_ffn1, ffn1_gate, ffn1_up, ffn1_down, norm_mix, w_in, hg_lb, hg_norm_w, gd_conv_w, gd_a_log, gd_dt_bias, gd_norm_w, rg_conv_w, rg_conv_b, rg_wr, rg_br, rg_wi, rg_bi, rg_lambda, w_out, norm_ffn2, ffn2_gate, ffn2_up, ffn2_down, norm_final, loss_target, m_norm_ffn1, m_ffn1_gate, m_ffn1_up, m_ffn1_down, m_norm_mix, m_w_in, m_hg_lb, m_hg_norm_w, m_gd_conv_w, m_gd_a_log, m_gd_dt_bias, m_gd_norm_w, m_rg_conv_w, m_rg_conv_b, m_rg_wr, m_rg_br, m_rg_wi, m_rg_bi, m_rg_lambda, m_w_out, m_norm_ffn2, m_ffn2_gate, m_ffn2_up, m_ffn2_down, m_norm_final, v_norm_ffn1, v_ffn1_gate, v_ffn1_up, v_ffn1_down, v_norm_mix, v_w_in, v_hg_lb, v_hg_norm_w, v_gd_conv_w, v_gd_a_log, v_gd_dt_bias, v_gd_norm_w, v_rg_conv_w, v_rg_conv_b, v_rg_wr, v_rg_br, v_rg_wi, v_rg_bi, v_rg_lambda, v_w_out, v_norm_ffn2, v_ffn2_gate, v_ffn2_up, v_ffn2_down, v_norm_final):
    given = dict(x=x, norm_ffn1=norm_ffn1, ffn1_gate=ffn1_gate, ffn1_up=ffn1_up, ffn1_down=ffn1_down, norm_mix=norm_mix, w_in=w_in, hg_lb=hg_lb, hg_norm_w=hg_norm_w, gd_conv_w=gd_conv_w, gd_a_log=gd_a_log, gd_dt_bias=gd_dt_bias, gd_norm_w=gd_norm_w, rg_conv_w=rg_conv_w, rg_conv_b=rg_conv_b, rg_wr=rg_wr, rg_br=rg_br, rg_wi=rg_wi, rg_bi=rg_bi, rg_lambda=rg_lambda, w_out=w_out, norm_ffn2=norm_ffn2, ffn2_gate=ffn2_gate, ffn2_up=ffn2_up, ffn2_down=ffn2_down, norm_final=norm_final, loss_target=loss_target, m_norm_ffn1=m_norm_ffn1, m_ffn1_gate=m_ffn1_gate, m_ffn1_up=m_ffn1_up, m_ffn1_down=m_ffn1_down, m_norm_mix=m_norm_mix, m_w_in=m_w_in, m_hg_lb=m_hg_lb, m_hg_norm_w=m_hg_norm_w, m_gd_conv_w=m_gd_conv_w, m_gd_a_log=m_gd_a_log, m_gd_dt_bias=m_gd_dt_bias, m_gd_norm_w=m_gd_norm_w, m_rg_conv_w=m_rg_conv_w, m_rg_conv_b=m_rg_conv_b, m_rg_wr=m_rg_wr, m_rg_br=m_rg_br, m_rg_wi=m_rg_wi, m_rg_bi=m_rg_bi, m_rg_lambda=m_rg_lambda, m_w_out=m_w_out, m_norm_ffn2=m_norm_ffn2, m_ffn2_gate=m_ffn2_gate, m_ffn2_up=m_ffn2_up, m_ffn2_down=m_ffn2_down, m_norm_final=m_norm_final, v_norm_ffn1=v_norm_ffn1, v_ffn1_gate=v_ffn1_gate, v_ffn1_up=v_ffn1_up, v_ffn1_down=v_ffn1_down, v_norm_mix=v_norm_mix, v_w_in=v_w_in, v_hg_lb=v_hg_lb, v_hg_norm_w=v_hg_norm_w, v_gd_conv_w=v_gd_conv_w, v_gd_a_log=v_gd_a_log, v_gd_dt_bias=v_gd_dt_bias, v_gd_norm_w=v_gd_norm_w, v_rg_conv_w=v_rg_conv_w, v_rg_conv_b=v_rg_conv_b, v_rg_wr=v_rg_wr, v_rg_br=v_rg_br, v_rg_wi=v_rg_wi, v_rg_bi=v_rg_bi, v_rg_lambda=v_rg_lambda, v_w_out=v_w_out, v_norm_ffn2=v_norm_ffn2, v_ffn2_gate=v_ffn2_gate, v_ffn2_up=v_ffn2_up, v_ffn2_down=v_ffn2_down, v_norm_final=v_norm_final)
    weights = {n: given[n] for n in TWIN_WEIGHTS}
    shared = {n: given[n] for n in SHARED_INPUTS}
    per_example = {n: given[n] for n in ['x']}
    grad_fn = _jax.value_and_grad(_loss, argnums=(0, 1))

    def one_microbatch(ex, loss_target):
        ex = dict(ex)
        diff = ex.pop(TWIN_DIFF_INPUT)
        return grad_fn(weights, diff, {**shared, **ex}, loss_target)

    if N_MICROBATCH == 1:
        loss, (grad_w, grad_x) = one_microbatch(per_example, given["loss_target"])
    else:
        def body(carry, xs):
            loss_sum, grad_sum = carry
            l_k, (gw_k, gx_k) = one_microbatch(xs[0], xs[1])
            with _jax.named_scope("update"):
                return (loss_sum + l_k, _jax.tree.map(_jnp.add, grad_sum, gw_k)), gx_k

        init = (_jnp.zeros((), _jnp.float32), _jax.tree.map(_jnp.zeros_like, weights))
        (loss, grad_w), grad_x = _jax.lax.scan(body, init, (per_example, given["loss_target"]))
    with _jax.named_scope("update"):
        delta_w, new_m, new_v = {}, {}, {}
        for n in TWIN_WEIGHTS:
            delta_w[n], new_m[n], new_v[n] = _adamw(weights[n], grad_w[n], given["m_" + n], given["v_" + n])
    return (loss, grad_x, *[grad_w[n] for n in TWIN_WEIGHTS], *[delta_w[n] for n in TWIN_WEIGHTS],
            *[new_m[n] for n in TWIN_WEIGHTS], *[new_v[n] for n in TWIN_WEIGHTS])
```

```python
import functools
import math

import jax
import jax.numpy as jnp
from jax import lax
from jax.experimental import pallas as pl
from jax.experimental.pallas import tpu as pltpu

F32 = jnp.float32
BF16 = jnp.bfloat16

D_MODEL = 1024
DEPTH = 2
D_FF = 2816
HG_HEADS, HG_DK = 4, 64
HG_W = 256
GD_HEADS, GD_DK = 4, 128
GD_W = 512
RG_W = 256
RG_BLOCKS, RG_BD = 4, 64
RG_C = 8.0
CONV_W = 4
CHUNK = 64
EPS = 1e-6
N_IN = 3592
ADAM_LR, ADAM_B1, ADAM_B2, ADAM_EPS, ADAM_WD, ADAM_STEP = 0.001, 0.9, 0.999, 1e-08, 0.01, 10

N_DEV = 8
FF_SHARD = D_FF // N_DEV
FF_PAD = 384
NIN_SHARD = N_IN // N_DEV
NIN_SHARD_PAD = 512
WOUT_SHARD = D_MODEL // N_DEV
GDC_SHARD = (2 * 512 + 512) // N_DEV
RGC_SHARD = RG_W // N_DEV

O_HQ, O_HF, O_HI, O_HG = 0, 256, 512, 768
O_GQ, O_GK, O_GV, O_GZ = 1024, 1536, 2048, 2560
O_GBA = 3072
O_RX, O_RGATE = 3200, 3456
NP = 3840
GBA_SPLIT = 3080

N_FFN_BLOCKS = DEPTH * 2 * 3
OFF_WIN = N_FFN_BLOCKS * FF_PAD
OFF_WOUT = OFF_WIN + DEPTH * NIN_SHARD_PAD
PACK_ROWS = OFF_WOUT + DEPTH * WOUT_SHARD

VMEM_LIMIT = 56 * 1024 * 1024


def _ffn_block(layer, which, part):
    return layer * 6 + (which - 1) * 3 + {"g": 0, "u": 1, "d": 2}[part]


def _cparams(sem, **kw):
    return pltpu.CompilerParams(dimension_semantics=sem, vmem_limit_bytes=VMEM_LIMIT, **kw)


def _dot_nn(a, b):
    return lax.dot_general(a, b, (((1,), (0,)), ((), ())), preferred_element_type=F32)


def _dot_nt(a, b):
    return lax.dot_general(a, b, (((1,), (1,)), ((), ())), preferred_element_type=F32)


def _dot_tn(a, b):
    return lax.dot_general(a, b, (((0,), (0,)), ((), ())), preferred_element_type=F32)


@jax.custom_vjp
def mm_nn(a, b):
    return _dot_nn(a.astype(BF16), b.astype(BF16))


def _mm_nn_fwd(a, b):
    a, b = a.astype(BF16), b.astype(BF16)
    return _dot_nn(a, b), (a, b)


def _mm_nn_bwd(res, g):
    a, b = res
    g = g.astype(BF16)
    return _dot_nt(g, b), _dot_tn(a, g)


mm_nn.defvjp(_mm_nn_fwd, _mm_nn_bwd)


@jax.custom_vjp
def mm_nt(a, b):
    return _dot_nt(a.astype(BF16), b.astype(BF16))


def _mm_nt_fwd(a, b):
    a, b = a.astype(BF16), b.astype(BF16)
    return _dot_nt(a, b), (a, b)


def _mm_nt_bwd(res, g):
    a, b = res
    g = g.astype(BF16)
    return _dot_nn(g, b), _dot_tn(g, a)


mm_nt.defvjp(_mm_nt_fwd, _mm_nt_bwd)


@jax.custom_vjp
def mm_tn(a, b):
    return _dot_tn(a.astype(BF16), b.astype(BF16))


def _mm_tn_fwd(a, b):
    a, b = a.astype(BF16), b.astype(BF16)
    return _dot_tn(a, b), (a, b)


def _mm_tn_bwd(res, g):
    a, b = res
    g = g.astype(BF16)
    return _dot_nt(b, g), _dot_nn(a, g)


mm_tn.defvjp(_mm_tn_fwd, _mm_tn_bwd)


def _split3(x):
    hi = x.astype(BF16)
    r = x - hi.astype(F32)
    mid = r.astype(BF16)
    lo = (r - mid.astype(F32)).astype(BF16)
    return hi, mid, lo


def _sel_apply(sel, x):
    hi, mid, lo = _split3(x)
    return _dot_nn(sel, hi) + _dot_nn(sel, mid) + _dot_nn(sel, lo)


def _sel_apply_t(sel, x):
    hi, mid, lo = _split3(x)
    return _dot_tn(sel, hi) + _dot_tn(sel, mid) + _dot_tn(sel, lo)


@jax.custom_vjp
def sel_mm(sel, x):
    return _sel_apply(sel, x)


def _sel_mm_fwd(sel, x):
    return _sel_apply(sel, x), sel


def _sel_mm_bwd(sel, g):
    return jnp.zeros_like(sel), _sel_apply_t(sel, g)


sel_mm.defvjp(_sel_mm_fwd, _sel_mm_bwd)


def _iota(shape, dim):
    return lax.broadcasted_iota(jnp.int32, shape, dim)


def _sigmoid(x):
    return jax.nn.sigmoid(x)


def _silu(x):
    return x * _sigmoid(x)


def _softplus(x):
    return jnp.maximum(x, 0.0) + jnp.log(1.0 + jnp.exp(-jnp.abs(x)))


def _gelu_tanh(x):
    return 0.5 * x * (1.0 + jnp.tanh(math.sqrt(2.0 / math.pi) * (x + 0.044715 * x * x * x)))


def _rms_fwd(x, w):
    rstd = lax.rsqrt(jnp.mean(x * x, axis=-1, keepdims=True) + EPS)
    return x * rstd * w


def _rms_bwd(dh, x, w):
    rstd = lax.rsqrt(jnp.mean(x * x, axis=-1, keepdims=True) + EPS)
    xhat = x * rstd
    dxhat = dh * w
    dx = rstd * (dxhat - xhat * jnp.mean(dxhat * xhat, axis=-1, keepdims=True))
    return dx, jnp.sum(dh * xhat, axis=0, keepdims=True)


def _shift_rows(x, halo, k):
    if k == 0:
        return x
    n = x.shape[0]
    ext = jnp.concatenate([jnp.zeros((n - halo.shape[0], x.shape[1]), x.dtype), halo], axis=0)
    t = _iota(x.shape, 0)
    return jnp.where(t >= k, jnp.roll(x, k, axis=0), jnp.roll(ext, k, axis=0))


def _causal_conv(x, halo, w):
    y = x * w[3:4, :]
    for k in range(1, CONV_W):
        y = y + _shift_rows(x, halo, k) * w[3 - k:4 - k, :]
    return y


def _head_sum_matrix(width, head):
    return (_iota((width, width), 0) // head == _iota((width, width), 1) // head)


def _hgrn2_chunk(hq, hf, hi, hg, st, lb, norm_w):
    n = hq.shape[0]
    w = HG_W
    q = _silu(hq) * (HG_DK ** -0.5)
    f = lb + (1.0 - lb) * _sigmoid(hf)
    k = 1.0 - f
    v = hi
    logf = jnp.log(f)
    tri = (_iota((n, n), 0) >= _iota((n, n), 1)).astype(BF16)
    cum = sel_mm(tri, logf)
    blk = _head_sum_matrix(w, HG_DK)
    lane_head = _iota((n, w), 1) // HG_DK

    def stack(x):
        return jnp.concatenate([jnp.where(lane_head == h, x, 0.0) for h in range(HG_HEADS)], axis=0)

    t_idx = _iota((n, HG_HEADS * n), 0)
    s_idx = _iota((n, HG_HEADS * n), 1) % n
    p = jnp.where(t_idx == s_idx, mm_nt(q, stack(k)), 0.0)
    b = n // 2
    while b >= 1:
        ref_row = (_iota((n, n), 0) // (2 * b)) * (2 * b) + b - 1
        sel = (_iota((n, n), 1) == ref_row).astype(BF16)
        ref = sel_mm(sel, cum)
        qe = q * jnp.exp(jnp.minimum(cum - ref, 0.0))
        ke = k * jnp.exp(jnp.minimum(ref - cum, 0.0))
        mask = (t_idx // (2 * b) == s_idx // (2 * b)) & (t_idx % (2 * b) >= b) & (s_idx % (2 * b) < b)
        p = p + jnp.where(mask, mm_nt(qe, stack(ke)), 0.0)
        b //= 2
    o = mm_nn(p, stack(v)) + mm_nt(q * jnp.exp(cum), st)
    last = cum[n - 1:n, :]
    st_new = st * jnp.exp(last) + jnp.where(blk, mm_tn(v, k * jnp.exp(last - cum)), 0.0)
    ms = _segment_mean(o * o, blk, HG_DK)
    nw = jnp.concatenate([norm_w] * HG_HEADS, axis=1)
    y = o * lax.rsqrt(ms + EPS) * nw * _silu(hg)
    return y, st_new


def _segment_mean(x, blk, head):
    return _seg_sum(x, blk.astype(BF16)) * (1.0 / head)


def _seg_apply(x, b):
    hi, mid, lo = _split3(x)
    return _dot_nn(hi, b) + _dot_nn(mid, b) + _dot_nn(lo, b)


@jax.custom_vjp
def _seg_sum(x, b):
    return _seg_apply(x, b)


def _seg_sum_fwd(x, b):
    return _seg_apply(x, b), b


def _seg_sum_bwd(b, g):
    return _seg_apply(g, b), jnp.zeros_like(b)


_seg_sum.defvjp(_seg_sum_fwd, _seg_sum_bwd)


def _gdn_chunk(gq, gk, gv, gz, gba, halo_q, halo_k, halo_v, s_gd, conv_w, alog_vec, dt_vec, norm_w):
    n = gq.shape[0]
    hn = GD_HEADS * n
    cq = _silu(_causal_conv(gq, halo_q, conv_w[:, 0:GD_W]))
    ck = _silu(_causal_conv(gk, halo_k, conv_w[:, GD_W:2 * GD_W]))
    cv = _silu(_causal_conv(gv, halo_v, conv_w[:, 2 * GD_W:3 * GD_W]))

    def stack(x):
        return jnp.concatenate([x[:, GD_DK * h:GD_DK * (h + 1)] for h in range(GD_HEADS)], axis=0)

    def l2n(x):
        return x * lax.rsqrt(jnp.sum(x * x, axis=-1, keepdims=True) + EPS)

    q = l2n(stack(cq)) * (GD_DK ** -0.5)
    k = l2n(stack(ck))
    v = stack(cv)
    beta_full = _sigmoid(gba)
    g_full = -jnp.exp(alog_vec) * _softplus(gba + dt_vec)
    tri = (_iota((n, n), 0) >= _iota((n, n), 1)).astype(BF16)
    cum_full = sel_mm(tri, g_full)
    beta = jnp.concatenate([beta_full[:, h:h + 1] for h in range(GD_HEADS)], axis=0)
    cum = jnp.concatenate([cum_full[:, 4 + h:5 + h] for h in range(GD_HEADS)], axis=0)
    ccol = jnp.broadcast_to(cum, (hn, hn))
    diff = ccol - jnp.transpose(ccol)
    r_i, c_i = _iota((hn, hn), 0), _iota((hn, hn), 1)
    same = (r_i // n) == (c_i // n)
    incl = same & (r_i >= c_i)
    strict = same & (r_i > c_i)
    decay = jnp.exp(jnp.minimum(diff, 0.0))
    kb = k * beta
    lower = jnp.where(strict, mm_nt(kb, k) * decay, 0.0)
    eye = (r_i == c_i).astype(F32)
    npow = -lower
    tinv = eye + npow
    j = 2
    while j < n:
        npow = mm3_nn(npow, npow)
        tinv = mm3_nn(tinv, eye + npow)
        j *= 2
    ecum = jnp.exp(cum)
    rhs = jnp.concatenate([v * beta, kb * ecum], axis=1)
    sol = mm3_nn(tinv, rhs)
    u, w = sol[:, 0:GD_DK], sol[:, GD_DK:2 * GD_DK]

    def per_head(x, fn):
        return jnp.concatenate([fn(x[n * h:n * (h + 1), :], s_gd[GD_DK * h:GD_DK * (h + 1), :]) for h in range(GD_HEADS)], axis=0)

    v_new = u - per_head(w, mm_nn)
    scores = jnp.where(incl, mm_nt(q, k) * decay, 0.0)
    o = per_head(q * ecum, mm_nn) + mm_nn(scores, v_new)
    last = jnp.concatenate([jnp.broadcast_to(cum[n * (h + 1) - 1:n * (h + 1), :], (n, 1)) for h in range(GD_HEADS)], axis=0)
    kd = k * jnp.exp(last - cum)
    s_parts = []
    for h in range(GD_HEADS):
        sl = slice(n * h, n * (h + 1))
        s_parts.append(s_gd[GD_DK * h:GD_DK * (h + 1), :] * jnp.exp(last[n * h:n * h + 1, :]) + mm_tn(kd[sl, :], v_new[sl, :]))
    s_new = jnp.concatenate(s_parts, axis=0)
    on = o * lax.rsqrt(jnp.mean(o * o, axis=-1, keepdims=True) + EPS) * norm_w
    y = jnp.concatenate([on[n * h:n * (h + 1), :] for h in range(GD_HEADS)], axis=1) * _silu(gz)
    return y, s_new


def mm3_nn(a, b):
    ah = a.astype(BF16).astype(F32)
    al = a - ah
    bh = b.astype(BF16).astype(F32)
    bl = b - bh
    return mm_nn(ah, bh) + mm_nn(ah, bl) + mm_nn(al, bh)


def _rglru_chunk(rx, rgate, halo_x, h_in, conv_w, conv_b, wr, br, wi, bi, lam):
    n = rx.shape[0]
    xc = _causal_conv(rx, halo_x, conv_w) + conv_b
    r = _sigmoid(mm_nn(xc, wr) + br)
    ig = _sigmoid(mm_nn(xc, wi) + bi)
    log_a = -RG_C * r * _softplus(-lam)
    a = jnp.exp(log_a)
    two = 2.0 * log_a
    one_minus = -jnp.tanh(0.5 * two) * (jnp.exp(two) + 1.0)
    bx = jnp.sqrt(one_minus) * (ig * xc)
    t = _iota((n, RG_W), 0)
    acc_a, acc_b = a, bx
    k = 1
    while k < n:
        sa = jnp.where(t >= k, jnp.roll(acc_a, k, axis=0), 1.0)
        sb = jnp.where(t >= k, jnp.roll(acc_b, k, axis=0), 0.0)
        acc_b = acc_a * sb + acc_b
        acc_a = acc_a * sa
        k *= 2
    hs = acc_b + acc_a * h_in
    return hs * _gelu_tanh(rgate), hs[n - 1:n, :]


def _lb_of_layer(hg_lb, layer):
    if layer == 0:
        return jnp.zeros((1, HG_W), F32)
    e = jnp.exp(hg_lb - jnp.max(hg_lb, axis=0, keepdims=True))
    sm = e / jnp.sum(e, axis=0, keepdims=True)
    lb = sm[1:2, :]
    for l in range(2, layer + 1):
        lb = lb + sm[l:l + 1, :]
    return lb


MIXER_PARAMS = ("hg_lb", "hg_norm_w", "gd_conv_w", "gd_alog_vec", "gd_dt_vec", "gd_norm_w",
                "rg_conv_w", "rg_conv_b", "rg_wr_bd", "rg_br", "rg_wi_bd", "rg_bi", "rg_lambda")


def mixer_chunk(layer, proj, halo, st_hg, s_gd, h_rg, params):
    (hg_lb, hg_norm_w, gd_conv_w, alog_vec, dt_vec, gd_norm_w, rg_conv_w, rg_conv_b, wr, br, wi, bi, lam) = params
    lb = _lb_of_layer(hg_lb, layer)
    y_hg, st_new = _hgrn2_chunk(proj[:, O_HQ:O_HQ + 256], proj[:, O_HF:O_HF + 256], proj[:, O_HI:O_HI + 256],
                                proj[:, O_HG:O_HG + 256], st_hg, lb, hg_norm_w)
    y_gd, s_new = _gdn_chunk(proj[:, O_GQ:O_GQ + 512], proj[:, O_GK:O_GK + 512], proj[:, O_GV:O_GV + 512],
                             proj[:, O_GZ:O_GZ + 512], proj[:, O_GBA:O_GBA + 128],
                             halo[:, O_GQ:O_GQ + 512], halo[:, O_GK:O_GK + 512], halo[:, O_GV:O_GV + 512],
                             s_gd, gd_conv_w, alog_vec, dt_vec, gd_norm_w)
    y_rg, h_new = _rglru_chunk(proj[:, O_RX:O_RX + 256], proj[:, O_RGATE:O_RGATE + 256], halo[:, O_RX:O_RX + 256],
                               h_rg, rg_conv_w, rg_conv_b, wr, br, wi, bi, lam)
    return jnp.concatenate([y_hg, y_gd, y_rg], axis=1), st_new, s_new, h_new


def _full(shape):
    return pl.BlockSpec(shape, lambda *_: (0,) * len(shape))


def _token_tile(seq, want):
    return min(want, seq)


def ffn_fwd(x, nw, pack, layer, which):
    seq = x.shape[0]
    t = _token_tile(seq, 512)
    bg, bu, bd = (_ffn_block(layer, which, p) for p in "gud")

    def body(x_ref, nw_ref, wg_ref, wu_ref, wd_ref, xo_ref, h_ref, g_ref, u_ref, a_ref, h_scr, acc):
        k = pl.program_id(1)

        @pl.when(k == 0)
        def _():
            xv = x_ref[...]
            h = _rms_fwd(xv, nw_ref[...]).astype(BF16)
            h_scr[...] = h
            h_ref[...] = h
            acc[...] = xv

        h = h_scr[...]
        g = _dot_nt(h, wg_ref[...])
        u = _dot_nt(h, wu_ref[...])
        a = (g * jax.nn.sigmoid(g) * u).astype(BF16)
        g_ref[...] = g.astype(BF16)
        u_ref[...] = u.astype(BF16)
        a_ref[...] = a
        acc[...] += 0.5 * _dot_nn(a, wd_ref[...])

        @pl.when(k == N_DEV - 1)
        def _():
            xo_ref[...] = acc[...]

    wspec = lambda blk: pl.BlockSpec((None, FF_PAD, D_MODEL), lambda i, k: (k, blk, 0))
    act = pl.BlockSpec((t, FF_PAD), lambda i, k: (i, k))
    row = pl.BlockSpec((t, D_MODEL), lambda i, k: (i, 0))
    ff = N_DEV * FF_PAD
    return pl.pallas_call(
        body, name=f"ffn_fwd_l{layer}_{which}", grid=(seq // t, N_DEV),
        in_specs=[row, _full((1, D_MODEL)), wspec(bg), wspec(bu), wspec(bd)],
        out_specs=[row, row, act, act, act],
        out_shape=[jax.ShapeDtypeStruct((seq, D_MODEL), F32), jax.ShapeDtypeStruct((seq, D_MODEL), BF16)]
        + [jax.ShapeDtypeStruct((seq, ff), BF16)] * 3,
        scratch_shapes=[pltpu.VMEM((t, D_MODEL), BF16), pltpu.VMEM((t, D_MODEL), F32)],
        compiler_params=_cparams(("parallel", "arbitrary")),
    )(x, nw, pack, pack, pack)


def ffn_bwd(dout, x, nw, g, u, pack, layer, which):
    seq = x.shape[0]
    t = _token_tile(seq, 512)
    bg, bu, bd = (_ffn_block(layer, which, p) for p in "gud")

    def body(do_ref, x_ref, nw_ref, g_ref, u_ref, wg_ref, wu_ref, wd_ref, dx_ref, dg_ref, du_ref, dnw_ref, doh, dh):
        i, k = pl.program_id(0), pl.program_id(1)

        @pl.when(k == 0)
        def _():
            doh[...] = (0.5 * do_ref[...]).astype(BF16)
            dh[...] = jnp.zeros_like(dh)

        @pl.when((k == 0) & (i == 0))
        def _():
            dnw_ref[...] = jnp.zeros_like(dnw_ref)

        da = _dot_nt(doh[...], wd_ref[...])
        gv = g_ref[...].astype(F32)
        uv = u_ref[...].astype(F32)
        s = jax.nn.sigmoid(gv)
        dg = (da * uv * (s * (1.0 + gv * (1.0 - s)))).astype(BF16)
        du = (da * gv * s).astype(BF16)
        dg_ref[...] = dg
        du_ref[...] = du
        dh[...] += _dot_nn(dg, wg_ref[...]) + _dot_nn(du, wu_ref[...])

        @pl.when(k == N_DEV - 1)
        def _():
            dx, dw = _rms_bwd(dh[...], x_ref[...], nw_ref[...])
            dx_ref[...] = do_ref[...] + dx
            dnw_ref[...] += dw

    wspec = lambda blk: pl.BlockSpec((None, FF_PAD, D_MODEL), lambda i, k: (k, blk, 0))
    act = pl.BlockSpec((t, FF_PAD), lambda i, k: (i, k))
    row = pl.BlockSpec((t, D_MODEL), lambda i, k: (i, 0))
    ff = N_DEV * FF_PAD
    return pl.pallas_call(
        body, name=f"ffn_bwd_l{layer}_{which}", grid=(seq // t, N_DEV),
        in_specs=[row, row, _full((1, D_MODEL)), act, act, wspec(bg), wspec(bu), wspec(bd)],
        out_specs=[row, act, act, _full((1, D_MODEL))],
        out_shape=[jax.ShapeDtypeStruct((seq, D_MODEL), F32), jax.ShapeDtypeStruct((seq, ff), BF16),
                   jax.ShapeDtypeStruct((seq, ff), BF16), jax.ShapeDtypeStruct((1, D_MODEL), F32)],
        scratch_shapes=[pltpu.VMEM((t, D_MODEL), BF16), pltpu.VMEM((t, D_MODEL), F32)],
        compiler_params=_cparams(("arbitrary", "arbitrary")),
    )(dout, x, nw, g, u, pack, pack, pack)


def tn_matmul(a, b, scale, tm, name, gbuf=None, blk_of_m=None):
    seq, m_total = a.shape
    tk = _token_tile(seq, 2048)
    nk = seq // tk

    def body(*refs):
        if gbuf is None:
            a_ref, b_ref, o_ref, acc = refs
        else:
            a_ref, b_ref, _, o_ref, acc = refs
        kk = pl.program_id(1)

        @pl.when(kk == 0)
        def _():
            acc[...] = jnp.zeros_like(acc)

        acc[...] += _dot_tn(a_ref[...].astype(BF16), b_ref[...].astype(BF16))

        @pl.when(kk == nk - 1)
        def _():
            o_ref[...] = acc[...] * scale

    in_specs = [pl.BlockSpec((tk, tm), lambda m, k: (k, m)), pl.BlockSpec((tk, D_MODEL), lambda m, k: (k, 0))]
    args = [a, b]
    if gbuf is None:
        out_spec = pl.BlockSpec((tm, D_MODEL), lambda m, k: (m, 0))
        out_shape = jax.ShapeDtypeStruct((m_total, D_MODEL), F32)
        aliases = {}
    else:
        in_specs.append(pl.BlockSpec(memory_space=pl.ANY))
        args.append(gbuf)
        out_spec = pl.BlockSpec((None, tm, D_MODEL), lambda m, k: (m, blk_of_m, 0))
        out_shape = jax.ShapeDtypeStruct(gbuf.shape, F32)
        aliases = {2: 0}
    return pl.pallas_call(
        body, name=name, grid=(m_total // tm, nk), in_specs=in_specs, out_specs=out_spec, out_shape=out_shape,
        scratch_shapes=[pltpu.VMEM((tm, D_MODEL), F32)], input_output_aliases=aliases,
        compiler_params=_cparams(("parallel", "arbitrary")),
    )(*args)


def in_proj_fwd(x, nw, wt, layer):
    seq = x.shape[0]
    t = _token_tile(seq, 256)

    def body(x_ref, nw_ref, w_ref, p_ref, h_ref):
        h = _rms_fwd(x_ref[...], nw_ref[...]).astype(BF16)
        h_ref[...] = h
        p_ref[...] = _dot_nt(h, w_ref[...])

    row = pl.BlockSpec((t, D_MODEL), lambda i: (i, 0))
    return pl.pallas_call(
        body, name=f"in_proj_fwd_l{layer}", grid=(seq // t,),
        in_specs=[row, _full((1, D_MODEL)), _full((NP, D_MODEL))],
        out_specs=[pl.BlockSpec((t, NP), lambda i: (i, 0)), row],
        out_shape=[jax.ShapeDtypeStruct((seq, NP), F32), jax.ShapeDtypeStruct((seq, D_MODEL), BF16)],
        compiler_params=_cparams(("parallel",)),
    )(x, nw, wt)


def in_proj_bwd(dproj, x, nw, dres, wt, layer):
    seq = x.shape[0]
    t = _token_tile(seq, 256)

    def body(dp_ref, x_ref, nw_ref, dr_ref, w_ref, dx_ref, dnw_ref):
        @pl.when(pl.program_id(0) == 0)
        def _():
            dnw_ref[...] = jnp.zeros_like(dnw_ref)

        dh = _dot_nn(dp_ref[...], w_ref[...])
        dx, dw = _rms_bwd(dh, x_ref[...], nw_ref[...])
        dx_ref[...] = dr_ref[...] + dx
        dnw_ref[...] += dw

    row = pl.BlockSpec((t, D_MODEL), lambda i: (i, 0))
    return pl.pallas_call(
        body, name=f"in_proj_bwd_l{layer}", grid=(seq // t,),
        in_specs=[pl.BlockSpec((t, NP), lambda i: (i, 0)), row, _full((1, D_MODEL)), row, _full((NP, D_MODEL))],
        out_specs=[row, _full((1, D_MODEL))],
        out_shape=[jax.ShapeDtypeStruct((seq, D_MODEL), F32), jax.ShapeDtypeStruct((1, D_MODEL), F32)],
        compiler_params=_cparams(("arbitrary",)),
    )(dproj, x, nw, dres, wt)


def out_proj_fwd(x, y, w, layer):
    seq = x.shape[0]
    t = _token_tile(seq, 512)

    def body(x_ref, y_ref, w_ref, o_ref):
        o_ref[...] = x_ref[...] + _dot_nn(y_ref[...], w_ref[...])

    row = pl.BlockSpec((t, D_MODEL), lambda i: (i, 0))
    return pl.pallas_call(
        body, name=f"out_proj_fwd_l{layer}", grid=(seq // t,),
        in_specs=[row, row, _full((D_MODEL, D_MODEL))], out_specs=row,
        out_shape=jax.ShapeDtypeStruct((seq, D_MODEL), F32), compiler_params=_cparams(("parallel",)),
    )(x, y, w)


def out_proj_bwd(dx, w, layer):
    seq = dx.shape[0]
    t = _token_tile(seq, 512)

    def body(d_ref, w_ref, o_ref):
        o_ref[...] = _dot_nt(d_ref[...].astype(BF16), w_ref[...])

    row = pl.BlockSpec((t, D_MODEL), lambda i: (i, 0))
    return pl.pallas_call(
        body, name=f"out_proj_bwd_l{layer}", grid=(seq // t,),
        in_specs=[row, _full((D_MODEL, D_MODEL))], out_specs=row,
        out_shape=jax.ShapeDtypeStruct((seq, D_MODEL), F32), compiler_params=_cparams(("parallel",)),
    )(dx, w)


def loss_head(x, nw, target):
    seq = x.shape[0]
    t = _token_tile(seq, 512)

    def body(x_ref, nw_ref, t_ref, loss_ref, dx_ref, dnw_ref):
        @pl.when(pl.program_id(0) == 0)
        def _():
            loss_ref[...] = jnp.zeros_like(loss_ref)
            dnw_ref[...] = jnp.zeros_like(dnw_ref)

        xv, w = x_ref[...], nw_ref[...]
        err = _rms_fwd(xv, w) - t_ref[...]
        loss_ref[...] += 0.5 * jnp.sum(jnp.mean(err * err, axis=-1, keepdims=True), axis=0, keepdims=True)
        dx, dw = _rms_bwd(err * (1.0 / D_MODEL), xv, w)
        dx_ref[...] = dx
        dnw_ref[...] += dw

    row = pl.BlockSpec((t, D_MODEL), lambda i: (i, 0))
    return pl.pallas_call(
        body, name="loss_head", grid=(seq // t,),
        in_specs=[row, _full((1, D_MODEL)), row], out_specs=[_full((1, 128)), row, _full((1, D_MODEL))],
        out_shape=[jax.ShapeDtypeStruct((1, 128), F32), jax.ShapeDtypeStruct((seq, D_MODEL), F32),
                   jax.ShapeDtypeStruct((1, D_MODEL), F32)],
        compiler_params=_cparams(("arbitrary",)),
    )(x, nw, target)


_MIXER_PARAM_SHAPES = ((DEPTH, HG_W), (1, HG_DK), (CONV_W, 3 * GD_W), (1, 128), (1, 128), (1, GD_DK),
                       (CONV_W, RG_W), (1, RG_W), (RG_W, RG_W), (1, RG_W), (RG_W, RG_W), (1, RG_W), (1, RG_W))
_STATE_SHAPES = ((HG_W, HG_W), (GD_HEADS * GD_DK, GD_DK), (1, RG_W))


def mixer_fwd(layer, proj, params):
    seq = proj.shape[0]
    n = seq // CHUNK

    def body(*refs):
        p_ref, halo_ref = refs[0], refs[1]
        prm = refs[2:15]
        y_ref, st_out, s_out, h_out = refs[15:19]
        st, s, h = refs[19:22]
        i = pl.program_id(0)

        @pl.when(i == 0)
        def _():
            st[...] = jnp.zeros_like(st)
            s[...] = jnp.zeros_like(s)
            h[...] = jnp.zeros_like(h)

        st_out[...] = st[...]
        s_out[...] = s[...]
        h_out[...] = h[...]
        halo = jnp.where(i > 0, halo_ref[...], 0.0)
        y, st_n, s_n, h_n = mixer_chunk(layer, p_ref[...], halo, st[...], s[...], h[...], tuple(r[...] for r in prm))
        y_ref[...] = y.astype(BF16)
        st[...] = st_n
        s[...] = s_n
        h[...] = h_n

    in_specs = [pl.BlockSpec((CHUNK, NP), lambda i: (i, 0)),
                pl.BlockSpec((8, NP), lambda i: (jnp.maximum(i * (CHUNK // 8) - 1, 0), 0))]
    in_specs += [_full(sh) for sh in _MIXER_PARAM_SHAPES]
    out_specs = [pl.BlockSpec((CHUNK, D_MODEL), lambda i: (i, 0))]
    out_specs += [pl.BlockSpec((None,) + sh, lambda i: (i, 0, 0)) for sh in _STATE_SHAPES]
    out_shape = [jax.ShapeDtypeStruct((seq, D_MODEL), BF16)]
    out_shape += [jax.ShapeDtypeStruct((n,) + sh, F32) for sh in _STATE_SHAPES]
    return pl.pallas_call(
        body, name=f"mixer_fwd_l{layer}", grid=(n,), in_specs=in_specs, out_specs=out_specs, out_shape=out_shape,
        scratch_shapes=[pltpu.VMEM(sh, F32) for sh in _STATE_SHAPES],
        compiler_params=_cparams(("arbitrary",)),
    )(proj, proj, *params)


def mixer_bwd(layer, proj, dy, states, params):
    seq = proj.shape[0]
    n = seq // CHUNK

    def body(*refs):
        p_ref, halo_ref, dy_ref, st_ref, s_ref, h_ref = refs[0:6]
        prm = refs[6:19]
        dp_ref = refs[19]
        dprm = refs[20:33]
        dst, ds, dh, dhalo = refs[33:37]
        i = pl.program_id(0)
        r = n - 1 - i

        @pl.when(i == 0)
        def _():
            for ref in (dst, ds, dh, dhalo) + tuple(dprm):
                ref[...] = jnp.zeros_like(ref)

        halo = jnp.where(r > 0, halo_ref[...], 0.0)
        fn = functools.partial(mixer_chunk, layer)
        _, vjp = jax.vjp(fn, p_ref[...], halo, st_ref[...], s_ref[...], h_ref[...], tuple(q[...] for q in prm))
        dproj, dhalo_n, dst_n, ds_n, dh_n, dparams = vjp((dy_ref[...], dst[...], ds[...], dh[...]))
        carry = jnp.concatenate([jnp.zeros((CHUNK - 8, NP), F32), dhalo[...]], axis=0)
        dp_ref[...] = (dproj + carry).astype(BF16)
        dst[...] = dst_n
        ds[...] = ds_n
        dh[...] = dh_n
        dhalo[...] = dhalo_n
        for ref, val in zip(dprm, dparams):
            ref[...] += val

    rev = lambda i: n - 1 - i
    in_specs = [pl.BlockSpec((CHUNK, NP), lambda i: (rev(i), 0)),
                pl.BlockSpec((8, NP), lambda i: (jnp.maximum(rev(i) * (CHUNK // 8) - 1, 0), 0)),
                pl.BlockSpec((CHUNK, D_MODEL), lambda i: (rev(i), 0))]
    in_specs += [pl.BlockSpec((None,) + sh, lambda i: (rev(i), 0, 0)) for sh in _STATE_SHAPES]
    in_specs += [_full(sh) for sh in _MIXER_PARAM_SHAPES]
    out_specs = [pl.BlockSpec((CHUNK, NP), lambda i: (rev(i), 0))] + [_full(sh) for sh in _MIXER_PARAM_SHAPES]
    out_shape = [jax.ShapeDtypeStruct((seq, NP), BF16)] + [jax.ShapeDtypeStruct(sh, F32) for sh in _MIXER_PARAM_SHAPES]
    return pl.pallas_call(
        body, name=f"mixer_bwd_l{layer}", grid=(n,), in_specs=in_specs, out_specs=out_specs, out_shape=out_shape,
        scratch_shapes=[pltpu.VMEM(sh, F32) for sh in _STATE_SHAPES] + [pltpu.VMEM((8, NP), F32)],
        compiler_params=_cparams(("arbitrary",)),
    )(proj, proj, dy, *states, *params)


def _win_segments():
    out = []
    for k in range(N_DEV):
        a, b = NIN_SHARD * k, NIN_SHARD * (k + 1)
        if a < GBA_SPLIT < b:
            out.append((k, 0, GBA_SPLIT - a, a))
            out.append((k, GBA_SPLIT - a, b - GBA_SPLIT, O_RX))
        elif b <= GBA_SPLIT:
            out.append((k, 0, NIN_SHARD, a))
        else:
            out.append((k, 0, NIN_SHARD, a + O_RX - GBA_SPLIT))
    return out


def win_to_padded(wt_shards, layer):
    lanes = 256

    def body(src, dst, scr):
        scr[...] = jnp.zeros_like(scr)
        for (k, s0, rows, d0) in _win_segments():
            scr[pl.ds(d0, rows), :] = src[k, pl.ds(s0, rows), :].astype(F32)
        dst[...] = scr[...].astype(BF16)

    return pl.pallas_call(
        body, name=f"win_to_padded_l{layer}", grid=(D_MODEL // lanes,),
        in_specs=[pl.BlockSpec((N_DEV, NIN_SHARD_PAD, lanes), lambda j: (0, 0, j))],
        out_specs=pl.BlockSpec((NP, lanes), lambda j: (0, j)),
        out_shape=jax.ShapeDtypeStruct((NP, D_MODEL), BF16),
        scratch_shapes=[pltpu.VMEM((NP, lanes), F32)], compiler_params=_cparams(("parallel",)),
    )(wt_shards)


def win_grad_to_pack(dwt, gbuf, layer):
    lanes = 256
    blk = (OFF_WIN + layer * NIN_SHARD_PAD) // NIN_SHARD_PAD

    def body(src, _, dst):
        dst[...] = jnp.zeros_like(dst)
        for (k, s0, rows, d0) in _win_segments():
            dst[k, pl.ds(s0, rows), :] = src[pl.ds(d0, rows), :]

    return pl.pallas_call(
        body, name=f"win_grad_to_pack_l{layer}", grid=(D_MODEL // lanes,),
        in_specs=[pl.BlockSpec((NP, lanes), lambda j: (0, j)), pl.BlockSpec(memory_space=pl.ANY)],
        out_specs=pl.BlockSpec((N_DEV, NIN_SHARD_PAD, lanes), lambda j: (0, blk, j)),
        out_shape=jax.ShapeDtypeStruct(gbuf.shape, F32), input_output_aliases={1: 0},
        compiler_params=_cparams(("parallel",)),
    )(dwt, gbuf)


def sum_adamw(parts, w, m, v, name, rows_per_step):
    n, rows, cols = parts.shape
    tr = min(rows_per_step, rows)
    c1 = 1.0 - ADAM_B1 ** ADAM_STEP
    c2 = 1.0 - ADAM_B2 ** ADAM_STEP

    def body(p_ref, w_ref, m_ref, v_ref, g_ref, d_ref, mo_ref, vo_ref):
        g = p_ref[0]
        for j in range(1, n):
            g = g + p_ref[j]
        mn = ADAM_B1 * m_ref[...] + (1.0 - ADAM_B1) * g
        vn = ADAM_B2 * v_ref[...] + (1.0 - ADAM_B2) * (g * g)
        g_ref[...] = g
        mo_ref[...] = mn
        vo_ref[...] = vn
        d_ref[...] = -ADAM_LR * ((mn / c1) / (jnp.sqrt(vn / c2) + ADAM_EPS) + ADAM_WD * w_ref[...])

    blk = pl.BlockSpec((tr, cols), lambda i: (i, 0))
    return pl.pallas_call(
        body, name=name, grid=(rows // tr,),
        in_specs=[pl.BlockSpec((n, tr, cols), lambda i: (0, i, 0)), blk, blk, blk], out_specs=[blk] * 4,
        out_shape=[jax.ShapeDtypeStruct((rows, cols), F32)] * 4, compiler_params=_cparams(("parallel",)),
    )(parts, w, m, v)


def exchange(arrays, scatter, name):
    n = len(arrays)

    def body(*refs):
        ins, outs = refs[:n], refs[n:2 * n]
        send_sems, recv_sems, local_sems = refs[2 * n:]
        x, y, c = lax.axis_index("x"), lax.axis_index("y"), lax.axis_index("c")
        me = 4 * x + 2 * y + c
        local = []
        for a in range(n):
            cp = pltpu.make_async_copy(ins[a].at[me] if scatter else ins[a], outs[a].at[me], local_sems.at[a])
            cp.start()
            local.append(cp)
        pending = []
        for j in range(1, N_DEV):
            px = 1 - x if j & 4 else x
            py = 1 - y if j & 2 else y
            pc = 1 - c if j & 1 else c
            peer = 4 * px + 2 * py + pc
            for a in range(n):
                src = ins[a].at[peer] if scatter else ins[a]
                k = a * (N_DEV - 1) + j - 1

                def copy(dst_slot):
                    return pltpu.make_async_remote_copy(
                        src_ref=src, dst_ref=outs[a].at[dst_slot], send_sem=send_sems.at[k], recv_sem=recv_sems.at[k],
                        device_id=(px, py, pc), device_id_type=pl.DeviceIdType.MESH)

                send = copy(me)
                send.start()
                pending.append((send, copy(peer)))
        for send, recv in pending:
            send.wait_send()
            recv.wait_recv()
        for cp in local:
            cp.wait()

    out_shape = [jax.ShapeDtypeStruct(a.shape if scatter else (N_DEV,) + a.shape, a.dtype) for a in arrays]
    any_spec = pl.BlockSpec(memory_space=pl.ANY)
    return pl.pallas_call(
        body, name=name, in_specs=[any_spec] * n, out_specs=[any_spec] * n, out_shape=out_shape,
        scratch_shapes=[pltpu.SemaphoreType.DMA((n * (N_DEV - 1),)), pltpu.SemaphoreType.DMA((n * (N_DEV - 1),)),
                        pltpu.SemaphoreType.DMA((n,))],
        compiler_params=pltpu.CompilerParams(has_side_effects=True),
    )(*arrays)


BIG = ("ffn1_gate", "ffn1_up", "ffn1_down", "w_in", "w_out", "ffn2_gate", "ffn2_up", "ffn2_down")
SMALL_REPLICATED = (("norm_ffn1", (DEPTH, D_MODEL)), ("norm_mix", (DEPTH, D_MODEL)), ("norm_ffn2", (DEPTH, D_MODEL)),
                    ("norm_final", (D_MODEL,)), ("hg_lb", (DEPTH, HG_W)), ("hg_norm_w", (DEPTH, HG_DK)),
                    ("gd_a_log", (DEPTH, GD_HEADS)), ("gd_dt_bias", (DEPTH, GD_HEADS)), ("gd_norm_w", (DEPTH, GD_DK)),
                    ("rg_conv_b", (DEPTH, RG_W)), ("rg_wr", (DEPTH, RG_BLOCKS, RG_BD, RG_BD)), ("rg_br", (DEPTH, RG_W)),
                    ("rg_wi", (DEPTH, RG_BLOCKS, RG_BD, RG_BD)), ("rg_bi", (DEPTH, RG_W)), ("rg_lambda", (DEPTH, RG_W)))
SMALL_EXTRA = (("gd_conv_w", (DEPTH, CONV_W, 3 * GD_W)), ("rg_conv_w", (DEPTH, CONV_W, RG_W)), ("loss", (1,)))
SMALL_SHARDED = (("gd_conv_w", (DEPTH, CONV_W, GDC_SHARD)), ("rg_conv_w", (DEPTH, CONV_W, RGC_SHARD)))


def _pad_rows(a, rows):
    return jnp.pad(a, ((0, rows - a.shape[0]), (0, 0)))


def pack_big(get):
    pieces = []
    for l in range(DEPTH):
        for which in (1, 2):
            pieces.append(_pad_rows(get(f"ffn{which}_gate")[l].T, FF_PAD))
            pieces.append(_pad_rows(get(f"ffn{which}_up")[l].T, FF_PAD))
            pieces.append(_pad_rows(get(f"ffn{which}_down")[l], FF_PAD))
    for l in range(DEPTH):
        pieces.append(_pad_rows(get("w_in")[l].T, NIN_SHARD_PAD))
    for l in range(DEPTH):
        pieces.append(get("w_out")[l])
    return jnp.concatenate(pieces, axis=0)


def unpack_big(p):
    out = {}
    for which in (1, 2):
        for part, nm in (("g", "gate"), ("u", "up"), ("d", "down")):
            per_layer = []
            for l in range(DEPTH):
                r0 = _ffn_block(l, which, part) * FF_PAD
                blk = p[r0:r0 + FF_SHARD, :]
                per_layer.append(blk if part == "d" else blk.T)
            out[f"ffn{which}_{nm}"] = jnp.stack(per_layer)
    out["w_in"] = jnp.stack([p[OFF_WIN + NIN_SHARD_PAD * l:OFF_WIN + NIN_SHARD_PAD * l + NIN_SHARD, :].T for l in range(DEPTH)])
    out["w_out"] = jnp.stack([p[OFF_WOUT + WOUT_SHARD * l:OFF_WOUT + WOUT_SHARD * (l + 1), :] for l in range(DEPTH)])
    return out


def pack_small(get, spec):
    flat = []
    for name, shape in spec:
        a = get(name)
        flat.append(jnp.zeros((math.prod(shape),), F32) if a is None else a.reshape(-1).astype(F32))
    v = jnp.concatenate(flat)
    total = -(-v.shape[0] // 1024) * 1024
    return jnp.pad(v, (0, total - v.shape[0])).reshape(total // 128, 128)


def unpack_small(p, spec):
    v = p.reshape(-1)
    out, off = {}, 0
    for name, shape in spec:
        size = math.prod(shape)
        out[name] = v[off:off + size].reshape(shape)
        off += size
    return out


def _block_diag(w):
    rows = []
    for i in range(RG_BLOCKS):
        rows.append(jnp.concatenate([w[i] if j == i else jnp.zeros((RG_BD, RG_BD), F32) for j in range(RG_BLOCKS)], axis=1))
    return jnp.concatenate(rows, axis=0)


def _diag_blocks(w):
    return jnp.stack([w[RG_BD * i:RG_BD * (i + 1), RG_BD * i:RG_BD * (i + 1)] for i in range(RG_BLOCKS)])


def _lane_vec(v):
    return jnp.pad(v.astype(F32), (GD_HEADS, 128 - 2 * GD_HEADS))[None]


INPUT_NAMES = ("x", "norm_ffn1", "ffn1_gate", "ffn1_up", "ffn1_down", "norm_mix", "w_in", "hg_lb", "hg_norm_w",
               "gd_conv_w", "gd_a_log", "gd_dt_bias", "gd_norm_w", "rg_conv_w", "rg_conv_b", "rg_wr", "rg_br", "rg_wi",
               "rg_bi", "rg_lambda", "w_out", "norm_ffn2", "ffn2_gate", "ffn2_up", "ffn2_down", "norm_final")
WEIGHT_NAMES = INPUT_NAMES[1:]


def _step(a):
    x = a["x"][0]
    target = a["loss_target"][0]
    me = 4 * lax.axis_index("x") + 2 * lax.axis_index("y") + lax.axis_index("c")

    w_pack = pack_big(lambda nm: a[nm])
    conv_local = pack_small(lambda nm: a[nm], SMALL_SHARDED)
    gw, gconv = exchange([w_pack.astype(BF16), conv_local], False, "gather_weights")
    conv_parts = [unpack_small(gconv[d], SMALL_SHARDED) for d in range(N_DEV)]
    gd_conv_w = jnp.concatenate([p["gd_conv_w"] for p in conv_parts], axis=-1)
    rg_conv_w = jnp.concatenate([p["rg_conv_w"] for p in conv_parts], axis=-1)

    def mixer_params(l):
        return (a["hg_lb"], a["hg_norm_w"][l][None], gd_conv_w[l], _lane_vec(a["gd_a_log"][l]), _lane_vec(a["gd_dt_bias"][l]),
                a["gd_norm_w"][l][None], rg_conv_w[l], a["rg_conv_b"][l][None], _block_diag(a["rg_wr"][l]), a["rg_br"][l][None],
                _block_diag(a["rg_wi"][l]), a["rg_bi"][l][None], a["rg_lambda"][l][None])

    saved = []
    xs = x
    for l in range(DEPTH):
        x1, h1, g1, u1, a1 = ffn_fwd(xs, a["norm_ffn1"][l][None], gw, l, 1)
        wt = win_to_padded(gw[:, OFF_WIN + NIN_SHARD_PAD * l:OFF_WIN + NIN_SHARD_PAD * (l + 1), :], l)
        proj, h2 = in_proj_fwd(x1, a["norm_mix"][l][None], wt, l)
        prm = mixer_params(l)
        y, st, s, h = mixer_fwd(l, proj, prm)
        wo = gw[:, OFF_WOUT + WOUT_SHARD * l:OFF_WOUT + WOUT_SHARD * (l + 1), :].reshape(D_MODEL, D_MODEL)
        x2 = out_proj_fwd(x1, y, wo, l)
        x3, h3, g3, u3, a3 = ffn_fwd(x2, a["norm_ffn2"][l][None], gw, l, 2)
        saved.append(dict(x0=xs, x1=x1, x2=x2, h1=h1, g1=g1, u1=u1, a1=a1, wt=wt, proj=proj, h2=h2, prm=prm, y=y,
                          states=(st, s, h), wo=wo, h3=h3, g3=g3, u3=u3, a3=a3))
        xs = x3
    loss_row, dx, d_norm_final = loss_head(xs, a["norm_final"][None], target)

    gbuf = jnp.zeros((N_DEV, PACK_ROWS, D_MODEL), F32)
    sg = {"norm_final": d_norm_final[0], "loss": loss_row[0, 0:1]}
    per_layer = {nm: [None] * DEPTH for nm in ("norm_ffn1", "norm_mix", "norm_ffn2", "hg_norm_w", "gd_conv_w", "gd_a_log",
                                                "gd_dt_bias", "gd_norm_w", "rg_conv_w", "rg_conv_b", "rg_wr", "rg_br", "rg_wi",
                                                "rg_bi", "rg_lambda")}
    d_hg_lb = jnp.zeros((DEPTH, HG_W), F32)
    for l in reversed(range(DEPTH)):
        sv = saved[l]
        dx2, dg, du, dn2 = ffn_bwd(dx, sv["x2"], a["norm_ffn2"][l][None], sv["g3"], sv["u3"], gw, l, 2)
        gbuf = tn_matmul(dg, sv["h3"], 1.0, FF_PAD, f"dw_gate_l{l}_2", gbuf, _ffn_block(l, 2, "g"))
        gbuf = tn_matmul(du, sv["h3"], 1.0, FF_PAD, f"dw_up_l{l}_2", gbuf, _ffn_block(l, 2, "u"))
        gbuf = tn_matmul(sv["a3"], dx, 0.5, FF_PAD, f"dw_down_l{l}_2", gbuf, _ffn_block(l, 2, "d"))
        dy = out_proj_bwd(dx2, sv["wo"], l)
        gbuf = tn_matmul(sv["y"], dx2, 1.0, WOUT_SHARD, f"dw_out_l{l}", gbuf, (OFF_WOUT + WOUT_SHARD * l) // WOUT_SHARD)
        mb = mixer_bwd(l, sv["proj"], dy, sv["states"], sv["prm"])
        dproj, dprm = mb[0], mb[1:]
        dwt = tn_matmul(dproj, sv["h2"], 1.0, FF_PAD, f"dw_in_l{l}")
        gbuf = win_grad_to_pack(dwt, gbuf, l)
        dx1, dnm = in_proj_bwd(dproj, sv["x1"], a["norm_mix"][l][None], dx2, sv["wt"], l)
        dx, dg, du, dn1 = ffn_bwd(dx1, sv["x0"], a["norm_ffn1"][l][None], sv["g1"], sv["u1"], gw, l, 1)
        gbuf = tn_matmul(dg, sv["h1"], 1.0, FF_PAD, f"dw_gate_l{l}_1", gbuf, _ffn_block(l, 1, "g"))
        gbuf = tn_matmul(du, sv["h1"], 1.0, FF_PAD, f"dw_up_l{l}_1", gbuf, _ffn_block(l, 1, "u"))
        gbuf = tn_matmul(sv["a1"], dx1, 0.5, FF_PAD, f"dw_down_l{l}_1", gbuf, _ffn_block(l, 1, "d"))
        (g_lb, g_hnw, g_gcw, g_alog, g_dt, g_gnw, g_rcw, g_rcb, g_wr, g_br, g_wi, g_bi, g_lam) = dprm
        d_hg_lb = d_hg_lb + g_lb
        for nm, val in (("norm_ffn1", dn1[0]), ("norm_mix", dnm[0]), ("norm_ffn2", dn2[0]), ("hg_norm_w", g_hnw[0]),
                        ("gd_conv_w", g_gcw), ("gd_a_log", g_alog[0, GD_HEADS:2 * GD_HEADS]),
                        ("gd_dt_bias", g_dt[0, GD_HEADS:2 * GD_HEADS]), ("gd_norm_w", g_gnw[0]), ("rg_conv_w", g_rcw),
                        ("rg_conv_b", g_rcb[0]), ("rg_wr", _diag_blocks(g_wr)), ("rg_br", g_br[0]),
                        ("rg_wi", _diag_blocks(g_wi)), ("rg_bi", g_bi[0]), ("rg_lambda", g_lam[0])):
            per_layer[nm][l] = val
    grad_x = dx
    for nm, vals in per_layer.items():
        sg[nm] = jnp.stack(vals)
    sg["hg_lb"] = d_hg_lb

    small_spec = SMALL_REPLICATED + SMALL_EXTRA
    (parts_big,) = exchange([gbuf], True, "scatter_grads")
    g_big, d_big, m_big, v_big = sum_adamw(parts_big, w_pack, pack_big(lambda nm: a["m_" + nm]),
                                           pack_big(lambda nm: a["v_" + nm]), "adamw_big", 128)
    (parts_small,) = exchange([pack_small(lambda nm: sg[nm], small_spec)], False, "gather_small_grads")
    repl = dict(SMALL_REPLICATED)
    small_in = lambda pre: pack_small(lambda nm: a[pre + nm] if nm in repl else None, small_spec)
    rows_small = parts_small.shape[1]
    g_sm, d_sm, m_sm, v_sm = sum_adamw(parts_small, small_in(""), small_in("m_"), small_in("v_"), "adamw_small", rows_small)
    g_small = unpack_small(g_sm, small_spec)
    conv_grads = {"gd_conv_w": lax.dynamic_slice_in_dim(g_small["gd_conv_w"], me * GDC_SHARD, GDC_SHARD, axis=2),
                  "rg_conv_w": lax.dynamic_slice_in_dim(g_small["rg_conv_w"], me * RGC_SHARD, RGC_SHARD, axis=2)}
    conv_in = lambda pre: pack_small(lambda nm: a[pre + nm], SMALL_SHARDED)
    g_cv, d_cv, m_cv, v_cv = sum_adamw(pack_small(lambda nm: conv_grads[nm], SMALL_SHARDED)[None], conv_in(""),
                                       conv_in("m_"), conv_in("v_"), "adamw_conv", conv_local.shape[0])

    results = []
    for big, small, conv in ((g_big, g_sm, g_cv), (d_big, d_sm, d_cv), (m_big, m_sm, m_cv), (v_big, v_sm, v_cv)):
        vals = unpack_big(big)
        vals.update({k: v for k, v in unpack_small(small, small_spec).items() if k in repl})
        vals.update(unpack_small(conv, SMALL_SHARDED))
        results.append(vals)
    loss = g_small["loss"][0]
    out = [loss, grad_x[None]]
    for vals in results:
        out.extend(vals[nm] for nm in WEIGHT_NAMES)
    return tuple(out)


def kernel(x, norm_ffn1, ffn1_gate, ffn1_up, ffn1_down, norm_mix, w_in, hg_lb, hg_norm_w, gd_conv_w, gd_a_log, gd_dt_bias, gd_norm_w, rg_conv_w, rg_conv_b, rg_wr, rg_br, rg_wi, rg_bi, rg_lambda, w_out, norm_ffn2, ffn2_gate, ffn2_up, ffn2_down, norm_final, loss_target, m_norm_ffn1, m_ffn1_gate, m_ffn1_up, m_ffn1_down, m_norm_mix, m_w_in, m_hg_lb, m_hg_norm_w, m_gd_conv_w, m_gd_a_log, m_gd_dt_bias, m_gd_norm_w, m_rg_conv_w, m_rg_conv_b, m_rg_wr, m_rg_br, m_rg_wi, m_rg_bi, m_rg_lambda, m_w_out, m_norm_ffn2, m_ffn2_gate, m_ffn2_up, m_ffn2_down, m_norm_final, v_norm_ffn1, v_ffn1_gate, v_ffn1_up, v_ffn1_down, v_norm_mix, v_w_in, v_hg_lb, v_hg_norm_w, v_gd_conv_w, v_gd_a_log, v_gd_dt_bias, v_gd_norm_w, v_rg_conv_w, v_rg_conv_b, v_rg_wr, v_rg_br, v_rg_wi, v_rg_bi, v_rg_lambda, v_w_out, v_norm_ffn2, v_ffn2_gate, v_ffn2_up, v_ffn2_down, v_norm_final):
    args = locals()
    return _step(dict(args))
```

```python
import functools
import math

import jax
import jax.numpy as jnp
from jax import lax
from jax.experimental import pallas as pl
from jax.experimental.pallas import tpu as pltpu

F32 = jnp.float32
BF16 = jnp.bfloat16

D_MODEL = 1024
DEPTH = 2
D_FF = 2816
HG_HEADS, HG_DK = 4, 64
HG_W = 256
GD_HEADS, GD_DK = 4, 128
GD_W = 512
RG_W = 256
RG_BLOCKS, RG_BD = 4, 64
RG_C = 8.0
CONV_W = 4
CHUNK = 64
EPS = 1e-6
N_IN = 3592
ADAM_LR, ADAM_B1, ADAM_B2, ADAM_EPS, ADAM_WD, ADAM_STEP = 0.001, 0.9, 0.999, 1e-08, 0.01, 10

N_DEV = 8
FF_SHARD = D_FF // N_DEV
FF_PAD = 384
NIN_SHARD = N_IN // N_DEV
NIN_SHARD_PAD = 512
WOUT_SHARD = D_MODEL // N_DEV
GDC_SHARD = (2 * 512 + 512) // N_DEV
RGC_SHARD = RG_W // N_DEV

O_HQ, O_HF, O_HI, O_HG = 0, 256, 512, 768
O_GQ, O_GK, O_GV, O_GZ = 1024, 1536, 2048, 2560
O_GBA = 3072
O_RX, O_RGATE = 3200, 3456
NP = 3840
GBA_SPLIT = 3080

N_FFN_BLOCKS = DEPTH * 2 * 3
OFF_WIN = N_FFN_BLOCKS * FF_PAD
OFF_WOUT = OFF_WIN + DEPTH * NIN_SHARD_PAD
PACK_ROWS = OFF_WOUT + DEPTH * WOUT_SHARD

VMEM_LIMIT = 56 * 1024 * 1024
FFN_GROUP = 2
FFN_STEPS = N_DEV // FFN_GROUP
FFN_COLS = FFN_GROUP * FF_PAD
MIXER_CHUNKS_PER_STEP = 2


def _ffn_block(layer, which, part):
    return layer * 6 + (which - 1) * 3 + {"g": 0, "u": 1, "d": 2}[part]


def _cparams(sem, **kw):
    return pltpu.CompilerParams(dimension_semantics=sem, vmem_limit_bytes=VMEM_LIMIT, **kw)


def _dot_nn(a, b):
    return lax.dot_general(a, b, (((1,), (0,)), ((), ())), preferred_element_type=F32)


def _dot_nt(a, b):
    return lax.dot_general(a, b, (((1,), (1,)), ((), ())), preferred_element_type=F32)


def _dot_tn(a, b):
    return lax.dot_general(a, b, (((0,), (0,)), ((), ())), preferred_element_type=F32)


@jax.custom_vjp
def mm_nn(a, b):
    return _dot_nn(a.astype(BF16), b.astype(BF16))


def _mm_nn_fwd(a, b):
    a, b = a.astype(BF16), b.astype(BF16)
    return _dot_nn(a, b), (a, b)


def _mm_nn_bwd(res, g):
    a, b = res
    g = g.astype(BF16)
    return _dot_nt(g, b), _dot_tn(a, g)


mm_nn.defvjp(_mm_nn_fwd, _mm_nn_bwd)


@jax.custom_vjp
def mm_nt(a, b):
    return _dot_nt(a.astype(BF16), b.astype(BF16))


def _mm_nt_fwd(a, b):
    a, b = a.astype(BF16), b.astype(BF16)
    return _dot_nt(a, b), (a, b)


def _mm_nt_bwd(res, g):
    a, b = res
    g = g.astype(BF16)
    return _dot_nn(g, b), _dot_tn(g, a)


mm_nt.defvjp(_mm_nt_fwd, _mm_nt_bwd)


@jax.custom_vjp
def mm_tn(a, b):
    return _dot_tn(a.astype(BF16), b.astype(BF16))


def _mm_tn_fwd(a, b):
    a, b = a.astype(BF16), b.astype(BF16)
    return _dot_tn(a, b), (a, b)


def _mm_tn_bwd(res, g):
    a, b = res
    g = g.astype(BF16)
    return _dot_nt(b, g), _dot_nn(a, g)


mm_tn.defvjp(_mm_tn_fwd, _mm_tn_bwd)


def _split3(x):
    hi = x.astype(BF16)
    r = x - hi.astype(F32)
    mid = r.astype(BF16)
    lo = (r - mid.astype(F32)).astype(BF16)
    return hi, mid, lo


def _sel_apply(sel, x):
    hi, mid, lo = _split3(x)
    return _dot_nn(sel, hi) + _dot_nn(sel, mid) + _dot_nn(sel, lo)


def _sel_apply_t(sel, x):
    hi, mid, lo = _split3(x)
    return _dot_tn(sel, hi) + _dot_tn(sel, mid) + _dot_tn(sel, lo)


@jax.custom_vjp
def sel_mm(sel, x):
    return _sel_apply(sel, x)


def _sel_mm_fwd(sel, x):
    return _sel_apply(sel, x), sel


def _sel_mm_bwd(sel, g):
    return jnp.zeros_like(sel), _sel_apply_t(sel, g)


sel_mm.defvjp(_sel_mm_fwd, _sel_mm_bwd)


def _iota(shape, dim):
    return lax.broadcasted_iota(jnp.int32, shape, dim)


def _sigmoid(x):
    return jax.nn.sigmoid(x)


def _silu(x):
    return x * _sigmoid(x)


def _softplus(x):
    return jnp.maximum(x, 0.0) + jnp.log(1.0 + jnp.exp(-jnp.abs(x)))


def _gelu_tanh(x):
    return 0.5 * x * (1.0 + jnp.tanh(math.sqrt(2.0 / math.pi) * (x + 0.044715 * x * x * x)))


def _rms_fwd(x, w):
    rstd = lax.rsqrt(jnp.mean(x * x, axis=-1, keepdims=True) + EPS)
    return x * rstd * w


def _rms_bwd(dh, x, w):
    rstd = lax.rsqrt(jnp.mean(x * x, axis=-1, keepdims=True) + EPS)
    xhat = x * rstd
    dxhat = dh * w
    dx = rstd * (dxhat - xhat * jnp.mean(dxhat * xhat, axis=-1, keepdims=True))
    return dx, jnp.sum(dh * xhat, axis=0, keepdims=True)


def _shift_rows(x, halo, k):
    if k == 0:
        return x
    n = x.shape[0]
    ext = jnp.concatenate([jnp.zeros((n - halo.shape[0], x.shape[1]), x.dtype), halo], axis=0)
    t = _iota(x.shape, 0)
    return jnp.where(t >= k, jnp.roll(x, k, axis=0), jnp.roll(ext, k, axis=0))


def _causal_conv(x, halo, w):
    y = x * w[3:4, :]
    for k in range(1, CONV_W):
        y = y + _shift_rows(x, halo, k) * w[3 - k:4 - k, :]
    return y


def _head_sum_matrix(width, head):
    return (_iota((width, width), 0) // head == _iota((width, width), 1) // head)


def _hgrn2_chunk(hq, hf, hi, hg, st, lb, norm_w):
    n = hq.shape[0]
    w = HG_W
    q = _silu(hq) * (HG_DK ** -0.5)
    f = lb + (1.0 - lb) * _sigmoid(hf)
    k = 1.0 - f
    v = hi
    logf = jnp.log(f)
    tri = (_iota((n, n), 0) >= _iota((n, n), 1)).astype(BF16)
    cum = sel_mm(tri, logf)
    blk = _head_sum_matrix(w, HG_DK)
    lane_head = _iota((n, w), 1) // HG_DK

    def stack(x):
        return jnp.concatenate([jnp.where(lane_head == h, x, 0.0) for h in range(HG_HEADS)], axis=0)

    t_idx = _iota((n, HG_HEADS * n), 0)
    s_idx = _iota((n, HG_HEADS * n), 1) % n
    p = jnp.where(t_idx == s_idx, mm_nt(q, stack(k)), 0.0)
    b = n // 2
    while b >= 1:
        ref_row = (_iota((n, n), 0) // (2 * b)) * (2 * b) + b - 1
        sel = (_iota((n, n), 1) == ref_row).astype(BF16)
        ref = sel_mm(sel, cum)
        qe = q * jnp.exp(jnp.minimum(cum - ref, 0.0))
        ke = k * jnp.exp(jnp.minimum(ref - cum, 0.0))
        mask = (t_idx // (2 * b) == s_idx // (2 * b)) & (t_idx % (2 * b) >= b) & (s_idx % (2 * b) < b)
        p = p + jnp.where(mask, mm_nt(qe, stack(ke)), 0.0)
        b //= 2
    o = mm_nn(p, stack(v)) + mm_nt(q * jnp.exp(cum), st)
    last = cum[n - 1:n, :]
    st_new = st * jnp.exp(last) + jnp.where(blk, mm_tn(v, k * jnp.exp(last - cum)), 0.0)
    ms = _segment_mean(o * o, blk, HG_DK)
    nw = jnp.concatenate([norm_w] * HG_HEADS, axis=1)
    y = o * lax.rsqrt(ms + EPS) * nw * _silu(hg)
    return y, st_new


def _segment_mean(x, blk, head):
    return _seg_sum(x, blk.astype(BF16)) * (1.0 / head)


def _seg_apply(x, b):
    hi, mid, lo = _split3(x)
    return _dot_nn(hi, b) + _dot_nn(mid, b) + _dot_nn(lo, b)


@jax.custom_vjp
def _seg_sum(x, b):
    return _seg_apply(x, b)


def _seg_sum_fwd(x, b):
    return _seg_apply(x, b), b


def _seg_sum_bwd(b, g):
    return _seg_apply(g, b), jnp.zeros_like(b)


_seg_sum.defvjp(_seg_sum_fwd, _seg_sum_bwd)


def _gdn_chunk(gq, gk, gv, gz, gba, halo_q, halo_k, halo_v, s_gd, conv_w, alog_vec, dt_vec, norm_w):
    n = gq.shape[0]
    hn = GD_HEADS * n
    cq = _silu(_causal_conv(gq, halo_q, conv_w[:, 0:GD_W]))
    ck = _silu(_causal_conv(gk, halo_k, conv_w[:, GD_W:2 * GD_W]))
    cv = _silu(_causal_conv(gv, halo_v, conv_w[:, 2 * GD_W:3 * GD_W]))

    def stack(x):
        return jnp.concatenate([x[:, GD_DK * h:GD_DK * (h + 1)] for h in range(GD_HEADS)], axis=0)

    def l2n(x):
        return x * lax.rsqrt(jnp.sum(x * x, axis=-1, keepdims=True) + EPS)

    q = l2n(stack(cq)) * (GD_DK ** -0.5)
    k = l2n(stack(ck))
    v = stack(cv)
    beta_full = _sigmoid(gba)
    g_full = -jnp.exp(alog_vec) * _softplus(gba + dt_vec)
    tri = (_iota((n, n), 0) >= _iota((n, n), 1)).astype(BF16)
    cum_full = sel_mm(tri, g_full)
    beta = jnp.concatenate([beta_full[:, h:h + 1] for h in range(GD_HEADS)], axis=0)
    cum = jnp.concatenate([cum_full[:, 4 + h:5 + h] for h in range(GD_HEADS)], axis=0)
    ccol = jnp.broadcast_to(cum, (hn, hn))
    diff = ccol - jnp.transpose(ccol)
    r_i, c_i = _iota((hn, hn), 0), _iota((hn, hn), 1)
    same = (r_i // n) == (c_i // n)
    incl = same & (r_i >= c_i)
    strict = same & (r_i > c_i)
    decay = jnp.exp(jnp.minimum(diff, 0.0))
    kb = k * beta
    lower = jnp.where(strict, mm_nt(kb, k) * decay, 0.0)
    tinv = unit_lower_inverse(lower, n)
    ecum = jnp.exp(cum)
    rhs = jnp.concatenate([v * beta, kb * ecum], axis=1)
    sol = mm3_nn(tinv, rhs)
    u, w = sol[:, 0:GD_DK], sol[:, GD_DK:2 * GD_DK]

    def per_head(x, fn):
        return jnp.concatenate([fn(x[n * h:n * (h + 1), :], s_gd[GD_DK * h:GD_DK * (h + 1), :]) for h in range(GD_HEADS)], axis=0)

    v_new = u - per_head(w, mm_nn)
    scores = jnp.where(incl, mm_nt(q, k) * decay, 0.0)
    o = per_head(q * ecum, mm_nn) + mm_nn(scores, v_new)
    last = jnp.concatenate([jnp.broadcast_to(cum[n * (h + 1) - 1:n * (h + 1), :], (n, 1)) for h in range(GD_HEADS)], axis=0)
    kd = k * jnp.exp(last - cum)
    s_parts = []
    for h in range(GD_HEADS):
        sl = slice(n * h, n * (h + 1))
        s_parts.append(s_gd[GD_DK * h:GD_DK * (h + 1), :] * jnp.exp(last[n * h:n * h + 1, :]) + mm_tn(kd[sl, :], v_new[sl, :]))
    s_new = jnp.concatenate(s_parts, axis=0)
    on = o * lax.rsqrt(jnp.mean(o * o, axis=-1, keepdims=True) + EPS) * norm_w
    y = jnp.concatenate([on[n * h:n * (h + 1), :] for h in range(GD_HEADS)], axis=1) * _silu(gz)
    return y, s_new


def _split2(x):
    hi = x.astype(BF16)
    return hi, (x - hi.astype(F32)).astype(BF16)


def _dot3(dot, a, b):
    ah, al = _split2(a)
    bh, bl = _split2(b)
    return dot(ah, bh) + dot(ah, bl) + dot(al, bh)


def _inverse_chain(lower, n):
    size = lower.shape[0]
    r, c = _iota((size, size), 0), _iota((size, size), 1)
    tinv = (r == c).astype(F32)
    b = 1
    while b < n:
        off = (r // (2 * b) == c // (2 * b)) & (r % (2 * b) >= b) & (c % (2 * b) < b)
        tinv = tinv - _dot3(_dot_nn, tinv, _dot3(_dot_nn, jnp.where(off, lower, 0.0), tinv))
        b *= 2
    return tinv


@functools.partial(jax.custom_vjp, nondiff_argnums=(1,))
def unit_lower_inverse(lower, n):
    return _inverse_chain(lower, n)


def _unit_lower_inverse_fwd(lower, n):
    tinv = _inverse_chain(lower, n)
    return tinv, tinv


def _unit_lower_inverse_bwd(n, tinv, g):
    return (-_dot3(_dot_nt, _dot3(_dot_tn, tinv, g), tinv),)


unit_lower_inverse.defvjp(_unit_lower_inverse_fwd, _unit_lower_inverse_bwd)


def mm3_nn(a, b):
    ah = a.astype(BF16).astype(F32)
    al = a - ah
    bh = b.astype(BF16).astype(F32)
    bl = b - bh
    return mm_nn(ah, bh) + mm_nn(ah, bl) + mm_nn(al, bh)


def _rglru_chunk(rx, rgate, halo_x, h_in, conv_w, conv_b, wr, br, wi, bi, lam):
    n = rx.shape[0]
    xc = _causal_conv(rx, halo_x, conv_w) + conv_b
    r = _sigmoid(mm_nn(xc, wr) + br)
    ig = _sigmoid(mm_nn(xc, wi) + bi)
    log_a = -RG_C * r * _softplus(-lam)
    a = jnp.exp(log_a)
    two = 2.0 * log_a
    one_minus = -jnp.tanh(0.5 * two) * (jnp.exp(two) + 1.0)
    bx = jnp.sqrt(one_minus) * (ig * xc)
    t = _iota((n, RG_W), 0)
    acc_a, acc_b = a, bx
    k = 1
    while k < n:
        sa = jnp.where(t >= k, jnp.roll(acc_a, k, axis=0), 1.0)
        sb = jnp.where(t >= k, jnp.roll(acc_b, k, axis=0), 0.0)
        acc_b = acc_a * sb + acc_b
        acc_a = acc_a * sa
        k *= 2
    hs = acc_b + acc_a * h_in
    return hs * _gelu_tanh(rgate), hs[n - 1:n, :]


def _lb_of_layer(hg_lb, layer):
    if layer == 0:
        return jnp.zeros((1, HG_W), F32)
    e = jnp.exp(hg_lb - jnp.max(hg_lb, axis=0, keepdims=True))
    sm = e / jnp.sum(e, axis=0, keepdims=True)
    lb = sm[1:2, :]
    for l in range(2, layer + 1):
        lb = lb + sm[l:l + 1, :]
    return lb


MIXER_PARAMS = ("hg_lb", "hg_norm_w", "gd_conv_w", "gd_alog_vec", "gd_dt_vec", "gd_norm_w",
                "rg_conv_w", "rg_conv_b", "rg_wr_bd", "rg_br", "rg_wi_bd", "rg_bi", "rg_lambda")


def mixer_chunk(layer, proj, halo, st_hg, s_gd, h_rg, params):
    (hg_lb, hg_norm_w, gd_conv_w, alog_vec, dt_vec, gd_norm_w, rg_conv_w, rg_conv_b, wr, br, wi, bi, lam) = params
    lb = _lb_of_layer(hg_lb, layer)
    y_hg, st_new = _hgrn2_chunk(proj[:, O_HQ:O_HQ + 256], proj[:, O_HF:O_HF + 256], proj[:, O_HI:O_HI + 256],
                                proj[:, O_HG:O_HG + 256], st_hg, lb, hg_norm_w)
    y_gd, s_new = _gdn_chunk(proj[:, O_GQ:O_GQ + 512], proj[:, O_GK:O_GK + 512], proj[:, O_GV:O_GV + 512],
                             proj[:, O_GZ:O_GZ + 512], proj[:, O_GBA:O_GBA + 128],
                             halo[:, O_GQ:O_GQ + 512], halo[:, O_GK:O_GK + 512], halo[:, O_GV:O_GV + 512],
                             s_gd, gd_conv_w, alog_vec, dt_vec, gd_norm_w)
    y_rg, h_new = _rglru_chunk(proj[:, O_RX:O_RX + 256], proj[:, O_RGATE:O_RGATE + 256], halo[:, O_RX:O_RX + 256],
                               h_rg, rg_conv_w, rg_conv_b, wr, br, wi, bi, lam)
    return jnp.concatenate([y_hg, y_gd, y_rg], axis=1), st_new, s_new, h_new


def _full(shape):
    return pl.BlockSpec(shape, lambda *_: (0,) * len(shape))


def _token_tile(seq, want):
    return min(want, seq)


def ffn_fwd(x, nw, pack, layer, which):
    seq = x.shape[0]
    t = _token_tile(seq, 1024)
    bg, bu, bd = (_ffn_block(layer, which, p) for p in "gud")

    def body(x_ref, nw_ref, wg_ref, wu_ref, wd_ref, xo_ref, h_ref, g_ref, u_ref, a_ref, h_scr, acc):
        k = pl.program_id(1)

        @pl.when(k == 0)
        def _():
            xv = x_ref[...]
            h = _rms_fwd(xv, nw_ref[...]).astype(BF16)
            h_scr[...] = h
            h_ref[...] = h
            acc[...] = xv

        h = h_scr[...]
        g = _dot_nt(h, wg_ref[...].reshape(FFN_COLS, D_MODEL))
        u = _dot_nt(h, wu_ref[...].reshape(FFN_COLS, D_MODEL))
        a = (g * jax.nn.sigmoid(g) * u).astype(BF16)
        g_ref[...] = g.astype(BF16)
        u_ref[...] = u.astype(BF16)
        a_ref[...] = a
        acc[...] += 0.5 * _dot_nn(a, wd_ref[...].reshape(FFN_COLS, D_MODEL))

        @pl.when(k == FFN_STEPS - 1)
        def _():
            xo_ref[...] = acc[...]

    wspec = lambda blk: pl.BlockSpec((FFN_GROUP, FF_PAD, D_MODEL), lambda i, k: (k, blk, 0))
    act = pl.BlockSpec((t, FFN_COLS), lambda i, k: (i, k))
    row = pl.BlockSpec((t, D_MODEL), lambda i, k: (i, 0))
    ff = N_DEV * FF_PAD
    return pl.pallas_call(
        body, name=f"ffn_fwd_l{layer}_{which}", grid=(seq // t, FFN_STEPS),
        in_specs=[row, _full((1, D_MODEL)), wspec(bg), wspec(bu), wspec(bd)],
        out_specs=[row, row, act, act, act],
        out_shape=[jax.ShapeDtypeStruct((seq, D_MODEL), F32), jax.ShapeDtypeStruct((seq, D_MODEL), BF16)]
        + [jax.ShapeDtypeStruct((seq, ff), BF16)] * 3,
        scratch_shapes=[pltpu.VMEM((t, D_MODEL), BF16), pltpu.VMEM((t, D_MODEL), F32)],
        compiler_params=_cparams(("parallel", "arbitrary")),
    )(x, nw, pack, pack, pack)


def ffn_bwd(dout, x, nw, g, u, pack, layer, which):
    seq = x.shape[0]
    t = _token_tile(seq, 512)
    bg, bu, bd = (_ffn_block(layer, which, p) for p in "gud")

    def body(do_ref, x_ref, nw_ref, g_ref, u_ref, wg_ref, wu_ref, wd_ref, dx_ref, dg_ref, du_ref, dnw_ref, doh, dh):
        i, k = pl.program_id(0), pl.program_id(1)

        @pl.when(k == 0)
        def _():
            doh[...] = (0.5 * do_ref[...]).astype(BF16)
            dh[...] = jnp.zeros_like(dh)

        @pl.when((k == 0) & (i == 0))
        def _():
            dnw_ref[...] = jnp.zeros_like(dnw_ref)

        da = _dot_nt(doh[...], wd_ref[...].reshape(FFN_COLS, D_MODEL))
        gv = g_ref[...].astype(F32)
        uv = u_ref[...].astype(F32)
        s = jax.nn.sigmoid(gv)
        dg = (da * uv * (s * (1.0 + gv * (1.0 - s)))).astype(BF16)
        du = (da * gv * s).astype(BF16)
        dg_ref[...] = dg
        du_ref[...] = du
        dh[...] += _dot_nn(dg, wg_ref[...].reshape(FFN_COLS, D_MODEL)) + _dot_nn(du, wu_ref[...].reshape(FFN_COLS, D_MODEL))

        @pl.when(k == FFN_STEPS - 1)
        def _():
            dx, dw = _rms_bwd(dh[...], x_ref[...], nw_ref[...])
            dx_ref[...] = do_ref[...] + dx
            dnw_ref[...] += dw

    wspec = lambda blk: pl.BlockSpec((FFN_GROUP, FF_PAD, D_MODEL), lambda i, k: (k, blk, 0))
    act = pl.BlockSpec((t, FFN_COLS), lambda i, k: (i, k))
    row = pl.BlockSpec((t, D_MODEL), lambda i, k: (i, 0))
    ff = N_DEV * FF_PAD
    return pl.pallas_call(
        body, name=f"ffn_bwd_l{layer}_{which}", grid=(seq // t, FFN_STEPS),
        in_specs=[row, row, _full((1, D_MODEL)), act, act, wspec(bg), wspec(bu), wspec(bd)],
        out_specs=[row, act, act, _full((1, D_MODEL))],
        out_shape=[jax.ShapeDtypeStruct((seq, D_MODEL), F32), jax.ShapeDtypeStruct((seq, ff), BF16),
                   jax.ShapeDtypeStruct((seq, ff), BF16), jax.ShapeDtypeStruct((1, D_MODEL), F32)],
        scratch_shapes=[pltpu.VMEM((t, D_MODEL), BF16), pltpu.VMEM((t, D_MODEL), F32)],
        compiler_params=_cparams(("arbitrary", "arbitrary")),
    )(dout, x, nw, g, u, pack, pack, pack)


def tn_matmul(a, b, scale, tm, name, gbuf=None, blk_of_m=None):
    seq, m_total = a.shape
    tk = _token_tile(seq, 2048)
    nk = seq // tk

    def body(*refs):
        if gbuf is None:
            a_ref, b_ref, o_ref, acc = refs
        else:
            a_ref, b_ref, _, o_ref, acc = refs
        kk = pl.program_id(1)

        @pl.when(kk == 0)
        def _():
            acc[...] = jnp.zeros_like(acc)

        acc[...] += _dot_tn(a_ref[...].astype(BF16), b_ref[...].astype(BF16))

        @pl.when(kk == nk - 1)
        def _():
            o_ref[...] = (acc[...] * scale).astype(o_ref.dtype)

    in_specs = [pl.BlockSpec((tk, tm), lambda m, k: (k, m)), pl.BlockSpec((tk, D_MODEL), lambda m, k: (k, 0))]
    args = [a, b]
    if gbuf is None:
        out_spec = pl.BlockSpec((tm, D_MODEL), lambda m, k: (m, 0))
        out_shape = jax.ShapeDtypeStruct((m_total, D_MODEL), F32)
        aliases = {}
    else:
        in_specs.append(pl.BlockSpec(memory_space=pl.ANY))
        args.append(gbuf)
        out_spec = pl.BlockSpec((None, tm, D_MODEL), lambda m, k: (m, blk_of_m, 0))
        out_shape = jax.ShapeDtypeStruct(gbuf.shape, gbuf.dtype)
        aliases = {2: 0}
    return pl.pallas_call(
        body, name=name, grid=(m_total // tm, nk), in_specs=in_specs, out_specs=out_spec, out_shape=out_shape,
        scratch_shapes=[pltpu.VMEM((tm, D_MODEL), F32)], input_output_aliases=aliases,
        compiler_params=_cparams(("parallel", "arbitrary")),
    )(*args)


def in_proj_fwd(x, nw, wt, layer):
    seq = x.shape[0]
    t = _token_tile(seq, 256)

    def body(x_ref, nw_ref, w_ref, p_ref, h_ref):
        h = _rms_fwd(x_ref[...], nw_ref[...]).astype(BF16)
        h_ref[...] = h
        p_ref[...] = _dot_nt(h, w_ref[...])

    row = pl.BlockSpec((t, D_MODEL), lambda i: (i, 0))
    return pl.pallas_call(
        body, name=f"in_proj_fwd_l{layer}", grid=(seq // t,),
        in_specs=[row, _full((1, D_MODEL)), _full((NP, D_MODEL))],
        out_specs=[pl.BlockSpec((t, NP), lambda i: (i, 0)), row],
        out_shape=[jax.ShapeDtypeStruct((seq, NP), F32), jax.ShapeDtypeStruct((seq, D_MODEL), BF16)],
        compiler_params=_cparams(("parallel",)),
    )(x, nw, wt)


def in_proj_bwd(dproj, x, nw, dres, wt, layer):
    seq = x.shape[0]
    t = _token_tile(seq, 256)

    def body(dp_ref, x_ref, nw_ref, dr_ref, w_ref, dx_ref, dnw_ref):
        @pl.when(pl.program_id(0) == 0)
        def _():
            dnw_ref[...] = jnp.zeros_like(dnw_ref)

        dh = _dot_nn(dp_ref[...], w_ref[...])
        dx, dw = _rms_bwd(dh, x_ref[...], nw_ref[...])
        dx_ref[...] = dr_ref[...] + dx
        dnw_ref[...] += dw

    row = pl.BlockSpec((t, D_MODEL), lambda i: (i, 0))
    return pl.pallas_call(
        body, name=f"in_proj_bwd_l{layer}", grid=(seq // t,),
        in_specs=[pl.BlockSpec((t, NP), lambda i: (i, 0)), row, _full((1, D_MODEL)), row, _full((NP, D_MODEL))],
        out_specs=[row, _full((1, D_MODEL))],
        out_shape=[jax.ShapeDtypeStruct((seq, D_MODEL), F32), jax.ShapeDtypeStruct((1, D_MODEL), F32)],
        compiler_params=_cparams(("arbitrary",)),
    )(dproj, x, nw, dres, wt)


def out_proj_fwd(x, y, w, layer):
    seq = x.shape[0]
    t = _token_tile(seq, 512)

    def body(x_ref, y_ref, w_ref, o_ref):
        o_ref[...] = x_ref[...] + _dot_nn(y_ref[...], w_ref[...])

    row = pl.BlockSpec((t, D_MODEL), lambda i: (i, 0))
    return pl.pallas_call(
        body, name=f"out_proj_fwd_l{layer}", grid=(seq // t,),
        in_specs=[row, row, _full((D_MODEL, D_MODEL))], out_specs=row,
        out_shape=jax.ShapeDtypeStruct((seq, D_MODEL), F32), compiler_params=_cparams(("parallel",)),
    )(x, y, w)


def out_proj_bwd(dx, w, layer):
    seq = dx.shape[0]
    t = _token_tile(seq, 512)

    def body(d_ref, w_ref, o_ref):
        o_ref[...] = _dot_nt(d_ref[...].astype(BF16), w_ref[...])

    row = pl.BlockSpec((t, D_MODEL), lambda i: (i, 0))
    return pl.pallas_call(
        body, name=f"out_proj_bwd_l{layer}", grid=(seq // t,),
        in_specs=[row, _full((D_MODEL, D_MODEL))], out_specs=row,
        out_shape=jax.ShapeDtypeStruct((seq, D_MODEL), F32), compiler_params=_cparams(("parallel",)),
    )(dx, w)


def loss_head(x, nw, target):
    seq = x.shape[0]
    t = _token_tile(seq, 512)

    def body(x_ref, nw_ref, t_ref, loss_ref, dx_ref, dnw_ref):
        @pl.when(pl.program_id(0) == 0)
        def _():
            loss_ref[...] = jnp.zeros_like(loss_ref)
            dnw_ref[...] = jnp.zeros_like(dnw_ref)

        xv, w = x_ref[...], nw_ref[...]
        err = _rms_fwd(xv, w) - t_ref[...]
        loss_ref[...] += 0.5 * jnp.sum(jnp.mean(err * err, axis=-1, keepdims=True), axis=0, keepdims=True)
        dx, dw = _rms_bwd(err * (1.0 / D_MODEL), xv, w)
        dx_ref[...] = dx
        dnw_ref[...] += dw

    row = pl.BlockSpec((t, D_MODEL), lambda i: (i, 0))
    return pl.pallas_call(
        body, name="loss_head", grid=(seq // t,),
        in_specs=[row, _full((1, D_MODEL)), row], out_specs=[_full((1, 128)), row, _full((1, D_MODEL))],
        out_shape=[jax.ShapeDtypeStruct((1, 128), F32), jax.ShapeDtypeStruct((seq, D_MODEL), F32),
                   jax.ShapeDtypeStruct((1, D_MODEL), F32)],
        compiler_params=_cparams(("arbitrary",)),
    )(x, nw, target)


_MIXER_PARAM_SHAPES = ((DEPTH, HG_W), (1, HG_DK), (CONV_W, 3 * GD_W), (1, 128), (1, 128), (1, GD_DK),
                       (CONV_W, RG_W), (1, RG_W), (RG_W, RG_W), (1, RG_W), (RG_W, RG_W), (1, RG_W), (1, RG_W))
_STATE_SHAPES = ((HG_W, HG_W), (GD_HEADS * GD_DK, GD_DK), (1, RG_W))


def mixer_fwd(layer, proj, params):
    seq = proj.shape[0]
    cps = min(MIXER_CHUNKS_PER_STEP, seq // CHUNK)
    rows = cps * CHUNK
    n = seq // rows

    def body(*refs):
        p_ref, halo_ref = refs[0], refs[1]
        prm = refs[2:15]
        y_ref, st_out, s_out, h_out = refs[15:19]
        st, s, h = refs[19:22]
        i = pl.program_id(0)

        @pl.when(i == 0)
        def _():
            st[...] = jnp.zeros_like(st)
            s[...] = jnp.zeros_like(s)
            h[...] = jnp.zeros_like(h)

        params_v = tuple(r[...] for r in prm)
        st_v, s_v, h_v = st[...], s[...], h[...]
        for c in range(cps):
            st_out[c], s_out[c], h_out[c] = st_v, s_v, h_v
            if c == 0:
                halo = jnp.where(i > 0, halo_ref[...], 0.0)
            else:
                halo = p_ref[pl.ds(c * CHUNK - 8, 8), :]
            y, st_v, s_v, h_v = mixer_chunk(layer, p_ref[pl.ds(c * CHUNK, CHUNK), :], halo, st_v, s_v, h_v, params_v)
            y_ref[pl.ds(c * CHUNK, CHUNK), :] = y.astype(BF16)
        st[...], s[...], h[...] = st_v, s_v, h_v

    in_specs = [pl.BlockSpec((rows, NP), lambda i: (i, 0)),
                pl.BlockSpec((8, NP), lambda i: (jnp.maximum(i * (rows // 8) - 1, 0), 0))]
    in_specs += [_full(sh) for sh in _MIXER_PARAM_SHAPES]
    out_specs = [pl.BlockSpec((rows, D_MODEL), lambda i: (i, 0))]
    out_specs += [pl.BlockSpec((cps,) + sh, lambda i: (i, 0, 0)) for sh in _STATE_SHAPES]
    out_shape = [jax.ShapeDtypeStruct((seq, D_MODEL), BF16)]
    out_shape += [jax.ShapeDtypeStruct((seq // CHUNK,) + sh, F32) for sh in _STATE_SHAPES]
    return pl.pallas_call(
        body, name=f"mixer_fwd_l{layer}", grid=(n,), in_specs=in_specs, out_specs=out_specs, out_shape=out_shape,
        scratch_shapes=[pltpu.VMEM(sh, F32) for sh in _STATE_SHAPES],
        compiler_params=_cparams(("arbitrary",)),
    )(proj, proj, *params)


def mixer_bwd(layer, proj, dy, states, params):
    seq = proj.shape[0]
    cps = min(MIXER_CHUNKS_PER_STEP, seq // CHUNK)
    rows = cps * CHUNK
    n = seq // rows

    def body(*refs):
        p_ref, halo_ref, dy_ref, st_ref, s_ref, h_ref = refs[0:6]
        prm = refs[6:19]
        dp_ref = refs[19]
        dprm = refs[20:33]
        dst, ds, dh, dhalo = refs[33:37]
        i = pl.program_id(0)
        r = n - 1 - i

        @pl.when(i == 0)
        def _():
            for ref in (dst, ds, dh, dhalo) + tuple(dprm):
                ref[...] = jnp.zeros_like(ref)

        params_v = tuple(q[...] for q in prm)
        fn = functools.partial(mixer_chunk, layer)
        dst_v, ds_v, dh_v, dhalo_v = dst[...], ds[...], dh[...], dhalo[...]
        dparams_acc = None
        for c in reversed(range(cps)):
            if c == 0:
                halo = jnp.where(r > 0, halo_ref[...], 0.0)
            else:
                halo = p_ref[pl.ds(c * CHUNK - 8, 8), :]
            _, vjp = jax.vjp(fn, p_ref[pl.ds(c * CHUNK, CHUNK), :], halo, st_ref[c], s_ref[c], h_ref[c], params_v)
            dproj, dhalo_n, dst_v, ds_v, dh_v, dparams = vjp((dy_ref[pl.ds(c * CHUNK, CHUNK), :], dst_v, ds_v, dh_v))
            carry = jnp.concatenate([jnp.zeros((CHUNK - 8, NP), F32), dhalo_v], axis=0)
            dp_ref[pl.ds(c * CHUNK, CHUNK), :] = (dproj + carry).astype(BF16)
            dhalo_v = dhalo_n
            dparams_acc = dparams if dparams_acc is None else tuple(a + b for a, b in zip(dparams_acc, dparams))
        dst[...], ds[...], dh[...], dhalo[...] = dst_v, ds_v, dh_v, dhalo_v
        for ref, val in zip(dprm, dparams_acc):
            ref[...] += val

    rev = lambda i: n - 1 - i
    in_specs = [pl.BlockSpec((rows, NP), lambda i: (rev(i), 0)),
                pl.BlockSpec((8, NP), lambda i: (jnp.maximum(rev(i) * (rows // 8) - 1, 0), 0)),
                pl.BlockSpec((rows, D_MODEL), lambda i: (rev(i), 0))]
    in_specs += [pl.BlockSpec((cps,) + sh, lambda i: (rev(i), 0, 0)) for sh in _STATE_SHAPES]
    in_specs += [_full(sh) for sh in _MIXER_PARAM_SHAPES]
    out_specs = [pl.BlockSpec((rows, NP), lambda i: (rev(i), 0))] + [_full(sh) for sh in _MIXER_PARAM_SHAPES]
    out_shape = [jax.ShapeDtypeStruct((seq, NP), BF16)] + [jax.ShapeDtypeStruct(sh, F32) for sh in _MIXER_PARAM_SHAPES]
    return pl.pallas_call(
        body, name=f"mixer_bwd_l{layer}", grid=(n,), in_specs=in_specs, out_specs=out_specs, out_shape=out_shape,
        scratch_shapes=[pltpu.VMEM(sh, F32) for sh in _STATE_SHAPES] + [pltpu.VMEM((8, NP), F32)],
        compiler_params=_cparams(("arbitrary",)),
    )(proj, proj, dy, *states, *params)


def _win_segments():
    out = []
    for k in range(N_DEV):
        a, b = NIN_SHARD * k, NIN_SHARD * (k + 1)
        if a < GBA_SPLIT < b:
            out.append((k, 0, GBA_SPLIT - a, a))
            out.append((k, GBA_SPLIT - a, b - GBA_SPLIT, O_RX))
        elif b <= GBA_SPLIT:
            out.append((k, 0, NIN_SHARD, a))
        else:
            out.append((k, 0, NIN_SHARD, a + O_RX - GBA_SPLIT))
    return out


def win_to_padded(wt_shards, layer):
    lanes = 256

    def body(src, dst, scr):
        scr[...] = jnp.zeros_like(scr)
        for (k, s0, rows, d0) in _win_segments():
            scr[pl.ds(d0, rows), :] = src[k, pl.ds(s0, rows), :].astype(F32)
        dst[...] = scr[...].astype(BF16)

    return pl.pallas_call(
        body, name=f"win_to_padded_l{layer}", grid=(D_MODEL // lanes,),
        in_specs=[pl.BlockSpec((N_DEV, NIN_SHARD_PAD, lanes), lambda j: (0, 0, j))],
        out_specs=pl.BlockSpec((NP, lanes), lambda j: (0, j)),
        out_shape=jax.ShapeDtypeStruct((NP, D_MODEL), BF16),
        scratch_shapes=[pltpu.VMEM((NP, lanes), F32)], compiler_params=_cparams(("parallel",)),
    )(wt_shards)


def win_grad_to_pack(dwt, gbuf, layer):
    lanes = 256
    blk = (OFF_WIN + layer * NIN_SHARD_PAD) // NIN_SHARD_PAD

    def body(src, _, dst, scr):
        scr[...] = jnp.zeros_like(scr)
        for (k, s0, rows, d0) in _win_segments():
            scr[k, pl.ds(s0, rows), :] = src[pl.ds(d0, rows), :]
        dst[...] = scr[...].astype(dst.dtype)

    return pl.pallas_call(
        body, name=f"win_grad_to_pack_l{layer}", grid=(D_MODEL // lanes,),
        in_specs=[pl.BlockSpec((NP, lanes), lambda j: (0, j)), pl.BlockSpec(memory_space=pl.ANY)],
        out_specs=pl.BlockSpec((N_DEV, NIN_SHARD_PAD, lanes), lambda j: (0, blk, j)),
        out_shape=jax.ShapeDtypeStruct(gbuf.shape, gbuf.dtype), input_output_aliases={1: 0},
        scratch_shapes=[pltpu.VMEM((N_DEV, NIN_SHARD_PAD, lanes), F32)],
        compiler_params=_cparams(("parallel",)),
    )(dwt, gbuf)


def sum_adamw(parts, w, m, v, name, rows_per_step):
    n, rows, cols = parts.shape
    tr = min(rows_per_step, rows)
    c1 = 1.0 - ADAM_B1 ** ADAM_STEP
    c2 = 1.0 - ADAM_B2 ** ADAM_STEP

    def body(p_ref, w_ref, m_ref, v_ref, g_ref, d_ref, mo_ref, vo_ref):
        g = p_ref[0].astype(F32)
        for j in range(1, n):
            g = g + p_ref[j].astype(F32)
        mn = ADAM_B1 * m_ref[...] + (1.0 - ADAM_B1) * g
        vn = ADAM_B2 * v_ref[...] + (1.0 - ADAM_B2) * (g * g)
        g_ref[...] = g
        mo_ref[...] = mn
        vo_ref[...] = vn
        d_ref[...] = -ADAM_LR * ((mn / c1) / (jnp.sqrt(vn / c2) + ADAM_EPS) + ADAM_WD * w_ref[...])

    blk = pl.BlockSpec((tr, cols), lambda i: (i, 0))
    return pl.pallas_call(
        body, name=name, grid=(rows // tr,),
        in_specs=[pl.BlockSpec((n, tr, cols), lambda i: (0, i, 0)), blk, blk, blk], out_specs=[blk] * 4,
        out_shape=[jax.ShapeDtypeStruct((rows, cols), F32)] * 4, compiler_params=_cparams(("parallel",)),
    )(parts, w, m, v)


def exchange(arrays, scatter, name):
    n = len(arrays)

    def body(*refs):
        ins, outs = refs[:n], refs[n:2 * n]
        send_sems, recv_sems, local_sems = refs[2 * n:]
        x, y, c = lax.axis_index("x"), lax.axis_index("y"), lax.axis_index("c")
        me = 4 * x + 2 * y + c
        local = []
        for a in range(n):
            cp = pltpu.make_async_copy(ins[a].at[me] if scatter else ins[a], outs[a].at[me], local_sems.at[a])
            cp.start()
            local.append(cp)
        pending = []
        for j in range(1, N_DEV):
            px = 1 - x if j & 4 else x
            py = 1 - y if j & 2 else y
            pc = 1 - c if j & 1 else c
            peer = 4 * px + 2 * py + pc
            for a in range(n):
                src = ins[a].at[peer] if scatter else ins[a]
                k = a * (N_DEV - 1) + j - 1

                def copy(dst_slot):
                    return pltpu.make_async_remote_copy(
                        src_ref=src, dst_ref=outs[a].at[dst_slot], send_sem=send_sems.at[k], recv_sem=recv_sems.at[k],
                        device_id=(px, py, pc), device_id_type=pl.DeviceIdType.MESH)

                send = copy(me)
                send.start()
                pending.append((send, copy(peer)))
        for send, recv in pending:
            send.wait_send()
            recv.wait_recv()
        for cp in local:
            cp.wait()

    out_shape = [jax.ShapeDtypeStruct(a.shape if scatter else (N_DEV,) + a.shape, a.dtype) for a in arrays]
    any_spec = pl.BlockSpec(memory_space=pl.ANY)
    return pl.pallas_call(
        body, name=name, in_specs=[any_spec] * n, out_specs=[any_spec] * n, out_shape=out_shape,
        scratch_shapes=[pltpu.SemaphoreType.DMA((n * (N_DEV - 1),)), pltpu.SemaphoreType.DMA((n * (N_DEV - 1),)),
                        pltpu.SemaphoreType.DMA((n,))],
        compiler_params=pltpu.CompilerParams(has_side_effects=True),
    )(*arrays)


BIG = ("ffn1_gate", "ffn1_up", "ffn1_down", "w_in", "w_out", "ffn2_gate", "ffn2_up", "ffn2_down")
SMALL_REPLICATED = (("norm_ffn1", (DEPTH, D_MODEL)), ("norm_mix", (DEPTH, D_MODEL)), ("norm_ffn2", (DEPTH, D_MODEL)),
                    ("norm_final", (D_MODEL,)), ("hg_lb", (DEPTH, HG_W)), ("hg_norm_w", (DEPTH, HG_DK)),
                    ("gd_a_log", (DEPTH, GD_HEADS)), ("gd_dt_bias", (DEPTH, GD_HEADS)), ("gd_norm_w", (DEPTH, GD_DK)),
                    ("rg_conv_b", (DEPTH, RG_W)), ("rg_wr", (DEPTH, RG_BLOCKS, RG_BD, RG_BD)), ("rg_br", (DEPTH, RG_W)),
                    ("rg_wi", (DEPTH, RG_BLOCKS, RG_BD, RG_BD)), ("rg_bi", (DEPTH, RG_W)), ("rg_lambda", (DEPTH, RG_W)))
SMALL_EXTRA = (("gd_conv_w", (DEPTH, CONV_W, 3 * GD_W)), ("rg_conv_w", (DEPTH, CONV_W, RG_W)), ("loss", (1,)))
SMALL_SHARDED = (("gd_conv_w", (DEPTH, CONV_W, GDC_SHARD)), ("rg_conv_w", (DEPTH, CONV_W, RGC_SHARD)))


def _pad_rows(a, rows):
    return jnp.pad(a, ((0, rows - a.shape[0]), (0, 0)))


def pack_big(get):
    pieces = []
    for l in range(DEPTH):
        for which in (1, 2):
            pieces.append(_pad_rows(get(f"ffn{which}_gate")[l].T, FF_PAD))
            pieces.append(_pad_rows(get(f"ffn{which}_up")[l].T, FF_PAD))
            pieces.append(_pad_rows(get(f"ffn{which}_down")[l], FF_PAD))
    for l in range(DEPTH):
        pieces.append(_pad_rows(get("w_in")[l].T, NIN_SHARD_PAD))
    for l in range(DEPTH):
        pieces.append(get("w_out")[l])
    return jnp.concatenate(pieces, axis=0)


def unpack_big(p):
    out = {}
    for which in (1, 2):
        for part, nm in (("g", "gate"), ("u", "up"), ("d", "down")):
            per_layer = []
            for l in range(DEPTH):
                r0 = _ffn_block(l, which, part) * FF_PAD
                blk = p[r0:r0 + FF_SHARD, :]
                per_layer.append(blk if part == "d" else blk.T)
            out[f"ffn{which}_{nm}"] = jnp.stack(per_layer)
    out["w_in"] = jnp.stack([p[OFF_WIN + NIN_SHARD_PAD * l:OFF_WIN + NIN_SHARD_PAD * l + NIN_SHARD, :].T for l in range(DEPTH)])
    out["w_out"] = jnp.stack([p[OFF_WOUT + WOUT_SHARD * l:OFF_WOUT + WOUT_SHARD * (l + 1), :] for l in range(DEPTH)])
    return out


def pack_small(get, spec):
    flat = []
    for name, shape in spec:
        a = get(name)
        flat.append(jnp.zeros((math.prod(shape),), F32) if a is None else a.reshape(-1).astype(F32))
    v = jnp.concatenate(flat)
    total = -(-v.shape[0] // 1024) * 1024
    return jnp.pad(v, (0, total - v.shape[0])).reshape(total // 128, 128)


def unpack_small(p, spec):
    v = p.reshape(-1)
    out, off = {}, 0
    for name, shape in spec:
        size = math.prod(shape)
        out[name] = v[off:off + size].reshape(shape)
        off += size
    return out


def _block_diag(w):
    rows = []
    for i in range(RG_BLOCKS):
        rows.append(jnp.concatenate([w[i] if j == i else jnp.zeros((RG_BD, RG_BD), F32) for j in range(RG_BLOCKS)], axis=1))
    return jnp.concatenate(rows, axis=0)


def _diag_blocks(w):
    return jnp.stack([w[RG_BD * i:RG_BD * (i + 1), RG_BD * i:RG_BD * (i + 1)] for i in range(RG_BLOCKS)])


def _lane_vec(v):
    return jnp.pad(v.astype(F32), (GD_HEADS, 128 - 2 * GD_HEADS))[None]


INPUT_NAMES = ("x", "norm_ffn1", "ffn1_gate", "ffn1_up", "ffn1_down", "norm_mix", "w_in", "hg_lb", "hg_norm_w",
               "gd_conv_w", "gd_a_log", "gd_dt_bias", "gd_norm_w", "rg_conv_w", "rg_conv_b", "rg_wr", "rg_br", "rg_wi",
               "rg_bi", "rg_lambda", "w_out", "norm_ffn2", "ffn2_gate", "ffn2_up", "ffn2_down", "norm_final")
WEIGHT_NAMES = INPUT_NAMES[1:]


def _step(a):
    x = a["x"][0]
    target = a["loss_target"][0]
    me = 4 * lax.axis_index("x") + 2 * lax.axis_index("y") + lax.axis_index("c")

    w_pack = pack_big(lambda nm: a[nm])
    conv_local = pack_small(lambda nm: a[nm], SMALL_SHARDED)
    gw, gconv = exchange([w_pack.astype(BF16), conv_local], False, "gather_weights")
    conv_parts = [unpack_small(gconv[d], SMALL_SHARDED) for d in range(N_DEV)]
    gd_conv_w = jnp.concatenate([p["gd_conv_w"] for p in conv_parts], axis=-1)
    rg_conv_w = jnp.concatenate([p["rg_conv_w"] for p in conv_parts], axis=-1)

    def mixer_params(l):
        return (a["hg_lb"], a["hg_norm_w"][l][None], gd_conv_w[l], _lane_vec(a["gd_a_log"][l]), _lane_vec(a["gd_dt_bias"][l]),
                a["gd_norm_w"][l][None], rg_conv_w[l], a["rg_conv_b"][l][None], _block_diag(a["rg_wr"][l]), a["rg_br"][l][None],
                _block_diag(a["rg_wi"][l]), a["rg_bi"][l][None], a["rg_lambda"][l][None])

    saved = []
    xs = x
    for l in range(DEPTH):
        x1, h1, g1, u1, a1 = ffn_fwd(xs, a["norm_ffn1"][l][None], gw, l, 1)
        wt = win_to_padded(gw[:, OFF_WIN + NIN_SHARD_PAD * l:OFF_WIN + NIN_SHARD_PAD * (l + 1), :], l)
        proj, h2 = in_proj_fwd(x1, a["norm_mix"][l][None], wt, l)
        prm = mixer_params(l)
        y, st, s, h = mixer_fwd(l, proj, prm)
        wo = gw[:, OFF_WOUT + WOUT_SHARD * l:OFF_WOUT + WOUT_SHARD * (l + 1), :].reshape(D_MODEL, D_MODEL)
        x2 = out_proj_fwd(x1, y, wo, l)
        x3, h3, g3, u3, a3 = ffn_fwd(x2, a["norm_ffn2"][l][None], gw, l, 2)
        saved.append(dict(x0=xs, x1=x1, x2=x2, h1=h1, g1=g1, u1=u1, a1=a1, wt=wt, proj=proj, h2=h2, prm=prm, y=y,
                          states=(st, s, h), wo=wo, h3=h3, g3=g3, u3=u3, a3=a3))
        xs = x3
    loss_row, dx, d_norm_final = loss_head(xs, a["norm_final"][None], target)

    gbuf = jnp.zeros((N_DEV, PACK_ROWS, D_MODEL), BF16)
    sg = {"norm_final": d_norm_final[0], "loss": loss_row[0, 0:1]}
    per_layer = {nm: [None] * DEPTH for nm in ("norm_ffn1", "norm_mix", "norm_ffn2", "hg_norm_w", "gd_conv_w", "gd_a_log",
                                                "gd_dt_bias", "gd_norm_w", "rg_conv_w", "rg_conv_b", "rg_wr", "rg_br", "rg_wi",
                                                "rg_bi", "rg_lambda")}
    d_hg_lb = jnp.zeros((DEPTH, HG_W), F32)
    for l in reversed(range(DEPTH)):
        sv = saved[l]
        dx2, dg, du, dn2 = ffn_bwd(dx, sv["x2"], a["norm_ffn2"][l][None], sv["g3"], sv["u3"], gw, l, 2)
        gbuf = tn_matmul(dg, sv["h3"], 1.0, FF_PAD, f"dw_gate_l{l}_2", gbuf, _ffn_block(l, 2, "g"))
        gbuf = tn_matmul(du, sv["h3"], 1.0, FF_PAD, f"dw_up_l{l}_2", gbuf, _ffn_block(l, 2, "u"))
        gbuf = tn_matmul(sv["a3"], dx, 0.5, FF_PAD, f"dw_down_l{l}_2", gbuf, _ffn_block(l, 2, "d"))
        dy = out_proj_bwd(dx2, sv["wo"], l)
        gbuf = tn_matmul(sv["y"], dx2, 1.0, WOUT_SHARD, f"dw_out_l{l}", gbuf, (OFF_WOUT + WOUT_SHARD * l) // WOUT_SHARD)
        mb = mixer_bwd(l, sv["proj"], dy, sv["states"], sv["prm"])
        dproj, dprm = mb[0], mb[1:]
        dwt = tn_matmul(dproj, sv["h2"], 1.0, FF_PAD, f"dw_in_l{l}")
        gbuf = win_grad_to_pack(dwt, gbuf, l)
        dx1, dnm = in_proj_bwd(dproj, sv["x1"], a["norm_mix"][l][None], dx2, sv["wt"], l)
        dx, dg, du, dn1 = ffn_bwd(dx1, sv["x0"], a["norm_ffn1"][l][None], sv["g1"], sv["u1"], gw, l, 1)
        gbuf = tn_matmul(dg, sv["h1"], 1.0, FF_PAD, f"dw_gate_l{l}_1", gbuf, _ffn_block(l, 1, "g"))
        gbuf = tn_matmul(du, sv["h1"], 1.0, FF_PAD, f"dw_up_l{l}_1", gbuf, _ffn_block(l, 1, "u"))
        gbuf = tn_matmul(sv["a1"], dx1, 0.5, FF_PAD, f"dw_down_l{l}_1", gbuf, _ffn_block(l, 1, "d"))
        (g_lb, g_hnw, g_gcw, g_alog, g_dt, g_gnw, g_rcw, g_rcb, g_wr, g_br, g_wi, g_bi, g_lam) = dprm
        d_hg_lb = d_hg_lb + g_lb
        for nm, val in (("norm_ffn1", dn1[0]), ("norm_mix", dnm[0]), ("norm_ffn2", dn2[0]), ("hg_norm_w", g_hnw[0]),
                        ("gd_conv_w", g_gcw), ("gd_a_log", g_alog[0, GD_HEADS:2 * GD_HEADS]),
                        ("gd_dt_bias", g_dt[0, GD_HEADS:2 * GD_HEADS]), ("gd_norm_w", g_gnw[0]), ("rg_conv_w", g_rcw),
                        ("rg_conv_b", g_rcb[0]), ("rg_wr", _diag_blocks(g_wr)), ("rg_br", g_br[0]),
                        ("rg_wi", _diag_blocks(g_wi)), ("rg_bi", g_bi[0]), ("rg_lambda", g_lam[0])):
            per_layer[nm][l] = val
    grad_x = dx
    for nm, vals in per_layer.items():
        sg[nm] = jnp.stack(vals)
    sg["hg_lb"] = d_hg_lb

    small_spec = SMALL_REPLICATED + SMALL_EXTRA
    (parts_big,) = exchange([gbuf], True, "scatter_grads")
    g_big, d_big, m_big, v_big = sum_adamw(parts_big, w_pack, pack_big(lambda nm: a["m_" + nm]),
                                           pack_big(lambda nm: a["v_" + nm]), "adamw_big", 128)
    (parts_small,) = exchange([pack_small(lambda nm: sg[nm], small_spec)], False, "gather_small_grads")
    repl = dict(SMALL_REPLICATED)
    small_in = lambda pre: pack_small(lambda nm: a[pre + nm] if nm in repl else None, small_spec)
    rows_small = parts_small.shape[1]
    g_sm, d_sm, m_sm, v_sm = sum_adamw(parts_small, small_in(""), small_in("m_"), small_in("v_"), "adamw_small", rows_small)
    g_small = unpack_small(g_sm, small_spec)
    conv_grads = {"gd_conv_w": lax.dynamic_slice_in_dim(g_small["gd_conv_w"], me * GDC_SHARD, GDC_SHARD, axis=2),
                  "rg_conv_w": lax.dynamic_slice_in_dim(g_small["rg_conv_w"], me * RGC_SHARD, RGC_SHARD, axis=2)}
    conv_in = lambda pre: pack_small(lambda nm: a[pre + nm], SMALL_SHARDED)
    g_cv, d_cv, m_cv, v_cv = sum_adamw(pack_small(lambda nm: conv_grads[nm], SMALL_SHARDED)[None], conv_in(""),
                                       conv_in("m_"), conv_in("v_"), "adamw_conv", conv_local.shape[0])

    results = []
    for big, small, conv in ((g_big, g_sm, g_cv), (d_big, d_sm, d_cv), (m_big, m_sm, m_cv), (v_big, v_sm, v_cv)):
        vals = unpack_big(big)
        vals.update({k: v for k, v in unpack_small(small, small_spec).items() if k in repl})
        vals.update(unpack_small(conv, SMALL_SHARDED))
        results.append(vals)
    loss = g_small["loss"][0]
    out = [loss, grad_x[None]]
    for vals in results:
        out.extend(vals[nm] for nm in WEIGHT_NAMES)
    return tuple(out)


def kernel(x, norm_ffn1, ffn1_gate, ffn1_up, ffn1_down, norm_mix, w_in, hg_lb, hg_norm_w, gd_conv_w, gd_a_log, gd_dt_bias, gd_norm_w, rg_conv_w, rg_conv_b, rg_wr, rg_br, rg_wi, rg_bi, rg_lambda, w_out, norm_ffn2, ffn2_gate, ffn2_up, ffn2_down, norm_final, loss_target, m_norm_ffn1, m_ffn1_gate, m_ffn1_up, m_ffn1_down, m_norm_mix, m_w_in, m_hg_lb, m_hg_norm_w, m_gd_conv_w, m_gd_a_log, m_gd_dt_bias, m_gd_norm_w, m_rg_conv_w, m_rg_conv_b, m_rg_wr, m_rg_br, m_rg_wi, m_rg_bi, m_rg_lambda, m_w_out, m_norm_ffn2, m_ffn2_gate, m_ffn2_up, m_ffn2_down, m_norm_final, v_norm_ffn1, v_ffn1_gate, v_ffn1_up, v_ffn1_down, v_norm_mix, v_w_in, v_hg_lb, v_hg_norm_w, v_gd_conv_w, v_gd_a_log, v_gd_dt_bias, v_gd_norm_w, v_rg_conv_w, v_rg_conv_b, v_rg_wr, v_rg_br, v_rg_wi, v_rg_bi, v_rg_lambda, v_w_out, v_norm_ffn2, v_ffn2_gate, v_ffn2_up, v_ffn2_down, v_norm_final):
    args = locals()
    return _step(dict(args))
```

```python
import functools
import math

import jax
import jax.numpy as jnp
from jax import lax
from jax.experimental import pallas as pl
from jax.experimental.pallas import tpu as pltpu

F32 = jnp.float32
BF16 = jnp.bfloat16

D_MODEL = 1024
DEPTH = 2
D_FF = 2816
HG_HEADS, HG_DK = 4, 64
HG_W = 256
GD_HEADS, GD_DK = 4, 128
GD_W = 512
RG_W = 256
RG_BLOCKS, RG_BD = 4, 64
RG_C = 8.0
CONV_W = 4
CHUNK = 64
EPS = 1e-6
N_IN = 3592
ADAM_LR, ADAM_B1, ADAM_B2, ADAM_EPS, ADAM_WD, ADAM_STEP = 0.001, 0.9, 0.999, 1e-08, 0.01, 10

N_DEV = 8
FF_SHARD = D_FF // N_DEV
FF_PAD = 384
NIN_SHARD = N_IN // N_DEV
NIN_SHARD_PAD = 512
WOUT_SHARD = D_MODEL // N_DEV
GDC_SHARD = (2 * 512 + 512) // N_DEV
RGC_SHARD = RG_W // N_DEV

O_HQ, O_HF, O_HI, O_HG = 0, 256, 512, 768
O_GQ, O_GK, O_GV, O_GZ = 1024, 1536, 2048, 2560
O_GBA = 3072
O_RX, O_RGATE = 3200, 3456
NP = 3840
GBA_SPLIT = 3080

N_FFN_BLOCKS = DEPTH * 2 * 3
OFF_WIN = N_FFN_BLOCKS * FF_PAD
OFF_WOUT = OFF_WIN + DEPTH * NIN_SHARD_PAD
PACK_ROWS = OFF_WOUT + DEPTH * WOUT_SHARD

VMEM_LIMIT = 56 * 1024 * 1024
FFN_GROUP = 2
FFN_STEPS = N_DEV // FFN_GROUP
FFN_COLS = FFN_GROUP * FF_PAD
MIXER_CHUNKS_FWD = 4
MIXER_CHUNKS_BWD = 2


def _ffn_block(layer, which, part):
    return layer * 6 + (which - 1) * 3 + {"g": 0, "u": 1, "d": 2}[part]


def _cparams(sem, **kw):
    return pltpu.CompilerParams(dimension_semantics=sem, vmem_limit_bytes=VMEM_LIMIT, **kw)


def _dot_nn(a, b):
    return lax.dot_general(a, b, (((1,), (0,)), ((), ())), preferred_element_type=F32)


def _dot_nt(a, b):
    return lax.dot_general(a, b, (((1,), (1,)), ((), ())), preferred_element_type=F32)


def _dot_tn(a, b):
    return lax.dot_general(a, b, (((0,), (0,)), ((), ())), preferred_element_type=F32)


def _split3(x):
    hi = x.astype(BF16)
    r = x - hi.astype(F32)
    mid = r.astype(BF16)
    lo = (r - mid.astype(F32)).astype(BF16)
    return hi, mid, lo


def _split2(x):
    hi = x.astype(BF16)
    return hi, (x - hi.astype(F32)).astype(BF16)


def _iota(shape, dim):
    return lax.broadcasted_iota(jnp.int32, shape, dim)


def _sigmoid(x):
    return jax.nn.sigmoid(x)


def _silu(x):
    return x * _sigmoid(x)


def _softplus(x):
    return jnp.maximum(x, 0.0) + jnp.log(1.0 + jnp.exp(-jnp.abs(x)))


def _gelu_tanh(x):
    return 0.5 * x * (1.0 + jnp.tanh(math.sqrt(2.0 / math.pi) * (x + 0.044715 * x * x * x)))


def _rms_fwd(x, w):
    rstd = lax.rsqrt(jnp.mean(x * x, axis=-1, keepdims=True) + EPS)
    return x * rstd * w


def _rms_bwd(dh, x, w):
    rstd = lax.rsqrt(jnp.mean(x * x, axis=-1, keepdims=True) + EPS)
    xhat = x * rstd
    dxhat = dh * w
    dx = rstd * (dxhat - xhat * jnp.mean(dxhat * xhat, axis=-1, keepdims=True))
    return dx, jnp.sum(dh * xhat, axis=0, keepdims=True)


def _rank_dims(kind, ndim):
    if ndim == 2:
        return {"nn": (((1,), (0,)), ((), ())), "nt": (((1,), (1,)), ((), ())), "tn": (((0,), (0,)), ((), ()))}[kind]
    return {"nn": (((2,), (1,)), ((0,), (0,))), "nt": (((2,), (2,)), ((0,), (0,))), "tn": (((1,), (1,)), ((0,), (0,)))}[kind]


def _rdot(kind, a, b):
    return lax.dot_general(a, b, _rank_dims(kind, a.ndim), preferred_element_type=F32)


def _make_mm(kind, kind_da, kind_db, swap_da, swap_db):
    @jax.custom_vjp
    def mm(a, b):
        return _rdot(kind, a.astype(BF16), b.astype(BF16))

    def fwd(a, b):
        a, b = a.astype(BF16), b.astype(BF16)
        return _rdot(kind, a, b), (a, b)

    def bwd(res, g):
        a, b = res
        g = g.astype(BF16)
        da = _rdot(kind_da, b, g) if swap_da else _rdot(kind_da, g, b)
        db = _rdot(kind_db, g, a) if swap_db else _rdot(kind_db, a, g)
        return da, db

    mm.defvjp(fwd, bwd)
    return mm


mm_nn = _make_mm("nn", "nt", "tn", False, False)
mm_nt = _make_mm("nt", "nn", "tn", False, True)
mm_tn = _make_mm("tn", "nt", "nn", True, False)


def _bcast_const(mat, like):
    return mat if like.ndim == 2 else jnp.broadcast_to(mat, (like.shape[0],) + mat.shape)


def _sel_apply(kind, sel, x):
    hi, mid, lo = _split3(x)
    s = _bcast_const(sel, x)
    return _rdot(kind, s, hi) + _rdot(kind, s, mid) + _rdot(kind, s, lo)


@jax.custom_vjp
def sel_mm(sel, x):
    return _sel_apply("nn", sel, x)


def _sel_mm_fwd(sel, x):
    return _sel_apply("nn", sel, x), sel


def _sel_mm_bwd(sel, g):
    return jnp.zeros_like(sel), _sel_apply("tn", sel, g)


sel_mm.defvjp(_sel_mm_fwd, _sel_mm_bwd)


def _transpose_apply(a):
    n = a.shape[-1]
    eye = (_iota((n, n), 0) == _iota((n, n), 1)).astype(BF16)
    hi, mid, lo = _split3(a)
    e = _bcast_const(eye, a)
    return _rdot("nt", e, hi) + _rdot("nt", e, mid) + _rdot("nt", e, lo)


@jax.custom_vjp
def exact_transpose(a):
    return _transpose_apply(a)


exact_transpose.defvjp(lambda a: (_transpose_apply(a), None), lambda _, g: (_transpose_apply(g),))


def _dot3(kind, a, b):
    ah, al = _split2(a)
    bh, bl = _split2(b)
    return _rdot(kind, ah, bh) + _rdot(kind, ah, bl) + _rdot(kind, al, bh)


def _inverse_levels(lower, n):
    nd = lower.ndim
    r, c = _iota(lower.shape, nd - 2), _iota(lower.shape, nd - 1)
    tinv = (r == c).astype(F32)
    b = 1
    while b < n:
        off = (r // (2 * b) == c // (2 * b)) & (r % (2 * b) >= b) & (c % (2 * b) < b)
        tinv = tinv - _dot3("nn", tinv, _dot3("nn", jnp.where(off, lower, 0.0), tinv))
        b *= 2
    return tinv


@functools.partial(jax.custom_vjp, nondiff_argnums=(1,))
def unit_lower_inverse(lower, n):
    return _inverse_levels(lower, n)


def _unit_lower_inverse_fwd(lower, n):
    tinv = _inverse_levels(lower, n)
    return tinv, tinv


def _unit_lower_inverse_bwd(n, tinv, g):
    return (-_dot3("nt", _dot3("tn", tinv, g), tinv),)


unit_lower_inverse.defvjp(_unit_lower_inverse_fwd, _unit_lower_inverse_bwd)


def mm3_nn(a, b):
    ah = a.astype(BF16).astype(F32)
    al = a - ah
    bh = b.astype(BF16).astype(F32)
    bl = b - bh
    return mm_nn(ah, bh) + mm_nn(ah, bl) + mm_nn(al, bh)


def _seg_apply(x, b):
    hi, mid, lo = _split3(x)
    return _rdot("nn", hi, b) + _rdot("nn", mid, b) + _rdot("nn", lo, b)


@jax.custom_vjp
def _seg_sum(x, b):
    return _seg_apply(x, b)


_seg_sum.defvjp(lambda x, b: (_seg_apply(x, b), b), lambda b, g: (_seg_apply(g, b), jnp.zeros_like(b)))


def _shift_rows(x, halo, k):
    n = x.shape[0]
    ext = jnp.concatenate([jnp.zeros((n - halo.shape[0], x.shape[1]), x.dtype), halo], axis=0)
    t = _iota(x.shape, 0)
    return jnp.where(t >= k, jnp.roll(x, k, axis=0), jnp.roll(ext, k, axis=0))


def _causal_conv(x, halo, w):
    y = x * w[3:4, :]
    for k in range(1, CONV_W):
        y = y + _shift_rows(x, halo, k) * w[3 - k:4 - k, :]
    return y


def _lb_of_layer(hg_lb, layer):
    if layer == 0:
        return jnp.zeros((1, HG_W), F32)
    e = jnp.exp(hg_lb - jnp.max(hg_lb, axis=0, keepdims=True))
    sm = e / jnp.sum(e, axis=0, keepdims=True)
    lb = sm[1:2, :]
    for l in range(2, layer + 1):
        lb = lb + sm[l:l + 1, :]
    return lb


def _chunks(x):
    return x.reshape(x.shape[0] // CHUNK, CHUNK, x.shape[1])


def _hgrn2_pre(hq, hf, hi, lb):
    n, w = CHUNK, HG_W
    q = _chunks(_silu(hq) * (HG_DK ** -0.5))
    f = lb + (1.0 - lb) * _sigmoid(hf)
    k = _chunks(1.0 - f)
    v = _chunks(hi)
    nb = q.shape[0]
    tri = (_iota((n, n), 0) >= _iota((n, n), 1)).astype(BF16)
    cum = sel_mm(tri, _chunks(jnp.log(f)))
    lane_head = _iota((nb, n, w), 2) // HG_DK

    def stack(x):
        return jnp.concatenate([jnp.where(lane_head == h, x, 0.0) for h in range(HG_HEADS)], axis=1)

    t_idx = _iota((nb, n, HG_HEADS * n), 1)
    s_idx = _iota((nb, n, HG_HEADS * n), 2) % n
    p = jnp.where(t_idx == s_idx, mm_nt(q, stack(k)), 0.0)
    b = n // 2
    while b >= 1:
        ref_row = (_iota((n, n), 0) // (2 * b)) * (2 * b) + b - 1
        ref = sel_mm((_iota((n, n), 1) == ref_row).astype(BF16), cum)
        qe = q * jnp.exp(jnp.minimum(cum - ref, 0.0))
        ke = k * jnp.exp(jnp.minimum(ref - cum, 0.0))
        mask = (t_idx // (2 * b) == s_idx // (2 * b)) & (t_idx % (2 * b) >= b) & (s_idx % (2 * b) < b)
        p = p + jnp.where(mask, mm_nt(qe, stack(ke)), 0.0)
        b //= 2
    last = cum[:, n - 1:n, :]
    return mm_nn(p, stack(v)), q * jnp.exp(cum), k * jnp.exp(last - cum), jnp.exp(last), v


def _gdn_pre(gq, gk, gv, gba, halo_q, halo_k, halo_v, conv_w, alog_vec, dt_vec):
    n = CHUNK
    cq = _silu(_causal_conv(gq, halo_q, conv_w[:, 0:GD_W]))
    ck = _silu(_causal_conv(gk, halo_k, conv_w[:, GD_W:2 * GD_W]))
    cv = _silu(_causal_conv(gv, halo_v, conv_w[:, 2 * GD_W:3 * GD_W]))

    def heads(x):
        return jnp.concatenate([_chunks(x[:, GD_DK * h:GD_DK * (h + 1)]) for h in range(GD_HEADS)], axis=0)

    def l2n(x):
        return x * lax.rsqrt(jnp.sum(x * x, axis=-1, keepdims=True) + EPS)

    q = l2n(heads(cq)) * (GD_DK ** -0.5)
    k = l2n(heads(ck))
    v = heads(cv)
    beta_full = _chunks(_sigmoid(gba))
    g_full = _chunks(-jnp.exp(alog_vec) * _softplus(gba + dt_vec))
    tri = (_iota((n, n), 0) >= _iota((n, n), 1)).astype(BF16)
    cum_full = sel_mm(tri, g_full)
    beta = jnp.concatenate([beta_full[:, :, h:h + 1] for h in range(GD_HEADS)], axis=0)
    cum = jnp.concatenate([cum_full[:, :, GD_HEADS + h:GD_HEADS + h + 1] for h in range(GD_HEADS)], axis=0)
    bsz = cum.shape[0]
    ccol = jnp.broadcast_to(cum, (bsz, n, n))
    diff = ccol - exact_transpose(ccol)
    r_i, c_i = _iota((bsz, n, n), 1), _iota((bsz, n, n), 2)
    decay = jnp.exp(jnp.minimum(diff, 0.0))
    kb = k * beta
    lower = jnp.where(r_i > c_i, mm_nt(kb, k) * decay, 0.0)
    tinv = unit_lower_inverse(lower, n)
    ecum = jnp.exp(cum)
    sol = mm3_nn(tinv, jnp.concatenate([v * beta, kb * ecum], axis=2))
    scores = jnp.where(r_i >= c_i, mm_nt(q, k) * decay, 0.0)
    last = cum[:, n - 1:n, :]
    return sol[:, :, 0:GD_DK], sol[:, :, GD_DK:2 * GD_DK], scores, q * ecum, k * jnp.exp(last - cum), jnp.exp(last)


def _rglru_pre(rx, halo_x, conv_w, conv_b, wr, br, wi, bi, lam):
    xc = _causal_conv(rx, halo_x, conv_w) + conv_b
    r = _sigmoid(mm_nn(xc, wr) + br)
    ig = _sigmoid(mm_nn(xc, wi) + bi)
    two = 2.0 * (-RG_C * r * _softplus(-lam))
    a = jnp.exp(0.5 * two)
    one_minus = -jnp.tanh(0.5 * two) * (jnp.exp(two) + 1.0)
    acc_a, acc_b = a, jnp.sqrt(one_minus) * (ig * xc)
    t = _iota(acc_a.shape, 0) % CHUNK
    k = 1
    while k < CHUNK:
        sa = jnp.where(t >= k, jnp.roll(acc_a, k, axis=0), 1.0)
        sb = jnp.where(t >= k, jnp.roll(acc_b, k, axis=0), 0.0)
        acc_b = acc_a * sb + acc_b
        acc_a = acc_a * sa
        k *= 2
    return acc_a, acc_b


MIXER_PARAMS = ("hg_lb", "hg_norm_w", "gd_conv_w", "gd_alog_vec", "gd_dt_vec", "gd_norm_w",
                "rg_conv_w", "rg_conv_b", "rg_wr_bd", "rg_br", "rg_wi_bd", "rg_bi", "rg_lambda")


def mixer_pre(layer, proj, halo, params):
    (hg_lb, _, gd_conv_w, alog_vec, dt_vec, _, rg_conv_w, rg_conv_b, wr, br, wi, bi, lam) = params
    hg = _hgrn2_pre(proj[:, O_HQ:O_HQ + 256], proj[:, O_HF:O_HF + 256], proj[:, O_HI:O_HI + 256], _lb_of_layer(hg_lb, layer))
    gd = _gdn_pre(proj[:, O_GQ:O_GQ + 512], proj[:, O_GK:O_GK + 512], proj[:, O_GV:O_GV + 512], proj[:, O_GBA:O_GBA + 128],
                  halo[:, O_GQ:O_GQ + 512], halo[:, O_GK:O_GK + 512], halo[:, O_GV:O_GV + 512], gd_conv_w, alog_vec, dt_vec)
    rg = _rglru_pre(proj[:, O_RX:O_RX + 256], halo[:, O_RX:O_RX + 256], rg_conv_w, rg_conv_b, wr, br, wi, bi, lam)
    return hg, gd, rg


def mixer_pre_chunk(pre, c, nb):
    hg, gd, rg = pre
    pick = lambda x: jnp.concatenate([x[h * nb + c:h * nb + c + 1] for h in range(GD_HEADS)], axis=0)
    return (tuple(x[c] for x in hg), tuple(pick(x) for x in gd), tuple(x[c * CHUNK:(c + 1) * CHUNK, :] for x in rg))


def mixer_pre_unchunk(parts, nb):
    hg = tuple(jnp.concatenate([parts[c][0][i][None] for c in range(nb)], axis=0) for i in range(len(parts[0][0])))
    gd = tuple(jnp.concatenate([parts[c][1][i][h:h + 1] for h in range(GD_HEADS) for c in range(nb)], axis=0)
               for i in range(len(parts[0][1])))
    rg = tuple(jnp.concatenate([parts[c][2][i] for c in range(nb)], axis=0) for i in range(len(parts[0][2])))
    return hg, gd, rg


def mixer_rec(pre_c, st_hg, s_gd, h_rg):
    (o_intra, qe, kd, elast, v), (u, w, scores, gqe, gkd, gel), (acc_a, acc_b) = pre_c
    o_hg = o_intra + mm_nt(qe, st_hg)
    blk = _iota((HG_W, HG_W), 0) // HG_DK == _iota((HG_W, HG_W), 1) // HG_DK
    st_new = st_hg * elast + jnp.where(blk, mm_tn(v, kd), 0.0)
    v_new = u - mm_nn(w, s_gd)
    o_gd = mm_nn(gqe, s_gd) + mm_nn(scores, v_new)
    s_new = s_gd * gel + mm_tn(gkd, v_new)
    hs = acc_b + acc_a * h_rg
    return (o_hg, o_gd, hs), (st_new, s_new, hs[CHUNK - 1:CHUNK, :])


def mixer_post(o_hg, o_gd, hs, proj, params):
    (_, hg_norm_w, _, _, _, gd_norm_w, _, _, _, _, _, _, _) = params
    blk = (_iota((HG_W, HG_W), 0) // HG_DK == _iota((HG_W, HG_W), 1) // HG_DK).astype(BF16)
    ms = _seg_sum(o_hg * o_hg, blk) * (1.0 / HG_DK)
    y_hg = o_hg * lax.rsqrt(ms + EPS) * jnp.concatenate([hg_norm_w] * HG_HEADS, axis=1) * _silu(proj[:, O_HG:O_HG + 256])
    parts = []
    for h in range(GD_HEADS):
        o = o_gd[:, GD_DK * h:GD_DK * (h + 1)]
        parts.append(o * lax.rsqrt(jnp.mean(o * o, axis=-1, keepdims=True) + EPS) * gd_norm_w)
    y_gd = jnp.concatenate(parts, axis=1) * _silu(proj[:, O_GZ:O_GZ + 512])
    y_rg = hs * _gelu_tanh(proj[:, O_RGATE:O_RGATE + 256])
    return jnp.concatenate([y_hg, y_gd, y_rg], axis=1)


def mixer_step_forward(layer, proj, halo, states, params):
    nb = proj.shape[0] // CHUNK
    pre = mixer_pre(layer, proj, halo, params)
    outs, entered = [], []
    for c in range(nb):
        entered.append(states)
        o, states = mixer_rec(mixer_pre_chunk(pre, c, nb), *states)
        outs.append(o)
    o_hg = jnp.concatenate([o[0] for o in outs], axis=0)
    o_gd = jnp.concatenate([jnp.concatenate([o[1][h] for h in range(GD_HEADS)], axis=1) for o in outs], axis=0)
    hs = jnp.concatenate([o[2] for o in outs], axis=0)
    return mixer_post(o_hg, o_gd, hs, proj, params), (o_hg, o_gd, hs), entered, states


def _full(shape):
    return pl.BlockSpec(shape, lambda *_: (0,) * len(shape))


def _token_tile(seq, want):
    return min(want, seq)


def ffn_fwd(x, nw, pack, layer, which):
    seq = x.shape[0]
    t = _token_tile(seq, 1024)
    bg, bu, bd = (_ffn_block(layer, which, p) for p in "gud")

    def body(x_ref, nw_ref, wg_ref, wu_ref, wd_ref, xo_ref, h_ref, g_ref, u_ref, a_ref, h_scr, acc):
        k = pl.program_id(1)

        @pl.when(k == 0)
        def _():
            xv = x_ref[...]
            h = _rms_fwd(xv, nw_ref[...]).astype(BF16)
            h_scr[...] = h
            h_ref[...] = h
            acc[...] = xv

        h = h_scr[...]
        g = _dot_nt(h, wg_ref[...].reshape(FFN_COLS, D_MODEL))
        u = _dot_nt(h, wu_ref[...].reshape(FFN_COLS, D_MODEL))
        a = (g * jax.nn.sigmoid(g) * u).astype(BF16)
        g_ref[...] = g.astype(BF16)
        u_ref[...] = u.astype(BF16)
        a_ref[...] = a
        acc[...] += 0.5 * _dot_nn(a, wd_ref[...].reshape(FFN_COLS, D_MODEL))

        @pl.when(k == FFN_STEPS - 1)
        def _():
            xo_ref[...] = acc[...]

    wspec = lambda blk: pl.BlockSpec((FFN_GROUP, FF_PAD, D_MODEL), lambda i, k: (k, blk, 0))
    act = pl.BlockSpec((t, FFN_COLS), lambda i, k: (i, k))
    row = pl.BlockSpec((t, D_MODEL), lambda i, k: (i, 0))
    ff = N_DEV * FF_PAD
    return pl.pallas_call(
        body, name=f"ffn_fwd_l{layer}_{which}", grid=(seq // t, FFN_STEPS),
        in_specs=[row, _full((1, D_MODEL)), wspec(bg), wspec(bu), wspec(bd)],
        out_specs=[row, row, act, act, act],
        out_shape=[jax.ShapeDtypeStruct((seq, D_MODEL), F32), jax.ShapeDtypeStruct((seq, D_MODEL), BF16)]
        + [jax.ShapeDtypeStruct((seq, ff), BF16)] * 3,
        scratch_shapes=[pltpu.VMEM((t, D_MODEL), BF16), pltpu.VMEM((t, D_MODEL), F32)],
        compiler_params=_cparams(("parallel", "arbitrary")),
    )(x, nw, pack, pack, pack)


def ffn_bwd(dout, x, nw, g, u, pack, layer, which):
    seq = x.shape[0]
    t = _token_tile(seq, 1024)
    bg, bu, bd = (_ffn_block(layer, which, p) for p in "gud")

    def body(do_ref, x_ref, nw_ref, g_ref, u_ref, wg_ref, wu_ref, wd_ref, dx_ref, dg_ref, du_ref, dnw_ref, doh, dh):
        i, k = pl.program_id(0), pl.program_id(1)

        @pl.when(k == 0)
        def _():
            doh[...] = (0.5 * do_ref[...]).astype(BF16)
            dh[...] = jnp.zeros_like(dh)

        @pl.when((k == 0) & (i == 0))
        def _():
            dnw_ref[...] = jnp.zeros_like(dnw_ref)

        da = _dot_nt(doh[...], wd_ref[...].reshape(FFN_COLS, D_MODEL))
        gv = g_ref[...].astype(F32)
        uv = u_ref[...].astype(F32)
        s = jax.nn.sigmoid(gv)
        dg = (da * uv * (s * (1.0 + gv * (1.0 - s)))).astype(BF16)
        du = (da * gv * s).astype(BF16)
        dg_ref[...] = dg
        du_ref[...] = du
        dh[...] += _dot_nn(dg, wg_ref[...].reshape(FFN_COLS, D_MODEL)) + _dot_nn(du, wu_ref[...].reshape(FFN_COLS, D_MODEL))

        @pl.when(k == FFN_STEPS - 1)
        def _():
            dx, dw = _rms_bwd(dh[...], x_ref[...], nw_ref[...])
            dx_ref[...] = do_ref[...] + dx
            dnw_ref[...] += dw

    wspec = lambda blk: pl.BlockSpec((FFN_GROUP, FF_PAD, D_MODEL), lambda i, k: (k, blk, 0))
    act = pl.BlockSpec((t, FFN_COLS), lambda i, k: (i, k))
    row = pl.BlockSpec((t, D_MODEL), lambda i, k: (i, 0))
    row_once = pl.BlockSpec((t, D_MODEL), lambda i, k: (i, 0), pipeline_mode=pl.Buffered(1))
    ff = N_DEV * FF_PAD
    return pl.pallas_call(
        body, name=f"ffn_bwd_l{layer}_{which}", grid=(seq // t, FFN_STEPS),
        in_specs=[row_once, row_once, _full((1, D_MODEL)), act, act, wspec(bg), wspec(bu), wspec(bd)],
        out_specs=[row, act, act, _full((1, D_MODEL))],
        out_shape=[jax.ShapeDtypeStruct((seq, D_MODEL), F32), jax.ShapeDtypeStruct((seq, ff), BF16),
                   jax.ShapeDtypeStruct((seq, ff), BF16), jax.ShapeDtypeStruct((1, D_MODEL), F32)],
        scratch_shapes=[pltpu.VMEM((t, D_MODEL), BF16), pltpu.VMEM((t, D_MODEL), F32)],
        compiler_params=_cparams(("arbitrary", "arbitrary")),
    )(dout, x, nw, g, u, pack, pack, pack)


def tn_matmul(a, b, scale, tm, name, gbuf=None, blk_of_m=None):
    seq, m_total = a.shape
    tk = _token_tile(seq, 1024 if gbuf is not None else 512)
    nk = seq // tk
    n_m = m_total // tm

    def body(*refs):
        if gbuf is None:
            a_ref, b_ref, o_ref = refs
            acc = o_ref
        else:
            a_ref, b_ref, _, o_ref, acc = refs
        kk = pl.program_id(0)

        @pl.when(kk == 0)
        def _():
            acc[...] = jnp.zeros_like(acc)

        bv = b_ref[...].astype(BF16)
        for m in range(n_m):
            acc[pl.ds(m * tm, tm), :] += _dot_tn(a_ref[:, pl.ds(m * tm, tm)].astype(BF16), bv)

        @pl.when(kk == nk - 1)
        def _():
            if gbuf is None:
                if scale != 1.0:
                    o_ref[...] = o_ref[...] * scale
            else:
                for m in range(n_m):
                    o_ref[m] = (acc[pl.ds(m * tm, tm), :] * scale).astype(o_ref.dtype)

    in_specs = [pl.BlockSpec((tk, m_total), lambda k: (k, 0)), pl.BlockSpec((tk, D_MODEL), lambda k: (k, 0))]
    args = [a, b]
    if gbuf is None:
        out_spec = pl.BlockSpec((m_total, D_MODEL), lambda k: (0, 0))
        out_shape = jax.ShapeDtypeStruct((m_total, D_MODEL), F32)
        aliases, scratch = {}, []
    else:
        assert n_m == N_DEV
        in_specs.append(pl.BlockSpec(memory_space=pl.ANY))
        args.append(gbuf)
        out_spec = pl.BlockSpec((N_DEV, tm, D_MODEL), lambda k: (0, blk_of_m, 0))
        out_shape = jax.ShapeDtypeStruct(gbuf.shape, gbuf.dtype)
        aliases, scratch = {2: 0}, [pltpu.VMEM((m_total, D_MODEL), F32)]
    return pl.pallas_call(
        body, name=name, grid=(nk,), in_specs=in_specs, out_specs=out_spec, out_shape=out_shape,
        scratch_shapes=scratch, input_output_aliases=aliases, compiler_params=_cparams(("arbitrary",)),
    )(*args)


def in_proj_fwd(x, nw, wt, layer):
    seq = x.shape[0]
    t = _token_tile(seq, 256)

    def body(x_ref, nw_ref, w_ref, p_ref, h_ref):
        h = _rms_fwd(x_ref[...], nw_ref[...]).astype(BF16)
        h_ref[...] = h
        p_ref[...] = _dot_nt(h, w_ref[...])

    row = pl.BlockSpec((t, D_MODEL), lambda i: (i, 0))
    return pl.pallas_call(
        body, name=f"in_proj_fwd_l{layer}", grid=(seq // t,),
        in_specs=[row, _full((1, D_MODEL)), _full((NP, D_MODEL))],
        out_specs=[pl.BlockSpec((t, NP), lambda i: (i, 0)), row],
        out_shape=[jax.ShapeDtypeStruct((seq, NP), F32), jax.ShapeDtypeStruct((seq, D_MODEL), BF16)],
        compiler_params=_cparams(("parallel",)),
    )(x, nw, wt)


def in_proj_bwd(dproj, x, nw, dres, wt, layer):
    seq = x.shape[0]
    t = _token_tile(seq, 256)

    def body(dp_ref, x_ref, nw_ref, dr_ref, w_ref, dx_ref, dnw_ref):
        @pl.when(pl.program_id(0) == 0)
        def _():
            dnw_ref[...] = jnp.zeros_like(dnw_ref)

        dh = _dot_nn(dp_ref[...], w_ref[...])
        dx, dw = _rms_bwd(dh, x_ref[...], nw_ref[...])
        dx_ref[...] = dr_ref[...] + dx
        dnw_ref[...] += dw

    row = pl.BlockSpec((t, D_MODEL), lambda i: (i, 0))
    return pl.pallas_call(
        body, name=f"in_proj_bwd_l{layer}", grid=(seq // t,),
        in_specs=[pl.BlockSpec((t, NP), lambda i: (i, 0)), row, _full((1, D_MODEL)), row, _full((NP, D_MODEL))],
        out_specs=[row, _full((1, D_MODEL))],
        out_shape=[jax.ShapeDtypeStruct((seq, D_MODEL), F32), jax.ShapeDtypeStruct((1, D_MODEL), F32)],
        compiler_params=_cparams(("arbitrary",)),
    )(dproj, x, nw, dres, wt)


def out_proj_fwd(x, y, w, layer):
    seq = x.shape[0]
    t = _token_tile(seq, 512)

    def body(x_ref, y_ref, w_ref, o_ref):
        o_ref[...] = x_ref[...] + _dot_nn(y_ref[...], w_ref[...])

    row = pl.BlockSpec((t, D_MODEL), lambda i: (i, 0))
    return pl.pallas_call(
        body, name=f"out_proj_fwd_l{layer}", grid=(seq // t,),
        in_specs=[row, row, _full((D_MODEL, D_MODEL))], out_specs=row,
        out_shape=jax.ShapeDtypeStruct((seq, D_MODEL), F32), compiler_params=_cparams(("parallel",)),
    )(x, y, w)


def out_proj_bwd(dx, w, layer):
    seq = dx.shape[0]
    t = _token_tile(seq, 512)

    def body(d_ref, w_ref, o_ref):
        o_ref[...] = _dot_nt(d_ref[...].astype(BF16), w_ref[...])

    row = pl.BlockSpec((t, D_MODEL), lambda i: (i, 0))
    return pl.pallas_call(
        body, name=f"out_proj_bwd_l{layer}", grid=(seq // t,),
        in_specs=[row, _full((D_MODEL, D_MODEL))], out_specs=row,
        out_shape=jax.ShapeDtypeStruct((seq, D_MODEL), F32), compiler_params=_cparams(("parallel",)),
    )(dx, w)


def loss_head(x, nw, target):
    seq = x.shape[0]
    t = _token_tile(seq, 512)

    def body(x_ref, nw_ref, t_ref, loss_ref, dx_ref, dnw_ref):
        @pl.when(pl.program_id(0) == 0)
        def _():
            loss_ref[...] = jnp.zeros_like(loss_ref)
            dnw_ref[...] = jnp.zeros_like(dnw_ref)

        xv, w = x_ref[...], nw_ref[...]
        err = _rms_fwd(xv, w) - t_ref[...]
        loss_ref[...] += 0.5 * jnp.sum(jnp.mean(err * err, axis=-1, keepdims=True), axis=0, keepdims=True)
        dx, dw = _rms_bwd(err * (1.0 / D_MODEL), xv, w)
        dx_ref[...] = dx
        dnw_ref[...] += dw

    row = pl.BlockSpec((t, D_MODEL), lambda i: (i, 0))
    return pl.pallas_call(
        body, name="loss_head", grid=(seq // t,),
        in_specs=[row, _full((1, D_MODEL)), row], out_specs=[_full((1, 128)), row, _full((1, D_MODEL))],
        out_shape=[jax.ShapeDtypeStruct((1, 128), F32), jax.ShapeDtypeStruct((seq, D_MODEL), F32),
                   jax.ShapeDtypeStruct((1, D_MODEL), F32)],
        compiler_params=_cparams(("arbitrary",)),
    )(x, nw, target)


_MIXER_PARAM_SHAPES = ((DEPTH, HG_W), (1, HG_DK), (CONV_W, 3 * GD_W), (1, 128), (1, 128), (1, GD_DK),
                       (CONV_W, RG_W), (1, RG_W), (RG_W, RG_W), (1, RG_W), (RG_W, RG_W), (1, RG_W), (1, RG_W))
_STATE_SHAPES = ((HG_W, HG_W), (GD_HEADS, GD_DK, GD_DK), (1, RG_W))


def _state_spec(cps, sh, index):
    return pl.BlockSpec((cps,) + sh, lambda i: (index(i),) + (0,) * len(sh))


def mixer_fwd(layer, proj, params):
    seq = proj.shape[0]
    cps = min(MIXER_CHUNKS_FWD, seq // CHUNK)
    rows = cps * CHUNK
    n = seq // rows

    def body(*refs):
        p_ref, halo_ref = refs[0], refs[1]
        prm = refs[2:15]
        y_ref, o_ref, st_out, s_out, h_out = refs[15:20]
        st, s, h = refs[20:23]
        i = pl.program_id(0)

        @pl.when(i == 0)
        def _():
            st[...] = jnp.zeros_like(st)
            s[...] = jnp.zeros_like(s)
            h[...] = jnp.zeros_like(h)

        halo = jnp.where(i > 0, halo_ref[...], 0.0)
        y, outs, entered, left = mixer_step_forward(layer, p_ref[...], halo, (st[...], s[...], h[...]),
                                                    tuple(r[...] for r in prm))
        y_ref[...] = y.astype(BF16)
        o_ref[...] = jnp.concatenate(outs, axis=1)
        for c in range(cps):
            st_out[c], s_out[c], h_out[c] = entered[c]
        st[...], s[...], h[...] = left

    in_specs = [pl.BlockSpec((rows, NP), lambda i: (i, 0)),
                pl.BlockSpec((8, NP), lambda i: (jnp.maximum(i * (rows // 8) - 1, 0), 0))]
    in_specs += [_full(sh) for sh in _MIXER_PARAM_SHAPES]
    out_specs = [pl.BlockSpec((rows, D_MODEL), lambda i: (i, 0))] * 2
    out_specs += [_state_spec(cps, sh, lambda i: i) for sh in _STATE_SHAPES]
    out_shape = [jax.ShapeDtypeStruct((seq, D_MODEL), BF16), jax.ShapeDtypeStruct((seq, D_MODEL), F32)]
    out_shape += [jax.ShapeDtypeStruct((seq // CHUNK,) + sh, F32) for sh in _STATE_SHAPES]
    return pl.pallas_call(
        body, name=f"mixer_fwd_l{layer}", grid=(n,), in_specs=in_specs, out_specs=out_specs, out_shape=out_shape,
        scratch_shapes=[pltpu.VMEM(sh, F32) for sh in _STATE_SHAPES],
        compiler_params=_cparams(("arbitrary",)),
    )(proj, proj, *params)


def mixer_bwd(layer, proj, dy, outs, states, params):
    seq = proj.shape[0]
    cps = min(MIXER_CHUNKS_BWD, seq // CHUNK)
    rows = cps * CHUNK
    n = seq // rows

    def body(*refs):
        p_ref, halo_ref, dy_ref, o_ref, st_ref, s_ref, h_ref = refs[0:7]
        prm = refs[7:20]
        dp_ref = refs[20]
        dprm = refs[21:34]
        dst, ds, dh, dhalo = refs[34:38]
        i = pl.program_id(0)
        r = n - 1 - i

        @pl.when(i == 0)
        def _():
            for ref in (dst, ds, dh, dhalo) + tuple(dprm):
                ref[...] = jnp.zeros_like(ref)

        params_v = tuple(q[...] for q in prm)
        proj_v = p_ref[...]
        halo = jnp.where(r > 0, halo_ref[...], 0.0)
        o = o_ref[...]
        _, post_vjp = jax.vjp(mixer_post, o[:, 0:HG_W], o[:, HG_W:HG_W + GD_W], o[:, HG_W + GD_W:D_MODEL], proj_v, params_v)
        d_ohg, d_ogd, d_hs, dproj_post, dparams_post = post_vjp(dy_ref[...])
        pre, pre_vjp = jax.vjp(functools.partial(mixer_pre, layer), proj_v, halo, params_v)
        dstates = (dst[...], ds[...], dh[...])
        parts = [None] * cps
        for c in reversed(range(cps)):
            rs = slice(c * CHUNK, (c + 1) * CHUNK)
            _, rec_vjp = jax.vjp(mixer_rec, mixer_pre_chunk(pre, c, cps), st_ref[c], s_ref[c], h_ref[c])
            d_ogd_c = jnp.concatenate([d_ogd[rs, GD_DK * hd:GD_DK * (hd + 1)][None] for hd in range(GD_HEADS)], axis=0)
            parts[c], dst_c, ds_c, dh_c = rec_vjp(((d_ohg[rs, :], d_ogd_c, d_hs[rs, :]), dstates))
            dstates = (dst_c, ds_c, dh_c)
        dproj_pre, dhalo_n, dparams_pre = pre_vjp(mixer_pre_unchunk(parts, cps))
        carry = jnp.concatenate([jnp.zeros((rows - 8, NP), F32), dhalo[...]], axis=0)
        dp_ref[...] = (dproj_post + dproj_pre + carry).astype(BF16)
        dst[...], ds[...], dh[...] = dstates
        dhalo[...] = dhalo_n
        for ref, a, b in zip(dprm, dparams_post, dparams_pre):
            ref[...] += a + b

    rev = lambda i: n - 1 - i
    in_specs = [pl.BlockSpec((rows, NP), lambda i: (rev(i), 0)),
                pl.BlockSpec((8, NP), lambda i: (jnp.maximum(rev(i) * (rows // 8) - 1, 0), 0)),
                pl.BlockSpec((rows, D_MODEL), lambda i: (rev(i), 0)), pl.BlockSpec((rows, D_MODEL), lambda i: (rev(i), 0))]
    in_specs += [_state_spec(cps, sh, rev) for sh in _STATE_SHAPES]
    in_specs += [_full(sh) for sh in _MIXER_PARAM_SHAPES]
    out_specs = [pl.BlockSpec((rows, NP), lambda i: (rev(i), 0))] + [_full(sh) for sh in _MIXER_PARAM_SHAPES]
    out_shape = [jax.ShapeDtypeStruct((seq, NP), BF16)] + [jax.ShapeDtypeStruct(sh, F32) for sh in _MIXER_PARAM_SHAPES]
    return pl.pallas_call(
        body, name=f"mixer_bwd_l{layer}", grid=(n,), in_specs=in_specs, out_specs=out_specs, out_shape=out_shape,
        scratch_shapes=[pltpu.VMEM(sh, F32) for sh in _STATE_SHAPES] + [pltpu.VMEM((8, NP), F32)],
        compiler_params=_cparams(("arbitrary",)),
    )(proj, proj, dy, outs, *states, *params)


def _win_segments():
    out = []
    for k in range(N_DEV):
        a, b = NIN_SHARD * k, NIN_SHARD * (k + 1)
        if a < GBA_SPLIT < b:
            out.append((k, 0, GBA_SPLIT - a, a))
            out.append((k, GBA_SPLIT - a, b - GBA_SPLIT, O_RX))
        elif b <= GBA_SPLIT:
            out.append((k, 0, NIN_SHARD, a))
        else:
            out.append((k, 0, NIN_SHARD, a + O_RX - GBA_SPLIT))
    return out


def win_to_padded(wt_shards, layer):
    lanes = 256

    def body(src, dst, scr):
        scr[...] = jnp.zeros_like(scr)
        for (k, s0, rows, d0) in _win_segments():
            scr[pl.ds(d0, rows), :] = src[k, pl.ds(s0, rows), :].astype(F32)
        dst[...] = scr[...].astype(BF16)

    return pl.pallas_call(
        body, name=f"win_to_padded_l{layer}", grid=(D_MODEL // lanes,),
        in_specs=[pl.BlockSpec((N_DEV, NIN_SHARD_PAD, lanes), lambda j: (0, 0, j))],
        out_specs=pl.BlockSpec((NP, lanes), lambda j: (0, j)),
        out_shape=jax.ShapeDtypeStruct((NP, D_MODEL), BF16),
        scratch_shapes=[pltpu.VMEM((NP, lanes), F32)], compiler_params=_cparams(("parallel",)),
    )(wt_shards)


def win_grad_to_pack(dwt, gbuf, layer):
    lanes = 256
    blk = (OFF_WIN + layer * NIN_SHARD_PAD) // NIN_SHARD_PAD

    def body(src, _, dst, scr):
        scr[...] = jnp.zeros_like(scr)
        for (k, s0, rows, d0) in _win_segments():
            scr[k, pl.ds(s0, rows), :] = src[pl.ds(d0, rows), :]
        dst[...] = scr[...].astype(dst.dtype)

    return pl.pallas_call(
        body, name=f"win_grad_to_pack_l{layer}", grid=(D_MODEL // lanes,),
        in_specs=[pl.BlockSpec((NP, lanes), lambda j: (0, j)), pl.BlockSpec(memory_space=pl.ANY)],
        out_specs=pl.BlockSpec((N_DEV, NIN_SHARD_PAD, lanes), lambda j: (0, blk, j)),
        out_shape=jax.ShapeDtypeStruct(gbuf.shape, gbuf.dtype), input_output_aliases={1: 0},
        scratch_shapes=[pltpu.VMEM((N_DEV, NIN_SHARD_PAD, lanes), F32)],
        compiler_params=_cparams(("parallel",)),
    )(dwt, gbuf)


def sum_adamw(parts, w, m, v, name, rows_per_step):
    n, rows, cols = parts.shape
    tr = min(rows_per_step, rows)
    c1 = 1.0 - ADAM_B1 ** ADAM_STEP
    c2 = 1.0 - ADAM_B2 ** ADAM_STEP

    def body(p_ref, w_ref, m_ref, v_ref, g_ref, d_ref, mo_ref, vo_ref):
        g = p_ref[0].astype(F32)
        for j in range(1, n):
            g = g + p_ref[j].astype(F32)
        mn = ADAM_B1 * m_ref[...] + (1.0 - ADAM_B1) * g
        vn = ADAM_B2 * v_ref[...] + (1.0 - ADAM_B2) * (g * g)
        g_ref[...] = g
        mo_ref[...] = mn
        vo_ref[...] = vn
        d_ref[...] = -ADAM_LR * ((mn / c1) / (jnp.sqrt(vn / c2) + ADAM_EPS) + ADAM_WD * w_ref[...])

    blk = pl.BlockSpec((tr, cols), lambda i: (i, 0))
    return pl.pallas_call(
        body, name=name, grid=(rows // tr,),
        in_specs=[pl.BlockSpec((n, tr, cols), lambda i: (0, i, 0)), blk, blk, blk], out_specs=[blk] * 4,
        out_shape=[jax.ShapeDtypeStruct((rows, cols), F32)] * 4, compiler_params=_cparams(("parallel",)),
    )(parts, w, m, v)


def exchange(arrays, scatter, name):
    n = len(arrays)

    def body(*refs):
        ins, outs = refs[:n], refs[n:2 * n]
        send_sems, recv_sems, local_sems = refs[2 * n:]
        x, y, c = lax.axis_index("x"), lax.axis_index("y"), lax.axis_index("c")
        me = 4 * x + 2 * y + c
        local = []
        for a in range(n):
            cp = pltpu.make_async_copy(ins[a].at[me] if scatter else ins[a], outs[a].at[me], local_sems.at[a])
            cp.start()
            local.append(cp)
        pending = []
        for j in range(1, N_DEV):
            px = 1 - x if j & 4 else x
            py = 1 - y if j & 2 else y
            pc = 1 - c if j & 1 else c
            peer = 4 * px + 2 * py + pc
            for a in range(n):
                src = ins[a].at[peer] if scatter else ins[a]
                k = a * (N_DEV - 1) + j - 1

                def copy(dst_slot):
                    return pltpu.make_async_remote_copy(
                        src_ref=src, dst_ref=outs[a].at[dst_slot], send_sem=send_sems.at[k], recv_sem=recv_sems.at[k],
                        device_id=(px, py, pc), device_id_type=pl.DeviceIdType.MESH)

                send = copy(me)
                send.start()
                pending.append((send, copy(peer)))
        for send, recv in pending:
            send.wait_send()
            recv.wait_recv()
        for cp in local:
            cp.wait()

    out_shape = [jax.ShapeDtypeStruct(a.shape if scatter else (N_DEV,) + a.shape, a.dtype) for a in arrays]
    any_spec = pl.BlockSpec(memory_space=pl.ANY)
    return pl.pallas_call(
        body, name=name, in_specs=[any_spec] * n, out_specs=[any_spec] * n, out_shape=out_shape,
        scratch_shapes=[pltpu.SemaphoreType.DMA((n * (N_DEV - 1),)), pltpu.SemaphoreType.DMA((n * (N_DEV - 1),)),
                        pltpu.SemaphoreType.DMA((n,))],
        compiler_params=pltpu.CompilerParams(has_side_effects=True),
    )(*arrays)


BIG = ("ffn1_gate", "ffn1_up", "ffn1_down", "w_in", "w_out", "ffn2_gate", "ffn2_up", "ffn2_down")
SMALL_REPLICATED = (("norm_ffn1", (DEPTH, D_MODEL)), ("norm_mix", (DEPTH, D_MODEL)), ("norm_ffn2", (DEPTH, D_MODEL)),
                    ("norm_final", (D_MODEL,)), ("hg_lb", (DEPTH, HG_W)), ("hg_norm_w", (DEPTH, HG_DK)),
                    ("gd_a_log", (DEPTH, GD_HEADS)), ("gd_dt_bias", (DEPTH, GD_HEADS)), ("gd_norm_w", (DEPTH, GD_DK)),
                    ("rg_conv_b", (DEPTH, RG_W)), ("rg_wr", (DEPTH, RG_BLOCKS, RG_BD, RG_BD)), ("rg_br", (DEPTH, RG_W)),
                    ("rg_wi", (DEPTH, RG_BLOCKS, RG_BD, RG_BD)), ("rg_bi", (DEPTH, RG_W)), ("rg_lambda", (DEPTH, RG_W)))
SMALL_EXTRA = (("gd_conv_w", (DEPTH, CONV_W, 3 * GD_W)), ("rg_conv_w", (DEPTH, CONV_W, RG_W)), ("loss", (1,)))
SMALL_SHARDED = (("gd_conv_w", (DEPTH, CONV_W, GDC_SHARD)), ("rg_conv_w", (DEPTH, CONV_W, RGC_SHARD)))


def _pad_rows(a, rows):
    return jnp.pad(a, ((0, rows - a.shape[0]), (0, 0)))


def pack_big(get):
    pieces = []
    for l in range(DEPTH):
        for which in (1, 2):
            pieces.append(_pad_rows(get(f"ffn{which}_gate")[l].T, FF_PAD))
            pieces.append(_pad_rows(get(f"ffn{which}_up")[l].T, FF_PAD))
            pieces.append(_pad_rows(get(f"ffn{which}_down")[l], FF_PAD))
    for l in range(DEPTH):
        pieces.append(_pad_rows(get("w_in")[l].T, NIN_SHARD_PAD))
    for l in range(DEPTH):
        pieces.append(get("w_out")[l])
    return jnp.concatenate(pieces, axis=0)


def unpack_big(p):
    out = {}
    for which in (1, 2):
        for part, nm in (("g", "gate"), ("u", "up"), ("d", "down")):
            per_layer = []
            for l in range(DEPTH):
                r0 = _ffn_block(l, which, part) * FF_PAD
                blk = p[r0:r0 + FF_SHARD, :]
                per_layer.append(blk if part == "d" else blk.T)
            out[f"ffn{which}_{nm}"] = jnp.stack(per_layer)
    out["w_in"] = jnp.stack([p[OFF_WIN + NIN_SHARD_PAD * l:OFF_WIN + NIN_SHARD_PAD * l + NIN_SHARD, :].T for l in range(DEPTH)])
    out["w_out"] = jnp.stack([p[OFF_WOUT + WOUT_SHARD * l:OFF_WOUT + WOUT_SHARD * (l + 1), :] for l in range(DEPTH)])
    return out


def pack_small(get, spec):
    flat = []
    for name, shape in spec:
        a = get(name)
        flat.append(jnp.zeros((math.prod(shape),), F32) if a is None else a.reshape(-1).astype(F32))
    v = jnp.concatenate(flat)
    total = -(-v.shape[0] // 1024) * 1024
    return jnp.pad(v, (0, total - v.shape[0])).reshape(total // 128, 128)


def unpack_small(p, spec):
    v = p.reshape(-1)
    out, off = {}, 0
    for name, shape in spec:
        size = math.prod(shape)
        out[name] = v[off:off + size].reshape(shape)
        off += size
    return out


def _block_diag(w):
    rows = []
    for i in range(RG_BLOCKS):
        rows.append(jnp.concatenate([w[i] if j == i else jnp.zeros((RG_BD, RG_BD), F32) for j in range(RG_BLOCKS)], axis=1))
    return jnp.concatenate(rows, axis=0)


def _diag_blocks(w):
    return jnp.stack([w[RG_BD * i:RG_BD * (i + 1), RG_BD * i:RG_BD * (i + 1)] for i in range(RG_BLOCKS)])


def _lane_vec(v):
    return jnp.pad(v.astype(F32), (GD_HEADS, 128 - 2 * GD_HEADS))[None]


INPUT_NAMES = ("x", "norm_ffn1", "ffn1_gate", "ffn1_up", "ffn1_down", "norm_mix", "w_in", "hg_lb", "hg_norm_w",
               "gd_conv_w", "gd_a_log", "gd_dt_bias", "gd_norm_w", "rg_conv_w", "rg_conv_b", "rg_wr", "rg_br", "rg_wi",
               "rg_bi", "rg_lambda", "w_out", "norm_ffn2", "ffn2_gate", "ffn2_up", "ffn2_down", "norm_final")
WEIGHT_NAMES = INPUT_NAMES[1:]


def _step(a):
    x = a["x"][0]
    target = a["loss_target"][0]
    me = 4 * lax.axis_index("x") + 2 * lax.axis_index("y") + lax.axis_index("c")

    w_pack = pack_big(lambda nm: a[nm])
    conv_local = pack_small(lambda nm: a[nm], SMALL_SHARDED)
    gw, gconv = exchange([w_pack.astype(BF16), conv_local], False, "gather_weights")
    conv_parts = [unpack_small(gconv[d], SMALL_SHARDED) for d in range(N_DEV)]
    gd_conv_w = jnp.concatenate([p["gd_conv_w"] for p in conv_parts], axis=-1)
    rg_conv_w = jnp.concatenate([p["rg_conv_w"] for p in conv_parts], axis=-1)

    def mixer_params(l):
        return (a["hg_lb"], a["hg_norm_w"][l][None], gd_conv_w[l], _lane_vec(a["gd_a_log"][l]), _lane_vec(a["gd_dt_bias"][l]),
                a["gd_norm_w"][l][None], rg_conv_w[l], a["rg_conv_b"][l][None], _block_diag(a["rg_wr"][l]), a["rg_br"][l][None],
                _block_diag(a["rg_wi"][l]), a["rg_bi"][l][None], a["rg_lambda"][l][None])

    saved = []
    xs = x
    for l in range(DEPTH):
        x1, h1, g1, u1, a1 = ffn_fwd(xs, a["norm_ffn1"][l][None], gw, l, 1)
        wt = win_to_padded(gw[:, OFF_WIN + NIN_SHARD_PAD * l:OFF_WIN + NIN_SHARD_PAD * (l + 1), :], l)
        proj, h2 = in_proj_fwd(x1, a["norm_mix"][l][None], wt, l)
        prm = mixer_params(l)
        y, o_mix, st, s, h = mixer_fwd(l, proj, prm)
        wo = gw[:, OFF_WOUT + WOUT_SHARD * l:OFF_WOUT + WOUT_SHARD * (l + 1), :].reshape(D_MODEL, D_MODEL)
        x2 = out_proj_fwd(x1, y, wo, l)
        x3, h3, g3, u3, a3 = ffn_fwd(x2, a["norm_ffn2"][l][None], gw, l, 2)
        saved.append(dict(x0=xs, x1=x1, x2=x2, h1=h1, g1=g1, u1=u1, a1=a1, wt=wt, proj=proj, h2=h2, prm=prm, y=y, o_mix=o_mix,
                          states=(st, s, h), wo=wo, h3=h3, g3=g3, u3=u3, a3=a3))
        xs = x3
    loss_row, dx, d_norm_final = loss_head(xs, a["norm_final"][None], target)

    gbuf = jnp.zeros((N_DEV, PACK_ROWS, D_MODEL), BF16)
    sg = {"norm_final": d_norm_final[0], "loss": loss_row[0, 0:1]}
    per_layer = {nm: [None] * DEPTH for nm in ("norm_ffn1", "norm_mix", "norm_ffn2", "hg_norm_w", "gd_conv_w", "gd_a_log",
                                                "gd_dt_bias", "gd_norm_w", "rg_conv_w", "rg_conv_b", "rg_wr", "rg_br", "rg_wi",
                                                "rg_bi", "rg_lambda")}
    d_hg_lb = jnp.zeros((DEPTH, HG_W), F32)
    for l in reversed(range(DEPTH)):
        sv = saved[l]
        dx2, dg, du, dn2 = ffn_bwd(dx, sv["x2"], a["norm_ffn2"][l][None], sv["g3"], sv["u3"], gw, l, 2)
        gbuf = tn_matmul(dg, sv["h3"], 1.0, FF_PAD, f"dw_gate_l{l}_2", gbuf, _ffn_block(l, 2, "g"))
        gbuf = tn_matmul(du, sv["h3"], 1.0, FF_PAD, f"dw_up_l{l}_2", gbuf, _ffn_block(l, 2, "u"))
        gbuf = tn_matmul(sv["a3"], dx, 0.5, FF_PAD, f"dw_down_l{l}_2", gbuf, _ffn_block(l, 2, "d"))
        dy = out_proj_bwd(dx2, sv["wo"], l)
        gbuf = tn_matmul(sv["y"], dx2, 1.0, WOUT_SHARD, f"dw_out_l{l}", gbuf, (OFF_WOUT + WOUT_SHARD * l) // WOUT_SHARD)
        mb = mixer_bwd(l, sv["proj"], dy, sv["o_mix"], sv["states"], sv["prm"])
        dproj, dprm = mb[0], mb[1:]
        dwt = tn_matmul(dproj, sv["h2"], 1.0, FF_PAD, f"dw_in_l{l}")
        gbuf = win_grad_to_pack(dwt, gbuf, l)
        dx1, dnm = in_proj_bwd(dproj, sv["x1"], a["norm_mix"][l][None], dx2, sv["wt"], l)
        dx, dg, du, dn1 = ffn_bwd(dx1, sv["x0"], a["norm_ffn1"][l][None], sv["g1"], sv["u1"], gw, l, 1)
        gbuf = tn_matmul(dg, sv["h1"], 1.0, FF_PAD, f"dw_gate_l{l}_1", gbuf, _ffn_block(l, 1, "g"))
        gbuf = tn_matmul(du, sv["h1"], 1.0, FF_PAD, f"dw_up_l{l}_1", gbuf, _ffn_block(l, 1, "u"))
        gbuf = tn_matmul(sv["a1"], dx1, 0.5, FF_PAD, f"dw_down_l{l}_1", gbuf, _ffn_block(l, 1, "d"))
        (g_lb, g_hnw, g_gcw, g_alog, g_dt, g_gnw, g_rcw, g_rcb, g_wr, g_br, g_wi, g_bi, g_lam) = dprm
        d_hg_lb = d_hg_lb + g_lb
        for nm, val in (("norm_ffn1", dn1[0]), ("norm_mix", dnm[0]), ("norm_ffn2", dn2[0]), ("hg_norm_w", g_hnw[0]),
                        ("gd_conv_w", g_gcw), ("gd_a_log", g_alog[0, GD_HEADS:2 * GD_HEADS]),
                        ("gd_dt_bias", g_dt[0, GD_HEADS:2 * GD_HEADS]), ("gd_norm_w", g_gnw[0]), ("rg_conv_w", g_rcw),
                        ("rg_conv_b", g_rcb[0]), ("rg_wr", _diag_blocks(g_wr)), ("rg_br", g_br[0]),
                        ("rg_wi", _diag_blocks(g_wi)), ("rg_bi", g_bi[0]), ("rg_lambda", g_lam[0])):
            per_layer[nm][l] = val
    grad_x = dx
    for nm, vals in per_layer.items():
        sg[nm] = jnp.stack(vals)
    sg["hg_lb"] = d_hg_lb

    small_spec = SMALL_REPLICATED + SMALL_EXTRA
    (parts_big,) = exchange([gbuf], True, "scatter_grads")
    g_big, d_big, m_big, v_big = sum_adamw(parts_big, w_pack, pack_big(lambda nm: a["m_" + nm]),
                                           pack_big(lambda nm: a["v_" + nm]), "adamw_big", 128)
    (parts_small,) = exchange([pack_small(lambda nm: sg[nm], small_spec)], False, "gather_small_grads")
    repl = dict(SMALL_REPLICATED)
    small_in = lambda pre: pack_small(lambda nm: a[pre + nm] if nm in repl else None, small_spec)
    rows_small = parts_small.shape[1]
    g_sm, d_sm, m_sm, v_sm = sum_adamw(parts_small, small_in(""), small_in("m_"), small_in("v_"), "adamw_small", rows_small)
    g_small = unpack_small(g_sm, small_spec)
    conv_grads = {"gd_conv_w": lax.dynamic_slice_in_dim(g_small["gd_conv_w"], me * GDC_SHARD, GDC_SHARD, axis=2),
                  "rg_conv_w": lax.dynamic_slice_in_dim(g_small["rg_conv_w"], me * RGC_SHARD, RGC_SHARD, axis=2)}
    conv_in = lambda pre: pack_small(lambda nm: a[pre + nm], SMALL_SHARDED)
    g_cv, d_cv, m_cv, v_cv = sum_adamw(pack_small(lambda nm: conv_grads[nm], SMALL_SHARDED)[None], conv_in(""),
                                       conv_in("m_"), conv_in("v_"), "adamw_conv", conv_local.shape[0])

    results = []
    for big, small, conv in ((g_big, g_sm, g_cv), (d_big, d_sm, d_cv), (m_big, m_sm, m_cv), (v_big, v_sm, v_cv)):
        vals = unpack_big(big)
        vals.update({k: v for k, v in unpack_small(small, small_spec).items() if k in repl})
        vals.update(unpack_small(conv, SMALL_SHARDED))
        results.append(vals)
    loss = g_small["loss"][0]
    out = [loss, grad_x[None]]
    for vals in results:
        out.extend(vals[nm] for nm in WEIGHT_NAMES)
    return tuple(out)


def kernel(x, norm_ffn1, ffn1_gate, ffn1_up, ffn1_down, norm_mix, w_in, hg_lb, hg_norm_w, gd_conv_w, gd_a_log, gd_dt_bias, gd_norm_w, rg_conv_w, rg_conv_b, rg_wr, rg_br, rg_wi, rg_bi, rg_lambda, w_out, norm_ffn2, ffn2_gate, ffn2_up, ffn2_down, norm_final, loss_target, m_norm_ffn1, m_ffn1_gate, m_ffn1_up, m_ffn1_down, m_norm_mix, m_w_in, m_hg_lb, m_hg_norm_w, m_gd_conv_w, m_gd_a_log, m_gd_dt_bias, m_gd_norm_w, m_rg_conv_w, m_rg_conv_b, m_rg_wr, m_rg_br, m_rg_wi, m_rg_bi, m_rg_lambda, m_w_out, m_norm_ffn2, m_ffn2_gate, m_ffn2_up, m_ffn2_down, m_norm_final, v_norm_ffn1, v_ffn1_gate, v_ffn1_up, v_ffn1_down, v_norm_mix, v_w_in, v_hg_lb, v_hg_norm_w, v_gd_conv_w, v_gd_a_log, v_gd_dt_bias, v_gd_norm_w, v_rg_conv_w, v_rg_conv_b, v_rg_wr, v_rg_br, v_rg_wi, v_rg_bi, v_rg_lambda, v_w_out, v_norm_ffn2, v_ffn2_gate, v_ffn2_up, v_ffn2_down, v_norm_final):
    args = locals()
    return _step(dict(args))
```

```python
import functools
import math

import jax
import jax.numpy as jnp
from jax import lax
from jax.experimental import pallas as pl
from jax.experimental.pallas import tpu as pltpu

F32 = jnp.float32
BF16 = jnp.bfloat16

D_MODEL = 1024
DEPTH = 2
D_FF = 2816
HG_HEADS, HG_DK = 4, 64
HG_W = 256
GD_HEADS, GD_DK = 4, 128
GD_W = 512
RG_W = 256
RG_BLOCKS, RG_BD = 4, 64
RG_C = 8.0
CONV_W = 4
CHUNK = 64
EPS = 1e-6
N_IN = 3592
ADAM_LR, ADAM_B1, ADAM_B2, ADAM_EPS, ADAM_WD, ADAM_STEP = 0.001, 0.9, 0.999, 1e-08, 0.01, 10

N_DEV = 8
FF_SHARD = D_FF // N_DEV
FF_PAD = 384
NIN_SHARD = N_IN // N_DEV
NIN_SHARD_PAD = 512
WOUT_SHARD = D_MODEL // N_DEV
GDC_SHARD = (2 * 512 + 512) // N_DEV
RGC_SHARD = RG_W // N_DEV

O_HQ, O_HF, O_HI, O_HG = 0, 256, 512, 768
O_GQ, O_GK, O_GV, O_GZ = 1024, 1536, 2048, 2560
O_GBA = 3072
O_RX, O_RGATE = 3200, 3456
NP = 3840
GBA_SPLIT = 3080

N_FFN_BLOCKS = DEPTH * 2 * 3
OFF_WIN = N_FFN_BLOCKS * FF_PAD
OFF_WOUT = OFF_WIN + DEPTH * NIN_SHARD_PAD
PACK_ROWS = OFF_WOUT + DEPTH * WOUT_SHARD

VMEM_LIMIT = 56 * 1024 * 1024
FFN_GROUP = 2
FFN_STEPS = N_DEV // FFN_GROUP
FFN_COLS = FFN_GROUP * FF_PAD
MIXER_CHUNKS_FWD = 4
MIXER_CHUNKS_BWD = 2


def _ffn_block(layer, which, part):
    return layer * 6 + (which - 1) * 3 + {"g": 0, "u": 1, "d": 2}[part]


def _cparams(sem, **kw):
    return pltpu.CompilerParams(dimension_semantics=sem, vmem_limit_bytes=VMEM_LIMIT, **kw)


def _dot_nn(a, b):
    return lax.dot_general(a, b, (((1,), (0,)), ((), ())), preferred_element_type=F32)


def _dot_nt(a, b):
    return lax.dot_general(a, b, (((1,), (1,)), ((), ())), preferred_element_type=F32)


def _dot_tn(a, b):
    return lax.dot_general(a, b, (((0,), (0,)), ((), ())), preferred_element_type=F32)


def _split3(x):
    hi = x.astype(BF16)
    r = x - hi.astype(F32)
    mid = r.astype(BF16)
    lo = (r - mid.astype(F32)).astype(BF16)
    return hi, mid, lo


def _split2(x):
    hi = x.astype(BF16)
    return hi, (x - hi.astype(F32)).astype(BF16)


def _iota(shape, dim):
    return lax.broadcasted_iota(jnp.int32, shape, dim)


def _sigmoid(x):
    return jax.nn.sigmoid(x)


def _silu(x):
    return x * _sigmoid(x)


def _softplus(x):
    return jnp.maximum(x, 0.0) + jnp.log(1.0 + jnp.exp(-jnp.abs(x)))


def _gelu_tanh(x):
    return 0.5 * x * (1.0 + jnp.tanh(math.sqrt(2.0 / math.pi) * (x + 0.044715 * x * x * x)))


def _rms_fwd(x, w):
    rstd = lax.rsqrt(jnp.mean(x * x, axis=-1, keepdims=True) + EPS)
    return x * rstd * w


def _rms_bwd(dh, x, w):
    rstd = lax.rsqrt(jnp.mean(x * x, axis=-1, keepdims=True) + EPS)
    xhat = x * rstd
    dxhat = dh * w
    dx = rstd * (dxhat - xhat * jnp.mean(dxhat * xhat, axis=-1, keepdims=True))
    return dx, jnp.sum(dh * xhat, axis=0, keepdims=True)


def _rank_dims(kind, ndim):
    if ndim == 2:
        return {"nn": (((1,), (0,)), ((), ())), "nt": (((1,), (1,)), ((), ())), "tn": (((0,), (0,)), ((), ()))}[kind]
    return {"nn": (((2,), (1,)), ((0,), (0,))), "nt": (((2,), (2,)), ((0,), (0,))), "tn": (((1,), (1,)), ((0,), (0,)))}[kind]


def _rdot(kind, a, b):
    return lax.dot_general(a, b, _rank_dims(kind, a.ndim), preferred_element_type=F32)


def _make_mm(kind, kind_da, kind_db, swap_da, swap_db):
    @jax.custom_vjp
    def mm(a, b):
        return _rdot(kind, a.astype(BF16), b.astype(BF16))

    def fwd(a, b):
        a, b = a.astype(BF16), b.astype(BF16)
        return _rdot(kind, a, b), (a, b)

    def bwd(res, g):
        a, b = res
        g = g.astype(BF16)
        da = _rdot(kind_da, b, g) if swap_da else _rdot(kind_da, g, b)
        db = _rdot(kind_db, g, a) if swap_db else _rdot(kind_db, a, g)
        return da, db

    mm.defvjp(fwd, bwd)
    return mm


mm_nn = _make_mm("nn", "nt", "tn", False, False)
mm_nt = _make_mm("nt", "nn", "tn", False, True)
mm_tn = _make_mm("tn", "nt", "nn", True, False)


def _bcast_const(mat, like):
    return mat if like.ndim == 2 else jnp.broadcast_to(mat, (like.shape[0],) + mat.shape)


def _sel_apply(kind, sel, x):
    hi, mid, lo = _split3(x)
    s = _bcast_const(sel, x)
    return _rdot(kind, s, hi) + _rdot(kind, s, mid) + _rdot(kind, s, lo)


@jax.custom_vjp
def sel_mm(sel, x):
    return _sel_apply("nn", sel, x)


def _sel_mm_fwd(sel, x):
    return _sel_apply("nn", sel, x), sel


def _sel_mm_bwd(sel, g):
    return jnp.zeros_like(sel), _sel_apply("tn", sel, g)


sel_mm.defvjp(_sel_mm_fwd, _sel_mm_bwd)


def _transpose_apply(a):
    n = a.shape[-1]
    eye = (_iota((n, n), 0) == _iota((n, n), 1)).astype(BF16)
    hi, mid, lo = _split3(a)
    e = _bcast_const(eye, a)
    return _rdot("nt", e, hi) + _rdot("nt", e, mid) + _rdot("nt", e, lo)


@jax.custom_vjp
def exact_transpose(a):
    return _transpose_apply(a)


exact_transpose.defvjp(lambda a: (_transpose_apply(a), None), lambda _, g: (_transpose_apply(g),))


def _dot3(kind, a, b):
    ah, al = _split2(a)
    bh, bl = _split2(b)
    return _rdot(kind, ah, bh) + _rdot(kind, ah, bl) + _rdot(kind, al, bh)


def _inverse_levels(lower, n):
    nd = lower.ndim
    r, c = _iota(lower.shape, nd - 2), _iota(lower.shape, nd - 1)
    tinv = (r == c).astype(F32)
    b = 1
    while b < n:
        off = (r // (2 * b) == c // (2 * b)) & (r % (2 * b) >= b) & (c % (2 * b) < b)
        tinv = tinv - _dot3("nn", tinv, _dot3("nn", jnp.where(off, lower, 0.0), tinv))
        b *= 2
    return tinv


@functools.partial(jax.custom_vjp, nondiff_argnums=(1,))
def unit_lower_inverse(lower, n):
    return _inverse_levels(lower, n)


def _unit_lower_inverse_fwd(lower, n):
    tinv = _inverse_levels(lower, n)
    return tinv, tinv


def _unit_lower_inverse_bwd(n, tinv, g):
    return (-_dot3("nt", _dot3("tn", tinv, g), tinv),)


unit_lower_inverse.defvjp(_unit_lower_inverse_fwd, _unit_lower_inverse_bwd)


def mm3_nn(a, b):
    ah = a.astype(BF16).astype(F32)
    al = a - ah
    bh = b.astype(BF16).astype(F32)
    bl = b - bh
    return mm_nn(ah, bh) + mm_nn(ah, bl) + mm_nn(al, bh)


def _seg_apply(x, b):
    hi, mid, lo = _split3(x)
    return _rdot("nn", hi, b) + _rdot("nn", mid, b) + _rdot("nn", lo, b)


@jax.custom_vjp
def _seg_sum(x, b):
    return _seg_apply(x, b)


_seg_sum.defvjp(lambda x, b: (_seg_apply(x, b), b), lambda b, g: (_seg_apply(g, b), jnp.zeros_like(b)))


def _shift_rows(x, halo, k):
    n = x.shape[0]
    ext = jnp.concatenate([jnp.zeros((n - halo.shape[0], x.shape[1]), x.dtype), halo], axis=0)
    t = _iota(x.shape, 0)
    return jnp.where(t >= k, jnp.roll(x, k, axis=0), jnp.roll(ext, k, axis=0))


def _causal_conv(x, halo, w):
    y = x * w[3:4, :]
    for k in range(1, CONV_W):
        y = y + _shift_rows(x, halo, k) * w[3 - k:4 - k, :]
    return y


def _lb_of_layer(hg_lb, layer):
    if layer == 0:
        return jnp.zeros((1, HG_W), F32)
    e = jnp.exp(hg_lb - jnp.max(hg_lb, axis=0, keepdims=True))
    sm = e / jnp.sum(e, axis=0, keepdims=True)
    lb = sm[1:2, :]
    for l in range(2, layer + 1):
        lb = lb + sm[l:l + 1, :]
    return lb


def _chunks(x):
    return x.reshape(x.shape[0] // CHUNK, CHUNK, x.shape[1])


def _hgrn2_pre(hq, hf, hi, lb):
    n, w = CHUNK, HG_W
    q = _chunks(_silu(hq) * (HG_DK ** -0.5))
    f = lb + (1.0 - lb) * _sigmoid(hf)
    k = _chunks(1.0 - f)
    v = _chunks(hi)
    nb = q.shape[0]
    tri = (_iota((n, n), 0) >= _iota((n, n), 1)).astype(BF16)
    cum = sel_mm(tri, _chunks(jnp.log(f)))
    lane_head = _iota((nb, n, w), 2) // HG_DK

    def stack(x):
        return jnp.concatenate([jnp.where(lane_head == h, x, 0.0) for h in range(HG_HEADS)], axis=1)

    t_idx = _iota((nb, n, HG_HEADS * n), 1)
    s_idx = _iota((nb, n, HG_HEADS * n), 2) % n
    p = jnp.where(t_idx == s_idx, mm_nt(q, stack(k)), 0.0)
    b = n // 2
    while b >= 1:
        ref_row = (_iota((n, n), 0) // (2 * b)) * (2 * b) + b - 1
        ref = sel_mm((_iota((n, n), 1) == ref_row).astype(BF16), cum)
        qe = q * jnp.exp(jnp.minimum(cum - ref, 0.0))
        ke = k * jnp.exp(jnp.minimum(ref - cum, 0.0))
        mask = (t_idx // (2 * b) == s_idx // (2 * b)) & (t_idx % (2 * b) >= b) & (s_idx % (2 * b) < b)
        p = p + jnp.where(mask, mm_nt(qe, stack(ke)), 0.0)
        b //= 2
    last = cum[:, n - 1:n, :]
    return mm_nn(p, stack(v)), q * jnp.exp(cum), k * jnp.exp(last - cum), jnp.exp(last), v


def _gdn_pre(gq, gk, gv, gba, halo_q, halo_k, halo_v, conv_w, alog_vec, dt_vec):
    n = CHUNK
    cq = _silu(_causal_conv(gq, halo_q, conv_w[:, 0:GD_W]))
    ck = _silu(_causal_conv(gk, halo_k, conv_w[:, GD_W:2 * GD_W]))
    cv = _silu(_causal_conv(gv, halo_v, conv_w[:, 2 * GD_W:3 * GD_W]))

    def heads(x):
        return jnp.concatenate([_chunks(x[:, GD_DK * h:GD_DK * (h + 1)]) for h in range(GD_HEADS)], axis=0)

    def l2n(x):
        return x * lax.rsqrt(jnp.sum(x * x, axis=-1, keepdims=True) + EPS)

    q = l2n(heads(cq)) * (GD_DK ** -0.5)
    k = l2n(heads(ck))
    v = heads(cv)
    beta_full = _chunks(_sigmoid(gba))
    g_full = _chunks(-jnp.exp(alog_vec) * _softplus(gba + dt_vec))
    tri = (_iota((n, n), 0) >= _iota((n, n), 1)).astype(BF16)
    cum_full = sel_mm(tri, g_full)
    beta = jnp.concatenate([beta_full[:, :, h:h + 1] for h in range(GD_HEADS)], axis=0)
    cum = jnp.concatenate([cum_full[:, :, GD_HEADS + h:GD_HEADS + h + 1] for h in range(GD_HEADS)], axis=0)
    bsz = cum.shape[0]
    ccol = jnp.broadcast_to(cum, (bsz, n, n))
    diff = ccol - exact_transpose(ccol)
    r_i, c_i = _iota((bsz, n, n), 1), _iota((bsz, n, n), 2)
    decay = jnp.exp(jnp.minimum(diff, 0.0))
    kb = k * beta
    lower = jnp.where(r_i > c_i, mm_nt(kb, k) * decay, 0.0)
    tinv = unit_lower_inverse(lower, n)
    ecum = jnp.exp(cum)
    sol = mm3_nn(tinv, jnp.concatenate([v * beta, kb * ecum], axis=2))
    scores = jnp.where(r_i >= c_i, mm_nt(q, k) * decay, 0.0)
    last = cum[:, n - 1:n, :]
    return sol[:, :, 0:GD_DK], sol[:, :, GD_DK:2 * GD_DK], scores, q * ecum, k * jnp.exp(last - cum), jnp.exp(last)


def _rglru_pre(rx, halo_x, conv_w, conv_b, wr, br, wi, bi, lam):
    xc = _causal_conv(rx, halo_x, conv_w) + conv_b
    r = _sigmoid(mm_nn(xc, wr) + br)
    ig = _sigmoid(mm_nn(xc, wi) + bi)
    two = 2.0 * (-RG_C * r * _softplus(-lam))
    a = jnp.exp(0.5 * two)
    one_minus = -jnp.tanh(0.5 * two) * (jnp.exp(two) + 1.0)
    acc_a, acc_b = a, jnp.sqrt(one_minus) * (ig * xc)
    t = _iota(acc_a.shape, 0) % CHUNK
    k = 1
    while k < CHUNK:
        sa = jnp.where(t >= k, jnp.roll(acc_a, k, axis=0), 1.0)
        sb = jnp.where(t >= k, jnp.roll(acc_b, k, axis=0), 0.0)
        acc_b = acc_a * sb + acc_b
        acc_a = acc_a * sa
        k *= 2
    return acc_a, acc_b


MIXER_PARAMS = ("hg_lb", "hg_norm_w", "gd_conv_w", "gd_alog_vec", "gd_dt_vec", "gd_norm_w",
                "rg_conv_w", "rg_conv_b", "rg_wr_bd", "rg_br", "rg_wi_bd", "rg_bi", "rg_lambda")


def mixer_pre(layer, proj, halo, params):
    (hg_lb, _, gd_conv_w, alog_vec, dt_vec, _, rg_conv_w, rg_conv_b, wr, br, wi, bi, lam) = params
    hg = _hgrn2_pre(proj[:, O_HQ:O_HQ + 256], proj[:, O_HF:O_HF + 256], proj[:, O_HI:O_HI + 256], _lb_of_layer(hg_lb, layer))
    gd = _gdn_pre(proj[:, O_GQ:O_GQ + 512], proj[:, O_GK:O_GK + 512], proj[:, O_GV:O_GV + 512], proj[:, O_GBA:O_GBA + 128],
                  halo[:, O_GQ:O_GQ + 512], halo[:, O_GK:O_GK + 512], halo[:, O_GV:O_GV + 512], gd_conv_w, alog_vec, dt_vec)
    rg = _rglru_pre(proj[:, O_RX:O_RX + 256], halo[:, O_RX:O_RX + 256], rg_conv_w, rg_conv_b, wr, br, wi, bi, lam)
    return hg, gd, rg


def mixer_pre_chunk(pre, c, nb):
    hg, gd, rg = pre
    pick = lambda x: jnp.concatenate([x[h * nb + c:h * nb + c + 1] for h in range(GD_HEADS)], axis=0)
    return (tuple(x[c] for x in hg), tuple(pick(x) for x in gd), tuple(x[c * CHUNK:(c + 1) * CHUNK, :] for x in rg))


def mixer_pre_unchunk(parts, nb):
    hg = tuple(jnp.concatenate([parts[c][0][i][None] for c in range(nb)], axis=0) for i in range(len(parts[0][0])))
    gd = tuple(jnp.concatenate([parts[c][1][i][h:h + 1] for h in range(GD_HEADS) for c in range(nb)], axis=0)
               for i in range(len(parts[0][1])))
    rg = tuple(jnp.concatenate([parts[c][2][i] for c in range(nb)], axis=0) for i in range(len(parts[0][2])))
    return hg, gd, rg


def mixer_rec(pre_c, st_hg, s_gd, h_rg):
    (o_intra, qe, kd, elast, v), (u, w, scores, gqe, gkd, gel), (acc_a, acc_b) = pre_c
    o_hg = o_intra + mm_nt(qe, st_hg)
    blk = _iota((HG_W, HG_W), 0) // HG_DK == _iota((HG_W, HG_W), 1) // HG_DK
    st_new = st_hg * elast + jnp.where(blk, mm_tn(v, kd), 0.0)
    v_new = u - mm_nn(w, s_gd)
    o_gd = mm_nn(gqe, s_gd) + mm_nn(scores, v_new)
    s_new = s_gd * gel + mm_tn(gkd, v_new)
    hs = acc_b + acc_a * h_rg
    return (o_hg, o_gd, hs), (st_new, s_new, hs[CHUNK - 1:CHUNK, :])


def mixer_post(o_hg, o_gd, hs, proj, params):
    (_, hg_norm_w, _, _, _, gd_norm_w, _, _, _, _, _, _, _) = params
    blk = (_iota((HG_W, HG_W), 0) // HG_DK == _iota((HG_W, HG_W), 1) // HG_DK).astype(BF16)
    ms = _seg_sum(o_hg * o_hg, blk) * (1.0 / HG_DK)
    y_hg = o_hg * lax.rsqrt(ms + EPS) * jnp.concatenate([hg_norm_w] * HG_HEADS, axis=1) * _silu(proj[:, O_HG:O_HG + 256])
    parts = []
    for h in range(GD_HEADS):
        o = o_gd[:, GD_DK * h:GD_DK * (h + 1)]
        parts.append(o * lax.rsqrt(jnp.mean(o * o, axis=-1, keepdims=True) + EPS) * gd_norm_w)
    y_gd = jnp.concatenate(parts, axis=1) * _silu(proj[:, O_GZ:O_GZ + 512])
    y_rg = hs * _gelu_tanh(proj[:, O_RGATE:O_RGATE + 256])
    return jnp.concatenate([y_hg, y_gd, y_rg], axis=1)


def mixer_step_forward(layer, proj, halo, states, params):
    nb = proj.shape[0] // CHUNK
    pre = mixer_pre(layer, proj, halo, params)
    outs, entered = [], []
    for c in range(nb):
        entered.append(states)
        o, states = mixer_rec(mixer_pre_chunk(pre, c, nb), *states)
        outs.append(o)
    o_hg = jnp.concatenate([o[0] for o in outs], axis=0)
    o_gd = jnp.concatenate([jnp.concatenate([o[1][h] for h in range(GD_HEADS)], axis=1) for o in outs], axis=0)
    hs = jnp.concatenate([o[2] for o in outs], axis=0)
    return mixer_post(o_hg, o_gd, hs, proj, params), (o_hg, o_gd, hs), entered, states


def _full(shape):
    return pl.BlockSpec(shape, lambda *_: (0,) * len(shape))


def _token_tile(seq, want):
    return min(want, seq)


def ffn_fwd(x, nw, pack, layer, which):
    seq = x.shape[0]
    t = _token_tile(seq, 1024)
    bg, bu, bd = (_ffn_block(layer, which, p) for p in "gud")

    def body(x_ref, nw_ref, wg_ref, wu_ref, wd_ref, xo_ref, h_ref, g_ref, u_ref, a_ref, h_scr, acc):
        k = pl.program_id(1)

        @pl.when(k == 0)
        def _():
            xv = x_ref[...]
            h = _rms_fwd(xv, nw_ref[...]).astype(BF16)
            h_scr[...] = h
            h_ref[...] = h
            acc[...] = xv

        h = h_scr[...]
        g = _dot_nt(h, wg_ref[...].reshape(FFN_COLS, D_MODEL))
        u = _dot_nt(h, wu_ref[...].reshape(FFN_COLS, D_MODEL))
        a = (g * jax.nn.sigmoid(g) * u).astype(BF16)
        g_ref[...] = g.astype(BF16)
        u_ref[...] = u.astype(BF16)
        a_ref[...] = a
        acc[...] += 0.5 * _dot_nn(a, wd_ref[...].reshape(FFN_COLS, D_MODEL))

        @pl.when(k == FFN_STEPS - 1)
        def _():
            xo_ref[...] = acc[...]

    wspec = lambda blk: pl.BlockSpec((FFN_GROUP, FF_PAD, D_MODEL), lambda i, k: (k, blk, 0))
    act = pl.BlockSpec((t, FFN_COLS), lambda i, k: (i, k))
    row = pl.BlockSpec((t, D_MODEL), lambda i, k: (i, 0))
    ff = N_DEV * FF_PAD
    return pl.pallas_call(
        body, name=f"ffn_fwd_l{layer}_{which}", grid=(seq // t, FFN_STEPS),
        in_specs=[row, _full((1, D_MODEL)), wspec(bg), wspec(bu), wspec(bd)],
        out_specs=[row, row, act, act, act],
        out_shape=[jax.ShapeDtypeStruct((seq, D_MODEL), F32), jax.ShapeDtypeStruct((seq, D_MODEL), BF16)]
        + [jax.ShapeDtypeStruct((seq, ff), BF16)] * 3,
        scratch_shapes=[pltpu.VMEM((t, D_MODEL), BF16), pltpu.VMEM((t, D_MODEL), F32)],
        compiler_params=_cparams(("parallel", "arbitrary")),
    )(x, nw, pack, pack, pack)


def ffn_bwd(dout, x, nw, g, u, pack, layer, which):
    seq = x.shape[0]
    t = _token_tile(seq, 1024)
    bg, bu, bd = (_ffn_block(layer, which, p) for p in "gud")

    def body(do_ref, x_ref, nw_ref, g_ref, u_ref, wg_ref, wu_ref, wd_ref, dx_ref, dg_ref, du_ref, dnw_ref, doh, dh):
        i, k = pl.program_id(0), pl.program_id(1)

        @pl.when(k == 0)
        def _():
            doh[...] = (0.5 * do_ref[...]).astype(BF16)
            dh[...] = jnp.zeros_like(dh)

        @pl.when((k == 0) & (i == 0))
        def _():
            dnw_ref[...] = jnp.zeros_like(dnw_ref)

        da = _dot_nt(doh[...], wd_ref[...].reshape(FFN_COLS, D_MODEL))
        gv = g_ref[...].astype(F32)
        uv = u_ref[...].astype(F32)
        s = jax.nn.sigmoid(gv)
        dg = (da * uv * (s * (1.0 + gv * (1.0 - s)))).astype(BF16)
        du = (da * gv * s).astype(BF16)
        dg_ref[...] = dg
        du_ref[...] = du
        w_gu = jnp.concatenate([wg_ref[...].reshape(FFN_COLS, D_MODEL), wu_ref[...].reshape(FFN_COLS, D_MODEL)], axis=0)
        dh[...] += _dot_nn(jnp.concatenate([dg, du], axis=1), w_gu)

        @pl.when(k == FFN_STEPS - 1)
        def _():
            dx, dw = _rms_bwd(dh[...], x_ref[...], nw_ref[...])
            dx_ref[...] = do_ref[...] + dx
            dnw_ref[...] += dw

    wspec = lambda blk: pl.BlockSpec((FFN_GROUP, FF_PAD, D_MODEL), lambda i, k: (k, blk, 0))
    act = pl.BlockSpec((t, FFN_COLS), lambda i, k: (i, k))
    row = pl.BlockSpec((t, D_MODEL), lambda i, k: (i, 0))
    row_once = pl.BlockSpec((t, D_MODEL), lambda i, k: (i, 0), pipeline_mode=pl.Buffered(1))
    ff = N_DEV * FF_PAD
    return pl.pallas_call(
        body, name=f"ffn_bwd_l{layer}_{which}", grid=(seq // t, FFN_STEPS),
        in_specs=[row_once, row_once, _full((1, D_MODEL)), act, act, wspec(bg), wspec(bu), wspec(bd)],
        out_specs=[row, act, act, _full((1, D_MODEL))],
        out_shape=[jax.ShapeDtypeStruct((seq, D_MODEL), F32), jax.ShapeDtypeStruct((seq, ff), BF16),
                   jax.ShapeDtypeStruct((seq, ff), BF16), jax.ShapeDtypeStruct((1, D_MODEL), F32)],
        scratch_shapes=[pltpu.VMEM((t, D_MODEL), BF16), pltpu.VMEM((t, D_MODEL), F32)],
        compiler_params=_cparams(("arbitrary", "arbitrary")),
    )(dout, x, nw, g, u, pack, pack, pack)


def tn_matmul(a, b, scale, tm, name, gbuf=None, blk_of_m=None):
    seq, m_total = a.shape
    tk = _token_tile(seq, 1024 if gbuf is not None else 512)
    nk = seq // tk
    n_m = m_total // tm

    def body(*refs):
        if gbuf is None:
            a_ref, b_ref, o_ref = refs
            acc = o_ref
        else:
            a_ref, b_ref, _, o_ref, acc = refs
        kk = pl.program_id(0)

        @pl.when(kk == 0)
        def _():
            acc[...] = jnp.zeros_like(acc)

        bv = b_ref[...].astype(BF16)
        for m in range(n_m):
            acc[pl.ds(m * tm, tm), :] += _dot_tn(a_ref[:, pl.ds(m * tm, tm)].astype(BF16), bv)

        @pl.when(kk == nk - 1)
        def _():
            if gbuf is None:
                if scale != 1.0:
                    o_ref[...] = o_ref[...] * scale
            else:
                for m in range(n_m):
                    o_ref[_grad_slot(m)] = (acc[pl.ds(m * tm, tm), :] * scale).astype(o_ref.dtype)

    in_specs = [pl.BlockSpec((tk, m_total), lambda k: (k, 0)), pl.BlockSpec((tk, D_MODEL), lambda k: (k, 0))]
    args = [a, b]
    if gbuf is None:
        out_spec = pl.BlockSpec((m_total, D_MODEL), lambda k: (0, 0))
        out_shape = jax.ShapeDtypeStruct((m_total, D_MODEL), F32)
        aliases, scratch = {}, []
    else:
        assert n_m == N_DEV
        in_specs.append(pl.BlockSpec(memory_space=pl.ANY))
        args.append(gbuf)
        out_spec = pl.BlockSpec((N_DEV, tm, D_MODEL), lambda k: (0, blk_of_m, 0))
        out_shape = jax.ShapeDtypeStruct(gbuf.shape, gbuf.dtype)
        aliases, scratch = {2: 0}, [pltpu.VMEM((m_total, D_MODEL), F32)]
    return pl.pallas_call(
        body, name=name, grid=(nk,), in_specs=in_specs, out_specs=out_spec, out_shape=out_shape,
        scratch_shapes=scratch, input_output_aliases=aliases, compiler_params=_cparams(("arbitrary",)),
    )(*args)


def in_proj_fwd(x, nw, wt, layer):
    seq = x.shape[0]
    t = _token_tile(seq, 256)

    def body(x_ref, nw_ref, w_ref, p_ref, h_ref):
        h = _rms_fwd(x_ref[...], nw_ref[...]).astype(BF16)
        h_ref[...] = h
        p_ref[...] = _dot_nt(h, w_ref[...])

    row = pl.BlockSpec((t, D_MODEL), lambda i: (i, 0))
    return pl.pallas_call(
        body, name=f"in_proj_fwd_l{layer}", grid=(seq // t,),
        in_specs=[row, _full((1, D_MODEL)), _full((NP, D_MODEL))],
        out_specs=[pl.BlockSpec((t, NP), lambda i: (i, 0)), row],
        out_shape=[jax.ShapeDtypeStruct((seq, NP), F32), jax.ShapeDtypeStruct((seq, D_MODEL), BF16)],
        compiler_params=_cparams(("parallel",)),
    )(x, nw, wt)


def in_proj_bwd(dproj, x, nw, dres, wt, layer):
    seq = x.shape[0]
    t = _token_tile(seq, 256)

    def body(dp_ref, x_ref, nw_ref, dr_ref, w_ref, dx_ref, dnw_ref):
        @pl.when(pl.program_id(0) == 0)
        def _():
            dnw_ref[...] = jnp.zeros_like(dnw_ref)

        dh = _dot_nn(dp_ref[...], w_ref[...])
        dx, dw = _rms_bwd(dh, x_ref[...], nw_ref[...])
        dx_ref[...] = dr_ref[...] + dx
        dnw_ref[...] += dw

    row = pl.BlockSpec((t, D_MODEL), lambda i: (i, 0))
    return pl.pallas_call(
        body, name=f"in_proj_bwd_l{layer}", grid=(seq // t,),
        in_specs=[pl.BlockSpec((t, NP), lambda i: (i, 0)), row, _full((1, D_MODEL)), row, _full((NP, D_MODEL))],
        out_specs=[row, _full((1, D_MODEL))],
        out_shape=[jax.ShapeDtypeStruct((seq, D_MODEL), F32), jax.ShapeDtypeStruct((1, D_MODEL), F32)],
        compiler_params=_cparams(("arbitrary",)),
    )(dproj, x, nw, dres, wt)


def out_proj_fwd(x, y, w, layer):
    seq = x.shape[0]
    t = _token_tile(seq, 512)

    def body(x_ref, y_ref, w_ref, o_ref):
        o_ref[...] = x_ref[...] + _dot_nn(y_ref[...], w_ref[...])

    row = pl.BlockSpec((t, D_MODEL), lambda i: (i, 0))
    return pl.pallas_call(
        body, name=f"out_proj_fwd_l{layer}", grid=(seq // t,),
        in_specs=[row, row, _full((D_MODEL, D_MODEL))], out_specs=row,
        out_shape=jax.ShapeDtypeStruct((seq, D_MODEL), F32), compiler_params=_cparams(("parallel",)),
    )(x, y, w)


def out_proj_bwd(dx, w, layer):
    seq = dx.shape[0]
    t = _token_tile(seq, 512)

    def body(d_ref, w_ref, o_ref):
        o_ref[...] = _dot_nt(d_ref[...].astype(BF16), w_ref[...])

    row = pl.BlockSpec((t, D_MODEL), lambda i: (i, 0))
    return pl.pallas_call(
        body, name=f"out_proj_bwd_l{layer}", grid=(seq // t,),
        in_specs=[row, _full((D_MODEL, D_MODEL))], out_specs=row,
        out_shape=jax.ShapeDtypeStruct((seq, D_MODEL), F32), compiler_params=_cparams(("parallel",)),
    )(dx, w)


def loss_head(x, nw, target):
    seq = x.shape[0]
    t = _token_tile(seq, 512)

    def body(x_ref, nw_ref, t_ref, loss_ref, dx_ref, dnw_ref):
        @pl.when(pl.program_id(0) == 0)
        def _():
            loss_ref[...] = jnp.zeros_like(loss_ref)
            dnw_ref[...] = jnp.zeros_like(dnw_ref)

        xv, w = x_ref[...], nw_ref[...]
        err = _rms_fwd(xv, w) - t_ref[...]
        loss_ref[...] += 0.5 * jnp.sum(jnp.mean(err * err, axis=-1, keepdims=True), axis=0, keepdims=True)
        dx, dw = _rms_bwd(err * (1.0 / D_MODEL), xv, w)
        dx_ref[...] = dx
        dnw_ref[...] += dw

    row = pl.BlockSpec((t, D_MODEL), lambda i: (i, 0))
    return pl.pallas_call(
        body, name="loss_head", grid=(seq // t,),
        in_specs=[row, _full((1, D_MODEL)), row], out_specs=[_full((1, 128)), row, _full((1, D_MODEL))],
        out_shape=[jax.ShapeDtypeStruct((1, 128), F32), jax.ShapeDtypeStruct((seq, D_MODEL), F32),
                   jax.ShapeDtypeStruct((1, D_MODEL), F32)],
        compiler_params=_cparams(("arbitrary",)),
    )(x, nw, target)


_MIXER_PARAM_SHAPES = ((DEPTH, HG_W), (1, HG_DK), (CONV_W, 3 * GD_W), (1, 128), (1, 128), (1, GD_DK),
                       (CONV_W, RG_W), (1, RG_W), (RG_W, RG_W), (1, RG_W), (RG_W, RG_W), (1, RG_W), (1, RG_W))
_STATE_SHAPES = ((HG_W, HG_W), (GD_HEADS, GD_DK, GD_DK), (1, RG_W))


def _state_spec(cps, sh, index):
    return pl.BlockSpec((cps,) + sh, lambda i: (index(i),) + (0,) * len(sh))


def mixer_fwd(layer, proj, params):
    seq = proj.shape[0]
    cps = min(MIXER_CHUNKS_FWD, seq // CHUNK)
    rows = cps * CHUNK
    n = seq // rows

    def body(*refs):
        p_ref, halo_ref = refs[0], refs[1]
        prm = refs[2:15]
        y_ref, o_ref, st_out, s_out, h_out = refs[15:20]
        st, s, h = refs[20:23]
        i = pl.program_id(0)

        @pl.when(i == 0)
        def _():
            st[...] = jnp.zeros_like(st)
            s[...] = jnp.zeros_like(s)
            h[...] = jnp.zeros_like(h)

        halo = jnp.where(i > 0, halo_ref[...], 0.0)
        y, outs, entered, left = mixer_step_forward(layer, p_ref[...], halo, (st[...], s[...], h[...]),
                                                    tuple(r[...] for r in prm))
        y_ref[...] = y.astype(BF16)
        o_ref[...] = jnp.concatenate(outs, axis=1)
        for c in range(cps):
            st_out[c], s_out[c], h_out[c] = entered[c]
        st[...], s[...], h[...] = left

    in_specs = [pl.BlockSpec((rows, NP), lambda i: (i, 0)),
                pl.BlockSpec((8, NP), lambda i: (jnp.maximum(i * (rows // 8) - 1, 0), 0))]
    in_specs += [_full(sh) for sh in _MIXER_PARAM_SHAPES]
    out_specs = [pl.BlockSpec((rows, D_MODEL), lambda i: (i, 0))] * 2
    out_specs += [_state_spec(cps, sh, lambda i: i) for sh in _STATE_SHAPES]
    out_shape = [jax.ShapeDtypeStruct((seq, D_MODEL), BF16), jax.ShapeDtypeStruct((seq, D_MODEL), F32)]
    out_shape += [jax.ShapeDtypeStruct((seq // CHUNK,) + sh, F32) for sh in _STATE_SHAPES]
    return pl.pallas_call(
        body, name=f"mixer_fwd_l{layer}", grid=(n,), in_specs=in_specs, out_specs=out_specs, out_shape=out_shape,
        scratch_shapes=[pltpu.VMEM(sh, F32) for sh in _STATE_SHAPES],
        compiler_params=_cparams(("arbitrary",)),
    )(proj, proj, *params)


def mixer_bwd(layer, proj, dy, outs, states, params):
    seq = proj.shape[0]
    cps = min(MIXER_CHUNKS_BWD, seq // CHUNK)
    rows = cps * CHUNK
    n = seq // rows

    def body(*refs):
        p_ref, halo_ref, dy_ref, o_ref, st_ref, s_ref, h_ref = refs[0:7]
        prm = refs[7:20]
        dp_ref = refs[20]
        dprm = refs[21:34]
        dst, ds, dh, dhalo = refs[34:38]
        i = pl.program_id(0)
        r = n - 1 - i

        @pl.when(i == 0)
        def _():
            for ref in (dst, ds, dh, dhalo) + tuple(dprm):
                ref[...] = jnp.zeros_like(ref)

        params_v = tuple(q[...] for q in prm)
        proj_v = p_ref[...]
        halo = jnp.where(r > 0, halo_ref[...], 0.0)
        o = o_ref[...]
        _, post_vjp = jax.vjp(mixer_post, o[:, 0:HG_W], o[:, HG_W:HG_W + GD_W], o[:, HG_W + GD_W:D_MODEL], proj_v, params_v)
        d_ohg, d_ogd, d_hs, dproj_post, dparams_post = post_vjp(dy_ref[...])
        pre, pre_vjp = jax.vjp(functools.partial(mixer_pre, layer), proj_v, halo, params_v)
        dstates = (dst[...], ds[...], dh[...])
        parts = [None] * cps
        for c in reversed(range(cps)):
            rs = slice(c * CHUNK, (c + 1) * CHUNK)
            _, rec_vjp = jax.vjp(mixer_rec, mixer_pre_chunk(pre, c, cps), st_ref[c], s_ref[c], h_ref[c])
            d_ogd_c = jnp.concatenate([d_ogd[rs, GD_DK * hd:GD_DK * (hd + 1)][None] for hd in range(GD_HEADS)], axis=0)
            parts[c], dst_c, ds_c, dh_c = rec_vjp(((d_ohg[rs, :], d_ogd_c, d_hs[rs, :]), dstates))
            dstates = (dst_c, ds_c, dh_c)
        dproj_pre, dhalo_n, dparams_pre = pre_vjp(mixer_pre_unchunk(parts, cps))
        carry = jnp.concatenate([jnp.zeros((rows - 8, NP), F32), dhalo[...]], axis=0)
        dp_ref[...] = (dproj_post + dproj_pre + carry).astype(BF16)
        dst[...], ds[...], dh[...] = dstates
        dhalo[...] = dhalo_n
        for ref, a, b in zip(dprm, dparams_post, dparams_pre):
            ref[...] += a + b

    rev = lambda i: n - 1 - i
    in_specs = [pl.BlockSpec((rows, NP), lambda i: (rev(i), 0)),
                pl.BlockSpec((8, NP), lambda i: (jnp.maximum(rev(i) * (rows // 8) - 1, 0), 0)),
                pl.BlockSpec((rows, D_MODEL), lambda i: (rev(i), 0)), pl.BlockSpec((rows, D_MODEL), lambda i: (rev(i), 0))]
    in_specs += [_state_spec(cps, sh, rev) for sh in _STATE_SHAPES]
    in_specs += [_full(sh) for sh in _MIXER_PARAM_SHAPES]
    out_specs = [pl.BlockSpec((rows, NP), lambda i: (rev(i), 0))] + [_full(sh) for sh in _MIXER_PARAM_SHAPES]
    out_shape = [jax.ShapeDtypeStruct((seq, NP), BF16)] + [jax.ShapeDtypeStruct(sh, F32) for sh in _MIXER_PARAM_SHAPES]
    return pl.pallas_call(
        body, name=f"mixer_bwd_l{layer}", grid=(n,), in_specs=in_specs, out_specs=out_specs, out_shape=out_shape,
        scratch_shapes=[pltpu.VMEM(sh, F32) for sh in _STATE_SHAPES] + [pltpu.VMEM((8, NP), F32)],
        compiler_params=_cparams(("arbitrary",)),
    )(proj, proj, dy, outs, *states, *params)


def _win_segments():
    out = []
    for k in range(N_DEV):
        a, b = NIN_SHARD * k, NIN_SHARD * (k + 1)
        if a < GBA_SPLIT < b:
            out.append((k, 0, GBA_SPLIT - a, a))
            out.append((k, GBA_SPLIT - a, b - GBA_SPLIT, O_RX))
        elif b <= GBA_SPLIT:
            out.append((k, 0, NIN_SHARD, a))
        else:
            out.append((k, 0, NIN_SHARD, a + O_RX - GBA_SPLIT))
    return out


def win_to_padded(wt_shards, layer):
    lanes = 256

    def body(src, dst, scr):
        scr[...] = jnp.zeros_like(scr)
        for (k, s0, rows, d0) in _win_segments():
            scr[pl.ds(d0, rows), :] = src[k, pl.ds(s0, rows), :].astype(F32)
        dst[...] = scr[...].astype(BF16)

    return pl.pallas_call(
        body, name=f"win_to_padded_l{layer}", grid=(D_MODEL // lanes,),
        in_specs=[pl.BlockSpec((N_DEV, NIN_SHARD_PAD, lanes), lambda j: (0, 0, j))],
        out_specs=pl.BlockSpec((NP, lanes), lambda j: (0, j)),
        out_shape=jax.ShapeDtypeStruct((NP, D_MODEL), BF16),
        scratch_shapes=[pltpu.VMEM((NP, lanes), F32)], compiler_params=_cparams(("parallel",)),
    )(wt_shards)


def win_grad_to_pack(dwt, gbuf, layer):
    lanes = 256
    blk = (OFF_WIN + layer * NIN_SHARD_PAD) // NIN_SHARD_PAD

    def body(src, _, dst, scr):
        scr[...] = jnp.zeros_like(scr)
        for (k, s0, rows, d0) in _win_segments():
            scr[_grad_slot(k), pl.ds(s0, rows), :] = src[pl.ds(d0, rows), :]
        dst[...] = scr[...].astype(dst.dtype)

    return pl.pallas_call(
        body, name=f"win_grad_to_pack_l{layer}", grid=(D_MODEL // lanes,),
        in_specs=[pl.BlockSpec((NP, lanes), lambda j: (0, j)), pl.BlockSpec(memory_space=pl.ANY)],
        out_specs=pl.BlockSpec((N_DEV, NIN_SHARD_PAD, lanes), lambda j: (0, blk, j)),
        out_shape=jax.ShapeDtypeStruct(gbuf.shape, gbuf.dtype), input_output_aliases={1: 0},
        scratch_shapes=[pltpu.VMEM((N_DEV, NIN_SHARD_PAD, lanes), F32)],
        compiler_params=_cparams(("parallel",)),
    )(dwt, gbuf)


def sum_adamw(parts, w, m, v, name, rows_per_step):
    n, rows, cols = parts.shape
    tr = min(rows_per_step, rows)
    c1 = 1.0 - ADAM_B1 ** ADAM_STEP
    c2 = 1.0 - ADAM_B2 ** ADAM_STEP

    def body(p_ref, w_ref, m_ref, v_ref, g_ref, d_ref, mo_ref, vo_ref):
        g = p_ref[0].astype(F32)
        for j in range(1, n):
            g = g + p_ref[j].astype(F32)
        mn = ADAM_B1 * m_ref[...] + (1.0 - ADAM_B1) * g
        vn = ADAM_B2 * v_ref[...] + (1.0 - ADAM_B2) * (g * g)
        g_ref[...] = g
        mo_ref[...] = mn
        vo_ref[...] = vn
        d_ref[...] = -ADAM_LR * ((mn / c1) / (jnp.sqrt(vn / c2) + ADAM_EPS) + ADAM_WD * w_ref[...])

    blk = pl.BlockSpec((tr, cols), lambda i: (i, 0))
    return pl.pallas_call(
        body, name=name, grid=(rows // tr,),
        in_specs=[pl.BlockSpec((n, tr, cols), lambda i: (0, i, 0)), blk, blk, blk], out_specs=[blk] * 4,
        out_shape=[jax.ShapeDtypeStruct((rows, cols), F32)] * 4, compiler_params=_cparams(("parallel",)),
    )(parts, w, m, v)


def gather_two_level(arrays, name):
    n = len(arrays)

    def body(*refs):
        ins, outs = refs[:n], refs[n:2 * n]
        send_sems, recv_sems, local_sems = refs[2 * n:]
        x, y, c = lax.axis_index("x"), lax.axis_index("y"), lax.axis_index("c")
        me = 4 * x + 2 * y + c
        sib_slot = 4 * x + 2 * y + (1 - c)
        chips = [(1 - x, y), (x, 1 - y), (1 - x, 1 - y)]

        def copy(a, k, slot, to, src=None):
            return pltpu.make_async_remote_copy(
                src_ref=outs[a].at[slot] if src is None else src, dst_ref=outs[a].at[slot],
                send_sem=send_sems.at[a * 7 + k], recv_sem=recv_sems.at[a * 7 + k],
                device_id=to, device_id_type=pl.DeviceIdType.MESH)

        local, started = [], []
        for a in range(n):
            cp = pltpu.make_async_copy(ins[a], outs[a].at[me], local_sems.at[a])
            cp.start()
            local.append(cp)
            first = [copy(a, 0, me, (x, y, 1 - c), src=ins[a])]
            first += [copy(a, 1 + j, me, (px, py, c), src=ins[a]) for j, (px, py) in enumerate(chips)]
            for cp in first:
                cp.start()
            started += first
        for a in range(n):
            for j, (px, py) in enumerate(chips):
                slot = 4 * px + 2 * py + c
                copy(a, 1 + j, slot, (x, y, c)).wait_recv()
                fwd = copy(a, 4 + j, slot, (x, y, 1 - c))
                fwd.start()
                started.append(fwd)
        for a in range(n):
            copy(a, 0, sib_slot, (x, y, c)).wait_recv()
            for j, (px, py) in enumerate(chips):
                copy(a, 4 + j, 4 * px + 2 * py + (1 - c), (x, y, c)).wait_recv()
        for cp in started:
            cp.wait_send()
        for cp in local:
            cp.wait()

    out_shape = [jax.ShapeDtypeStruct((N_DEV,) + a.shape, a.dtype) for a in arrays]
    any_spec = pl.BlockSpec(memory_space=pl.ANY)
    return pl.pallas_call(
        body, name=name, in_specs=[any_spec] * n, out_specs=[any_spec] * n, out_shape=out_shape,
        scratch_shapes=[pltpu.SemaphoreType.DMA((7 * n,)), pltpu.SemaphoreType.DMA((7 * n,)), pltpu.SemaphoreType.DMA((n,))],
        compiler_params=pltpu.CompilerParams(has_side_effects=True),
    )(*arrays)


def _grad_slot(d):
    return (d % 2) * (N_DEV // 2) + d // 2


def sibling_exchange(gbuf, name):
    half = N_DEV // 2

    def body(g_ref, o_ref, send_sem, recv_sem):
        x, y, c = lax.axis_index("x"), lax.axis_index("y"), lax.axis_index("c")
        cp = pltpu.make_async_remote_copy(
            src_ref=g_ref.at[pl.ds((1 - c) * half, half)], dst_ref=o_ref, send_sem=send_sem, recv_sem=recv_sem,
            device_id=(x, y, 1 - c), device_id_type=pl.DeviceIdType.MESH)
        cp.start()
        cp.wait()

    any_spec = pl.BlockSpec(memory_space=pl.ANY)
    return pl.pallas_call(
        body, name=name, in_specs=[any_spec], out_specs=any_spec,
        out_shape=jax.ShapeDtypeStruct((half,) + gbuf.shape[1:], gbuf.dtype),
        scratch_shapes=[pltpu.SemaphoreType.DMA, pltpu.SemaphoreType.DMA],
        compiler_params=pltpu.CompilerParams(has_side_effects=True),
    )(gbuf)


def pair_sum(a, b, name):
    n, rows, cols = a.shape
    tr = 256

    def body(a_ref, b_ref, o_ref):
        o_ref[...] = (a_ref[...].astype(F32) + b_ref[...].astype(F32)).astype(o_ref.dtype)

    blk = pl.BlockSpec((None, tr, cols), lambda i, j: (i, j, 0))
    return pl.pallas_call(
        body, name=name, grid=(n, rows // tr), in_specs=[blk, blk], out_specs=blk,
        out_shape=jax.ShapeDtypeStruct(a.shape, a.dtype), compiler_params=_cparams(("parallel", "parallel")),
    )(a, b)


def chip_exchange(p, name):
    half = N_DEV // 2

    def body(p_ref, o_ref, send_sems, recv_sems, local_sem):
        x, y, c = lax.axis_index("x"), lax.axis_index("y"), lax.axis_index("c")
        q = 2 * x + y
        own = pltpu.make_async_copy(p_ref.at[q], o_ref.at[q], local_sem)
        own.start()
        pending = []
        for j in range(1, half):
            px = 1 - x if j & 2 else x
            py = 1 - y if j & 1 else y
            pq = 2 * px + py

            def copy(dst_slot):
                return pltpu.make_async_remote_copy(
                    src_ref=p_ref.at[pq], dst_ref=o_ref.at[dst_slot], send_sem=send_sems.at[j - 1],
                    recv_sem=recv_sems.at[j - 1], device_id=(px, py, c), device_id_type=pl.DeviceIdType.MESH)

            send = copy(q)
            send.start()
            pending.append((send, copy(pq)))
        for send, recv in pending:
            send.wait_send()
            recv.wait_recv()
        own.wait()

    any_spec = pl.BlockSpec(memory_space=pl.ANY)
    return pl.pallas_call(
        body, name=name, in_specs=[any_spec], out_specs=any_spec, out_shape=jax.ShapeDtypeStruct(p.shape, p.dtype),
        scratch_shapes=[pltpu.SemaphoreType.DMA((half - 1,)), pltpu.SemaphoreType.DMA((half - 1,)), pltpu.SemaphoreType.DMA],
        compiler_params=pltpu.CompilerParams(has_side_effects=True),
    )(p)


BIG = ("ffn1_gate", "ffn1_up", "ffn1_down", "w_in", "w_out", "ffn2_gate", "ffn2_up", "ffn2_down")
SMALL_REPLICATED = (("norm_ffn1", (DEPTH, D_MODEL)), ("norm_mix", (DEPTH, D_MODEL)), ("norm_ffn2", (DEPTH, D_MODEL)),
                    ("norm_final", (D_MODEL,)), ("hg_lb", (DEPTH, HG_W)), ("hg_norm_w", (DEPTH, HG_DK)),
                    ("gd_a_log", (DEPTH, GD_HEADS)), ("gd_dt_bias", (DEPTH, GD_HEADS)), ("gd_norm_w", (DEPTH, GD_DK)),
                    ("rg_conv_b", (DEPTH, RG_W)), ("rg_wr", (DEPTH, RG_BLOCKS, RG_BD, RG_BD)), ("rg_br", (DEPTH, RG_W)),
                    ("rg_wi", (DEPTH, RG_BLOCKS, RG_BD, RG_BD)), ("rg_bi", (DEPTH, RG_W)), ("rg_lambda", (DEPTH, RG_W)))
SMALL_EXTRA = (("gd_conv_w", (DEPTH, CONV_W, 3 * GD_W)), ("rg_conv_w", (DEPTH, CONV_W, RG_W)), ("loss", (1,)))
SMALL_SHARDED = (("gd_conv_w", (DEPTH, CONV_W, GDC_SHARD)), ("rg_conv_w", (DEPTH, CONV_W, RGC_SHARD)))


def _pad_rows(a, rows):
    return jnp.pad(a, ((0, rows - a.shape[0]), (0, 0)))


def pack_big(get):
    pieces = []
    for l in range(DEPTH):
        for which in (1, 2):
            pieces.append(_pad_rows(get(f"ffn{which}_gate")[l].T, FF_PAD))
            pieces.append(_pad_rows(get(f"ffn{which}_up")[l].T, FF_PAD))
            pieces.append(_pad_rows(get(f"ffn{which}_down")[l], FF_PAD))
    for l in range(DEPTH):
        pieces.append(_pad_rows(get("w_in")[l].T, NIN_SHARD_PAD))
    for l in range(DEPTH):
        pieces.append(get("w_out")[l])
    return jnp.concatenate(pieces, axis=0)


def unpack_big(p):
    out = {}
    for which in (1, 2):
        for part, nm in (("g", "gate"), ("u", "up"), ("d", "down")):
            per_layer = []
            for l in range(DEPTH):
                r0 = _ffn_block(l, which, part) * FF_PAD
                blk = p[r0:r0 + FF_SHARD, :]
                per_layer.append(blk if part == "d" else blk.T)
            out[f"ffn{which}_{nm}"] = jnp.stack(per_layer)
    out["w_in"] = jnp.stack([p[OFF_WIN + NIN_SHARD_PAD * l:OFF_WIN + NIN_SHARD_PAD * l + NIN_SHARD, :].T for l in range(DEPTH)])
    out["w_out"] = jnp.stack([p[OFF_WOUT + WOUT_SHARD * l:OFF_WOUT + WOUT_SHARD * (l + 1), :] for l in range(DEPTH)])
    return out


def pack_small(get, spec):
    flat = []
    for name, shape in spec:
        a = get(name)
        flat.append(jnp.zeros((math.prod(shape),), F32) if a is None else a.reshape(-1).astype(F32))
    v = jnp.concatenate(flat)
    total = -(-v.shape[0] // 1024) * 1024
    return jnp.pad(v, (0, total - v.shape[0])).reshape(total // 128, 128)


def unpack_small(p, spec):
    v = p.reshape(-1)
    out, off = {}, 0
    for name, shape in spec:
        size = math.prod(shape)
        out[name] = v[off:off + size].reshape(shape)
        off += size
    return out


def _block_diag(w):
    rows = []
    for i in range(RG_BLOCKS):
        rows.append(jnp.concatenate([w[i] if j == i else jnp.zeros((RG_BD, RG_BD), F32) for j in range(RG_BLOCKS)], axis=1))
    return jnp.concatenate(rows, axis=0)


def _diag_blocks(w):
    return jnp.stack([w[RG_BD * i:RG_BD * (i + 1), RG_BD * i:RG_BD * (i + 1)] for i in range(RG_BLOCKS)])


def _lane_vec(v):
    return jnp.pad(v.astype(F32), (GD_HEADS, 128 - 2 * GD_HEADS))[None]


INPUT_NAMES = ("x", "norm_ffn1", "ffn1_gate", "ffn1_up", "ffn1_down", "norm_mix", "w_in", "hg_lb", "hg_norm_w",
               "gd_conv_w", "gd_a_log", "gd_dt_bias", "gd_norm_w", "rg_conv_w", "rg_conv_b", "rg_wr", "rg_br", "rg_wi",
               "rg_bi", "rg_lambda", "w_out", "norm_ffn2", "ffn2_gate", "ffn2_up", "ffn2_down", "norm_final")
WEIGHT_NAMES = INPUT_NAMES[1:]


def _step(a):
    x = a["x"][0]
    target = a["loss_target"][0]
    me = 4 * lax.axis_index("x") + 2 * lax.axis_index("y") + lax.axis_index("c")

    w_pack = pack_big(lambda nm: a[nm])
    conv_local = pack_small(lambda nm: a[nm], SMALL_SHARDED)
    gw, gconv = gather_two_level([w_pack.astype(BF16), conv_local], "gather_weights")
    conv_parts = [unpack_small(gconv[d], SMALL_SHARDED) for d in range(N_DEV)]
    gd_conv_w = jnp.concatenate([p["gd_conv_w"] for p in conv_parts], axis=-1)
    rg_conv_w = jnp.concatenate([p["rg_conv_w"] for p in conv_parts], axis=-1)

    def mixer_params(l):
        return (a["hg_lb"], a["hg_norm_w"][l][None], gd_conv_w[l], _lane_vec(a["gd_a_log"][l]), _lane_vec(a["gd_dt_bias"][l]),
                a["gd_norm_w"][l][None], rg_conv_w[l], a["rg_conv_b"][l][None], _block_diag(a["rg_wr"][l]), a["rg_br"][l][None],
                _block_diag(a["rg_wi"][l]), a["rg_bi"][l][None], a["rg_lambda"][l][None])

    saved = []
    xs = x
    for l in range(DEPTH):
        x1, h1, g1, u1, a1 = ffn_fwd(xs, a["norm_ffn1"][l][None], gw, l, 1)
        wt = win_to_padded(gw[:, OFF_WIN + NIN_SHARD_PAD * l:OFF_WIN + NIN_SHARD_PAD * (l + 1), :], l)
        proj, h2 = in_proj_fwd(x1, a["norm_mix"][l][None], wt, l)
        prm = mixer_params(l)
        y, o_mix, st, s, h = mixer_fwd(l, proj, prm)
        wo = gw[:, OFF_WOUT + WOUT_SHARD * l:OFF_WOUT + WOUT_SHARD * (l + 1), :].reshape(D_MODEL, D_MODEL)
        x2 = out_proj_fwd(x1, y, wo, l)
        x3, h3, g3, u3, a3 = ffn_fwd(x2, a["norm_ffn2"][l][None], gw, l, 2)
        saved.append(dict(x0=xs, x1=x1, x2=x2, h1=h1, g1=g1, u1=u1, a1=a1, wt=wt, proj=proj, h2=h2, prm=prm, y=y, o_mix=o_mix,
                          states=(st, s, h), wo=wo, h3=h3, g3=g3, u3=u3, a3=a3))
        xs = x3
    loss_row, dx, d_norm_final = loss_head(xs, a["norm_final"][None], target)

    gbuf = jnp.zeros((N_DEV, PACK_ROWS, D_MODEL), BF16)
    sg = {"norm_final": d_norm_final[0], "loss": loss_row[0, 0:1]}
    per_layer = {nm: [None] * DEPTH for nm in ("norm_ffn1", "norm_mix", "norm_ffn2", "hg_norm_w", "gd_conv_w", "gd_a_log",
                                                "gd_dt_bias", "gd_norm_w", "rg_conv_w", "rg_conv_b", "rg_wr", "rg_br", "rg_wi",
                                                "rg_bi", "rg_lambda")}
    d_hg_lb = jnp.zeros((DEPTH, HG_W), F32)
    for l in reversed(range(DEPTH)):
        sv = saved[l]
        dx2, dg, du, dn2 = ffn_bwd(dx, sv["x2"], a["norm_ffn2"][l][None], sv["g3"], sv["u3"], gw, l, 2)
        gbuf = tn_matmul(dg, sv["h3"], 1.0, FF_PAD, f"dw_gate_l{l}_2", gbuf, _ffn_block(l, 2, "g"))
        gbuf = tn_matmul(du, sv["h3"], 1.0, FF_PAD, f"dw_up_l{l}_2", gbuf, _ffn_block(l, 2, "u"))
        gbuf = tn_matmul(sv["a3"], dx, 0.5, FF_PAD, f"dw_down_l{l}_2", gbuf, _ffn_block(l, 2, "d"))
        dy = out_proj_bwd(dx2, sv["wo"], l)
        gbuf = tn_matmul(sv["y"], dx2, 1.0, WOUT_SHARD, f"dw_out_l{l}", gbuf, (OFF_WOUT + WOUT_SHARD * l) // WOUT_SHARD)
        mb = mixer_bwd(l, sv["proj"], dy, sv["o_mix"], sv["states"], sv["prm"])
        dproj, dprm = mb[0], mb[1:]
        dwt = tn_matmul(dproj, sv["h2"], 1.0, FF_PAD, f"dw_in_l{l}")
        gbuf = win_grad_to_pack(dwt, gbuf, l)
        dx1, dnm = in_proj_bwd(dproj, sv["x1"], a["norm_mix"][l][None], dx2, sv["wt"], l)
        dx, dg, du, dn1 = ffn_bwd(dx1, sv["x0"], a["norm_ffn1"][l][None], sv["g1"], sv["u1"], gw, l, 1)
        gbuf = tn_matmul(dg, sv["h1"], 1.0, FF_PAD, f"dw_gate_l{l}_1", gbuf, _ffn_block(l, 1, "g"))
        gbuf = tn_matmul(du, sv["h1"], 1.0, FF_PAD, f"dw_up_l{l}_1", gbuf, _ffn_block(l, 1, "u"))
        gbuf = tn_matmul(sv["a1"], dx1, 0.5, FF_PAD, f"dw_down_l{l}_1", gbuf, _ffn_block(l, 1, "d"))
        (g_lb, g_hnw, g_gcw, g_alog, g_dt, g_gnw, g_rcw, g_rcb, g_wr, g_br, g_wi, g_bi, g_lam) = dprm
        d_hg_lb = d_hg_lb + g_lb
        for nm, val in (("norm_ffn1", dn1[0]), ("norm_mix", dnm[0]), ("norm_ffn2", dn2[0]), ("hg_norm_w", g_hnw[0]),
                        ("gd_conv_w", g_gcw), ("gd_a_log", g_alog[0, GD_HEADS:2 * GD_HEADS]),
                        ("gd_dt_bias", g_dt[0, GD_HEADS:2 * GD_HEADS]), ("gd_norm_w", g_gnw[0]), ("rg_conv_w", g_rcw),
                        ("rg_conv_b", g_rcb[0]), ("rg_wr", _diag_blocks(g_wr)), ("rg_br", g_br[0]),
                        ("rg_wi", _diag_blocks(g_wi)), ("rg_bi", g_bi[0]), ("rg_lambda", g_lam[0])):
            per_layer[nm][l] = val
    grad_x = dx
    for nm, vals in per_layer.items():
        sg[nm] = jnp.stack(vals)
    sg["hg_lb"] = d_hg_lb

    small_spec = SMALL_REPLICATED + SMALL_EXTRA
    from_sibling = sibling_exchange(gbuf, "grads_to_sibling")
    mine = lax.dynamic_slice_in_dim(gbuf, lax.axis_index("c") * (N_DEV // 2), N_DEV // 2, axis=0)
    parts_big = chip_exchange(pair_sum(mine, from_sibling, "grads_pair_sum"), "grads_to_chips")
    g_big, d_big, m_big, v_big = sum_adamw(parts_big, w_pack, pack_big(lambda nm: a["m_" + nm]),
                                           pack_big(lambda nm: a["v_" + nm]), "adamw_big", 128)
    (parts_small,) = gather_two_level([pack_small(lambda nm: sg[nm], small_spec)], "gather_small_grads")
    repl = dict(SMALL_REPLICATED)
    small_in = lambda pre: pack_small(lambda nm: a[pre + nm] if nm in repl else None, small_spec)
    rows_small = parts_small.shape[1]
    g_sm, d_sm, m_sm, v_sm = sum_adamw(parts_small, small_in(""), small_in("m_"), small_in("v_"), "adamw_small", rows_small)
    g_small = unpack_small(g_sm, small_spec)
    conv_grads = {"gd_conv_w": lax.dynamic_slice_in_dim(g_small["gd_conv_w"], me * GDC_SHARD, GDC_SHARD, axis=2),
                  "rg_conv_w": lax.dynamic_slice_in_dim(g_small["rg_conv_w"], me * RGC_SHARD, RGC_SHARD, axis=2)}
    conv_in = lambda pre: pack_small(lambda nm: a[pre + nm], SMALL_SHARDED)
    g_cv, d_cv, m_cv, v_cv = sum_adamw(pack_small(lambda nm: conv_grads[nm], SMALL_SHARDED)[None], conv_in(""),
                                       conv_in("m_"), conv_in("v_"), "adamw_conv", conv_local.shape[0])

    results = []
    for big, small, conv in ((g_big, g_sm, g_cv), (d_big, d_sm, d_cv), (m_big, m_sm, m_cv), (v_big, v_sm, v_cv)):
        vals = unpack_big(big)
        vals.update({k: v for k, v in unpack_small(small, small_spec).items() if k in repl})
        vals.update(unpack_small(conv, SMALL_SHARDED))
        results.append(vals)
    loss = g_small["loss"][0]
    out = [loss, grad_x[None]]
    for vals in results:
        out.extend(vals[nm] for nm in WEIGHT_NAMES)
    return tuple(out)


def kernel(x, norm_ffn1, ffn1_gate, ffn1_up, ffn1_down, norm_mix, w_in, hg_lb, hg_norm_w, gd_conv_w, gd_a_log, gd_dt_bias, gd_norm_w, rg_conv_w, rg_conv_b, rg_wr, rg_br, rg_wi, rg_bi, rg_lambda, w_out, norm_ffn2, ffn2_gate, ffn2_up, ffn2_down, norm_final, loss_target, m_norm_ffn1, m_ffn1_gate, m_ffn1_up, m_ffn1_down, m_norm_mix, m_w_in, m_hg_lb, m_hg_norm_w, m_gd_conv_w, m_gd_a_log, m_gd_dt_bias, m_gd_norm_w, m_rg_conv_w, m_rg_conv_b, m_rg_wr, m_rg_br, m_rg_wi, m_rg_bi, m_rg_lambda, m_w_out, m_norm_ffn2, m_ffn2_gate, m_ffn2_up, m_ffn2_down, m_norm_final, v_norm_ffn1, v_ffn1_gate, v_ffn1_up, v_ffn1_down, v_norm_mix, v_w_in, v_hg_lb, v_hg_norm_w, v_gd_conv_w, v_gd_a_log, v_gd_dt_bias, v_gd_norm_w, v_rg_conv_w, v_rg_conv_b, v_rg_wr, v_rg_br, v_rg_wi, v_rg_bi, v_rg_lambda, v_w_out, v_norm_ffn2, v_ffn2_gate, v_ffn2_up, v_ffn2_down, v_norm_final):
    args = locals()
    return _step(dict(args))
```

```python
import functools
import math

import jax
import jax.numpy as jnp
from jax import lax
from jax.experimental import pallas as pl
from jax.experimental.pallas import tpu as pltpu

F32 = jnp.float32
BF16 = jnp.bfloat16

D_MODEL = 1024
DEPTH = 2
D_FF = 2816
HG_HEADS, HG_DK = 4, 64
HG_W = 256
GD_HEADS, GD_DK = 4, 128
GD_W = 512
RG_W = 256
RG_BLOCKS, RG_BD = 4, 64
RG_C = 8.0
CONV_W = 4
CHUNK = 64
EPS = 1e-6
N_IN = 3592
ADAM_LR, ADAM_B1, ADAM_B2, ADAM_EPS, ADAM_WD, ADAM_STEP = 0.001, 0.9, 0.999, 1e-08, 0.01, 10

N_DEV = 8
FF_SHARD = D_FF // N_DEV
FF_PAD = 384
NIN_SHARD = N_IN // N_DEV
NIN_SHARD_PAD = 512
WOUT_SHARD = D_MODEL // N_DEV
GDC_SHARD = (2 * 512 + 512) // N_DEV
RGC_SHARD = RG_W // N_DEV

O_HQ, O_HF, O_HI, O_HG = 0, 256, 512, 768
O_GQ, O_GK, O_GV, O_GZ = 1024, 1536, 2048, 2560
O_GBA = 3072
O_RX, O_RGATE = 3200, 3456
NP = 3840
GBA_SPLIT = 3080

N_FFN_BLOCKS = DEPTH * 2 * 3
OFF_WIN = N_FFN_BLOCKS * FF_PAD
OFF_WOUT = OFF_WIN + DEPTH * NIN_SHARD_PAD
PACK_ROWS = OFF_WOUT + DEPTH * WOUT_SHARD

VMEM_LIMIT = 56 * 1024 * 1024
FFN_GROUP = 2
FFN_STEPS = N_DEV // FFN_GROUP
FFN_COLS = FFN_GROUP * FF_PAD
MIXER_CHUNKS_FWD = 4
MIXER_CHUNKS_BWD = 4
MIXER_BWD_VMEM_LIMIT = 63 * 1024 * 1024


def _ffn_block(layer, which, part):
    return layer * 6 + (which - 1) * 3 + {"g": 0, "u": 1, "d": 2}[part]


def _cparams(sem, **kw):
    return pltpu.CompilerParams(dimension_semantics=sem, vmem_limit_bytes=VMEM_LIMIT, **kw)


def _dot_nn(a, b):
    return lax.dot_general(a, b, (((1,), (0,)), ((), ())), preferred_element_type=F32)


def _dot_nt(a, b):
    return lax.dot_general(a, b, (((1,), (1,)), ((), ())), preferred_element_type=F32)


def _dot_tn(a, b):
    return lax.dot_general(a, b, (((0,), (0,)), ((), ())), preferred_element_type=F32)


def _split3(x):
    hi = x.astype(BF16)
    r = x - hi.astype(F32)
    mid = r.astype(BF16)
    lo = (r - mid.astype(F32)).astype(BF16)
    return hi, mid, lo


def _split2(x):
    hi = x.astype(BF16)
    return hi, (x - hi.astype(F32)).astype(BF16)


def _iota(shape, dim):
    return lax.broadcasted_iota(jnp.int32, shape, dim)


def _sigmoid(x):
    return 0.5 * (jnp.tanh(0.5 * x) + 1.0)


def _logistic(x):
    return jax.nn.sigmoid(x)


def _silu(x):
    return x * _sigmoid(x)


def _softplus(x):
    return jnp.maximum(x, 0.0) + jnp.log(1.0 + jnp.exp(-jnp.abs(x)))


def _gelu_tanh(x):
    return 0.5 * x * (1.0 + jnp.tanh(math.sqrt(2.0 / math.pi) * (x + 0.044715 * x * x * x)))


def _rms_fwd(x, w):
    rstd = lax.rsqrt(jnp.mean(x * x, axis=-1, keepdims=True) + EPS)
    return x * rstd * w


def _rms_bwd(dh, x, w):
    rstd = lax.rsqrt(jnp.mean(x * x, axis=-1, keepdims=True) + EPS)
    xhat = x * rstd
    dxhat = dh * w
    dx = rstd * (dxhat - xhat * jnp.mean(dxhat * xhat, axis=-1, keepdims=True))
    return dx, jnp.sum(dh * xhat, axis=0, keepdims=True)


def _rank_dims(kind, ndim):
    if ndim == 2:
        return {"nn": (((1,), (0,)), ((), ())), "nt": (((1,), (1,)), ((), ())), "tn": (((0,), (0,)), ((), ()))}[kind]
    return {"nn": (((2,), (1,)), ((0,), (0,))), "nt": (((2,), (2,)), ((0,), (0,))), "tn": (((1,), (1,)), ((0,), (0,)))}[kind]


def _rdot(kind, a, b):
    return lax.dot_general(a, b, _rank_dims(kind, a.ndim), preferred_element_type=F32)


def _make_mm(kind, kind_da, kind_db, swap_da, swap_db):
    @jax.custom_vjp
    def mm(a, b):
        return _rdot(kind, a.astype(BF16), b.astype(BF16))

    def fwd(a, b):
        a, b = a.astype(BF16), b.astype(BF16)
        return _rdot(kind, a, b), (a, b)

    def bwd(res, g):
        a, b = res
        g = g.astype(BF16)
        da = _rdot(kind_da, b, g) if swap_da else _rdot(kind_da, g, b)
        db = _rdot(kind_db, g, a) if swap_db else _rdot(kind_db, a, g)
        return da, db

    mm.defvjp(fwd, bwd)
    return mm


mm_nn = _make_mm("nn", "nt", "tn", False, False)
mm_nt = _make_mm("nt", "nn", "tn", False, True)
mm_tn = _make_mm("tn", "nt", "nn", True, False)


def _bcast_const(mat, like):
    return mat if like.ndim == 2 else jnp.broadcast_to(mat, (like.shape[0],) + mat.shape)


def _sel_apply(kind, sel, x):
    hi, mid, lo = _split3(x)
    s = _bcast_const(sel, x)
    return _rdot(kind, s, hi) + _rdot(kind, s, mid) + _rdot(kind, s, lo)


@jax.custom_vjp
def sel_mm(sel, x):
    return _sel_apply("nn", sel, x)


def _sel_mm_fwd(sel, x):
    return _sel_apply("nn", sel, x), sel


def _sel_mm_bwd(sel, g):
    return jnp.zeros_like(sel), _sel_apply("tn", sel, g)


sel_mm.defvjp(_sel_mm_fwd, _sel_mm_bwd)


def _transpose_apply(a):
    n = a.shape[-1]
    eye = (_iota((n, n), 0) == _iota((n, n), 1)).astype(BF16)
    hi, mid, lo = _split3(a)
    e = _bcast_const(eye, a)
    return _rdot("nt", e, hi) + _rdot("nt", e, mid) + _rdot("nt", e, lo)


@jax.custom_vjp
def exact_transpose(a):
    return _transpose_apply(a)


exact_transpose.defvjp(lambda a: (_transpose_apply(a), None), lambda _, g: (_transpose_apply(g),))


def _dot3(kind, a, b):
    ah, al = _split2(a)
    bh, bl = _split2(b)
    return _rdot(kind, ah, bh) + _rdot(kind, ah, bl) + _rdot(kind, al, bh)


def _inverse_levels(lower, n):
    nd = lower.ndim
    r, c = _iota(lower.shape, nd - 2), _iota(lower.shape, nd - 1)
    tinv = (r == c).astype(F32)
    b = 1
    while b < n:
        off = (r // (2 * b) == c // (2 * b)) & (r % (2 * b) >= b) & (c % (2 * b) < b)
        tinv = tinv - _dot3("nn", tinv, _dot3("nn", jnp.where(off, lower, 0.0), tinv))
        b *= 2
    return tinv


@functools.partial(jax.custom_vjp, nondiff_argnums=(1,))
def unit_lower_inverse(lower, n):
    return _inverse_levels(lower, n)


def _unit_lower_inverse_fwd(lower, n):
    tinv = _inverse_levels(lower, n)
    return tinv, tinv


def _unit_lower_inverse_bwd(n, tinv, g):
    return (-_dot3("nt", _dot3("tn", tinv, g), tinv),)


unit_lower_inverse.defvjp(_unit_lower_inverse_fwd, _unit_lower_inverse_bwd)


def mm3_nn(a, b):
    ah = a.astype(BF16).astype(F32)
    al = a - ah
    bh = b.astype(BF16).astype(F32)
    bl = b - bh
    return mm_nn(ah, bh) + mm_nn(ah, bl) + mm_nn(al, bh)


def _seg_apply(x, b):
    hi, mid, lo = _split3(x)
    return _rdot("nn", hi, b) + _rdot("nn", mid, b) + _rdot("nn", lo, b)


@jax.custom_vjp
def _seg_sum(x, b):
    return _seg_apply(x, b)


_seg_sum.defvjp(lambda x, b: (_seg_apply(x, b), b), lambda b, g: (_seg_apply(g, b), jnp.zeros_like(b)))


def _shift_rows(x, halo, k):
    n = x.shape[0]
    ext = jnp.concatenate([jnp.zeros((n - halo.shape[0], x.shape[1]), x.dtype), halo], axis=0)
    t = _iota(x.shape, 0)
    return jnp.where(t >= k, jnp.roll(x, k, axis=0), jnp.roll(ext, k, axis=0))


def _causal_conv(x, halo, w):
    y = x * w[3:4, :]
    for k in range(1, CONV_W):
        y = y + _shift_rows(x, halo, k) * w[3 - k:4 - k, :]
    return y


def _lb_of_layer(hg_lb, layer):
    if layer == 0:
        return jnp.zeros((1, HG_W), F32)
    e = jnp.exp(hg_lb - jnp.max(hg_lb, axis=0, keepdims=True))
    sm = e / jnp.sum(e, axis=0, keepdims=True)
    lb = sm[1:2, :]
    for l in range(2, layer + 1):
        lb = lb + sm[l:l + 1, :]
    return lb


def _chunks(x):
    return x.reshape(x.shape[0] // CHUNK, CHUNK, x.shape[1])


def _hgrn2_pre(hq, hf, hi, lb):
    n, w = CHUNK, HG_W
    q = _chunks(_silu(hq) * (HG_DK ** -0.5))
    f = lb + (1.0 - lb) * _logistic(hf)
    k = _chunks(1.0 - f)
    v = _chunks(hi)
    nb = q.shape[0]
    row, col = _iota((n, n), 0), _iota((n, n), 1)
    levels = [n >> j for j in range(1, n.bit_length())]
    picks = [col <= row] + [col <= (row // (2 * b)) * (2 * b) + b - 1 for b in levels]
    cums = sel_mm(jnp.concatenate(picks, axis=0).astype(BF16), _chunks(jnp.log(f)))
    cum = cums[:, 0:n, :]
    lane_head = _iota((nb, n, w), 2) // HG_DK

    def stack(x):
        return jnp.concatenate([jnp.where(lane_head == h, x, 0.0) for h in range(HG_HEADS)], axis=1)

    t_idx = _iota((nb, n, HG_HEADS * n), 1)
    s_idx = _iota((nb, n, HG_HEADS * n), 2) % n
    p = jnp.where(t_idx == s_idx, mm_nt(q, stack(k)), 0.0)
    for j, b in enumerate(levels):
        ref = cums[:, (j + 1) * n:(j + 2) * n, :]
        qe = q * jnp.exp(jnp.minimum(cum - ref, 0.0))
        ke = k * jnp.exp(jnp.minimum(ref - cum, 0.0))
        mask = (t_idx // (2 * b) == s_idx // (2 * b)) & (t_idx % (2 * b) >= b) & (s_idx % (2 * b) < b)
        p = p + jnp.where(mask, mm_nt(qe, stack(ke)), 0.0)
    last = cum[:, n - 1:n, :]
    return mm_nn(p, stack(v)), q * jnp.exp(cum), k * jnp.exp(last - cum), jnp.exp(last), v


def _gdn_pre(gq, gk, gv, gba, halo_q, halo_k, halo_v, conv_w, alog_vec, dt_vec):
    n = CHUNK
    cq = _silu(_causal_conv(gq, halo_q, conv_w[:, 0:GD_W]))
    ck = _silu(_causal_conv(gk, halo_k, conv_w[:, GD_W:2 * GD_W]))
    cv = _silu(_causal_conv(gv, halo_v, conv_w[:, 2 * GD_W:3 * GD_W]))

    def heads(x):
        return jnp.concatenate([_chunks(x[:, GD_DK * h:GD_DK * (h + 1)]) for h in range(GD_HEADS)], axis=0)

    def l2n(x):
        return x * lax.rsqrt(jnp.sum(x * x, axis=-1, keepdims=True) + EPS)

    q = l2n(heads(cq)) * (GD_DK ** -0.5)
    k = l2n(heads(ck))
    v = heads(cv)
    beta_full = _chunks(_sigmoid(gba))
    g_full = _chunks(-jnp.exp(alog_vec) * _softplus(gba + dt_vec))
    tri = (_iota((n, n), 0) >= _iota((n, n), 1)).astype(BF16)
    cum_full = sel_mm(tri, g_full)
    beta = jnp.concatenate([beta_full[:, :, h:h + 1] for h in range(GD_HEADS)], axis=0)
    cum = jnp.concatenate([cum_full[:, :, GD_HEADS + h:GD_HEADS + h + 1] for h in range(GD_HEADS)], axis=0)
    bsz = cum.shape[0]
    ccol = jnp.broadcast_to(cum, (bsz, n, n))
    diff = ccol - exact_transpose(ccol)
    r_i, c_i = _iota((bsz, n, n), 1), _iota((bsz, n, n), 2)
    decay = jnp.exp(jnp.minimum(diff, 0.0))
    kb = k * beta
    lower = jnp.where(r_i > c_i, mm_nt(kb, k) * decay, 0.0)
    tinv = unit_lower_inverse(lower, n)
    ecum = jnp.exp(cum)
    sol = mm3_nn(tinv, jnp.concatenate([v * beta, kb * ecum], axis=2))
    scores = jnp.where(r_i >= c_i, mm_nt(q, k) * decay, 0.0)
    last = cum[:, n - 1:n, :]
    return sol[:, :, 0:GD_DK], sol[:, :, GD_DK:2 * GD_DK], scores, q * ecum, k * jnp.exp(last - cum), jnp.exp(last)


def _rglru_pre(rx, halo_x, conv_w, conv_b, wr, br, wi, bi, lam):
    xc = _causal_conv(rx, halo_x, conv_w) + conv_b
    r = _logistic(mm_nn(xc, wr) + br)
    ig = _sigmoid(mm_nn(xc, wi) + bi)
    two = 2.0 * (-RG_C * r * _softplus(-lam))
    a = jnp.exp(0.5 * two)
    one_minus = -jnp.tanh(0.5 * two) * (jnp.exp(two) + 1.0)
    acc_a, acc_b = a, jnp.sqrt(one_minus) * (ig * xc)
    t = _iota(acc_a.shape, 0) % CHUNK
    k = 1
    while k < CHUNK:
        sa = jnp.where(t >= k, jnp.roll(acc_a, k, axis=0), 1.0)
        sb = jnp.where(t >= k, jnp.roll(acc_b, k, axis=0), 0.0)
        acc_b = acc_a * sb + acc_b
        acc_a = acc_a * sa
        k *= 2
    return acc_a, acc_b


MIXER_PARAMS = ("hg_lb", "hg_norm_w", "gd_conv_w", "gd_alog_vec", "gd_dt_vec", "gd_norm_w",
                "rg_conv_w", "rg_conv_b", "rg_wr_bd", "rg_br", "rg_wi_bd", "rg_bi", "rg_lambda")


def mixer_pre(layer, proj, halo, params):
    (hg_lb, _, gd_conv_w, alog_vec, dt_vec, _, rg_conv_w, rg_conv_b, wr, br, wi, bi, lam) = params
    hg = _hgrn2_pre(proj[:, O_HQ:O_HQ + 256], proj[:, O_HF:O_HF + 256], proj[:, O_HI:O_HI + 256], _lb_of_layer(hg_lb, layer))
    gd = _gdn_pre(proj[:, O_GQ:O_GQ + 512], proj[:, O_GK:O_GK + 512], proj[:, O_GV:O_GV + 512], proj[:, O_GBA:O_GBA + 128],
                  halo[:, O_GQ:O_GQ + 512], halo[:, O_GK:O_GK + 512], halo[:, O_GV:O_GV + 512], gd_conv_w, alog_vec, dt_vec)
    rg = _rglru_pre(proj[:, O_RX:O_RX + 256], halo[:, O_RX:O_RX + 256], rg_conv_w, rg_conv_b, wr, br, wi, bi, lam)
    return hg, gd, rg


def mixer_pre_chunk(pre, c, nb):
    hg, gd, rg = pre
    pick = lambda x: jnp.concatenate([x[h * nb + c:h * nb + c + 1] for h in range(GD_HEADS)], axis=0)
    return (tuple(x[c] for x in hg), tuple(pick(x) for x in gd), tuple(x[c * CHUNK:(c + 1) * CHUNK, :] for x in rg))


def mixer_pre_unchunk(parts, nb):
    hg = tuple(jnp.concatenate([parts[c][0][i][None] for c in range(nb)], axis=0) for i in range(len(parts[0][0])))
    gd = tuple(jnp.concatenate([parts[c][1][i][h:h + 1] for h in range(GD_HEADS) for c in range(nb)], axis=0)
               for i in range(len(parts[0][1])))
    rg = tuple(jnp.concatenate([parts[c][2][i] for c in range(nb)], axis=0) for i in range(len(parts[0][2])))
    return hg, gd, rg


def mixer_rec(pre_c, st_hg, s_gd, h_rg):
    (o_intra, qe, kd, elast, v), (u, w, scores, gqe, gkd, gel), (acc_a, acc_b) = pre_c
    o_hg = o_intra + mm_nt(qe, st_hg)
    blk = _iota((HG_W, HG_W), 0) // HG_DK == _iota((HG_W, HG_W), 1) // HG_DK
    st_new = st_hg * elast + jnp.where(blk, mm_tn(v, kd), 0.0)
    v_new = u - mm_nn(w, s_gd)
    o_gd = mm_nn(gqe, s_gd) + mm_nn(scores, v_new)
    s_new = s_gd * gel + mm_tn(gkd, v_new)
    hs = acc_b + acc_a * h_rg
    return (o_hg, o_gd, hs), (st_new, s_new, hs[CHUNK - 1:CHUNK, :])


def mixer_post(o_hg, o_gd, hs, proj, params):
    (_, hg_norm_w, _, _, _, gd_norm_w, _, _, _, _, _, _, _) = params
    blk = (_iota((HG_W, HG_W), 0) // HG_DK == _iota((HG_W, HG_W), 1) // HG_DK).astype(BF16)
    ms = _seg_sum(o_hg * o_hg, blk) * (1.0 / HG_DK)
    y_hg = o_hg * lax.rsqrt(ms + EPS) * jnp.concatenate([hg_norm_w] * HG_HEADS, axis=1) * _silu(proj[:, O_HG:O_HG + 256])
    parts = []
    for h in range(GD_HEADS):
        o = o_gd[:, GD_DK * h:GD_DK * (h + 1)]
        parts.append(o * lax.rsqrt(jnp.mean(o * o, axis=-1, keepdims=True) + EPS) * gd_norm_w)
    y_gd = jnp.concatenate(parts, axis=1) * _silu(proj[:, O_GZ:O_GZ + 512])
    y_rg = hs * _gelu_tanh(proj[:, O_RGATE:O_RGATE + 256])
    return jnp.concatenate([y_hg, y_gd, y_rg], axis=1)


def mixer_step_forward(layer, proj, halo, states, params):
    nb = proj.shape[0] // CHUNK
    pre = mixer_pre(layer, proj, halo, params)
    outs, entered = [], []
    for c in range(nb):
        entered.append(states)
        o, states = mixer_rec(mixer_pre_chunk(pre, c, nb), *states)
        outs.append(o)
    o_hg = jnp.concatenate([o[0] for o in outs], axis=0)
    o_gd = jnp.concatenate([jnp.concatenate([o[1][h] for h in range(GD_HEADS)], axis=1) for o in outs], axis=0)
    hs = jnp.concatenate([o[2] for o in outs], axis=0)
    return mixer_post(o_hg, o_gd, hs, proj, params), (o_hg, o_gd, hs), entered, states


def _full(shape):
    return pl.BlockSpec(shape, lambda *_: (0,) * len(shape))


def _token_tile(seq, want):
    return min(want, seq)


def ffn_fwd(x, nw, pack, layer, which):
    seq = x.shape[0]
    t = _token_tile(seq, 1024)
    bg, bu, bd = (_ffn_block(layer, which, p) for p in "gud")

    def body(x_ref, nw_ref, wg_ref, wu_ref, wd_ref, xo_ref, h_ref, g_ref, u_ref, a_ref, h_scr, acc):
        k = pl.program_id(1)

        @pl.when(k == 0)
        def _():
            xv = x_ref[...]
            h = _rms_fwd(xv, nw_ref[...]).astype(BF16)
            h_scr[...] = h
            h_ref[...] = h
            acc[...] = xv

        h = h_scr[...]
        g = _dot_nt(h, wg_ref[...].reshape(FFN_COLS, D_MODEL))
        u = _dot_nt(h, wu_ref[...].reshape(FFN_COLS, D_MODEL))
        a = (g * _sigmoid(g) * u).astype(BF16)
        g_ref[...] = g.astype(BF16)
        u_ref[...] = u.astype(BF16)
        a_ref[...] = a
        acc[...] += 0.5 * _dot_nn(a, wd_ref[...].reshape(FFN_COLS, D_MODEL))

        @pl.when(k == FFN_STEPS - 1)
        def _():
            xo_ref[...] = acc[...]

    wspec = lambda blk: pl.BlockSpec((FFN_GROUP, FF_PAD, D_MODEL), lambda i, k: (k, blk, 0))
    act = pl.BlockSpec((t, FFN_COLS), lambda i, k: (i, k))
    row = pl.BlockSpec((t, D_MODEL), lambda i, k: (i, 0))
    ff = N_DEV * FF_PAD
    return pl.pallas_call(
        body, name=f"ffn_fwd_l{layer}_{which}", grid=(seq // t, FFN_STEPS),
        in_specs=[row, _full((1, D_MODEL)), wspec(bg), wspec(bu), wspec(bd)],
        out_specs=[row, row, act, act, act],
        out_shape=[jax.ShapeDtypeStruct((seq, D_MODEL), F32), jax.ShapeDtypeStruct((seq, D_MODEL), BF16)]
        + [jax.ShapeDtypeStruct((seq, ff), BF16)] * 3,
        scratch_shapes=[pltpu.VMEM((t, D_MODEL), BF16), pltpu.VMEM((t, D_MODEL), F32)],
        compiler_params=_cparams(("parallel", "arbitrary")),
    )(x, nw, pack, pack, pack)


def ffn_bwd(dout, x, nw, g, u, pack, layer, which):
    seq = x.shape[0]
    t = _token_tile(seq, 1024)
    bg, bu, bd = (_ffn_block(layer, which, p) for p in "gud")

    def body(do_ref, x_ref, nw_ref, g_ref, u_ref, wg_ref, wu_ref, wd_ref, dx_ref, dg_ref, du_ref, dnw_ref, doh, dh):
        i, k = pl.program_id(0), pl.program_id(1)

        @pl.when(k == 0)
        def _():
            doh[...] = (0.5 * do_ref[...]).astype(BF16)
            dh[...] = jnp.zeros_like(dh)

        @pl.when((k == 0) & (i == 0))
        def _():
            dnw_ref[...] = jnp.zeros_like(dnw_ref)

        da = _dot_nt(doh[...], wd_ref[...].reshape(FFN_COLS, D_MODEL))
        gv = g_ref[...].astype(F32)
        uv = u_ref[...].astype(F32)
        s = _sigmoid(gv)
        dg = (da * uv * (s * (1.0 + gv * (1.0 - s)))).astype(BF16)
        du = (da * gv * s).astype(BF16)
        dg_ref[...] = dg
        du_ref[...] = du
        w_gu = jnp.concatenate([wg_ref[...].reshape(FFN_COLS, D_MODEL), wu_ref[...].reshape(FFN_COLS, D_MODEL)], axis=0)
        dh[...] += _dot_nn(jnp.concatenate([dg, du], axis=1), w_gu)

        @pl.when(k == FFN_STEPS - 1)
        def _():
            dx, dw = _rms_bwd(dh[...], x_ref[...], nw_ref[...])
            dx_ref[...] = do_ref[...] + dx
            dnw_ref[...] += dw

    wspec = lambda blk: pl.BlockSpec((FFN_GROUP, FF_PAD, D_MODEL), lambda i, k: (k, blk, 0))
    act = pl.BlockSpec((t, FFN_COLS), lambda i, k: (i, k))
    row = pl.BlockSpec((t, D_MODEL), lambda i, k: (i, 0))
    row_once = pl.BlockSpec((t, D_MODEL), lambda i, k: (i, 0), pipeline_mode=pl.Buffered(1))
    ff = N_DEV * FF_PAD
    return pl.pallas_call(
        body, name=f"ffn_bwd_l{layer}_{which}", grid=(seq // t, FFN_STEPS),
        in_specs=[row_once, row_once, _full((1, D_MODEL)), act, act, wspec(bg), wspec(bu), wspec(bd)],
        out_specs=[row, act, act, _full((1, D_MODEL))],
        out_shape=[jax.ShapeDtypeStruct((seq, D_MODEL), F32), jax.ShapeDtypeStruct((seq, ff), BF16),
                   jax.ShapeDtypeStruct((seq, ff), BF16), jax.ShapeDtypeStruct((1, D_MODEL), F32)],
        scratch_shapes=[pltpu.VMEM((t, D_MODEL), BF16), pltpu.VMEM((t, D_MODEL), F32)],
        compiler_params=_cparams(("arbitrary", "arbitrary")),
    )(dout, x, nw, g, u, pack, pack, pack)


def tn_matmul(a, b, scale, tm, name, gbuf=None, blk_of_m=None):
    seq, m_total = a.shape
    tk = _token_tile(seq, 1024 if gbuf is not None else 512)
    nk = seq // tk
    n_m = m_total // tm

    def body(*refs):
        if gbuf is None:
            a_ref, b_ref, o_ref = refs
            acc = o_ref
        else:
            a_ref, b_ref, _, o_ref, acc = refs
        kk = pl.program_id(0)

        @pl.when(kk == 0)
        def _():
            acc[...] = jnp.zeros_like(acc)

        bv = b_ref[...].astype(BF16)
        for m in range(n_m):
            acc[pl.ds(m * tm, tm), :] += _dot_tn(a_ref[:, pl.ds(m * tm, tm)].astype(BF16), bv)

        @pl.when(kk == nk - 1)
        def _():
            if gbuf is None:
                if scale != 1.0:
                    o_ref[...] = o_ref[...] * scale
            else:
                for m in range(n_m):
                    o_ref[_grad_slot(m)] = (acc[pl.ds(m * tm, tm), :] * scale).astype(o_ref.dtype)

    in_specs = [pl.BlockSpec((tk, m_total), lambda k: (k, 0)), pl.BlockSpec((tk, D_MODEL), lambda k: (k, 0))]
    args = [a, b]
    if gbuf is None:
        out_spec = pl.BlockSpec((m_total, D_MODEL), lambda k: (0, 0))
        out_shape = jax.ShapeDtypeStruct((m_total, D_MODEL), F32)
        aliases, scratch = {}, []
    else:
        assert n_m == N_DEV
        in_specs.append(pl.BlockSpec(memory_space=pl.ANY))
        args.append(gbuf)
        out_spec = pl.BlockSpec((N_DEV, tm, D_MODEL), lambda k: (0, blk_of_m, 0))
        out_shape = jax.ShapeDtypeStruct(gbuf.shape, gbuf.dtype)
        aliases, scratch = {2: 0}, [pltpu.VMEM((m_total, D_MODEL), F32)]
    return pl.pallas_call(
        body, name=name, grid=(nk,), in_specs=in_specs, out_specs=out_spec, out_shape=out_shape,
        scratch_shapes=scratch, input_output_aliases=aliases, compiler_params=_cparams(("arbitrary",)),
    )(*args)


def in_proj_fwd(x, nw, wt, layer):
    seq = x.shape[0]
    t = _token_tile(seq, 256)

    def body(x_ref, nw_ref, w_ref, p_ref, h_ref):
        h = _rms_fwd(x_ref[...], nw_ref[...]).astype(BF16)
        h_ref[...] = h
        p_ref[...] = _dot_nt(h, w_ref[...])

    row = pl.BlockSpec((t, D_MODEL), lambda i: (i, 0))
    return pl.pallas_call(
        body, name=f"in_proj_fwd_l{layer}", grid=(seq // t,),
        in_specs=[row, _full((1, D_MODEL)), _full((NP, D_MODEL))],
        out_specs=[pl.BlockSpec((t, NP), lambda i: (i, 0)), row],
        out_shape=[jax.ShapeDtypeStruct((seq, NP), F32), jax.ShapeDtypeStruct((seq, D_MODEL), BF16)],
        compiler_params=_cparams(("parallel",)),
    )(x, nw, wt)


def in_proj_bwd(dproj, x, nw, dres, wt, layer):
    seq = x.shape[0]
    t = _token_tile(seq, 256)

    def body(dp_ref, x_ref, nw_ref, dr_ref, w_ref, dx_ref, dnw_ref):
        @pl.when(pl.program_id(0) == 0)
        def _():
            dnw_ref[...] = jnp.zeros_like(dnw_ref)

        dh = _dot_nn(dp_ref[...], w_ref[...])
        dx, dw = _rms_bwd(dh, x_ref[...], nw_ref[...])
        dx_ref[...] = dr_ref[...] + dx
        dnw_ref[...] += dw

    row = pl.BlockSpec((t, D_MODEL), lambda i: (i, 0))
    return pl.pallas_call(
        body, name=f"in_proj_bwd_l{layer}", grid=(seq // t,),
        in_specs=[pl.BlockSpec((t, NP), lambda i: (i, 0)), row, _full((1, D_MODEL)), row, _full((NP, D_MODEL))],
        out_specs=[row, _full((1, D_MODEL))],
        out_shape=[jax.ShapeDtypeStruct((seq, D_MODEL), F32), jax.ShapeDtypeStruct((1, D_MODEL), F32)],
        compiler_params=_cparams(("arbitrary",)),
    )(dproj, x, nw, dres, wt)


def out_proj_fwd(x, y, w, layer):
    seq = x.shape[0]
    t = _token_tile(seq, 512)

    def body(x_ref, y_ref, w_ref, o_ref):
        o_ref[...] = x_ref[...] + _dot_nn(y_ref[...], w_ref[...])

    row = pl.BlockSpec((t, D_MODEL), lambda i: (i, 0))
    return pl.pallas_call(
        body, name=f"out_proj_fwd_l{layer}", grid=(seq // t,),
        in_specs=[row, row, _full((D_MODEL, D_MODEL))], out_specs=row,
        out_shape=jax.ShapeDtypeStruct((seq, D_MODEL), F32), compiler_params=_cparams(("parallel",)),
    )(x, y, w)


def out_proj_bwd(dx, w, layer):
    seq = dx.shape[0]
    t = _token_tile(seq, 512)

    def body(d_ref, w_ref, o_ref):
        o_ref[...] = _dot_nt(d_ref[...].astype(BF16), w_ref[...])

    row = pl.BlockSpec((t, D_MODEL), lambda i: (i, 0))
    return pl.pallas_call(
        body, name=f"out_proj_bwd_l{layer}", grid=(seq // t,),
        in_specs=[row, _full((D_MODEL, D_MODEL))], out_specs=row,
        out_shape=jax.ShapeDtypeStruct((seq, D_MODEL), F32), compiler_params=_cparams(("parallel",)),
    )(dx, w)


def loss_head(x, nw, target):
    seq = x.shape[0]
    t = _token_tile(seq, 512)

    def body(x_ref, nw_ref, t_ref, loss_ref, dx_ref, dnw_ref):
        @pl.when(pl.program_id(0) == 0)
        def _():
            loss_ref[...] = jnp.zeros_like(loss_ref)
            dnw_ref[...] = jnp.zeros_like(dnw_ref)

        xv, w = x_ref[...], nw_ref[...]
        err = _rms_fwd(xv, w) - t_ref[...]
        loss_ref[...] += 0.5 * jnp.sum(jnp.mean(err * err, axis=-1, keepdims=True), axis=0, keepdims=True)
        dx, dw = _rms_bwd(err * (1.0 / D_MODEL), xv, w)
        dx_ref[...] = dx
        dnw_ref[...] += dw

    row = pl.BlockSpec((t, D_MODEL), lambda i: (i, 0))
    return pl.pallas_call(
        body, name="loss_head", grid=(seq // t,),
        in_specs=[row, _full((1, D_MODEL)), row], out_specs=[_full((1, 128)), row, _full((1, D_MODEL))],
        out_shape=[jax.ShapeDtypeStruct((1, 128), F32), jax.ShapeDtypeStruct((seq, D_MODEL), F32),
                   jax.ShapeDtypeStruct((1, D_MODEL), F32)],
        compiler_params=_cparams(("arbitrary",)),
    )(x, nw, target)


_MIXER_PARAM_SHAPES = ((DEPTH, HG_W), (1, HG_DK), (CONV_W, 3 * GD_W), (1, 128), (1, 128), (1, GD_DK),
                       (CONV_W, RG_W), (1, RG_W), (RG_W, RG_W), (1, RG_W), (RG_W, RG_W), (1, RG_W), (1, RG_W))
_STATE_SHAPES = ((HG_W, HG_W), (GD_HEADS, GD_DK, GD_DK), (1, RG_W))


def _state_spec(cps, sh, index):
    return pl.BlockSpec((cps,) + sh, lambda i: (index(i),) + (0,) * len(sh))


def mixer_fwd(layer, proj, params):
    seq = proj.shape[0]
    cps = min(MIXER_CHUNKS_FWD, seq // CHUNK)
    rows = cps * CHUNK
    n = seq // rows

    def body(*refs):
        p_ref, halo_ref = refs[0], refs[1]
        prm = refs[2:15]
        y_ref, o_ref, st_out, s_out, h_out = refs[15:20]
        st, s, h = refs[20:23]
        i = pl.program_id(0)

        @pl.when(i == 0)
        def _():
            st[...] = jnp.zeros_like(st)
            s[...] = jnp.zeros_like(s)
            h[...] = jnp.zeros_like(h)

        halo = jnp.where(i > 0, halo_ref[...], 0.0)
        y, outs, entered, left = mixer_step_forward(layer, p_ref[...], halo, (st[...], s[...], h[...]),
                                                    tuple(r[...] for r in prm))
        y_ref[...] = y.astype(BF16)
        o_ref[...] = jnp.concatenate(outs, axis=1)
        for c in range(cps):
            st_out[c], s_out[c], h_out[c] = entered[c]
        st[...], s[...], h[...] = left

    in_specs = [pl.BlockSpec((rows, NP), lambda i: (i, 0)),
                pl.BlockSpec((8, NP), lambda i: (jnp.maximum(i * (rows // 8) - 1, 0), 0))]
    in_specs += [_full(sh) for sh in _MIXER_PARAM_SHAPES]
    out_specs = [pl.BlockSpec((rows, D_MODEL), lambda i: (i, 0))] * 2
    out_specs += [_state_spec(cps, sh, lambda i: i) for sh in _STATE_SHAPES]
    out_shape = [jax.ShapeDtypeStruct((seq, D_MODEL), BF16), jax.ShapeDtypeStruct((seq, D_MODEL), F32)]
    out_shape += [jax.ShapeDtypeStruct((seq // CHUNK,) + sh, F32) for sh in _STATE_SHAPES]
    return pl.pallas_call(
        body, name=f"mixer_fwd_l{layer}", grid=(n,), in_specs=in_specs, out_specs=out_specs, out_shape=out_shape,
        scratch_shapes=[pltpu.VMEM(sh, F32) for sh in _STATE_SHAPES],
        compiler_params=_cparams(("arbitrary",)),
    )(proj, proj, *params)


def mixer_bwd(layer, proj, dy, outs, states, params):
    seq = proj.shape[0]
    cps = min(MIXER_CHUNKS_BWD, seq // CHUNK)
    rows = cps * CHUNK
    n = seq // rows

    def body(*refs):
        p_ref, halo_ref, dy_ref, o_ref, st_ref, s_ref, h_ref = refs[0:7]
        prm = refs[7:20]
        dp_ref = refs[20]
        dprm = refs[21:34]
        dst, ds, dh, dhalo = refs[34:38]
        i = pl.program_id(0)
        r = n - 1 - i

        @pl.when(i == 0)
        def _():
            for ref in (dst, ds, dh, dhalo) + tuple(dprm):
                ref[...] = jnp.zeros_like(ref)

        params_v = tuple(q[...] for q in prm)
        proj_v = p_ref[...]
        halo = jnp.where(r > 0, halo_ref[...], 0.0)
        o = o_ref[...]
        _, post_vjp = jax.vjp(mixer_post, o[:, 0:HG_W], o[:, HG_W:HG_W + GD_W], o[:, HG_W + GD_W:D_MODEL], proj_v, params_v)
        d_ohg, d_ogd, d_hs, dproj_post, dparams_post = post_vjp(dy_ref[...])
        pre, pre_vjp = jax.vjp(functools.partial(mixer_pre, layer), proj_v, halo, params_v)
        dstates = (dst[...], ds[...], dh[...])
        parts = [None] * cps
        for c in reversed(range(cps)):
            rs = slice(c * CHUNK, (c + 1) * CHUNK)
            _, rec_vjp = jax.vjp(mixer_rec, mixer_pre_chunk(pre, c, cps), st_ref[c], s_ref[c], h_ref[c])
            d_ogd_c = jnp.concatenate([d_ogd[rs, GD_DK * hd:GD_DK * (hd + 1)][None] for hd in range(GD_HEADS)], axis=0)
            parts[c], dst_c, ds_c, dh_c = rec_vjp(((d_ohg[rs, :], d_ogd_c, d_hs[rs, :]), dstates))
            dstates = (dst_c, ds_c, dh_c)
        dproj_pre, dhalo_n, dparams_pre = pre_vjp(mixer_pre_unchunk(parts, cps))
        carry = jnp.concatenate([jnp.zeros((rows - 8, NP), F32), dhalo[...]], axis=0)
        dp_ref[...] = (dproj_post + dproj_pre + carry).astype(BF16)
        dst[...], ds[...], dh[...] = dstates
        dhalo[...] = dhalo_n
        for ref, a, b in zip(dprm, dparams_post, dparams_pre):
            ref[...] += a + b

    rev = lambda i: n - 1 - i
    once = pl.Buffered(1)
    in_specs = [pl.BlockSpec((rows, NP), lambda i: (rev(i), 0), pipeline_mode=once),
                pl.BlockSpec((8, NP), lambda i: (jnp.maximum(rev(i) * (rows // 8) - 1, 0), 0)),
                pl.BlockSpec((rows, D_MODEL), lambda i: (rev(i), 0), pipeline_mode=once),
                pl.BlockSpec((rows, D_MODEL), lambda i: (rev(i), 0), pipeline_mode=once)]
    in_specs += [_state_spec(cps, sh, rev) for sh in _STATE_SHAPES]
    in_specs += [_full(sh) for sh in _MIXER_PARAM_SHAPES]
    out_specs = [pl.BlockSpec((rows, NP), lambda i: (rev(i), 0))] + [_full(sh) for sh in _MIXER_PARAM_SHAPES]
    out_shape = [jax.ShapeDtypeStruct((seq, NP), BF16)] + [jax.ShapeDtypeStruct(sh, F32) for sh in _MIXER_PARAM_SHAPES]
    return pl.pallas_call(
        body, name=f"mixer_bwd_l{layer}", grid=(n,), in_specs=in_specs, out_specs=out_specs, out_shape=out_shape,
        scratch_shapes=[pltpu.VMEM(sh, F32) for sh in _STATE_SHAPES] + [pltpu.VMEM((8, NP), F32)],
        compiler_params=pltpu.CompilerParams(dimension_semantics=("arbitrary",), vmem_limit_bytes=MIXER_BWD_VMEM_LIMIT),
    )(proj, proj, dy, outs, *states, *params)


def _win_segments():
    out = []
    for k in range(N_DEV):
        a, b = NIN_SHARD * k, NIN_SHARD * (k + 1)
        if a < GBA_SPLIT < b:
            out.append((k, 0, GBA_SPLIT - a, a))
            out.append((k, GBA_SPLIT - a, b - GBA_SPLIT, O_RX))
        elif b <= GBA_SPLIT:
            out.append((k, 0, NIN_SHARD, a))
        else:
            out.append((k, 0, NIN_SHARD, a + O_RX - GBA_SPLIT))
    return out


def win_to_padded(wt_shards, layer):
    lanes = 256

    def body(src, dst, scr):
        scr[...] = jnp.zeros_like(scr)
        for (k, s0, rows, d0) in _win_segments():
            scr[pl.ds(d0, rows), :] = src[k, pl.ds(s0, rows), :].astype(F32)
        dst[...] = scr[...].astype(BF16)

    return pl.pallas_call(
        body, name=f"win_to_padded_l{layer}", grid=(D_MODEL // lanes,),
        in_specs=[pl.BlockSpec((N_DEV, NIN_SHARD_PAD, lanes), lambda j: (0, 0, j))],
        out_specs=pl.BlockSpec((NP, lanes), lambda j: (0, j)),
        out_shape=jax.ShapeDtypeStruct((NP, D_MODEL), BF16),
        scratch_shapes=[pltpu.VMEM((NP, lanes), F32)], compiler_params=_cparams(("parallel",)),
    )(wt_shards)


def win_grad_to_pack(dwt, gbuf, layer):
    lanes = 256
    blk = (OFF_WIN + layer * NIN_SHARD_PAD) // NIN_SHARD_PAD

    def body(src, _, dst, scr):
        scr[...] = jnp.zeros_like(scr)
        for (k, s0, rows, d0) in _win_segments():
            scr[_grad_slot(k), pl.ds(s0, rows), :] = src[pl.ds(d0, rows), :]
        dst[...] = scr[...].astype(dst.dtype)

    return pl.pallas_call(
        body, name=f"win_grad_to_pack_l{layer}", grid=(D_MODEL // lanes,),
        in_specs=[pl.BlockSpec((NP, lanes), lambda j: (0, j)), pl.BlockSpec(memory_space=pl.ANY)],
        out_specs=pl.BlockSpec((N_DEV, NIN_SHARD_PAD, lanes), lambda j: (0, blk, j)),
        out_shape=jax.ShapeDtypeStruct(gbuf.shape, gbuf.dtype), input_output_aliases={1: 0},
        scratch_shapes=[pltpu.VMEM((N_DEV, NIN_SHARD_PAD, lanes), F32)],
        compiler_params=_cparams(("parallel",)),
    )(dwt, gbuf)


def sum_adamw(parts, w, m, v, name, rows_per_step):
    n, rows, cols = parts.shape
    tr = min(rows_per_step, rows)
    c1 = 1.0 - ADAM_B1 ** ADAM_STEP
    c2 = 1.0 - ADAM_B2 ** ADAM_STEP

    def body(p_ref, w_ref, m_ref, v_ref, g_ref, d_ref, mo_ref, vo_ref):
        g = p_ref[0].astype(F32)
        for j in range(1, n):
            g = g + p_ref[j].astype(F32)
        mn = ADAM_B1 * m_ref[...] + (1.0 - ADAM_B1) * g
        vn = ADAM_B2 * v_ref[...] + (1.0 - ADAM_B2) * (g * g)
        g_ref[...] = g
        mo_ref[...] = mn
        vo_ref[...] = vn
        d_ref[...] = -ADAM_LR * ((mn / c1) / (jnp.sqrt(vn / c2) + ADAM_EPS) + ADAM_WD * w_ref[...])

    blk = pl.BlockSpec((tr, cols), lambda i: (i, 0))
    return pl.pallas_call(
        body, name=name, grid=(rows // tr,),
        in_specs=[pl.BlockSpec((n, tr, cols), lambda i: (0, i, 0)), blk, blk, blk], out_specs=[blk] * 4,
        out_shape=[jax.ShapeDtypeStruct((rows, cols), F32)] * 4, compiler_params=_cparams(("parallel",)),
    )(parts, w, m, v)


def gather_two_level(arrays, name):
    n = len(arrays)

    def body(*refs):
        ins, outs = refs[:n], refs[n:2 * n]
        send_sems, recv_sems, local_sems = refs[2 * n:]
        x, y, c = lax.axis_index("x"), lax.axis_index("y"), lax.axis_index("c")
        me = 4 * x + 2 * y + c
        sib_slot = 4 * x + 2 * y + (1 - c)
        chips = [(1 - x, y), (x, 1 - y), (1 - x, 1 - y)]

        def copy(a, k, slot, to, src=None):
            return pltpu.make_async_remote_copy(
                src_ref=outs[a].at[slot] if src is None else src, dst_ref=outs[a].at[slot],
                send_sem=send_sems.at[a * 7 + k], recv_sem=recv_sems.at[a * 7 + k],
                device_id=to, device_id_type=pl.DeviceIdType.MESH)

        local, started = [], []
        for a in range(n):
            cp = pltpu.make_async_copy(ins[a], outs[a].at[me], local_sems.at[a])
            cp.start()
            local.append(cp)
            first = [copy(a, 0, me, (x, y, 1 - c), src=ins[a])]
            first += [copy(a, 1 + j, me, (px, py, c), src=ins[a]) for j, (px, py) in enumerate(chips)]
            for cp in first:
                cp.start()
            started += first
        for a in range(n):
            for j, (px, py) in enumerate(chips):
                slot = 4 * px + 2 * py + c
                copy(a, 1 + j, slot, (x, y, c)).wait_recv()
                fwd = copy(a, 4 + j, slot, (x, y, 1 - c))
                fwd.start()
                started.append(fwd)
        for a in range(n):
            copy(a, 0, sib_slot, (x, y, c)).wait_recv()
            for j, (px, py) in enumerate(chips):
                copy(a, 4 + j, 4 * px + 2 * py + (1 - c), (x, y, c)).wait_recv()
        for cp in started:
            cp.wait_send()
        for cp in local:
            cp.wait()

    out_shape = [jax.ShapeDtypeStruct((N_DEV,) + a.shape, a.dtype) for a in arrays]
    any_spec = pl.BlockSpec(memory_space=pl.ANY)
    return pl.pallas_call(
        body, name=name, in_specs=[any_spec] * n, out_specs=[any_spec] * n, out_shape=out_shape,
        scratch_shapes=[pltpu.SemaphoreType.DMA((7 * n,)), pltpu.SemaphoreType.DMA((7 * n,)), pltpu.SemaphoreType.DMA((n,))],
        compiler_params=pltpu.CompilerParams(has_side_effects=True),
    )(*arrays)


def _grad_slot(d):
    return (d % 2) * (N_DEV // 2) + d // 2


def sibling_exchange(gbuf, name):
    half = N_DEV // 2

    def body(g_ref, o_ref, send_sem, recv_sem):
        x, y, c = lax.axis_index("x"), lax.axis_index("y"), lax.axis_index("c")
        cp = pltpu.make_async_remote_copy(
            src_ref=g_ref.at[pl.ds((1 - c) * half, half)], dst_ref=o_ref, send_sem=send_sem, recv_sem=recv_sem,
            device_id=(x, y, 1 - c), device_id_type=pl.DeviceIdType.MESH)
        cp.start()
        cp.wait()

    any_spec = pl.BlockSpec(memory_space=pl.ANY)
    return pl.pallas_call(
        body, name=name, in_specs=[any_spec], out_specs=any_spec,
        out_shape=jax.ShapeDtypeStruct((half,) + gbuf.shape[1:], gbuf.dtype),
        scratch_shapes=[pltpu.SemaphoreType.DMA, pltpu.SemaphoreType.DMA],
        compiler_params=pltpu.CompilerParams(has_side_effects=True),
    )(gbuf)


def pair_sum(a, b, name):
    n, rows, cols = a.shape
    tr = 256

    def body(a_ref, b_ref, o_ref):
        o_ref[...] = (a_ref[...].astype(F32) + b_ref[...].astype(F32)).astype(o_ref.dtype)

    blk = pl.BlockSpec((None, tr, cols), lambda i, j: (i, j, 0))
    return pl.pallas_call(
        body, name=name, grid=(n, rows // tr), in_specs=[blk, blk], out_specs=blk,
        out_shape=jax.ShapeDtypeStruct(a.shape, a.dtype), compiler_params=_cparams(("parallel", "parallel")),
    )(a, b)


def chip_exchange(p, name):
    half = N_DEV // 2

    def body(p_ref, o_ref, send_sems, recv_sems, local_sem):
        x, y, c = lax.axis_index("x"), lax.axis_index("y"), lax.axis_index("c")
        q = 2 * x + y
        own = pltpu.make_async_copy(p_ref.at[q], o_ref.at[q], local_sem)
        own.start()
        pending = []
        for j in range(1, half):
            px = 1 - x if j & 2 else x
            py = 1 - y if j & 1 else y
            pq = 2 * px + py

            def copy(dst_slot):
                return pltpu.make_async_remote_copy(
                    src_ref=p_ref.at[pq], dst_ref=o_ref.at[dst_slot], send_sem=send_sems.at[j - 1],
                    recv_sem=recv_sems.at[j - 1], device_id=(px, py, c), device_id_type=pl.DeviceIdType.MESH)

            send = copy(q)
            send.start()
            pending.append((send, copy(pq)))
        for send, recv in pending:
            send.wait_send()
            recv.wait_recv()
        own.wait()

    any_spec = pl.BlockSpec(memory_space=pl.ANY)
    return pl.pallas_call(
        body, name=name, in_specs=[any_spec], out_specs=any_spec, out_shape=jax.ShapeDtypeStruct(p.shape, p.dtype),
        scratch_shapes=[pltpu.SemaphoreType.DMA((half - 1,)), pltpu.SemaphoreType.DMA((half - 1,)), pltpu.SemaphoreType.DMA],
        compiler_params=pltpu.CompilerParams(has_side_effects=True),
    )(p)


BIG = ("ffn1_gate", "ffn1_up", "ffn1_down", "w_in", "w_out", "ffn2_gate", "ffn2_up", "ffn2_down")
SMALL_REPLICATED = (("norm_ffn1", (DEPTH, D_MODEL)), ("norm_mix", (DEPTH, D_MODEL)), ("norm_ffn2", (DEPTH, D_MODEL)),
                    ("norm_final", (D_MODEL,)), ("hg_lb", (DEPTH, HG_W)), ("hg_norm_w", (DEPTH, HG_DK)),
                    ("gd_a_log", (DEPTH, GD_HEADS)), ("gd_dt_bias", (DEPTH, GD_HEADS)), ("gd_norm_w", (DEPTH, GD_DK)),
                    ("rg_conv_b", (DEPTH, RG_W)), ("rg_wr", (DEPTH, RG_BLOCKS, RG_BD, RG_BD)), ("rg_br", (DEPTH, RG_W)),
                    ("rg_wi", (DEPTH, RG_BLOCKS, RG_BD, RG_BD)), ("rg_bi", (DEPTH, RG_W)), ("rg_lambda", (DEPTH, RG_W)))
SMALL_EXTRA = (("gd_conv_w", (DEPTH, CONV_W, 3 * GD_W)), ("rg_conv_w", (DEPTH, CONV_W, RG_W)), ("loss", (1,)))
SMALL_SHARDED = (("gd_conv_w", (DEPTH, CONV_W, GDC_SHARD)), ("rg_conv_w", (DEPTH, CONV_W, RGC_SHARD)))


def _pad_rows(a, rows):
    return jnp.pad(a, ((0, rows - a.shape[0]), (0, 0)))


def pack_big(get):
    pieces = []
    for l in range(DEPTH):
        for which in (1, 2):
            pieces.append(_pad_rows(get(f"ffn{which}_gate")[l].T, FF_PAD))
            pieces.append(_pad_rows(get(f"ffn{which}_up")[l].T, FF_PAD))
            pieces.append(_pad_rows(get(f"ffn{which}_down")[l], FF_PAD))
    for l in range(DEPTH):
        pieces.append(_pad_rows(get("w_in")[l].T, NIN_SHARD_PAD))
    for l in range(DEPTH):
        pieces.append(get("w_out")[l])
    return jnp.concatenate(pieces, axis=0)


def unpack_big(p):
    out = {}
    for which in (1, 2):
        for part, nm in (("g", "gate"), ("u", "up"), ("d", "down")):
            per_layer = []
            for l in range(DEPTH):
                r0 = _ffn_block(l, which, part) * FF_PAD
                blk = p[r0:r0 + FF_SHARD, :]
                per_layer.append(blk if part == "d" else blk.T)
            out[f"ffn{which}_{nm}"] = jnp.stack(per_layer)
    out["w_in"] = jnp.stack([p[OFF_WIN + NIN_SHARD_PAD * l:OFF_WIN + NIN_SHARD_PAD * l + NIN_SHARD, :].T for l in range(DEPTH)])
    out["w_out"] = jnp.stack([p[OFF_WOUT + WOUT_SHARD * l:OFF_WOUT + WOUT_SHARD * (l + 1), :] for l in range(DEPTH)])
    return out


def pack_small(get, spec):
    flat = []
    for name, shape in spec:
        a = get(name)
        flat.append(jnp.zeros((math.prod(shape),), F32) if a is None else a.reshape(-1).astype(F32))
    v = jnp.concatenate(flat)
    total = -(-v.shape[0] // 1024) * 1024
    return jnp.pad(v, (0, total - v.shape[0])).reshape(total // 128, 128)


def unpack_small(p, spec):
    v = p.reshape(-1)
    out, off = {}, 0
    for name, shape in spec:
        size = math.prod(shape)
        out[name] = v[off:off + size].reshape(shape)
        off += size
    return out


def _block_diag(w):
    rows = []
    for i in range(RG_BLOCKS):
        rows.append(jnp.concatenate([w[i] if j == i else jnp.zeros((RG_BD, RG_BD), F32) for j in range(RG_BLOCKS)], axis=1))
    return jnp.concatenate(rows, axis=0)


def _diag_blocks(w):
    return jnp.stack([w[RG_BD * i:RG_BD * (i + 1), RG_BD * i:RG_BD * (i + 1)] for i in range(RG_BLOCKS)])


def _lane_vec(v):
    return jnp.pad(v.astype(F32), (GD_HEADS, 128 - 2 * GD_HEADS))[None]


INPUT_NAMES = ("x", "norm_ffn1", "ffn1_gate", "ffn1_up", "ffn1_down", "norm_mix", "w_in", "hg_lb", "hg_norm_w",
               "gd_conv_w", "gd_a_log", "gd_dt_bias", "gd_norm_w", "rg_conv_w", "rg_conv_b", "rg_wr", "rg_br", "rg_wi",
               "rg_bi", "rg_lambda", "w_out", "norm_ffn2", "ffn2_gate", "ffn2_up", "ffn2_down", "norm_final")
WEIGHT_NAMES = INPUT_NAMES[1:]


def _step(a):
    x = a["x"][0]
    target = a["loss_target"][0]
    me = 4 * lax.axis_index("x") + 2 * lax.axis_index("y") + lax.axis_index("c")

    w_pack = pack_big(lambda nm: a[nm])
    conv_local = pack_small(lambda nm: a[nm], SMALL_SHARDED)
    gw, gconv = gather_two_level([w_pack.astype(BF16), conv_local], "gather_weights")
    conv_parts = [unpack_small(gconv[d], SMALL_SHARDED) for d in range(N_DEV)]
    gd_conv_w = jnp.concatenate([p["gd_conv_w"] for p in conv_parts], axis=-1)
    rg_conv_w = jnp.concatenate([p["rg_conv_w"] for p in conv_parts], axis=-1)

    def mixer_params(l):
        return (a["hg_lb"], a["hg_norm_w"][l][None], gd_conv_w[l], _lane_vec(a["gd_a_log"][l]), _lane_vec(a["gd_dt_bias"][l]),
                a["gd_norm_w"][l][None], rg_conv_w[l], a["rg_conv_b"][l][None], _block_diag(a["rg_wr"][l]), a["rg_br"][l][None],
                _block_diag(a["rg_wi"][l]), a["rg_bi"][l][None], a["rg_lambda"][l][None])

    saved = []
    xs = x
    for l in range(DEPTH):
        x1, h1, g1, u1, a1 = ffn_fwd(xs, a["norm_ffn1"][l][None], gw, l, 1)
        wt = win_to_padded(gw[:, OFF_WIN + NIN_SHARD_PAD * l:OFF_WIN + NIN_SHARD_PAD * (l + 1), :], l)
        proj, h2 = in_proj_fwd(x1, a["norm_mix"][l][None], wt, l)
        prm = mixer_params(l)
        y, o_mix, st, s, h = mixer_fwd(l, proj, prm)
        wo = gw[:, OFF_WOUT + WOUT_SHARD * l:OFF_WOUT + WOUT_SHARD * (l + 1), :].reshape(D_MODEL, D_MODEL)
        x2 = out_proj_fwd(x1, y, wo, l)
        x3, h3, g3, u3, a3 = ffn_fwd(x2, a["norm_ffn2"][l][None], gw, l, 2)
        saved.append(dict(x0=xs, x1=x1, x2=x2, h1=h1, g1=g1, u1=u1, a1=a1, wt=wt, proj=proj, h2=h2, prm=prm, y=y, o_mix=o_mix,
                          states=(st, s, h), wo=wo, h3=h3, g3=g3, u3=u3, a3=a3))
        xs = x3
    loss_row, dx, d_norm_final = loss_head(xs, a["norm_final"][None], target)

    gbuf = jnp.zeros((N_DEV, PACK_ROWS, D_MODEL), BF16)
    sg = {"norm_final": d_norm_final[0], "loss": loss_row[0, 0:1]}
    per_layer = {nm: [None] * DEPTH for nm in ("norm_ffn1", "norm_mix", "norm_ffn2", "hg_norm_w", "gd_conv_w", "gd_a_log",
                                                "gd_dt_bias", "gd_norm_w", "rg_conv_w", "rg_conv_b", "rg_wr", "rg_br", "rg_wi",
                                                "rg_bi", "rg_lambda")}
    d_hg_lb = jnp.zeros((DEPTH, HG_W), F32)
    for l in reversed(range(DEPTH)):
        sv = saved[l]
        dx2, dg, du, dn2 = ffn_bwd(dx, sv["x2"], a["norm_ffn2"][l][None], sv["g3"], sv["u3"], gw, l, 2)
        gbuf = tn_matmul(dg, sv["h3"], 1.0, FF_PAD, f"dw_gate_l{l}_2", gbuf, _ffn_block(l, 2, "g"))
        gbuf = tn_matmul(du, sv["h3"], 1.0, FF_PAD, f"dw_up_l{l}_2", gbuf, _ffn_block(l, 2, "u"))
        gbuf = tn_matmul(sv["a3"], dx, 0.5, FF_PAD, f"dw_down_l{l}_2", gbuf, _ffn_block(l, 2, "d"))
        dy = out_proj_bwd(dx2, sv["wo"], l)
        gbuf = tn_matmul(sv["y"], dx2, 1.0, WOUT_SHARD, f"dw_out_l{l}", gbuf, (OFF_WOUT + WOUT_SHARD * l) // WOUT_SHARD)
        mb = mixer_bwd(l, sv["proj"], dy, sv["o_mix"], sv["states"], sv["prm"])
        dproj, dprm = mb[0], mb[1:]
        dwt = tn_matmul(dproj, sv["h2"], 1.0, FF_PAD, f"dw_in_l{l}")
        gbuf = win_grad_to_pack(dwt, gbuf, l)
        dx1, dnm = in_proj_bwd(dproj, sv["x1"], a["norm_mix"][l][None], dx2, sv["wt"], l)
        dx, dg, du, dn1 = ffn_bwd(dx1, sv["x0"], a["norm_ffn1"][l][None], sv["g1"], sv["u1"], gw, l, 1)
        gbuf = tn_matmul(dg, sv["h1"], 1.0, FF_PAD, f"dw_gate_l{l}_1", gbuf, _ffn_block(l, 1, "g"))
        gbuf = tn_matmul(du, sv["h1"], 1.0, FF_PAD, f"dw_up_l{l}_1", gbuf, _ffn_block(l, 1, "u"))
        gbuf = tn_matmul(sv["a1"], dx1, 0.5, FF_PAD, f"dw_down_l{l}_1", gbuf, _ffn_block(l, 1, "d"))
        (g_lb, g_hnw, g_gcw, g_alog, g_dt, g_gnw, g_rcw, g_rcb, g_wr, g_br, g_wi, g_bi, g_lam) = dprm
        d_hg_lb = d_hg_lb + g_lb
        for nm, val in (("norm_ffn1", dn1[0]), ("norm_mix", dnm[0]), ("norm_ffn2", dn2[0]), ("hg_norm_w", g_hnw[0]),
                        ("gd_conv_w", g_gcw), ("gd_a_log", g_alog[0, GD_HEADS:2 * GD_HEADS]),
                        ("gd_dt_bias", g_dt[0, GD_HEADS:2 * GD_HEADS]), ("gd_norm_w", g_gnw[0]), ("rg_conv_w", g_rcw),
                        ("rg_conv_b", g_rcb[0]), ("rg_wr", _diag_blocks(g_wr)), ("rg_br", g_br[0]),
                        ("rg_wi", _diag_blocks(g_wi)), ("rg_bi", g_bi[0]), ("rg_lambda", g_lam[0])):
            per_layer[nm][l] = val
    grad_x = dx
    for nm, vals in per_layer.items():
        sg[nm] = jnp.stack(vals)
    sg["hg_lb"] = d_hg_lb

    small_spec = SMALL_REPLICATED + SMALL_EXTRA
    from_sibling = sibling_exchange(gbuf, "grads_to_sibling")
    mine = lax.dynamic_slice_in_dim(gbuf, lax.axis_index("c") * (N_DEV // 2), N_DEV // 2, axis=0)
    parts_big = chip_exchange(pair_sum(mine, from_sibling, "grads_pair_sum"), "grads_to_chips")
    g_big, d_big, m_big, v_big = sum_adamw(parts_big, w_pack, pack_big(lambda nm: a["m_" + nm]),
                                           pack_big(lambda nm: a["v_" + nm]), "adamw_big", 128)
    (parts_small,) = gather_two_level([pack_small(lambda nm: sg[nm], small_spec)], "gather_small_grads")
    repl = dict(SMALL_REPLICATED)
    small_in = lambda pre: pack_small(lambda nm: a[pre + nm] if nm in repl else None, small_spec)
    rows_small = parts_small.shape[1]
    g_sm, d_sm, m_sm, v_sm = sum_adamw(parts_small, small_in(""), small_in("m_"), small_in("v_"), "adamw_small", rows_small)
    g_small = unpack_small(g_sm, small_spec)
    conv_grads = {"gd_conv_w": lax.dynamic_slice_in_dim(g_small["gd_conv_w"], me * GDC_SHARD, GDC_SHARD, axis=2),
                  "rg_conv_w": lax.dynamic_slice_in_dim(g_small["rg_conv_w"], me * RGC_SHARD, RGC_SHARD, axis=2)}
    conv_in = lambda pre: pack_small(lambda nm: a[pre + nm], SMALL_SHARDED)
    g_cv, d_cv, m_cv, v_cv = sum_adamw(pack_small(lambda nm: conv_grads[nm], SMALL_SHARDED)[None], conv_in(""),
                                       conv_in("m_"), conv_in("v_"), "adamw_conv", conv_local.shape[0])

    results = []
    for big, small, conv in ((g_big, g_sm, g_cv), (d_big, d_sm, d_cv), (m_big, m_sm, m_cv), (v_big, v_sm, v_cv)):
        vals = unpack_big(big)
        vals.update({k: v for k, v in unpack_small(small, small_spec).items() if k in repl})
        vals.update(unpack_small(conv, SMALL_SHARDED))
        results.append(vals)
    loss = g_small["loss"][0]
    out = [loss, grad_x[None]]
    for vals in results:
        out.extend(vals[nm] for nm in WEIGHT_NAMES)
    return tuple(out)


def kernel(x, norm_ffn1, ffn1_gate, ffn1_up, ffn1_down, norm_mix, w_in, hg_lb, hg_norm_w, gd_conv_w, gd_a_log, gd_dt_bias, gd_norm_w, rg_conv_w, rg_conv_b, rg_wr, rg_br, rg_wi, rg_bi, rg_lambda, w_out, norm_ffn2, ffn2_gate, ffn2_up, ffn2_down, norm_final, loss_target, m_norm_ffn1, m_ffn1_gate, m_ffn1_up, m_ffn1_down, m_norm_mix, m_w_in, m_hg_lb, m_hg_norm_w, m_gd_conv_w, m_gd_a_log, m_gd_dt_bias, m_gd_norm_w, m_rg_conv_w, m_rg_conv_b, m_rg_wr, m_rg_br, m_rg_wi, m_rg_bi, m_rg_lambda, m_w_out, m_norm_ffn2, m_ffn2_gate, m_ffn2_up, m_ffn2_down, m_norm_final, v_norm_ffn1, v_ffn1_gate, v_ffn1_up, v_ffn1_down, v_norm_mix, v_w_in, v_hg_lb, v_hg_norm_w, v_gd_conv_w, v_gd_a_log, v_gd_dt_bias, v_gd_norm_w, v_rg_conv_w, v_rg_conv_b, v_rg_wr, v_rg_br, v_rg_wi, v_rg_bi, v_rg_lambda, v_w_out, v_norm_ffn2, v_ffn2_gate, v_ffn2_up, v_ffn2_down, v_norm_final):
    args = locals()
    return _step(dict(args))
```

```python
import functools
import math

import jax
import jax.numpy as jnp
from jax import lax
from jax.experimental import pallas as pl
from jax.experimental.pallas import tpu as pltpu

F32 = jnp.float32
BF16 = jnp.bfloat16

D_MODEL = 1024
DEPTH = 2
D_FF = 2816
HG_HEADS, HG_DK = 4, 64
HG_W = 256
GD_HEADS, GD_DK = 4, 128
GD_W = 512
RG_W = 256
RG_BLOCKS, RG_BD = 4, 64
RG_C = 8.0
CONV_W = 4
CHUNK = 64
EPS = 1e-6
N_IN = 3592
ADAM_LR, ADAM_B1, ADAM_B2, ADAM_EPS, ADAM_WD, ADAM_STEP = 0.001, 0.9, 0.999, 1e-08, 0.01, 10

N_DEV = 8
FF_SHARD = D_FF // N_DEV
FF_PAD = 384
NIN_SHARD = N_IN // N_DEV
NIN_SHARD_PAD = 512
WOUT_SHARD = D_MODEL // N_DEV
GDC_SHARD = (2 * 512 + 512) // N_DEV
RGC_SHARD = RG_W // N_DEV

O_HQ, O_HF, O_HI, O_HG = 0, 256, 512, 768
O_GQ, O_GK, O_GV, O_GZ = 1024, 1536, 2048, 2560
O_GBA = 3072
O_RX, O_RGATE = 3200, 3456
NP = 3840
GBA_SPLIT = 3080

N_FFN_BLOCKS = DEPTH * 2 * 3
OFF_WIN = N_FFN_BLOCKS * FF_PAD
OFF_WOUT = OFF_WIN + DEPTH * NIN_SHARD_PAD
PACK_ROWS = OFF_WOUT + DEPTH * WOUT_SHARD
FIRST_BLOCKS = 3

VMEM_LIMIT = 56 * 1024 * 1024
FFN_GROUP = 2
FFN_STEPS = N_DEV // FFN_GROUP
FFN_COLS = FFN_GROUP * FF_PAD
MIXER_CHUNKS_FWD = 4
MIXER_CHUNKS_BWD = 4
MIXER_BWD_VMEM_LIMIT = 63 * 1024 * 1024


def _ffn_block(layer, which, part):
    return layer * 6 + (which - 1) * 3 + {"g": 0, "u": 1, "d": 2}[part]


def _cparams(sem, **kw):
    return pltpu.CompilerParams(dimension_semantics=sem, vmem_limit_bytes=VMEM_LIMIT, **kw)


def _dot_nn(a, b):
    return lax.dot_general(a, b, (((1,), (0,)), ((), ())), preferred_element_type=F32)


def _dot_nt(a, b):
    return lax.dot_general(a, b, (((1,), (1,)), ((), ())), preferred_element_type=F32)


def _dot_tn(a, b):
    return lax.dot_general(a, b, (((0,), (0,)), ((), ())), preferred_element_type=F32)


def _split3(x):
    hi = x.astype(BF16)
    r = x - hi.astype(F32)
    mid = r.astype(BF16)
    lo = (r - mid.astype(F32)).astype(BF16)
    return hi, mid, lo


def _split2(x):
    hi = x.astype(BF16)
    return hi, (x - hi.astype(F32)).astype(BF16)


def _iota(shape, dim):
    return lax.broadcasted_iota(jnp.int32, shape, dim)


def _sigmoid(x):
    return 0.5 * (jnp.tanh(0.5 * x) + 1.0)


def _logistic(x):
    return jax.nn.sigmoid(x)


def _silu(x):
    return x * _sigmoid(x)


def _softplus(x):
    return jnp.maximum(x, 0.0) + jnp.log(1.0 + jnp.exp(-jnp.abs(x)))


def _gelu_tanh(x):
    return 0.5 * x * (1.0 + jnp.tanh(math.sqrt(2.0 / math.pi) * (x + 0.044715 * x * x * x)))


def _rms_fwd(x, w):
    rstd = lax.rsqrt(jnp.mean(x * x, axis=-1, keepdims=True) + EPS)
    return x * rstd * w


def _rms_bwd(dh, x, w):
    rstd = lax.rsqrt(jnp.mean(x * x, axis=-1, keepdims=True) + EPS)
    xhat = x * rstd
    dxhat = dh * w
    dx = rstd * (dxhat - xhat * jnp.mean(dxhat * xhat, axis=-1, keepdims=True))
    return dx, jnp.sum(dh * xhat, axis=0, keepdims=True)


def _rank_dims(kind, ndim):
    if ndim == 2:
        return {"nn": (((1,), (0,)), ((), ())), "nt": (((1,), (1,)), ((), ())), "tn": (((0,), (0,)), ((), ()))}[kind]
    return {"nn": (((2,), (1,)), ((0,), (0,))), "nt": (((2,), (2,)), ((0,), (0,))), "tn": (((1,), (1,)), ((0,), (0,)))}[kind]


def _rdot(kind, a, b):
    return lax.dot_general(a, b, _rank_dims(kind, a.ndim), preferred_element_type=F32)


def _make_mm(kind, kind_da, kind_db, swap_da, swap_db):
    @jax.custom_vjp
    def mm(a, b):
        return _rdot(kind, a.astype(BF16), b.astype(BF16))

    def fwd(a, b):
        a, b = a.astype(BF16), b.astype(BF16)
        return _rdot(kind, a, b), (a, b)

    def bwd(res, g):
        a, b = res
        g = g.astype(BF16)
        da = _rdot(kind_da, b, g) if swap_da else _rdot(kind_da, g, b)
        db = _rdot(kind_db, g, a) if swap_db else _rdot(kind_db, a, g)
        return da, db

    mm.defvjp(fwd, bwd)
    return mm


mm_nn = _make_mm("nn", "nt", "tn", False, False)
mm_nt = _make_mm("nt", "nn", "tn", False, True)
mm_tn = _make_mm("tn", "nt", "nn", True, False)


def _bcast_const(mat, like):
    return mat if like.ndim == 2 else jnp.broadcast_to(mat, (like.shape[0],) + mat.shape)


def _sel_apply(kind, sel, x):
    hi, mid, lo = _split3(x)
    s = _bcast_const(sel, x)
    return _rdot(kind, s, hi) + _rdot(kind, s, mid) + _rdot(kind, s, lo)


@jax.custom_vjp
def sel_mm(sel, x):
    return _sel_apply("nn", sel, x)


def _sel_mm_fwd(sel, x):
    return _sel_apply("nn", sel, x), sel


def _sel_mm_bwd(sel, g):
    return jnp.zeros_like(sel), _sel_apply("tn", sel, g)


sel_mm.defvjp(_sel_mm_fwd, _sel_mm_bwd)


def _transpose_apply(a):
    n = a.shape[-1]
    eye = (_iota((n, n), 0) == _iota((n, n), 1)).astype(BF16)
    hi, mid, lo = _split3(a)
    e = _bcast_const(eye, a)
    return _rdot("nt", e, hi) + _rdot("nt", e, mid) + _rdot("nt", e, lo)


@jax.custom_vjp
def exact_transpose(a):
    return _transpose_apply(a)


exact_transpose.defvjp(lambda a: (_transpose_apply(a), None), lambda _, g: (_transpose_apply(g),))


def _dot3(kind, a, b):
    ah, al = _split2(a)
    bh, bl = _split2(b)
    return _rdot(kind, ah, bh) + _rdot(kind, ah, bl) + _rdot(kind, al, bh)


def _inverse_levels(lower, n):
    nd = lower.ndim
    r, c = _iota(lower.shape, nd - 2), _iota(lower.shape, nd - 1)
    tinv = (r == c).astype(F32)
    b = 1
    while b < n:
        off = (r // (2 * b) == c // (2 * b)) & (r % (2 * b) >= b) & (c % (2 * b) < b)
        tinv = tinv - _dot3("nn", tinv, _dot3("nn", jnp.where(off, lower, 0.0), tinv))
        b *= 2
    return tinv


@functools.partial(jax.custom_vjp, nondiff_argnums=(1,))
def unit_lower_inverse(lower, n):
    return _inverse_levels(lower, n)


def _unit_lower_inverse_fwd(lower, n):
    tinv = _inverse_levels(lower, n)
    return tinv, tinv


def _unit_lower_inverse_bwd(n, tinv, g):
    return (-_dot3("nt", _dot3("tn", tinv, g), tinv),)


unit_lower_inverse.defvjp(_unit_lower_inverse_fwd, _unit_lower_inverse_bwd)


def mm3_nn(a, b):
    ah = a.astype(BF16).astype(F32)
    al = a - ah
    bh = b.astype(BF16).astype(F32)
    bl = b - bh
    return mm_nn(ah, bh) + mm_nn(ah, bl) + mm_nn(al, bh)


def _seg_apply(x, b):
    hi, mid, lo = _split3(x)
    return _rdot("nn", hi, b) + _rdot("nn", mid, b) + _rdot("nn", lo, b)


@jax.custom_vjp
def _seg_sum(x, b):
    return _seg_apply(x, b)


_seg_sum.defvjp(lambda x, b: (_seg_apply(x, b), b), lambda b, g: (_seg_apply(g, b), jnp.zeros_like(b)))


def _shift_rows(x, halo, k):
    n = x.shape[0]
    ext = jnp.concatenate([jnp.zeros((n - halo.shape[0], x.shape[1]), x.dtype), halo], axis=0)
    t = _iota(x.shape, 0)
    return jnp.where(t >= k, jnp.roll(x, k, axis=0), jnp.roll(ext, k, axis=0))


def _causal_conv(x, halo, w):
    y = x * w[3:4, :]
    for k in range(1, CONV_W):
        y = y + _shift_rows(x, halo, k) * w[3 - k:4 - k, :]
    return y


def _lb_of_layer(hg_lb, layer):
    if layer == 0:
        return jnp.zeros((1, HG_W), F32)
    e = jnp.exp(hg_lb - jnp.max(hg_lb, axis=0, keepdims=True))
    sm = e / jnp.sum(e, axis=0, keepdims=True)
    lb = sm[1:2, :]
    for l in range(2, layer + 1):
        lb = lb + sm[l:l + 1, :]
    return lb


def _chunks(x):
    return x.reshape(x.shape[0] // CHUNK, CHUNK, x.shape[1])


def _hgrn2_pre(hq, hf, hi, lb):
    n, w = CHUNK, HG_W
    q = _chunks(_silu(hq) * (HG_DK ** -0.5))
    f = lb + (1.0 - lb) * _logistic(hf)
    k = _chunks(1.0 - f)
    v = _chunks(hi)
    nb = q.shape[0]
    row, col = _iota((n, n), 0), _iota((n, n), 1)
    levels = [n >> j for j in range(1, n.bit_length())]
    picks = [col <= row] + [col <= (row // (2 * b)) * (2 * b) + b - 1 for b in levels]
    cums = sel_mm(jnp.concatenate(picks, axis=0).astype(BF16), _chunks(jnp.log(f)))
    cum = cums[:, 0:n, :]
    lane_head = _iota((nb, n, w), 2) // HG_DK

    def stack(x):
        return jnp.concatenate([jnp.where(lane_head == h, x, 0.0) for h in range(HG_HEADS)], axis=1)

    t_idx = _iota((nb, n, HG_HEADS * n), 1)
    s_idx = _iota((nb, n, HG_HEADS * n), 2) % n
    p = jnp.where(t_idx == s_idx, mm_nt(q, stack(k)), 0.0)
    for j, b in enumerate(levels):
        ref = cums[:, (j + 1) * n:(j + 2) * n, :]
        qe = q * jnp.exp(jnp.minimum(cum - ref, 0.0))
        ke = k * jnp.exp(jnp.minimum(ref - cum, 0.0))
        mask = (t_idx // (2 * b) == s_idx // (2 * b)) & (t_idx % (2 * b) >= b) & (s_idx % (2 * b) < b)
        p = p + jnp.where(mask, mm_nt(qe, stack(ke)), 0.0)
    last = cum[:, n - 1:n, :]
    return mm_nn(p, stack(v)), q * jnp.exp(cum), k * jnp.exp(last - cum), jnp.exp(last), v


def _gdn_pre(gq, gk, gv, gba, halo_q, halo_k, halo_v, conv_w, alog_vec, dt_vec):
    n = CHUNK
    cq = _silu(_causal_conv(gq, halo_q, conv_w[:, 0:GD_W]))
    ck = _silu(_causal_conv(gk, halo_k, conv_w[:, GD_W:2 * GD_W]))
    cv = _silu(_causal_conv(gv, halo_v, conv_w[:, 2 * GD_W:3 * GD_W]))

    def heads(x):
        return jnp.concatenate([_chunks(x[:, GD_DK * h:GD_DK * (h + 1)]) for h in range(GD_HEADS)], axis=0)

    def l2n(x):
        return x * lax.rsqrt(jnp.sum(x * x, axis=-1, keepdims=True) + EPS)

    q = l2n(heads(cq)) * (GD_DK ** -0.5)
    k = l2n(heads(ck))
    v = heads(cv)
    beta_full = _chunks(_sigmoid(gba))
    g_full = _chunks(-jnp.exp(alog_vec) * _softplus(gba + dt_vec))
    tri = (_iota((n, n), 0) >= _iota((n, n), 1)).astype(BF16)
    cum_full = sel_mm(tri, g_full)
    beta = jnp.concatenate([beta_full[:, :, h:h + 1] for h in range(GD_HEADS)], axis=0)
    cum = jnp.concatenate([cum_full[:, :, GD_HEADS + h:GD_HEADS + h + 1] for h in range(GD_HEADS)], axis=0)
    bsz = cum.shape[0]
    ccol = jnp.broadcast_to(cum, (bsz, n, n))
    diff = ccol - exact_transpose(ccol)
    r_i, c_i = _iota((bsz, n, n), 1), _iota((bsz, n, n), 2)
    decay = jnp.exp(jnp.minimum(diff, 0.0))
    kb = k * beta
    lower = jnp.where(r_i > c_i, mm_nt(kb, k) * decay, 0.0)
    tinv = unit_lower_inverse(lower, n)
    ecum = jnp.exp(cum)
    sol = mm3_nn(tinv, jnp.concatenate([v * beta, kb * ecum], axis=2))
    scores = jnp.where(r_i >= c_i, mm_nt(q, k) * decay, 0.0)
    last = cum[:, n - 1:n, :]
    return sol[:, :, 0:GD_DK], sol[:, :, GD_DK:2 * GD_DK], scores, q * ecum, k * jnp.exp(last - cum), jnp.exp(last)


def _rglru_pre(rx, halo_x, conv_w, conv_b, wr, br, wi, bi, lam):
    xc = _causal_conv(rx, halo_x, conv_w) + conv_b
    r = _logistic(mm_nn(xc, wr) + br)
    ig = _sigmoid(mm_nn(xc, wi) + bi)
    two = 2.0 * (-RG_C * r * _softplus(-lam))
    a = jnp.exp(0.5 * two)
    one_minus = -jnp.tanh(0.5 * two) * (jnp.exp(two) + 1.0)
    acc_a, acc_b = a, jnp.sqrt(one_minus) * (ig * xc)
    t = _iota(acc_a.shape, 0) % CHUNK
    k = 1
    while k < CHUNK:
        sa = jnp.where(t >= k, jnp.roll(acc_a, k, axis=0), 1.0)
        sb = jnp.where(t >= k, jnp.roll(acc_b, k, axis=0), 0.0)
        acc_b = acc_a * sb + acc_b
        acc_a = acc_a * sa
        k *= 2
    return acc_a, acc_b


MIXER_PARAMS = ("hg_lb", "hg_norm_w", "gd_conv_w", "gd_alog_vec", "gd_dt_vec", "gd_norm_w",
                "rg_conv_w", "rg_conv_b", "rg_wr_bd", "rg_br", "rg_wi_bd", "rg_bi", "rg_lambda")


def mixer_pre(layer, proj, halo, params):
    (hg_lb, _, gd_conv_w, alog_vec, dt_vec, _, rg_conv_w, rg_conv_b, wr, br, wi, bi, lam) = params
    hg = _hgrn2_pre(proj[:, O_HQ:O_HQ + 256], proj[:, O_HF:O_HF + 256], proj[:, O_HI:O_HI + 256], _lb_of_layer(hg_lb, layer))
    gd = _gdn_pre(proj[:, O_GQ:O_GQ + 512], proj[:, O_GK:O_GK + 512], proj[:, O_GV:O_GV + 512], proj[:, O_GBA:O_GBA + 128],
                  halo[:, O_GQ:O_GQ + 512], halo[:, O_GK:O_GK + 512], halo[:, O_GV:O_GV + 512], gd_conv_w, alog_vec, dt_vec)
    rg = _rglru_pre(proj[:, O_RX:O_RX + 256], halo[:, O_RX:O_RX + 256], rg_conv_w, rg_conv_b, wr, br, wi, bi, lam)
    return hg, gd, rg


def mixer_pre_chunk(pre, c, nb):
    hg, gd, rg = pre
    pick = lambda x: jnp.concatenate([x[h * nb + c:h * nb + c + 1] for h in range(GD_HEADS)], axis=0)
    return (tuple(x[c] for x in hg), tuple(pick(x) for x in gd), tuple(x[c * CHUNK:(c + 1) * CHUNK, :] for x in rg))


def mixer_pre_unchunk(parts, nb):
    hg = tuple(jnp.concatenate([parts[c][0][i][None] for c in range(nb)], axis=0) for i in range(len(parts[0][0])))
    gd = tuple(jnp.concatenate([parts[c][1][i][h:h + 1] for h in range(GD_HEADS) for c in range(nb)], axis=0)
               for i in range(len(parts[0][1])))
    rg = tuple(jnp.concatenate([parts[c][2][i] for c in range(nb)], axis=0) for i in range(len(parts[0][2])))
    return hg, gd, rg


def mixer_rec(pre_c, st_hg, s_gd, h_rg):
    (o_intra, qe, kd, elast, v), (u, w, scores, gqe, gkd, gel), (acc_a, acc_b) = pre_c
    o_hg = o_intra + mm_nt(qe, st_hg)
    blk = _iota((HG_W, HG_W), 0) // HG_DK == _iota((HG_W, HG_W), 1) // HG_DK
    st_new = st_hg * elast + jnp.where(blk, mm_tn(v, kd), 0.0)
    v_new = u - mm_nn(w, s_gd)
    o_gd = mm_nn(gqe, s_gd) + mm_nn(scores, v_new)
    s_new = s_gd * gel + mm_tn(gkd, v_new)
    hs = acc_b + acc_a * h_rg
    return (o_hg, o_gd, hs), (st_new, s_new, hs[CHUNK - 1:CHUNK, :])


def mixer_post(o_hg, o_gd, hs, proj, params):
    (_, hg_norm_w, _, _, _, gd_norm_w, _, _, _, _, _, _, _) = params
    blk = (_iota((HG_W, HG_W), 0) // HG_DK == _iota((HG_W, HG_W), 1) // HG_DK).astype(BF16)
    ms = _seg_sum(o_hg * o_hg, blk) * (1.0 / HG_DK)
    y_hg = o_hg * lax.rsqrt(ms + EPS) * jnp.concatenate([hg_norm_w] * HG_HEADS, axis=1) * _silu(proj[:, O_HG:O_HG + 256])
    parts = []
    for h in range(GD_HEADS):
        o = o_gd[:, GD_DK * h:GD_DK * (h + 1)]
        parts.append(o * lax.rsqrt(jnp.mean(o * o, axis=-1, keepdims=True) + EPS) * gd_norm_w)
    y_gd = jnp.concatenate(parts, axis=1) * _silu(proj[:, O_GZ:O_GZ + 512])
    y_rg = hs * _gelu_tanh(proj[:, O_RGATE:O_RGATE + 256])
    return jnp.concatenate([y_hg, y_gd, y_rg], axis=1)


def mixer_step_forward(layer, proj, halo, states, params):
    nb = proj.shape[0] // CHUNK
    pre = mixer_pre(layer, proj, halo, params)
    outs, entered = [], []
    for c in range(nb):
        entered.append(states)
        o, states = mixer_rec(mixer_pre_chunk(pre, c, nb), *states)
        outs.append(o)
    o_hg = jnp.concatenate([o[0] for o in outs], axis=0)
    o_gd = jnp.concatenate([jnp.concatenate([o[1][h] for h in range(GD_HEADS)], axis=1) for o in outs], axis=0)
    hs = jnp.concatenate([o[2] for o in outs], axis=0)
    return mixer_post(o_hg, o_gd, hs, proj, params), (o_hg, o_gd, hs), entered, states


def _full(shape):
    return pl.BlockSpec(shape, lambda *_: (0,) * len(shape))


def _token_tile(seq, want):
    return min(want, seq)


def _gather_phases(src, out, send_sems, recv_sems, local_sem):
    x, y, c = lax.axis_index("x"), lax.axis_index("y"), lax.axis_index("c")
    me = 4 * x + 2 * y + c
    sibling = (x, y, 1 - c)
    chips = [(1 - x, y), (x, 1 - y), (1 - x, 1 - y)]

    def copy(k, slot, to, from_src=False):
        return pltpu.make_async_remote_copy(
            src_ref=src if from_src else out.at[slot], dst_ref=out.at[slot], send_sem=send_sems.at[k],
            recv_sem=recv_sems.at[k], device_id=to, device_id_type=pl.DeviceIdType.MESH)

    def first_sends():
        return [copy(0, me, sibling, True)] + [copy(1 + j, me, (px, py, c), True) for j, (px, py) in enumerate(chips)]

    def forward(j):
        px, py = chips[j]
        return copy(4 + j, 4 * px + 2 * py + c, sibling)

    def start():
        pltpu.make_async_copy(src, out.at[me], local_sem).start()
        for cp in first_sends():
            cp.start()

    def middle():
        for j, (px, py) in enumerate(chips):
            copy(1 + j, 4 * px + 2 * py + c, (x, y, c)).wait_recv()
            forward(j).start()

    def finish():
        copy(0, 4 * x + 2 * y + (1 - c), (x, y, c)).wait_recv()
        for j, (px, py) in enumerate(chips):
            copy(4 + j, 4 * px + 2 * py + (1 - c), (x, y, c)).wait_recv()
        for cp in first_sends() + [forward(j) for j in range(len(chips))]:
            cp.wait_send()
        pltpu.make_async_copy(src, out.at[me], local_sem).wait()

    return start, middle, finish


def ffn_fwd(x, nw, pack, layer, which, blk0=0, carry=None):
    seq = x.shape[0]
    t = _token_tile(seq, 1024)
    n_i = seq // t
    bg, bu, bd = (_ffn_block(layer, which, p) - blk0 for p in "gud")

    def body(*refs):
        if carry is None:
            x_ref, nw_ref, wg_ref, wu_ref, wd_ref, xo_ref, h_ref, g_ref, u_ref, a_ref, h_scr, acc = refs
        else:
            (x_ref, nw_ref, wg_ref, wu_ref, wd_ref, c_ref, xo_ref, h_ref, g_ref, u_ref, a_ref, co_ref, h_scr, acc,
             send_sems, recv_sems, local_sem) = refs
            start, middle, finish = _gather_phases(c_ref, co_ref, send_sems, recv_sems, local_sem)
            first = (pl.program_id(0) == 0) & (pl.program_id(1) == 0)
            half = (pl.program_id(0) == n_i // 2) & (pl.program_id(1) == 0)
            pl.when(first)(start)
            if n_i > 1:
                pl.when(half)(middle)
        k = pl.program_id(1)

        @pl.when(k == 0)
        def _():
            xv = x_ref[...]
            h = _rms_fwd(xv, nw_ref[...]).astype(BF16)
            h_scr[...] = h
            h_ref[...] = h
            acc[...] = xv

        h = h_scr[...]
        g = _dot_nt(h, wg_ref[...].reshape(FFN_COLS, D_MODEL))
        u = _dot_nt(h, wu_ref[...].reshape(FFN_COLS, D_MODEL))
        a = (g * _sigmoid(g) * u).astype(BF16)
        g_ref[...] = g.astype(BF16)
        u_ref[...] = u.astype(BF16)
        a_ref[...] = a
        acc[...] += 0.5 * _dot_nn(a, wd_ref[...].reshape(FFN_COLS, D_MODEL))

        @pl.when(k == FFN_STEPS - 1)
        def _():
            xo_ref[...] = acc[...]

        if carry is not None:
            @pl.when((pl.program_id(0) == n_i - 1) & (k == FFN_STEPS - 1))
            def _():
                if n_i == 1:
                    middle()
                finish()

    wspec = lambda blk: pl.BlockSpec((FFN_GROUP, FF_PAD, D_MODEL), lambda i, k: (k, blk, 0))
    act = pl.BlockSpec((t, FFN_COLS), lambda i, k: (i, k))
    row = pl.BlockSpec((t, D_MODEL), lambda i, k: (i, 0))
    ff = N_DEV * FF_PAD
    in_specs = [row, _full((1, D_MODEL)), wspec(bg), wspec(bu), wspec(bd)]
    out_specs = [row, row, act, act, act]
    out_shape = [jax.ShapeDtypeStruct((seq, D_MODEL), F32), jax.ShapeDtypeStruct((seq, D_MODEL), BF16)]
    out_shape += [jax.ShapeDtypeStruct((seq, ff), BF16)] * 3
    scratch = [pltpu.VMEM((t, D_MODEL), BF16), pltpu.VMEM((t, D_MODEL), F32)]
    args = [x, nw, pack, pack, pack]
    if carry is not None:
        any_spec = pl.BlockSpec(memory_space=pl.ANY)
        in_specs.append(any_spec)
        out_specs.append(any_spec)
        out_shape.append(jax.ShapeDtypeStruct((N_DEV,) + carry.shape, carry.dtype))
        scratch += [pltpu.SemaphoreType.DMA((7,)), pltpu.SemaphoreType.DMA((7,)), pltpu.SemaphoreType.DMA]
        args.append(carry)
    return pl.pallas_call(
        body, name=f"ffn_fwd_l{layer}_{which}", grid=(n_i, FFN_STEPS), in_specs=in_specs, out_specs=out_specs,
        out_shape=out_shape, scratch_shapes=scratch,
        compiler_params=_cparams(("parallel", "arbitrary") if carry is None else ("arbitrary", "arbitrary")),
    )(*args)


def ffn_bwd(dout, x, nw, g, u, pack, layer, which, blk0=0):
    seq = x.shape[0]
    t = _token_tile(seq, 1024)
    bg, bu, bd = (_ffn_block(layer, which, p) - blk0 for p in "gud")

    def body(do_ref, x_ref, nw_ref, g_ref, u_ref, wg_ref, wu_ref, wd_ref, dx_ref, dg_ref, du_ref, dnw_ref, doh, dh):
        i, k = pl.program_id(0), pl.program_id(1)

        @pl.when(k == 0)
        def _():
            doh[...] = (0.5 * do_ref[...]).astype(BF16)
            dh[...] = jnp.zeros_like(dh)

        @pl.when((k == 0) & (i == 0))
        def _():
            dnw_ref[...] = jnp.zeros_like(dnw_ref)

        da = _dot_nt(doh[...], wd_ref[...].reshape(FFN_COLS, D_MODEL))
        gv = g_ref[...].astype(F32)
        uv = u_ref[...].astype(F32)
        s = _sigmoid(gv)
        dg = (da * uv * (s * (1.0 + gv * (1.0 - s)))).astype(BF16)
        du = (da * gv * s).astype(BF16)
        dg_ref[...] = dg
        du_ref[...] = du
        w_gu = jnp.concatenate([wg_ref[...].reshape(FFN_COLS, D_MODEL), wu_ref[...].reshape(FFN_COLS, D_MODEL)], axis=0)
        dh[...] += _dot_nn(jnp.concatenate([dg, du], axis=1), w_gu)

        @pl.when(k == FFN_STEPS - 1)
        def _():
            dx, dw = _rms_bwd(dh[...], x_ref[...], nw_ref[...])
            dx_ref[...] = do_ref[...] + dx
            dnw_ref[...] += dw

    wspec = lambda blk: pl.BlockSpec((FFN_GROUP, FF_PAD, D_MODEL), lambda i, k: (k, blk, 0))
    act = pl.BlockSpec((t, FFN_COLS), lambda i, k: (i, k))
    row = pl.BlockSpec((t, D_MODEL), lambda i, k: (i, 0))
    row_once = pl.BlockSpec((t, D_MODEL), lambda i, k: (i, 0), pipeline_mode=pl.Buffered(1))
    ff = N_DEV * FF_PAD
    return pl.pallas_call(
        body, name=f"ffn_bwd_l{layer}_{which}", grid=(seq // t, FFN_STEPS),
        in_specs=[row_once, row_once, _full((1, D_MODEL)), act, act, wspec(bg), wspec(bu), wspec(bd)],
        out_specs=[row, act, act, _full((1, D_MODEL))],
        out_shape=[jax.ShapeDtypeStruct((seq, D_MODEL), F32), jax.ShapeDtypeStruct((seq, ff), BF16),
                   jax.ShapeDtypeStruct((seq, ff), BF16), jax.ShapeDtypeStruct((1, D_MODEL), F32)],
        scratch_shapes=[pltpu.VMEM((t, D_MODEL), BF16), pltpu.VMEM((t, D_MODEL), F32)],
        compiler_params=_cparams(("arbitrary", "arbitrary")),
    )(dout, x, nw, g, u, pack, pack, pack)


def tn_matmul(a, b, scale, tm, name, gbuf=None, blk_of_m=None):
    seq, m_total = a.shape
    tk = _token_tile(seq, 1024 if gbuf is not None else 512)
    nk = seq // tk
    n_m = m_total // tm

    def body(*refs):
        if gbuf is None:
            a_ref, b_ref, o_ref = refs
            acc = o_ref
        else:
            a_ref, b_ref, _, o_ref, acc = refs
        kk = pl.program_id(0)

        @pl.when(kk == 0)
        def _():
            acc[...] = jnp.zeros_like(acc)

        bv = b_ref[...].astype(BF16)
        for m in range(n_m):
            acc[pl.ds(m * tm, tm), :] += _dot_tn(a_ref[:, pl.ds(m * tm, tm)].astype(BF16), bv)

        @pl.when(kk == nk - 1)
        def _():
            if gbuf is None:
                if scale != 1.0:
                    o_ref[...] = o_ref[...] * scale
            else:
                for m in range(n_m):
                    o_ref[_grad_slot(m)] = (acc[pl.ds(m * tm, tm), :] * scale).astype(o_ref.dtype)

    in_specs = [pl.BlockSpec((tk, m_total), lambda k: (k, 0)), pl.BlockSpec((tk, D_MODEL), lambda k: (k, 0))]
    args = [a, b]
    if gbuf is None:
        out_spec = pl.BlockSpec((m_total, D_MODEL), lambda k: (0, 0))
        out_shape = jax.ShapeDtypeStruct((m_total, D_MODEL), F32)
        aliases, scratch = {}, []
    else:
        assert n_m == N_DEV
        in_specs.append(pl.BlockSpec(memory_space=pl.ANY))
        args.append(gbuf)
        out_spec = pl.BlockSpec((N_DEV, tm, D_MODEL), lambda k: (0, blk_of_m, 0))
        out_shape = jax.ShapeDtypeStruct(gbuf.shape, gbuf.dtype)
        aliases, scratch = {2: 0}, [pltpu.VMEM((m_total, D_MODEL), F32)]
    return pl.pallas_call(
        body, name=name, grid=(nk,), in_specs=in_specs, out_specs=out_spec, out_shape=out_shape,
        scratch_shapes=scratch, input_output_aliases=aliases, compiler_params=_cparams(("arbitrary",)),
    )(*args)


def in_proj_fwd(x, nw, wt, layer):
    seq = x.shape[0]
    t = _token_tile(seq, 512)

    def body(x_ref, nw_ref, w_ref, p_ref, h_ref):
        h = _rms_fwd(x_ref[...], nw_ref[...]).astype(BF16)
        h_ref[...] = h
        p_ref[...] = _dot_nt(h, w_ref[...])

    row = pl.BlockSpec((t, D_MODEL), lambda i: (i, 0))
    return pl.pallas_call(
        body, name=f"in_proj_fwd_l{layer}", grid=(seq // t,),
        in_specs=[row, _full((1, D_MODEL)), _full((NP, D_MODEL))],
        out_specs=[pl.BlockSpec((t, NP), lambda i: (i, 0)), row],
        out_shape=[jax.ShapeDtypeStruct((seq, NP), F32), jax.ShapeDtypeStruct((seq, D_MODEL), BF16)],
        compiler_params=_cparams(("parallel",)),
    )(x, nw, wt)


def in_proj_bwd(dproj, x, nw, dres, wt, layer):
    seq = x.shape[0]
    t = _token_tile(seq, 512)

    def body(dp_ref, x_ref, nw_ref, dr_ref, w_ref, dx_ref, dnw_ref):
        @pl.when(pl.program_id(0) == 0)
        def _():
            dnw_ref[...] = jnp.zeros_like(dnw_ref)

        dh = _dot_nn(dp_ref[...], w_ref[...])
        dx, dw = _rms_bwd(dh, x_ref[...], nw_ref[...])
        dx_ref[...] = dr_ref[...] + dx
        dnw_ref[...] += dw

    row = pl.BlockSpec((t, D_MODEL), lambda i: (i, 0))
    return pl.pallas_call(
        body, name=f"in_proj_bwd_l{layer}", grid=(seq // t,),
        in_specs=[pl.BlockSpec((t, NP), lambda i: (i, 0)), row, _full((1, D_MODEL)), row, _full((NP, D_MODEL))],
        out_specs=[row, _full((1, D_MODEL))],
        out_shape=[jax.ShapeDtypeStruct((seq, D_MODEL), F32), jax.ShapeDtypeStruct((1, D_MODEL), F32)],
        compiler_params=_cparams(("arbitrary",)),
    )(dproj, x, nw, dres, wt)


def out_proj_fwd(x, y, w, layer):
    seq = x.shape[0]
    t = _token_tile(seq, 512)

    def body(x_ref, y_ref, w_ref, o_ref):
        o_ref[...] = x_ref[...] + _dot_nn(y_ref[...], w_ref[...])

    row = pl.BlockSpec((t, D_MODEL), lambda i: (i, 0))
    return pl.pallas_call(
        body, name=f"out_proj_fwd_l{layer}", grid=(seq // t,),
        in_specs=[row, row, _full((D_MODEL, D_MODEL))], out_specs=row,
        out_shape=jax.ShapeDtypeStruct((seq, D_MODEL), F32), compiler_params=_cparams(("parallel",)),
    )(x, y, w)


def out_proj_bwd(dx, w, layer):
    seq = dx.shape[0]
    t = _token_tile(seq, 512)

    def body(d_ref, w_ref, o_ref):
        o_ref[...] = _dot_nt(d_ref[...].astype(BF16), w_ref[...])

    row = pl.BlockSpec((t, D_MODEL), lambda i: (i, 0))
    return pl.pallas_call(
        body, name=f"out_proj_bwd_l{layer}", grid=(seq // t,),
        in_specs=[row, _full((D_MODEL, D_MODEL))], out_specs=row,
        out_shape=jax.ShapeDtypeStruct((seq, D_MODEL), F32), compiler_params=_cparams(("parallel",)),
    )(dx, w)


def loss_head(x, nw, target):
    seq = x.shape[0]
    t = _token_tile(seq, 512)

    def body(x_ref, nw_ref, t_ref, loss_ref, dx_ref, dnw_ref):
        @pl.when(pl.program_id(0) == 0)
        def _():
            loss_ref[...] = jnp.zeros_like(loss_ref)
            dnw_ref[...] = jnp.zeros_like(dnw_ref)

        xv, w = x_ref[...], nw_ref[...]
        err = _rms_fwd(xv, w) - t_ref[...]
        loss_ref[...] += 0.5 * jnp.sum(jnp.mean(err * err, axis=-1, keepdims=True), axis=0, keepdims=True)
        dx, dw = _rms_bwd(err * (1.0 / D_MODEL), xv, w)
        dx_ref[...] = dx
        dnw_ref[...] += dw

    row = pl.BlockSpec((t, D_MODEL), lambda i: (i, 0))
    return pl.pallas_call(
        body, name="loss_head", grid=(seq // t,),
        in_specs=[row, _full((1, D_MODEL)), row], out_specs=[_full((1, 128)), row, _full((1, D_MODEL))],
        out_shape=[jax.ShapeDtypeStruct((1, 128), F32), jax.ShapeDtypeStruct((seq, D_MODEL), F32),
                   jax.ShapeDtypeStruct((1, D_MODEL), F32)],
        compiler_params=_cparams(("arbitrary",)),
    )(x, nw, target)


_MIXER_PARAM_SHAPES = ((DEPTH, HG_W), (1, HG_DK), (CONV_W, 3 * GD_W), (1, 128), (1, 128), (1, GD_DK),
                       (CONV_W, RG_W), (1, RG_W), (RG_W, RG_W), (1, RG_W), (RG_W, RG_W), (1, RG_W), (1, RG_W))
_STATE_SHAPES = ((HG_W, HG_W), (GD_HEADS, GD_DK, GD_DK), (1, RG_W))


def _state_spec(cps, sh, index):
    return pl.BlockSpec((cps,) + sh, lambda i: (index(i),) + (0,) * len(sh))


def mixer_fwd(layer, proj, params):
    seq = proj.shape[0]
    cps = min(MIXER_CHUNKS_FWD, seq // CHUNK)
    rows = cps * CHUNK
    n = seq // rows

    def body(*refs):
        p_ref, halo_ref = refs[0], refs[1]
        prm = refs[2:15]
        y_ref, o_ref, st_out, s_out, h_out = refs[15:20]
        st, s, h = refs[20:23]
        i = pl.program_id(0)

        @pl.when(i == 0)
        def _():
            st[...] = jnp.zeros_like(st)
            s[...] = jnp.zeros_like(s)
            h[...] = jnp.zeros_like(h)

        halo = jnp.where(i > 0, halo_ref[...], 0.0)
        y, outs, entered, left = mixer_step_forward(layer, p_ref[...], halo, (st[...], s[...], h[...]),
                                                    tuple(r[...] for r in prm))
        y_ref[...] = y.astype(BF16)
        o_ref[...] = jnp.concatenate(outs, axis=1)
        for c in range(cps):
            st_out[c], s_out[c], h_out[c] = entered[c]
        st[...], s[...], h[...] = left

    in_specs = [pl.BlockSpec((rows, NP), lambda i: (i, 0)),
                pl.BlockSpec((8, NP), lambda i: (jnp.maximum(i * (rows // 8) - 1, 0), 0))]
    in_specs += [_full(sh) for sh in _MIXER_PARAM_SHAPES]
    out_specs = [pl.BlockSpec((rows, D_MODEL), lambda i: (i, 0))] * 2
    out_specs += [_state_spec(cps, sh, lambda i: i) for sh in _STATE_SHAPES]
    out_shape = [jax.ShapeDtypeStruct((seq, D_MODEL), BF16), jax.ShapeDtypeStruct((seq, D_MODEL), F32)]
    out_shape += [jax.ShapeDtypeStruct((seq // CHUNK,) + sh, F32) for sh in _STATE_SHAPES]
    return pl.pallas_call(
        body, name=f"mixer_fwd_l{layer}", grid=(n,), in_specs=in_specs, out_specs=out_specs, out_shape=out_shape,
        scratch_shapes=[pltpu.VMEM(sh, F32) for sh in _STATE_SHAPES],
        compiler_params=_cparams(("arbitrary",)),
    )(proj, proj, *params)


def mixer_bwd(layer, proj, dy, outs, states, params):
    seq = proj.shape[0]
    cps = min(MIXER_CHUNKS_BWD, seq // CHUNK)
    rows = cps * CHUNK
    n = seq // rows

    def body(*refs):
        p_ref, halo_ref, dy_ref, o_ref, st_ref, s_ref, h_ref = refs[0:7]
        prm = refs[7:20]
        dp_ref = refs[20]
        dprm = refs[21:34]
        dst, ds, dh, dhalo = refs[34:38]
        i = pl.program_id(0)
        r = n - 1 - i

        @pl.when(i == 0)
        def _():
            for ref in (dst, ds, dh, dhalo) + tuple(dprm):
                ref[...] = jnp.zeros_like(ref)

        params_v = tuple(q[...] for q in prm)
        proj_v = p_ref[...]
        halo = jnp.where(r > 0, halo_ref[...], 0.0)
        o = o_ref[...]
        _, post_vjp = jax.vjp(mixer_post, o[:, 0:HG_W], o[:, HG_W:HG_W + GD_W], o[:, HG_W + GD_W:D_MODEL], proj_v, params_v)
        d_ohg, d_ogd, d_hs, dproj_post, dparams_post = post_vjp(dy_ref[...])
        pre, pre_vjp = jax.vjp(functools.partial(mixer_pre, layer), proj_v, halo, params_v)
        dstates = (dst[...], ds[...], dh[...])
        parts = [None] * cps
        for c in reversed(range(cps)):
            rs = slice(c * CHUNK, (c + 1) * CHUNK)
            _, rec_vjp = jax.vjp(mixer_rec, mixer_pre_chunk(pre, c, cps), st_ref[c], s_ref[c], h_ref[c])
            d_ogd_c = jnp.concatenate([d_ogd[rs, GD_DK * hd:GD_DK * (hd + 1)][None] for hd in range(GD_HEADS)], axis=0)
            parts[c], dst_c, ds_c, dh_c = rec_vjp(((d_ohg[rs, :], d_ogd_c, d_hs[rs, :]), dstates))
            dstates = (dst_c, ds_c, dh_c)
        dproj_pre, dhalo_n, dparams_pre = pre_vjp(mixer_pre_unchunk(parts, cps))
        carry = jnp.concatenate([jnp.zeros((rows - 8, NP), F32), dhalo[...]], axis=0)
        dp_ref[...] = (dproj_post + dproj_pre + carry).astype(BF16)
        dst[...], ds[...], dh[...] = dstates
        dhalo[...] = dhalo_n
        for ref, a, b in zip(dprm, dparams_post, dparams_pre):
            ref[...] += a + b

    rev = lambda i: n - 1 - i
    once = pl.Buffered(1)
    in_specs = [pl.BlockSpec((rows, NP), lambda i: (rev(i), 0), pipeline_mode=once),
                pl.BlockSpec((8, NP), lambda i: (jnp.maximum(rev(i) * (rows // 8) - 1, 0), 0)),
                pl.BlockSpec((rows, D_MODEL), lambda i: (rev(i), 0), pipeline_mode=once),
                pl.BlockSpec((rows, D_MODEL), lambda i: (rev(i), 0), pipeline_mode=once)]
    in_specs += [_state_spec(cps, sh, rev) for sh in _STATE_SHAPES]
    in_specs += [_full(sh) for sh in _MIXER_PARAM_SHAPES]
    out_specs = [pl.BlockSpec((rows, NP), lambda i: (rev(i), 0))] + [_full(sh) for sh in _MIXER_PARAM_SHAPES]
    out_shape = [jax.ShapeDtypeStruct((seq, NP), BF16)] + [jax.ShapeDtypeStruct(sh, F32) for sh in _MIXER_PARAM_SHAPES]
    return pl.pallas_call(
        body, name=f"mixer_bwd_l{layer}", grid=(n,), in_specs=in_specs, out_specs=out_specs, out_shape=out_shape,
        scratch_shapes=[pltpu.VMEM(sh, F32) for sh in _STATE_SHAPES] + [pltpu.VMEM((8, NP), F32)],
        compiler_params=pltpu.CompilerParams(dimension_semantics=("arbitrary",), vmem_limit_bytes=MIXER_BWD_VMEM_LIMIT),
    )(proj, proj, dy, outs, *states, *params)


def _win_segments():
    out = []
    for k in range(N_DEV):
        a, b = NIN_SHARD * k, NIN_SHARD * (k + 1)
        if a < GBA_SPLIT < b:
            out.append((k, 0, GBA_SPLIT - a, a))
            out.append((k, GBA_SPLIT - a, b - GBA_SPLIT, O_RX))
        elif b <= GBA_SPLIT:
            out.append((k, 0, NIN_SHARD, a))
        else:
            out.append((k, 0, NIN_SHARD, a + O_RX - GBA_SPLIT))
    return out


def win_to_padded(wt_shards, layer):
    lanes = 256

    def body(src, dst, scr):
        scr[...] = jnp.zeros_like(scr)
        for (k, s0, rows, d0) in _win_segments():
            scr[pl.ds(d0, rows), :] = src[k, pl.ds(s0, rows), :].astype(F32)
        dst[...] = scr[...].astype(BF16)

    return pl.pallas_call(
        body, name=f"win_to_padded_l{layer}", grid=(D_MODEL // lanes,),
        in_specs=[pl.BlockSpec((N_DEV, NIN_SHARD_PAD, lanes), lambda j: (0, 0, j))],
        out_specs=pl.BlockSpec((NP, lanes), lambda j: (0, j)),
        out_shape=jax.ShapeDtypeStruct((NP, D_MODEL), BF16),
        scratch_shapes=[pltpu.VMEM((NP, lanes), F32)], compiler_params=_cparams(("parallel",)),
    )(wt_shards)


def win_grad_to_pack(dwt, gbuf, layer):
    lanes = 256
    blk = (OFF_WIN + layer * NIN_SHARD_PAD) // NIN_SHARD_PAD

    def body(src, _, dst, scr):
        scr[...] = jnp.zeros_like(scr)
        for (k, s0, rows, d0) in _win_segments():
            scr[_grad_slot(k), pl.ds(s0, rows), :] = src[pl.ds(d0, rows), :]
        dst[...] = scr[...].astype(dst.dtype)

    return pl.pallas_call(
        body, name=f"win_grad_to_pack_l{layer}", grid=(D_MODEL // lanes,),
        in_specs=[pl.BlockSpec((NP, lanes), lambda j: (0, j)), pl.BlockSpec(memory_space=pl.ANY)],
        out_specs=pl.BlockSpec((N_DEV, NIN_SHARD_PAD, lanes), lambda j: (0, blk, j)),
        out_shape=jax.ShapeDtypeStruct(gbuf.shape, gbuf.dtype), input_output_aliases={1: 0},
        scratch_shapes=[pltpu.VMEM((N_DEV, NIN_SHARD_PAD, lanes), F32)],
        compiler_params=_cparams(("parallel",)),
    )(dwt, gbuf)


def sum_adamw(parts, w, m, v, name, rows_per_step):
    n, rows, cols = parts.shape
    tr = min(rows_per_step, rows)
    c1 = 1.0 - ADAM_B1 ** ADAM_STEP
    c2 = 1.0 - ADAM_B2 ** ADAM_STEP

    def body(p_ref, w_ref, m_ref, v_ref, g_ref, d_ref, mo_ref, vo_ref):
        g = p_ref[0].astype(F32)
        for j in range(1, n):
            g = g + p_ref[j].astype(F32)
        mn = ADAM_B1 * m_ref[...] + (1.0 - ADAM_B1) * g
        vn = ADAM_B2 * v_ref[...] + (1.0 - ADAM_B2) * (g * g)
        g_ref[...] = g
        mo_ref[...] = mn
        vo_ref[...] = vn
        d_ref[...] = -ADAM_LR * ((mn / c1) / (jnp.sqrt(vn / c2) + ADAM_EPS) + ADAM_WD * w_ref[...])

    blk = pl.BlockSpec((tr, cols), lambda i: (i, 0))
    return pl.pallas_call(
        body, name=name, grid=(rows // tr,),
        in_specs=[pl.BlockSpec((n, tr, cols), lambda i: (0, i, 0)), blk, blk, blk], out_specs=[blk] * 4,
        out_shape=[jax.ShapeDtypeStruct((rows, cols), F32)] * 4, compiler_params=_cparams(("parallel",)),
    )(parts, w, m, v)


def gather_two_level(arrays, name):
    n = len(arrays)

    def body(*refs):
        ins, outs = refs[:n], refs[n:2 * n]
        send_sems, recv_sems, local_sems = refs[2 * n:]
        x, y, c = lax.axis_index("x"), lax.axis_index("y"), lax.axis_index("c")
        me = 4 * x + 2 * y + c
        sib_slot = 4 * x + 2 * y + (1 - c)
        chips = [(1 - x, y), (x, 1 - y), (1 - x, 1 - y)]

        def copy(a, k, slot, to, src=None):
            return pltpu.make_async_remote_copy(
                src_ref=outs[a].at[slot] if src is None else src, dst_ref=outs[a].at[slot],
                send_sem=send_sems.at[a * 7 + k], recv_sem=recv_sems.at[a * 7 + k],
                device_id=to, device_id_type=pl.DeviceIdType.MESH)

        local, started = [], []
        for a in range(n):
            cp = pltpu.make_async_copy(ins[a], outs[a].at[me], local_sems.at[a])
            cp.start()
            local.append(cp)
            first = [copy(a, 0, me, (x, y, 1 - c), src=ins[a])]
            first += [copy(a, 1 + j, me, (px, py, c), src=ins[a]) for j, (px, py) in enumerate(chips)]
            for cp in first:
                cp.start()
            started += first
        for a in range(n):
            for j, (px, py) in enumerate(chips):
                slot = 4 * px + 2 * py + c
                copy(a, 1 + j, slot, (x, y, c)).wait_recv()
                fwd = copy(a, 4 + j, slot, (x, y, 1 - c))
                fwd.start()
                started.append(fwd)
        for a in range(n):
            copy(a, 0, sib_slot, (x, y, c)).wait_recv()
            for j, (px, py) in enumerate(chips):
                copy(a, 4 + j, 4 * px + 2 * py + (1 - c), (x, y, c)).wait_recv()
        for cp in started:
            cp.wait_send()
        for cp in local:
            cp.wait()

    out_shape = [jax.ShapeDtypeStruct((N_DEV,) + a.shape, a.dtype) for a in arrays]
    any_spec = pl.BlockSpec(memory_space=pl.ANY)
    return pl.pallas_call(
        body, name=name, in_specs=[any_spec] * n, out_specs=[any_spec] * n, out_shape=out_shape,
        scratch_shapes=[pltpu.SemaphoreType.DMA((7 * n,)), pltpu.SemaphoreType.DMA((7 * n,)), pltpu.SemaphoreType.DMA((n,))],
        compiler_params=pltpu.CompilerParams(has_side_effects=True),
    )(*arrays)


def _grad_slot(d):
    return (d % 2) * (N_DEV // 2) + d // 2


def sibling_exchange(gbuf, name):
    half = N_DEV // 2

    def body(g_ref, o_ref, send_sem, recv_sem):
        x, y, c = lax.axis_index("x"), lax.axis_index("y"), lax.axis_index("c")
        cp = pltpu.make_async_remote_copy(
            src_ref=g_ref.at[pl.ds((1 - c) * half, half)], dst_ref=o_ref, send_sem=send_sem, recv_sem=recv_sem,
            device_id=(x, y, 1 - c), device_id_type=pl.DeviceIdType.MESH)
        cp.start()
        cp.wait()

    any_spec = pl.BlockSpec(memory_space=pl.ANY)
    return pl.pallas_call(
        body, name=name, in_specs=[any_spec], out_specs=any_spec,
        out_shape=jax.ShapeDtypeStruct((half,) + gbuf.shape[1:], gbuf.dtype),
        scratch_shapes=[pltpu.SemaphoreType.DMA, pltpu.SemaphoreType.DMA],
        compiler_params=pltpu.CompilerParams(has_side_effects=True),
    )(gbuf)


def pair_sum(a, b, name):
    n, rows, cols = a.shape
    tr = 256

    def body(a_ref, b_ref, o_ref):
        o_ref[...] = (a_ref[...].astype(F32) + b_ref[...].astype(F32)).astype(o_ref.dtype)

    blk = pl.BlockSpec((None, tr, cols), lambda i, j: (i, j, 0))
    return pl.pallas_call(
        body, name=name, grid=(n, rows // tr), in_specs=[blk, blk], out_specs=blk,
        out_shape=jax.ShapeDtypeStruct(a.shape, a.dtype), compiler_params=_cparams(("parallel", "parallel")),
    )(a, b)


def chip_exchange(p, name):
    half = N_DEV // 2

    def body(p_ref, o_ref, send_sems, recv_sems, local_sem):
        x, y, c = lax.axis_index("x"), lax.axis_index("y"), lax.axis_index("c")
        q = 2 * x + y
        own = pltpu.make_async_copy(p_ref.at[q], o_ref.at[q], local_sem)
        own.start()
        pending = []
        for j in range(1, half):
            px = 1 - x if j & 2 else x
            py = 1 - y if j & 1 else y
            pq = 2 * px + py

            def copy(dst_slot):
                return pltpu.make_async_remote_copy(
                    src_ref=p_ref.at[pq], dst_ref=o_ref.at[dst_slot], send_sem=send_sems.at[j - 1],
                    recv_sem=recv_sems.at[j - 1], device_id=(px, py, c), device_id_type=pl.DeviceIdType.MESH)

            send = copy(q)
            send.start()
            pending.append((send, copy(pq)))
        for send, recv in pending:
            send.wait_send()
            recv.wait_recv()
        own.wait()

    any_spec = pl.BlockSpec(memory_space=pl.ANY)
    return pl.pallas_call(
        body, name=name, in_specs=[any_spec], out_specs=any_spec, out_shape=jax.ShapeDtypeStruct(p.shape, p.dtype),
        scratch_shapes=[pltpu.SemaphoreType.DMA((half - 1,)), pltpu.SemaphoreType.DMA((half - 1,)), pltpu.SemaphoreType.DMA],
        compiler_params=pltpu.CompilerParams(has_side_effects=True),
    )(p)


BIG = ("ffn1_gate", "ffn1_up", "ffn1_down", "w_in", "w_out", "ffn2_gate", "ffn2_up", "ffn2_down")
SMALL_REPLICATED = (("norm_ffn1", (DEPTH, D_MODEL)), ("norm_mix", (DEPTH, D_MODEL)), ("norm_ffn2", (DEPTH, D_MODEL)),
                    ("norm_final", (D_MODEL,)), ("hg_lb", (DEPTH, HG_W)), ("hg_norm_w", (DEPTH, HG_DK)),
                    ("gd_a_log", (DEPTH, GD_HEADS)), ("gd_dt_bias", (DEPTH, GD_HEADS)), ("gd_norm_w", (DEPTH, GD_DK)),
                    ("rg_conv_b", (DEPTH, RG_W)), ("rg_wr", (DEPTH, RG_BLOCKS, RG_BD, RG_BD)), ("rg_br", (DEPTH, RG_W)),
                    ("rg_wi", (DEPTH, RG_BLOCKS, RG_BD, RG_BD)), ("rg_bi", (DEPTH, RG_W)), ("rg_lambda", (DEPTH, RG_W)))
SMALL_EXTRA = (("gd_conv_w", (DEPTH, CONV_W, 3 * GD_W)), ("rg_conv_w", (DEPTH, CONV_W, RG_W)), ("loss", (1,)))
SMALL_SHARDED = (("gd_conv_w", (DEPTH, CONV_W, GDC_SHARD)), ("rg_conv_w", (DEPTH, CONV_W, RGC_SHARD)))


def _pad_rows(a, rows):
    return jnp.pad(a, ((0, rows - a.shape[0]), (0, 0)))


def pack_big(get):
    pieces = []
    for l in range(DEPTH):
        for which in (1, 2):
            pieces.append(_pad_rows(get(f"ffn{which}_gate")[l].T, FF_PAD))
            pieces.append(_pad_rows(get(f"ffn{which}_up")[l].T, FF_PAD))
            pieces.append(_pad_rows(get(f"ffn{which}_down")[l], FF_PAD))
    for l in range(DEPTH):
        pieces.append(_pad_rows(get("w_in")[l].T, NIN_SHARD_PAD))
    for l in range(DEPTH):
        pieces.append(get("w_out")[l])
    return jnp.concatenate(pieces, axis=0)


def unpack_big(p):
    out = {}
    for which in (1, 2):
        for part, nm in (("g", "gate"), ("u", "up"), ("d", "down")):
            per_layer = []
            for l in range(DEPTH):
                r0 = _ffn_block(l, which, part) * FF_PAD
                blk = p[r0:r0 + FF_SHARD, :]
                per_layer.append(blk if part == "d" else blk.T)
            out[f"ffn{which}_{nm}"] = jnp.stack(per_layer)
    out["w_in"] = jnp.stack([p[OFF_WIN + NIN_SHARD_PAD * l:OFF_WIN + NIN_SHARD_PAD * l + NIN_SHARD, :].T for l in range(DEPTH)])
    out["w_out"] = jnp.stack([p[OFF_WOUT + WOUT_SHARD * l:OFF_WOUT + WOUT_SHARD * (l + 1), :] for l in range(DEPTH)])
    return out


def pack_small(get, spec):
    flat = []
    for name, shape in spec:
        a = get(name)
        flat.append(jnp.zeros((math.prod(shape),), F32) if a is None else a.reshape(-1).astype(F32))
    v = jnp.concatenate(flat)
    total = -(-v.shape[0] // 1024) * 1024
    return jnp.pad(v, (0, total - v.shape[0])).reshape(total // 128, 128)


def unpack_small(p, spec):
    v = p.reshape(-1)
    out, off = {}, 0
    for name, shape in spec:
        size = math.prod(shape)
        out[name] = v[off:off + size].reshape(shape)
        off += size
    return out


def _block_diag(w):
    rows = []
    for i in range(RG_BLOCKS):
        rows.append(jnp.concatenate([w[i] if j == i else jnp.zeros((RG_BD, RG_BD), F32) for j in range(RG_BLOCKS)], axis=1))
    return jnp.concatenate(rows, axis=0)


def _diag_blocks(w):
    return jnp.stack([w[RG_BD * i:RG_BD * (i + 1), RG_BD * i:RG_BD * (i + 1)] for i in range(RG_BLOCKS)])


def _lane_vec(v):
    return jnp.pad(v.astype(F32), (GD_HEADS, 128 - 2 * GD_HEADS))[None]


INPUT_NAMES = ("x", "norm_ffn1", "ffn1_gate", "ffn1_up", "ffn1_down", "norm_mix", "w_in", "hg_lb", "hg_norm_w",
               "gd_conv_w", "gd_a_log", "gd_dt_bias", "gd_norm_w", "rg_conv_w", "rg_conv_b", "rg_wr", "rg_br", "rg_wi",
               "rg_bi", "rg_lambda", "w_out", "norm_ffn2", "ffn2_gate", "ffn2_up", "ffn2_down", "norm_final")
WEIGHT_NAMES = INPUT_NAMES[1:]


def _step(a):
    x = a["x"][0]
    target = a["loss_target"][0]
    me = 4 * lax.axis_index("x") + 2 * lax.axis_index("y") + lax.axis_index("c")

    w_pack = pack_big(lambda nm: a[nm])
    conv_local = pack_small(lambda nm: a[nm], SMALL_SHARDED)
    w_bf16 = w_pack.astype(BF16)
    gw_first, gconv = gather_two_level([w_bf16[:FIRST_BLOCKS * FF_PAD], conv_local], "gather_weights")
    rest0 = FIRST_BLOCKS * FF_PAD

    def weights(l, which):
        return (gw_first, 0) if (l, which) == (0, 1) else (gw_rest, FIRST_BLOCKS)

    conv_parts = [unpack_small(gconv[d], SMALL_SHARDED) for d in range(N_DEV)]
    gd_conv_w = jnp.concatenate([p["gd_conv_w"] for p in conv_parts], axis=-1)
    rg_conv_w = jnp.concatenate([p["rg_conv_w"] for p in conv_parts], axis=-1)

    def mixer_params(l):
        return (a["hg_lb"], a["hg_norm_w"][l][None], gd_conv_w[l], _lane_vec(a["gd_a_log"][l]), _lane_vec(a["gd_dt_bias"][l]),
                a["gd_norm_w"][l][None], rg_conv_w[l], a["rg_conv_b"][l][None], _block_diag(a["rg_wr"][l]), a["rg_br"][l][None],
                _block_diag(a["rg_wi"][l]), a["rg_bi"][l][None], a["rg_lambda"][l][None])

    saved = []
    xs = x
    for l in range(DEPTH):
        if l == 0:
            x1, h1, g1, u1, a1, gw_rest = ffn_fwd(xs, a["norm_ffn1"][l][None], gw_first, l, 1, carry=w_bf16[rest0:])
        else:
            x1, h1, g1, u1, a1 = ffn_fwd(xs, a["norm_ffn1"][l][None], gw_rest, l, 1, blk0=FIRST_BLOCKS)
        wt = win_to_padded(gw_rest[:, OFF_WIN - rest0 + NIN_SHARD_PAD * l:OFF_WIN - rest0 + NIN_SHARD_PAD * (l + 1), :], l)
        proj, h2 = in_proj_fwd(x1, a["norm_mix"][l][None], wt, l)
        prm = mixer_params(l)
        y, o_mix, st, s, h = mixer_fwd(l, proj, prm)
        wo = gw_rest[:, OFF_WOUT - rest0 + WOUT_SHARD * l:OFF_WOUT - rest0 + WOUT_SHARD * (l + 1), :].reshape(D_MODEL, D_MODEL)
        x2 = out_proj_fwd(x1, y, wo, l)
        x3, h3, g3, u3, a3 = ffn_fwd(x2, a["norm_ffn2"][l][None], gw_rest, l, 2, blk0=FIRST_BLOCKS)
        saved.append(dict(x0=xs, x1=x1, x2=x2, h1=h1, g1=g1, u1=u1, a1=a1, wt=wt, proj=proj, h2=h2, prm=prm, y=y, o_mix=o_mix,
                          states=(st, s, h), wo=wo, h3=h3, g3=g3, u3=u3, a3=a3))
        xs = x3
    loss_row, dx, d_norm_final = loss_head(xs, a["norm_final"][None], target)

    gbuf = jnp.zeros((N_DEV, PACK_ROWS, D_MODEL), BF16)
    sg = {"norm_final": d_norm_final[0], "loss": loss_row[0, 0:1]}
    per_layer = {nm: [None] * DEPTH for nm in ("norm_ffn1", "norm_mix", "norm_ffn2", "hg_norm_w", "gd_conv_w", "gd_a_log",
                                                "gd_dt_bias", "gd_norm_w", "rg_conv_w", "rg_conv_b", "rg_wr", "rg_br", "rg_wi",
                                                "rg_bi", "rg_lambda")}
    d_hg_lb = jnp.zeros((DEPTH, HG_W), F32)
    for l in reversed(range(DEPTH)):
        sv = saved[l]
        dx2, dg, du, dn2 = ffn_bwd(dx, sv["x2"], a["norm_ffn2"][l][None], sv["g3"], sv["u3"], gw_rest, l, 2, FIRST_BLOCKS)
        gbuf = tn_matmul(dg, sv["h3"], 1.0, FF_PAD, f"dw_gate_l{l}_2", gbuf, _ffn_block(l, 2, "g"))
        gbuf = tn_matmul(du, sv["h3"], 1.0, FF_PAD, f"dw_up_l{l}_2", gbuf, _ffn_block(l, 2, "u"))
        gbuf = tn_matmul(sv["a3"], dx, 0.5, FF_PAD, f"dw_down_l{l}_2", gbuf, _ffn_block(l, 2, "d"))
        dy = out_proj_bwd(dx2, sv["wo"], l)
        gbuf = tn_matmul(sv["y"], dx2, 1.0, WOUT_SHARD, f"dw_out_l{l}", gbuf, (OFF_WOUT + WOUT_SHARD * l) // WOUT_SHARD)
        mb = mixer_bwd(l, sv["proj"], dy, sv["o_mix"], sv["states"], sv["prm"])
        dproj, dprm = mb[0], mb[1:]
        dwt = tn_matmul(dproj, sv["h2"], 1.0, FF_PAD, f"dw_in_l{l}")
        gbuf = win_grad_to_pack(dwt, gbuf, l)
        dx1, dnm = in_proj_bwd(dproj, sv["x1"], a["norm_mix"][l][None], dx2, sv["wt"], l)
        pack1, blk1 = weights(l, 1)
        dx, dg, du, dn1 = ffn_bwd(dx1, sv["x0"], a["norm_ffn1"][l][None], sv["g1"], sv["u1"], pack1, l, 1, blk1)
        gbuf = tn_matmul(dg, sv["h1"], 1.0, FF_PAD, f"dw_gate_l{l}_1", gbuf, _ffn_block(l, 1, "g"))
        gbuf = tn_matmul(du, sv["h1"], 1.0, FF_PAD, f"dw_up_l{l}_1", gbuf, _ffn_block(l, 1, "u"))
        gbuf = tn_matmul(sv["a1"], dx1, 0.5, FF_PAD, f"dw_down_l{l}_1", gbuf, _ffn_block(l, 1, "d"))
        (g_lb, g_hnw, g_gcw, g_alog, g_dt, g_gnw, g_rcw, g_rcb, g_wr, g_br, g_wi, g_bi, g_lam) = dprm
        d_hg_lb = d_hg_lb + g_lb
        for nm, val in (("norm_ffn1", dn1[0]), ("norm_mix", dnm[0]), ("norm_ffn2", dn2[0]), ("hg_norm_w", g_hnw[0]),
                        ("gd_conv_w", g_gcw), ("gd_a_log", g_alog[0, GD_HEADS:2 * GD_HEADS]),
                        ("gd_dt_bias", g_dt[0, GD_HEADS:2 * GD_HEADS]), ("gd_norm_w", g_gnw[0]), ("rg_conv_w", g_rcw),
                        ("rg_conv_b", g_rcb[0]), ("rg_wr", _diag_blocks(g_wr)), ("rg_br", g_br[0]),
                        ("rg_wi", _diag_blocks(g_wi)), ("rg_bi", g_bi[0]), ("rg_lambda", g_lam[0])):
            per_layer[nm][l] = val
    grad_x = dx
    for nm, vals in per_layer.items():
        sg[nm] = jnp.stack(vals)
    sg["hg_lb"] = d_hg_lb

    small_spec = SMALL_REPLICATED + SMALL_EXTRA
    from_sibling = sibling_exchange(gbuf, "grads_to_sibling")
    mine = lax.dynamic_slice_in_dim(gbuf, lax.axis_index("c") * (N_DEV // 2), N_DEV // 2, axis=0)
    parts_big = chip_exchange(pair_sum(mine, from_sibling, "grads_pair_sum"), "grads_to_chips")
    g_big, d_big, m_big, v_big = sum_adamw(parts_big, w_pack, pack_big(lambda nm: a["m_" + nm]),
                                           pack_big(lambda nm: a["v_" + nm]), "adamw_big", 128)
    (parts_small,) = gather_two_level([pack_small(lambda nm: sg[nm], small_spec)], "gather_small_grads")
    repl = dict(SMALL_REPLICATED)
    small_in = lambda pre: pack_small(lambda nm: a[pre + nm] if nm in repl else None, small_spec)
    rows_small = parts_small.shape[1]
    g_sm, d_sm, m_sm, v_sm = sum_adamw(parts_small, small_in(""), small_in("m_"), small_in("v_"), "adamw_small", rows_small)
    g_small = unpack_small(g_sm, small_spec)
    conv_grads = {"gd_conv_w": lax.dynamic_slice_in_dim(g_small["gd_conv_w"], me * GDC_SHARD, GDC_SHARD, axis=2),
                  "rg_conv_w": lax.dynamic_slice_in_dim(g_small["rg_conv_w"], me * RGC_SHARD, RGC_SHARD, axis=2)}
    conv_in = lambda pre: pack_small(lambda nm: a[pre + nm], SMALL_SHARDED)
    g_cv, d_cv, m_cv, v_cv = sum_adamw(pack_small(lambda nm: conv_grads[nm], SMALL_SHARDED)[None], conv_in(""),
                                       conv_in("m_"), conv_in("v_"), "adamw_conv", conv_local.shape[0])

    results = []
    for big, small, conv in ((g_big, g_sm, g_cv), (d_big, d_sm, d_cv), (m_big, m_sm, m_cv), (v_big, v_sm, v_cv)):
        vals = unpack_big(big)
        vals.update({k: v for k, v in unpack_small(small, small_spec).items() if k in repl})
        vals.update(unpack_small(conv, SMALL_SHARDED))
        results.append(vals)
    loss = g_small["loss"][0]
    out = [loss, grad_x[None]]
    for vals in results:
        out.extend(vals[nm] for nm in WEIGHT_NAMES)
    return tuple(out)


def kernel(x, norm_ffn1, ffn1_gate, ffn1_up, ffn1_down, norm_mix, w_in, hg_lb, hg_norm_w, gd_conv_w, gd_a_log, gd_dt_bias, gd_norm_w, rg_conv_w, rg_conv_b, rg_wr, rg_br, rg_wi, rg_bi, rg_lambda, w_out, norm_ffn2, ffn2_gate, ffn2_up, ffn2_down, norm_final, loss_target, m_norm_ffn1, m_ffn1_gate, m_ffn1_up, m_ffn1_down, m_norm_mix, m_w_in, m_hg_lb, m_hg_norm_w, m_gd_conv_w, m_gd_a_log, m_gd_dt_bias, m_gd_norm_w, m_rg_conv_w, m_rg_conv_b, m_rg_wr, m_rg_br, m_rg_wi, m_rg_bi, m_rg_lambda, m_w_out, m_norm_ffn2, m_ffn2_gate, m_ffn2_up, m_ffn2_down, m_norm_final, v_norm_ffn1, v_ffn1_gate, v_ffn1_up, v_ffn1_down, v_norm_mix, v_w_in, v_hg_lb, v_hg_norm_w, v_gd_conv_w, v_gd_a_log, v_gd_dt_bias, v_gd_norm_w, v_rg_conv_w, v_rg_conv_b, v_rg_wr, v_rg_br, v_rg_wi, v_rg_bi, v_rg_lambda, v_w_out, v_norm_ffn2, v_ffn2_gate, v_ffn2_up, v_ffn2_down, v_norm_final):
    args = locals()
    return _step(dict(args))
```

```python
import functools
import math

import jax
import jax.numpy as jnp
from jax import lax
from jax.experimental import pallas as pl
from jax.experimental.pallas import tpu as pltpu

F32 = jnp.float32
BF16 = jnp.bfloat16

D_MODEL = 1024
DEPTH = 2
D_FF = 2816
HG_HEADS, HG_DK = 4, 64
HG_W = 256
GD_HEADS, GD_DK = 4, 128
GD_W = 512
RG_W = 256
RG_BLOCKS, RG_BD = 4, 64
RG_C = 8.0
CONV_W = 4
CHUNK = 64
EPS = 1e-6
N_IN = 3592
ADAM_LR, ADAM_B1, ADAM_B2, ADAM_EPS, ADAM_WD, ADAM_STEP = 0.001, 0.9, 0.999, 1e-08, 0.01, 10

N_DEV = 8
FF_SHARD = D_FF // N_DEV
FF_PAD = 384
NIN_SHARD = N_IN // N_DEV
NIN_SHARD_PAD = 512
WOUT_SHARD = D_MODEL // N_DEV
GDC_SHARD = (2 * 512 + 512) // N_DEV
RGC_SHARD = RG_W // N_DEV

O_HQ, O_HF, O_HI, O_HG = 0, 256, 512, 768
O_GQ, O_GK, O_GV, O_GZ = 1024, 1536, 2048, 2560
O_GBA = 3072
O_RX, O_RGATE = 3200, 3456
NP = 3840
GBA_SPLIT = 3080

N_FFN_BLOCKS = DEPTH * 2 * 3
OFF_WIN = N_FFN_BLOCKS * FF_PAD
OFF_WOUT = OFF_WIN + DEPTH * NIN_SHARD_PAD
PACK_ROWS = OFF_WOUT + DEPTH * WOUT_SHARD
FIRST_BLOCKS = 3

VMEM_LIMIT = 56 * 1024 * 1024
FFN_GROUP = 2
FFN_STEPS = N_DEV // FFN_GROUP
FFN_COLS = FFN_GROUP * FF_PAD
MIXER_CHUNKS_FWD = 4
MIXER_CHUNKS_BWD = 4
MIXER_BWD_VMEM_LIMIT = 63 * 1024 * 1024


def _ffn_block(layer, which, part):
    return layer * 6 + (which - 1) * 3 + {"g": 0, "u": 1, "d": 2}[part]


def _cparams(sem, **kw):
    return pltpu.CompilerParams(dimension_semantics=sem, vmem_limit_bytes=VMEM_LIMIT, **kw)


def _dot_nn(a, b):
    return lax.dot_general(a, b, (((1,), (0,)), ((), ())), preferred_element_type=F32)


def _dot_nt(a, b):
    return lax.dot_general(a, b, (((1,), (1,)), ((), ())), preferred_element_type=F32)


def _dot_tn(a, b):
    return lax.dot_general(a, b, (((0,), (0,)), ((), ())), preferred_element_type=F32)


def _split3(x):
    hi = x.astype(BF16)
    r = x - hi.astype(F32)
    mid = r.astype(BF16)
    lo = (r - mid.astype(F32)).astype(BF16)
    return hi, mid, lo


def _split2(x):
    hi = x.astype(BF16)
    return hi, (x - hi.astype(F32)).astype(BF16)


def _iota(shape, dim):
    return lax.broadcasted_iota(jnp.int32, shape, dim)


def _sigmoid(x):
    return 0.5 * (jnp.tanh(0.5 * x) + 1.0)


def _logistic(x):
    return jax.nn.sigmoid(x)


def _silu(x):
    return x * _sigmoid(x)


def _softplus(x):
    return jnp.maximum(x, 0.0) + jnp.log(1.0 + jnp.exp(-jnp.abs(x)))


def _gelu_tanh(x):
    return 0.5 * x * (1.0 + jnp.tanh(math.sqrt(2.0 / math.pi) * (x + 0.044715 * x * x * x)))


def _rms_fwd(x, w):
    rstd = lax.rsqrt(jnp.mean(x * x, axis=-1, keepdims=True) + EPS)
    return x * rstd * w


def _rms_bwd(dh, x, w):
    rstd = lax.rsqrt(jnp.mean(x * x, axis=-1, keepdims=True) + EPS)
    xhat = x * rstd
    dxhat = dh * w
    dx = rstd * (dxhat - xhat * jnp.mean(dxhat * xhat, axis=-1, keepdims=True))
    return dx, jnp.sum(dh * xhat, axis=0, keepdims=True)


def _rank_dims(kind, ndim):
    if ndim == 2:
        return {"nn": (((1,), (0,)), ((), ())), "nt": (((1,), (1,)), ((), ())), "tn": (((0,), (0,)), ((), ()))}[kind]
    return {"nn": (((2,), (1,)), ((0,), (0,))), "nt": (((2,), (2,)), ((0,), (0,))), "tn": (((1,), (1,)), ((0,), (0,)))}[kind]


def _rdot(kind, a, b):
    return lax.dot_general(a, b, _rank_dims(kind, a.ndim), preferred_element_type=F32)


def _make_mm(kind, kind_da, kind_db, swap_da, swap_db):
    @jax.custom_vjp
    def mm(a, b):
        return _rdot(kind, a.astype(BF16), b.astype(BF16))

    def fwd(a, b):
        a, b = a.astype(BF16), b.astype(BF16)
        return _rdot(kind, a, b), (a, b)

    def bwd(res, g):
        a, b = res
        g = g.astype(BF16)
        da = _rdot(kind_da, b, g) if swap_da else _rdot(kind_da, g, b)
        db = _rdot(kind_db, g, a) if swap_db else _rdot(kind_db, a, g)
        return da, db

    mm.defvjp(fwd, bwd)
    return mm


mm_nn = _make_mm("nn", "nt", "tn", False, False)
mm_nt = _make_mm("nt", "nn", "tn", False, True)
mm_tn = _make_mm("tn", "nt", "nn", True, False)


def _bcast_const(mat, like):
    return mat if like.ndim == 2 else jnp.broadcast_to(mat, (like.shape[0],) + mat.shape)


def _sel_apply(kind, sel, x):
    hi, mid, lo = _split3(x)
    s = _bcast_const(sel, x)
    return _rdot(kind, s, hi) + _rdot(kind, s, mid) + _rdot(kind, s, lo)


@jax.custom_vjp
def sel_mm(sel, x):
    return _sel_apply("nn", sel, x)


def _sel_mm_fwd(sel, x):
    return _sel_apply("nn", sel, x), sel


def _sel_mm_bwd(sel, g):
    return jnp.zeros_like(sel), _sel_apply("tn", sel, g)


sel_mm.defvjp(_sel_mm_fwd, _sel_mm_bwd)


def _transpose_apply(a):
    n = a.shape[-1]
    eye = (_iota((n, n), 0) == _iota((n, n), 1)).astype(BF16)
    hi, mid, lo = _split3(a)
    e = _bcast_const(eye, a)
    return _rdot("nt", e, hi) + _rdot("nt", e, mid) + _rdot("nt", e, lo)


@jax.custom_vjp
def exact_transpose(a):
    return _transpose_apply(a)


exact_transpose.defvjp(lambda a: (_transpose_apply(a), None), lambda _, g: (_transpose_apply(g),))


def _dot3(kind, a, b):
    ah, al = _split2(a)
    bh, bl = _split2(b)
    return _rdot(kind, ah, bh) + _rdot(kind, ah, bl) + _rdot(kind, al, bh)


def _inverse_levels(lower, n):
    nd = lower.ndim
    r, c = _iota(lower.shape, nd - 2), _iota(lower.shape, nd - 1)
    tinv = (r == c).astype(F32)
    b = 1
    while b < n:
        off = (r // (2 * b) == c // (2 * b)) & (r % (2 * b) >= b) & (c % (2 * b) < b)
        tinv = tinv - _dot3("nn", tinv, _dot3("nn", jnp.where(off, lower, 0.0), tinv))
        b *= 2
    return tinv


@functools.partial(jax.custom_vjp, nondiff_argnums=(1,))
def unit_lower_inverse(lower, n):
    return _inverse_levels(lower, n)


def _unit_lower_inverse_fwd(lower, n):
    tinv = _inverse_levels(lower, n)
    return tinv, tinv


def _unit_lower_inverse_bwd(n, tinv, g):
    return (-_dot3("nt", _dot3("tn", tinv, g), tinv),)


unit_lower_inverse.defvjp(_unit_lower_inverse_fwd, _unit_lower_inverse_bwd)


def mm3_nn(a, b):
    ah = a.astype(BF16).astype(F32)
    al = a - ah
    bh = b.astype(BF16).astype(F32)
    bl = b - bh
    return mm_nn(ah, bh) + mm_nn(ah, bl) + mm_nn(al, bh)


def _seg_apply(x, b):
    hi, mid, lo = _split3(x)
    return _rdot("nn", hi, b) + _rdot("nn", mid, b) + _rdot("nn", lo, b)


@jax.custom_vjp
def _seg_sum(x, b):
    return _seg_apply(x, b)


_seg_sum.defvjp(lambda x, b: (_seg_apply(x, b), b), lambda b, g: (_seg_apply(g, b), jnp.zeros_like(b)))


def _shift_rows(x, halo, k):
    n = x.shape[0]
    ext = jnp.concatenate([jnp.zeros((n - halo.shape[0], x.shape[1]), x.dtype), halo], axis=0)
    t = _iota(x.shape, 0)
    return jnp.where(t >= k, jnp.roll(x, k, axis=0), jnp.roll(ext, k, axis=0))


def _causal_conv(x, halo, w):
    y = x * w[3:4, :]
    for k in range(1, CONV_W):
        y = y + _shift_rows(x, halo, k) * w[3 - k:4 - k, :]
    return y


def _lb_of_layer(hg_lb, layer):
    if layer == 0:
        return jnp.zeros((1, HG_W), F32)
    e = jnp.exp(hg_lb - jnp.max(hg_lb, axis=0, keepdims=True))
    sm = e / jnp.sum(e, axis=0, keepdims=True)
    lb = sm[1:2, :]
    for l in range(2, layer + 1):
        lb = lb + sm[l:l + 1, :]
    return lb


def _chunks(x):
    return x.reshape(x.shape[0] // CHUNK, CHUNK, x.shape[1])


def _hgrn2_pre(hq, hf, hi, lb):
    n, w = CHUNK, HG_W
    q = _chunks(_silu(hq) * (HG_DK ** -0.5))
    f = lb + (1.0 - lb) * _logistic(hf)
    k = _chunks(1.0 - f)
    v = _chunks(hi)
    nb = q.shape[0]
    row, col = _iota((n, n), 0), _iota((n, n), 1)
    levels = [n >> j for j in range(1, n.bit_length())]
    picks = [col <= row] + [col <= (row // (2 * b)) * (2 * b) + b - 1 for b in levels]
    cums = sel_mm(jnp.concatenate(picks, axis=0).astype(BF16), _chunks(jnp.log(f)))
    cum = cums[:, 0:n, :]
    lane_head = _iota((nb, n, w), 2) // HG_DK

    def stack(x):
        return jnp.concatenate([jnp.where(lane_head == h, x, 0.0) for h in range(HG_HEADS)], axis=1)

    t_idx = _iota((nb, n, HG_HEADS * n), 1)
    s_idx = _iota((nb, n, HG_HEADS * n), 2) % n
    p = jnp.where(t_idx == s_idx, mm_nt(q, stack(k)), 0.0)
    for j, b in enumerate(levels):
        ref = cums[:, (j + 1) * n:(j + 2) * n, :]
        qe = q * jnp.exp(jnp.minimum(cum - ref, 0.0))
        ke = k * jnp.exp(jnp.minimum(ref - cum, 0.0))
        mask = (t_idx // (2 * b) == s_idx // (2 * b)) & (t_idx % (2 * b) >= b) & (s_idx % (2 * b) < b)
        p = p + jnp.where(mask, mm_nt(qe, stack(ke)), 0.0)
    last = cum[:, n - 1:n, :]
    return mm_nn(p, stack(v)), q * jnp.exp(cum), k * jnp.exp(last - cum), jnp.exp(last), v


def _gdn_pre(gq, gk, gv, gba, halo_q, halo_k, halo_v, conv_w, alog_vec, dt_vec):
    n = CHUNK
    cq = _silu(_causal_conv(gq, halo_q, conv_w[:, 0:GD_W]))
    ck = _silu(_causal_conv(gk, halo_k, conv_w[:, GD_W:2 * GD_W]))
    cv = _silu(_causal_conv(gv, halo_v, conv_w[:, 2 * GD_W:3 * GD_W]))

    def heads(x):
        return jnp.concatenate([_chunks(x[:, GD_DK * h:GD_DK * (h + 1)]) for h in range(GD_HEADS)], axis=0)

    def l2n(x):
        return x * lax.rsqrt(jnp.sum(x * x, axis=-1, keepdims=True) + EPS)

    q = l2n(heads(cq)) * (GD_DK ** -0.5)
    k = l2n(heads(ck))
    v = heads(cv)
    beta_full = _chunks(_sigmoid(gba))
    g_full = _chunks(-jnp.exp(alog_vec) * _softplus(gba + dt_vec))
    tri = (_iota((n, n), 0) >= _iota((n, n), 1)).astype(BF16)
    cum_full = sel_mm(tri, g_full)
    beta = jnp.concatenate([beta_full[:, :, h:h + 1] for h in range(GD_HEADS)], axis=0)
    cum = jnp.concatenate([cum_full[:, :, GD_HEADS + h:GD_HEADS + h + 1] for h in range(GD_HEADS)], axis=0)
    bsz = cum.shape[0]
    ccol = jnp.broadcast_to(cum, (bsz, n, n))
    diff = ccol - exact_transpose(ccol)
    r_i, c_i = _iota((bsz, n, n), 1), _iota((bsz, n, n), 2)
    decay = jnp.exp(jnp.minimum(diff, 0.0))
    kb = k * beta
    lower = jnp.where(r_i > c_i, mm_nt(kb, k) * decay, 0.0)
    tinv = unit_lower_inverse(lower, n)
    ecum = jnp.exp(cum)
    sol = mm3_nn(tinv, jnp.concatenate([v * beta, kb * ecum], axis=2))
    scores = jnp.where(r_i >= c_i, mm_nt(q, k) * decay, 0.0)
    last = cum[:, n - 1:n, :]
    return sol[:, :, 0:GD_DK], sol[:, :, GD_DK:2 * GD_DK], scores, q * ecum, k * jnp.exp(last - cum), jnp.exp(last)


def _rglru_pre(rx, halo_x, conv_w, conv_b, wr, br, wi, bi, lam):
    xc = _causal_conv(rx, halo_x, conv_w) + conv_b
    r = _logistic(mm_nn(xc, wr) + br)
    ig = _sigmoid(mm_nn(xc, wi) + bi)
    two = 2.0 * (-RG_C * r * _softplus(-lam))
    a = jnp.exp(0.5 * two)
    one_minus = -jnp.tanh(0.5 * two) * (jnp.exp(two) + 1.0)
    acc_a, acc_b = a, jnp.sqrt(one_minus) * (ig * xc)
    t = _iota(acc_a.shape, 0) % CHUNK
    k = 1
    while k < CHUNK:
        sa = jnp.where(t >= k, jnp.roll(acc_a, k, axis=0), 1.0)
        sb = jnp.where(t >= k, jnp.roll(acc_b, k, axis=0), 0.0)
        acc_b = acc_a * sb + acc_b
        acc_a = acc_a * sa
        k *= 2
    return acc_a, acc_b


MIXER_PARAMS = ("hg_lb", "hg_norm_w", "gd_conv_w", "gd_alog_vec", "gd_dt_vec", "gd_norm_w",
                "rg_conv_w", "rg_conv_b", "rg_wr_bd", "rg_br", "rg_wi_bd", "rg_bi", "rg_lambda")


def mixer_pre(layer, proj, halo, params):
    (hg_lb, _, gd_conv_w, alog_vec, dt_vec, _, rg_conv_w, rg_conv_b, wr, br, wi, bi, lam) = params
    hg = _hgrn2_pre(proj[:, O_HQ:O_HQ + 256], proj[:, O_HF:O_HF + 256], proj[:, O_HI:O_HI + 256], _lb_of_layer(hg_lb, layer))
    gd = _gdn_pre(proj[:, O_GQ:O_GQ + 512], proj[:, O_GK:O_GK + 512], proj[:, O_GV:O_GV + 512], proj[:, O_GBA:O_GBA + 128],
                  halo[:, O_GQ:O_GQ + 512], halo[:, O_GK:O_GK + 512], halo[:, O_GV:O_GV + 512], gd_conv_w, alog_vec, dt_vec)
    rg = _rglru_pre(proj[:, O_RX:O_RX + 256], halo[:, O_RX:O_RX + 256], rg_conv_w, rg_conv_b, wr, br, wi, bi, lam)
    return hg, gd, rg


def mixer_pre_chunk(pre, c, nb):
    hg, gd, rg = pre
    pick = lambda x: jnp.concatenate([x[h * nb + c:h * nb + c + 1] for h in range(GD_HEADS)], axis=0)
    return (tuple(x[c] for x in hg), tuple(pick(x) for x in gd), tuple(x[c * CHUNK:(c + 1) * CHUNK, :] for x in rg))


def mixer_pre_unchunk(parts, nb):
    hg = tuple(jnp.concatenate([parts[c][0][i][None] for c in range(nb)], axis=0) for i in range(len(parts[0][0])))
    gd = tuple(jnp.concatenate([parts[c][1][i][h:h + 1] for h in range(GD_HEADS) for c in range(nb)], axis=0)
               for i in range(len(parts[0][1])))
    rg = tuple(jnp.concatenate([parts[c][2][i] for c in range(nb)], axis=0) for i in range(len(parts[0][2])))
    return hg, gd, rg


def mixer_rec(pre_c, st_hg, s_gd, h_rg):
    (o_intra, qe, kd, elast, v), (u, w, scores, gqe, gkd, gel), (acc_a, acc_b) = pre_c
    o_hg = o_intra + mm_nt(qe, st_hg)
    blk = _iota((HG_W, HG_W), 0) // HG_DK == _iota((HG_W, HG_W), 1) // HG_DK
    st_new = st_hg * elast + jnp.where(blk, mm_tn(v, kd), 0.0)
    v_new = u - mm_nn(w, s_gd)
    o_gd = mm_nn(gqe, s_gd) + mm_nn(scores, v_new)
    s_new = s_gd * gel + mm_tn(gkd, v_new)
    hs = acc_b + acc_a * h_rg
    return (o_hg, o_gd, hs), (st_new, s_new, hs[CHUNK - 1:CHUNK, :])


def mixer_post(o_hg, o_gd, hs, proj, params):
    (_, hg_norm_w, _, _, _, gd_norm_w, _, _, _, _, _, _, _) = params
    blk = (_iota((HG_W, HG_W), 0) // HG_DK == _iota((HG_W, HG_W), 1) // HG_DK).astype(BF16)
    ms = _seg_sum(o_hg * o_hg, blk) * (1.0 / HG_DK)
    y_hg = o_hg * lax.rsqrt(ms + EPS) * jnp.concatenate([hg_norm_w] * HG_HEADS, axis=1) * _silu(proj[:, O_HG:O_HG + 256])
    parts = []
    for h in range(GD_HEADS):
        o = o_gd[:, GD_DK * h:GD_DK * (h + 1)]
        parts.append(o * lax.rsqrt(jnp.mean(o * o, axis=-1, keepdims=True) + EPS) * gd_norm_w)
    y_gd = jnp.concatenate(parts, axis=1) * _silu(proj[:, O_GZ:O_GZ + 512])
    y_rg = hs * _gelu_tanh(proj[:, O_RGATE:O_RGATE + 256])
    return jnp.concatenate([y_hg, y_gd, y_rg], axis=1)


def mixer_step_forward(layer, proj, halo, states, params):
    nb = proj.shape[0] // CHUNK
    pre = mixer_pre(layer, proj, halo, params)
    outs, entered = [], []
    for c in range(nb):
        entered.append(states)
        o, states = mixer_rec(mixer_pre_chunk(pre, c, nb), *states)
        outs.append(o)
    o_hg = jnp.concatenate([o[0] for o in outs], axis=0)
    o_gd = jnp.concatenate([jnp.concatenate([o[1][h] for h in range(GD_HEADS)], axis=1) for o in outs], axis=0)
    hs = jnp.concatenate([o[2] for o in outs], axis=0)
    return mixer_post(o_hg, o_gd, hs, proj, params), (o_hg, o_gd, hs), entered, states


def _full(shape):
    return pl.BlockSpec(shape, lambda *_: (0,) * len(shape))


def _token_tile(seq, want):
    return min(want, seq)


def _gather_phases(src, out, send_sems, recv_sems, local_sem):
    x, y, c = lax.axis_index("x"), lax.axis_index("y"), lax.axis_index("c")
    me = 4 * x + 2 * y + c
    sibling = (x, y, 1 - c)
    chips = [(1 - x, y), (x, 1 - y), (1 - x, 1 - y)]

    def copy(k, slot, to, from_src=False):
        return pltpu.make_async_remote_copy(
            src_ref=src if from_src else out.at[slot], dst_ref=out.at[slot], send_sem=send_sems.at[k],
            recv_sem=recv_sems.at[k], device_id=to, device_id_type=pl.DeviceIdType.MESH)

    def first_sends():
        return [copy(0, me, sibling, True)] + [copy(1 + j, me, (px, py, c), True) for j, (px, py) in enumerate(chips)]

    def forward(j):
        px, py = chips[j]
        return copy(4 + j, 4 * px + 2 * py + c, sibling)

    def start():
        pltpu.make_async_copy(src, out.at[me], local_sem).start()
        for cp in first_sends():
            cp.start()

    def middle():
        for j, (px, py) in enumerate(chips):
            copy(1 + j, 4 * px + 2 * py + c, (x, y, c)).wait_recv()
            forward(j).start()

    def finish():
        copy(0, 4 * x + 2 * y + (1 - c), (x, y, c)).wait_recv()
        for j, (px, py) in enumerate(chips):
            copy(4 + j, 4 * px + 2 * py + (1 - c), (x, y, c)).wait_recv()
        for cp in first_sends() + [forward(j) for j in range(len(chips))]:
            cp.wait_send()
        pltpu.make_async_copy(src, out.at[me], local_sem).wait()

    return start, middle, finish


def ffn_fwd(x, nw, pack, layer, which, blk0=0, carry=None):
    seq = x.shape[0]
    t = _token_tile(seq, 1024)
    n_i = seq // t
    bg, bu, bd = (_ffn_block(layer, which, p) - blk0 for p in "gud")

    def body(*refs):
        if carry is None:
            x_ref, nw_ref, wg_ref, wu_ref, wd_ref, xo_ref, h_ref, g_ref, u_ref, a_ref, h_scr, acc = refs
        else:
            (x_ref, nw_ref, wg_ref, wu_ref, wd_ref, c_ref, xo_ref, h_ref, g_ref, u_ref, a_ref, co_ref, h_scr, acc,
             send_sems, recv_sems, local_sem) = refs
            start, middle, finish = _gather_phases(c_ref, co_ref, send_sems, recv_sems, local_sem)
            pl.when((pl.program_id(0) == 0) & (pl.program_id(1) == 0))(start)
        k = pl.program_id(1)

        @pl.when(k == 0)
        def _():
            xv = x_ref[...]
            h = _rms_fwd(xv, nw_ref[...]).astype(BF16)
            h_scr[...] = h
            h_ref[...] = h
            acc[...] = xv

        h = h_scr[...]
        g = _dot_nt(h, wg_ref[...].reshape(FFN_COLS, D_MODEL))
        u = _dot_nt(h, wu_ref[...].reshape(FFN_COLS, D_MODEL))
        a = (g * _sigmoid(g) * u).astype(BF16)
        g_ref[...] = g.astype(BF16)
        u_ref[...] = u.astype(BF16)
        a_ref[...] = a
        acc[...] += 0.5 * _dot_nn(a, wd_ref[...].reshape(FFN_COLS, D_MODEL))

        @pl.when(k == FFN_STEPS - 1)
        def _():
            xo_ref[...] = acc[...]

        if carry is not None:
            @pl.when((pl.program_id(0) == n_i - 1) & (k == FFN_STEPS - 1))
            def _():
                middle()
                finish()

    wspec = lambda blk: pl.BlockSpec((FFN_GROUP, FF_PAD, D_MODEL), lambda i, k: (k, blk, 0))
    act = pl.BlockSpec((t, FFN_COLS), lambda i, k: (i, k))
    row = pl.BlockSpec((t, D_MODEL), lambda i, k: (i, 0))
    ff = N_DEV * FF_PAD
    in_specs = [row, _full((1, D_MODEL)), wspec(bg), wspec(bu), wspec(bd)]
    out_specs = [row, row, act, act, act]
    out_shape = [jax.ShapeDtypeStruct((seq, D_MODEL), F32), jax.ShapeDtypeStruct((seq, D_MODEL), BF16)]
    out_shape += [jax.ShapeDtypeStruct((seq, ff), BF16)] * 3
    scratch = [pltpu.VMEM((t, D_MODEL), BF16), pltpu.VMEM((t, D_MODEL), F32)]
    args = [x, nw, pack, pack, pack]
    if carry is not None:
        any_spec = pl.BlockSpec(memory_space=pl.ANY)
        in_specs.append(any_spec)
        out_specs.append(any_spec)
        out_shape.append(jax.ShapeDtypeStruct((N_DEV,) + carry.shape, carry.dtype))
        scratch += [pltpu.SemaphoreType.DMA((7,)), pltpu.SemaphoreType.DMA((7,)), pltpu.SemaphoreType.DMA]
        args.append(carry)
    return pl.pallas_call(
        body, name=f"ffn_fwd_l{layer}_{which}", grid=(n_i, FFN_STEPS), in_specs=in_specs, out_specs=out_specs,
        out_shape=out_shape, scratch_shapes=scratch,
        compiler_params=_cparams(("parallel", "arbitrary") if carry is None else ("arbitrary", "arbitrary")),
    )(*args)


def ffn_bwd(dout, x, nw, g, u, pack, layer, which, blk0=0):
    seq = x.shape[0]
    t = _token_tile(seq, 1024)
    bg, bu, bd = (_ffn_block(layer, which, p) - blk0 for p in "gud")

    def body(do_ref, x_ref, nw_ref, g_ref, u_ref, wg_ref, wu_ref, wd_ref, dx_ref, dg_ref, du_ref, dnw_ref, doh, dh):
        i, k = pl.program_id(0), pl.program_id(1)

        @pl.when(k == 0)
        def _():
            doh[...] = (0.5 * do_ref[...]).astype(BF16)
            dh[...] = jnp.zeros_like(dh)

        @pl.when((k == 0) & (i == 0))
        def _():
            dnw_ref[...] = jnp.zeros_like(dnw_ref)

        da = _dot_nt(doh[...], wd_ref[...].reshape(FFN_COLS, D_MODEL))
        gv = g_ref[...].astype(F32)
        uv = u_ref[...].astype(F32)
        s = _sigmoid(gv)
        dg = (da * uv * (s * (1.0 + gv * (1.0 - s)))).astype(BF16)
        du = (da * gv * s).astype(BF16)
        dg_ref[...] = dg
        du_ref[...] = du
        w_gu = jnp.concatenate([wg_ref[...].reshape(FFN_COLS, D_MODEL), wu_ref[...].reshape(FFN_COLS, D_MODEL)], axis=0)
        dh[...] += _dot_nn(jnp.concatenate([dg, du], axis=1), w_gu)

        @pl.when(k == FFN_STEPS - 1)
        def _():
            dx, dw = _rms_bwd(dh[...], x_ref[...], nw_ref[...])
            dx_ref[...] = do_ref[...] + dx
            dnw_ref[...] += dw

    wspec = lambda blk: pl.BlockSpec((FFN_GROUP, FF_PAD, D_MODEL), lambda i, k: (k, blk, 0))
    act = pl.BlockSpec((t, FFN_COLS), lambda i, k: (i, k))
    row = pl.BlockSpec((t, D_MODEL), lambda i, k: (i, 0))
    row_once = pl.BlockSpec((t, D_MODEL), lambda i, k: (i, 0), pipeline_mode=pl.Buffered(1))
    ff = N_DEV * FF_PAD
    return pl.pallas_call(
        body, name=f"ffn_bwd_l{layer}_{which}", grid=(seq // t, FFN_STEPS),
        in_specs=[row_once, row_once, _full((1, D_MODEL)), act, act, wspec(bg), wspec(bu), wspec(bd)],
        out_specs=[row, act, act, _full((1, D_MODEL))],
        out_shape=[jax.ShapeDtypeStruct((seq, D_MODEL), F32), jax.ShapeDtypeStruct((seq, ff), BF16),
                   jax.ShapeDtypeStruct((seq, ff), BF16), jax.ShapeDtypeStruct((1, D_MODEL), F32)],
        scratch_shapes=[pltpu.VMEM((t, D_MODEL), BF16), pltpu.VMEM((t, D_MODEL), F32)],
        compiler_params=_cparams(("arbitrary", "arbitrary")),
    )(dout, x, nw, g, u, pack, pack, pack)


def tn_matmul(a, b, scale, tm, name, gbuf=None, blk_of_m=None):
    seq, m_total = a.shape
    tk = _token_tile(seq, 1024 if gbuf is not None else 512)
    nk = seq // tk
    n_m = m_total // tm

    def body(*refs):
        if gbuf is None:
            a_ref, b_ref, o_ref = refs
            acc = o_ref
        else:
            a_ref, b_ref, _, o_ref, acc = refs
        kk = pl.program_id(0)

        @pl.when(kk == 0)
        def _():
            acc[...] = jnp.zeros_like(acc)

        bv = b_ref[...].astype(BF16)
        for m in range(n_m):
            acc[pl.ds(m * tm, tm), :] += _dot_tn(a_ref[:, pl.ds(m * tm, tm)].astype(BF16), bv)

        @pl.when(kk == nk - 1)
        def _():
            if gbuf is None:
                if scale != 1.0:
                    o_ref[...] = o_ref[...] * scale
            else:
                for m in range(n_m):
                    o_ref[_grad_slot(m)] = (acc[pl.ds(m * tm, tm), :] * scale).astype(o_ref.dtype)

    in_specs = [pl.BlockSpec((tk, m_total), lambda k: (k, 0)), pl.BlockSpec((tk, D_MODEL), lambda k: (k, 0))]
    args = [a, b]
    if gbuf is None:
        out_spec = pl.BlockSpec((m_total, D_MODEL), lambda k: (0, 0))
        out_shape = jax.ShapeDtypeStruct((m_total, D_MODEL), F32)
        aliases, scratch = {}, []
    else:
        assert n_m == N_DEV
        in_specs.append(pl.BlockSpec(memory_space=pl.ANY))
        args.append(gbuf)
        out_spec = pl.BlockSpec((N_DEV, tm, D_MODEL), lambda k: (0, blk_of_m, 0))
        out_shape = jax.ShapeDtypeStruct(gbuf.shape, gbuf.dtype)
        aliases, scratch = {2: 0}, [pltpu.VMEM((m_total, D_MODEL), F32)]
    return pl.pallas_call(
        body, name=name, grid=(nk,), in_specs=in_specs, out_specs=out_spec, out_shape=out_shape,
        scratch_shapes=scratch, input_output_aliases=aliases, compiler_params=_cparams(("arbitrary",)),
    )(*args)


def in_proj_fwd(x, nw, wt, layer):
    seq = x.shape[0]
    t = _token_tile(seq, 512)

    def body(x_ref, nw_ref, w_ref, p_ref, h_ref):
        h = _rms_fwd(x_ref[...], nw_ref[...]).astype(BF16)
        h_ref[...] = h
        p_ref[...] = _dot_nt(h, w_ref[...])

    row = pl.BlockSpec((t, D_MODEL), lambda i: (i, 0))
    return pl.pallas_call(
        body, name=f"in_proj_fwd_l{layer}", grid=(seq // t,),
        in_specs=[row, _full((1, D_MODEL)), _full((NP, D_MODEL))],
        out_specs=[pl.BlockSpec((t, NP), lambda i: (i, 0)), row],
        out_shape=[jax.ShapeDtypeStruct((seq, NP), F32), jax.ShapeDtypeStruct((seq, D_MODEL), BF16)],
        compiler_params=_cparams(("parallel",)),
    )(x, nw, wt)


def in_proj_bwd(dproj, x, nw, dres, wt, layer):
    seq = x.shape[0]
    t = _token_tile(seq, 512)

    def body(dp_ref, x_ref, nw_ref, dr_ref, w_ref, dx_ref, dnw_ref):
        @pl.when(pl.program_id(0) == 0)
        def _():
            dnw_ref[...] = jnp.zeros_like(dnw_ref)

        dh = _dot_nn(dp_ref[...], w_ref[...])
        dx, dw = _rms_bwd(dh, x_ref[...], nw_ref[...])
        dx_ref[...] = dr_ref[...] + dx
        dnw_ref[...] += dw

    row = pl.BlockSpec((t, D_MODEL), lambda i: (i, 0))
    return pl.pallas_call(
        body, name=f"in_proj_bwd_l{layer}", grid=(seq // t,),
        in_specs=[pl.BlockSpec((t, NP), lambda i: (i, 0)), row, _full((1, D_MODEL)), row, _full((NP, D_MODEL))],
        out_specs=[row, _full((1, D_MODEL))],
        out_shape=[jax.ShapeDtypeStruct((seq, D_MODEL), F32), jax.ShapeDtypeStruct((1, D_MODEL), F32)],
        compiler_params=_cparams(("arbitrary",)),
    )(dproj, x, nw, dres, wt)


def out_proj_fwd(x, y, w, layer):
    seq = x.shape[0]
    t = _token_tile(seq, 512)

    def body(x_ref, y_ref, w_ref, o_ref):
        o_ref[...] = x_ref[...] + _dot_nn(y_ref[...], w_ref[...])

    row = pl.BlockSpec((t, D_MODEL), lambda i: (i, 0))
    return pl.pallas_call(
        body, name=f"out_proj_fwd_l{layer}", grid=(seq // t,),
        in_specs=[row, row, _full((D_MODEL, D_MODEL))], out_specs=row,
        out_shape=jax.ShapeDtypeStruct((seq, D_MODEL), F32), compiler_params=_cparams(("parallel",)),
    )(x, y, w)


def out_proj_bwd(dx, w, layer):
    seq = dx.shape[0]
    t = _token_tile(seq, 512)

    def body(d_ref, w_ref, o_ref):
        o_ref[...] = _dot_nt(d_ref[...].astype(BF16), w_ref[...])

    row = pl.BlockSpec((t, D_MODEL), lambda i: (i, 0))
    return pl.pallas_call(
        body, name=f"out_proj_bwd_l{layer}", grid=(seq // t,),
        in_specs=[row, _full((D_MODEL, D_MODEL))], out_specs=row,
        out_shape=jax.ShapeDtypeStruct((seq, D_MODEL), F32), compiler_params=_cparams(("parallel",)),
    )(dx, w)


def loss_head(x, nw, target):
    seq = x.shape[0]
    t = _token_tile(seq, 512)

    def body(x_ref, nw_ref, t_ref, loss_ref, dx_ref, dnw_ref):
        @pl.when(pl.program_id(0) == 0)
        def _():
            loss_ref[...] = jnp.zeros_like(loss_ref)
            dnw_ref[...] = jnp.zeros_like(dnw_ref)

        xv, w = x_ref[...], nw_ref[...]
        err = _rms_fwd(xv, w) - t_ref[...]
        loss_ref[...] += 0.5 * jnp.sum(jnp.mean(err * err, axis=-1, keepdims=True), axis=0, keepdims=True)
        dx, dw = _rms_bwd(err * (1.0 / D_MODEL), xv, w)
        dx_ref[...] = dx
        dnw_ref[...] += dw

    row = pl.BlockSpec((t, D_MODEL), lambda i: (i, 0))
    return pl.pallas_call(
        body, name="loss_head", grid=(seq // t,),
        in_specs=[row, _full((1, D_MODEL)), row], out_specs=[_full((1, 128)), row, _full((1, D_MODEL))],
        out_shape=[jax.ShapeDtypeStruct((1, 128), F32), jax.ShapeDtypeStruct((seq, D_MODEL), F32),
                   jax.ShapeDtypeStruct((1, D_MODEL), F32)],
        compiler_params=_cparams(("arbitrary",)),
    )(x, nw, target)


_MIXER_PARAM_SHAPES = ((DEPTH, HG_W), (1, HG_DK), (CONV_W, 3 * GD_W), (1, 128), (1, 128), (1, GD_DK),
                       (CONV_W, RG_W), (1, RG_W), (RG_W, RG_W), (1, RG_W), (RG_W, RG_W), (1, RG_W), (1, RG_W))
_STATE_SHAPES = ((HG_W, HG_W), (GD_HEADS, GD_DK, GD_DK), (1, RG_W))


def _state_spec(cps, sh, index):
    return pl.BlockSpec((cps,) + sh, lambda i: (index(i),) + (0,) * len(sh))


def mixer_fwd(layer, proj, params):
    seq = proj.shape[0]
    cps = min(MIXER_CHUNKS_FWD, seq // CHUNK)
    rows = cps * CHUNK
    n = seq // rows

    def body(*refs):
        p_ref, halo_ref = refs[0], refs[1]
        prm = refs[2:15]
        y_ref, o_ref, st_out, s_out, h_out = refs[15:20]
        st, s, h = refs[20:23]
        i = pl.program_id(0)

        @pl.when(i == 0)
        def _():
            st[...] = jnp.zeros_like(st)
            s[...] = jnp.zeros_like(s)
            h[...] = jnp.zeros_like(h)

        halo = jnp.where(i > 0, halo_ref[...], 0.0)
        y, outs, entered, left = mixer_step_forward(layer, p_ref[...], halo, (st[...], s[...], h[...]),
                                                    tuple(r[...] for r in prm))
        y_ref[...] = y.astype(BF16)
        o_ref[...] = jnp.concatenate(outs, axis=1)
        for c in range(cps):
            st_out[c], s_out[c], h_out[c] = entered[c]
        st[...], s[...], h[...] = left

    in_specs = [pl.BlockSpec((rows, NP), lambda i: (i, 0)),
                pl.BlockSpec((8, NP), lambda i: (jnp.maximum(i * (rows // 8) - 1, 0), 0))]
    in_specs += [_full(sh) for sh in _MIXER_PARAM_SHAPES]
    out_specs = [pl.BlockSpec((rows, D_MODEL), lambda i: (i, 0))] * 2
    out_specs += [_state_spec(cps, sh, lambda i: i) for sh in _STATE_SHAPES]
    out_shape = [jax.ShapeDtypeStruct((seq, D_MODEL), BF16), jax.ShapeDtypeStruct((seq, D_MODEL), F32)]
    out_shape += [jax.ShapeDtypeStruct((seq // CHUNK,) + sh, F32) for sh in _STATE_SHAPES]
    return pl.pallas_call(
        body, name=f"mixer_fwd_l{layer}", grid=(n,), in_specs=in_specs, out_specs=out_specs, out_shape=out_shape,
        scratch_shapes=[pltpu.VMEM(sh, F32) for sh in _STATE_SHAPES],
        compiler_params=_cparams(("arbitrary",)),
    )(proj, proj, *params)


def mixer_bwd(layer, proj, dy, outs, states, params):
    seq = proj.shape[0]
    cps = min(MIXER_CHUNKS_BWD, seq // CHUNK)
    rows = cps * CHUNK
    n = seq // rows

    def body(*refs):
        p_ref, halo_ref, dy_ref, o_ref, st_ref, s_ref, h_ref = refs[0:7]
        prm = refs[7:20]
        dp_ref = refs[20]
        dprm = refs[21:34]
        dst, ds, dh, dhalo = refs[34:38]
        i = pl.program_id(0)
        r = n - 1 - i

        @pl.when(i == 0)
        def _():
            for ref in (dst, ds, dh, dhalo) + tuple(dprm):
                ref[...] = jnp.zeros_like(ref)

        params_v = tuple(q[...] for q in prm)
        proj_v = p_ref[...]
        halo = jnp.where(r > 0, halo_ref[...], 0.0)
        o = o_ref[...]
        _, post_vjp = jax.vjp(mixer_post, o[:, 0:HG_W], o[:, HG_W:HG_W + GD_W], o[:, HG_W + GD_W:D_MODEL], proj_v, params_v)
        d_ohg, d_ogd, d_hs, dproj_post, dparams_post = post_vjp(dy_ref[...])
        pre, pre_vjp = jax.vjp(functools.partial(mixer_pre, layer), proj_v, halo, params_v)
        dstates = (dst[...], ds[...], dh[...])
        parts = [None] * cps
        for c in reversed(range(cps)):
            rs = slice(c * CHUNK, (c + 1) * CHUNK)
            _, rec_vjp = jax.vjp(mixer_rec, mixer_pre_chunk(pre, c, cps), st_ref[c], s_ref[c], h_ref[c])
            d_ogd_c = jnp.concatenate([d_ogd[rs, GD_DK * hd:GD_DK * (hd + 1)][None] for hd in range(GD_HEADS)], axis=0)
            parts[c], dst_c, ds_c, dh_c = rec_vjp(((d_ohg[rs, :], d_ogd_c, d_hs[rs, :]), dstates))
            dstates = (dst_c, ds_c, dh_c)
        dproj_pre, dhalo_n, dparams_pre = pre_vjp(mixer_pre_unchunk(parts, cps))
        carry = jnp.concatenate([jnp.zeros((rows - 8, NP), F32), dhalo[...]], axis=0)
        dp_ref[...] = (dproj_post + dproj_pre + carry).astype(BF16)
        dst[...], ds[...], dh[...] = dstates
        dhalo[...] = dhalo_n
        for ref, a, b in zip(dprm, dparams_post, dparams_pre):
            ref[...] += a + b

    rev = lambda i: n - 1 - i
    once = pl.Buffered(1)
    in_specs = [pl.BlockSpec((rows, NP), lambda i: (rev(i), 0), pipeline_mode=once),
                pl.BlockSpec((8, NP), lambda i: (jnp.maximum(rev(i) * (rows // 8) - 1, 0), 0)),
                pl.BlockSpec((rows, D_MODEL), lambda i: (rev(i), 0), pipeline_mode=once),
                pl.BlockSpec((rows, D_MODEL), lambda i: (rev(i), 0), pipeline_mode=once)]
    in_specs += [_state_spec(cps, sh, rev) for sh in _STATE_SHAPES]
    in_specs += [_full(sh) for sh in _MIXER_PARAM_SHAPES]
    out_specs = [pl.BlockSpec((rows, NP), lambda i: (rev(i), 0))] + [_full(sh) for sh in _MIXER_PARAM_SHAPES]
    out_shape = [jax.ShapeDtypeStruct((seq, NP), BF16)] + [jax.ShapeDtypeStruct(sh, F32) for sh in _MIXER_PARAM_SHAPES]
    return pl.pallas_call(
        body, name=f"mixer_bwd_l{layer}", grid=(n,), in_specs=in_specs, out_specs=out_specs, out_shape=out_shape,
        scratch_shapes=[pltpu.VMEM(sh, F32) for sh in _STATE_SHAPES] + [pltpu.VMEM((8, NP), F32)],
        compiler_params=pltpu.CompilerParams(dimension_semantics=("arbitrary",), vmem_limit_bytes=MIXER_BWD_VMEM_LIMIT),
    )(proj, proj, dy, outs, *states, *params)


def _win_segments():
    out = []
    for k in range(N_DEV):
        a, b = NIN_SHARD * k, NIN_SHARD * (k + 1)
        if a < GBA_SPLIT < b:
            out.append((k, 0, GBA_SPLIT - a, a))
            out.append((k, GBA_SPLIT - a, b - GBA_SPLIT, O_RX))
        elif b <= GBA_SPLIT:
            out.append((k, 0, NIN_SHARD, a))
        else:
            out.append((k, 0, NIN_SHARD, a + O_RX - GBA_SPLIT))
    return out


def win_to_padded(wt_shards, layer):
    lanes = 256

    def body(src, dst, scr):
        scr[...] = jnp.zeros_like(scr)
        for (k, s0, rows, d0) in _win_segments():
            scr[pl.ds(d0, rows), :] = src[k, pl.ds(s0, rows), :].astype(F32)
        dst[...] = scr[...].astype(BF16)

    return pl.pallas_call(
        body, name=f"win_to_padded_l{layer}", grid=(D_MODEL // lanes,),
        in_specs=[pl.BlockSpec((N_DEV, NIN_SHARD_PAD, lanes), lambda j: (0, 0, j))],
        out_specs=pl.BlockSpec((NP, lanes), lambda j: (0, j)),
        out_shape=jax.ShapeDtypeStruct((NP, D_MODEL), BF16),
        scratch_shapes=[pltpu.VMEM((NP, lanes), F32)], compiler_params=_cparams(("parallel",)),
    )(wt_shards)


def win_grad_to_pack(dwt, gbuf, layer):
    lanes = 256
    blk = (OFF_WIN + layer * NIN_SHARD_PAD) // NIN_SHARD_PAD

    def body(src, _, dst, scr):
        scr[...] = jnp.zeros_like(scr)
        for (k, s0, rows, d0) in _win_segments():
            scr[_grad_slot(k), pl.ds(s0, rows), :] = src[pl.ds(d0, rows), :]
        dst[...] = scr[...].astype(dst.dtype)

    return pl.pallas_call(
        body, name=f"win_grad_to_pack_l{layer}", grid=(D_MODEL // lanes,),
        in_specs=[pl.BlockSpec((NP, lanes), lambda j: (0, j)), pl.BlockSpec(memory_space=pl.ANY)],
        out_specs=pl.BlockSpec((N_DEV, NIN_SHARD_PAD, lanes), lambda j: (0, blk, j)),
        out_shape=jax.ShapeDtypeStruct(gbuf.shape, gbuf.dtype), input_output_aliases={1: 0},
        scratch_shapes=[pltpu.VMEM((N_DEV, NIN_SHARD_PAD, lanes), F32)],
        compiler_params=_cparams(("parallel",)),
    )(dwt, gbuf)


def sum_adamw(parts, w, m, v, name, rows_per_step):
    n, rows, cols = parts.shape
    tr = min(rows_per_step, rows)
    c1 = 1.0 - ADAM_B1 ** ADAM_STEP
    c2 = 1.0 - ADAM_B2 ** ADAM_STEP

    def body(p_ref, w_ref, m_ref, v_ref, g_ref, d_ref, mo_ref, vo_ref):
        g = p_ref[0].astype(F32)
        for j in range(1, n):
            g = g + p_ref[j].astype(F32)
        mn = ADAM_B1 * m_ref[...] + (1.0 - ADAM_B1) * g
        vn = ADAM_B2 * v_ref[...] + (1.0 - ADAM_B2) * (g * g)
        g_ref[...] = g
        mo_ref[...] = mn
        vo_ref[...] = vn
        d_ref[...] = -ADAM_LR * ((mn / c1) / (jnp.sqrt(vn / c2) + ADAM_EPS) + ADAM_WD * w_ref[...])

    blk = pl.BlockSpec((tr, cols), lambda i: (i, 0))
    return pl.pallas_call(
        body, name=name, grid=(rows // tr,),
        in_specs=[pl.BlockSpec((n, tr, cols), lambda i: (0, i, 0)), blk, blk, blk], out_specs=[blk] * 4,
        out_shape=[jax.ShapeDtypeStruct((rows, cols), F32)] * 4, compiler_params=_cparams(("parallel",)),
    )(parts, w, m, v)


def gather_two_level(arrays, name):
    n = len(arrays)

    def body(*refs):
        ins, outs = refs[:n], refs[n:2 * n]
        send_sems, recv_sems, local_sems = refs[2 * n:]
        x, y, c = lax.axis_index("x"), lax.axis_index("y"), lax.axis_index("c")
        me = 4 * x + 2 * y + c
        sib_slot = 4 * x + 2 * y + (1 - c)
        chips = [(1 - x, y), (x, 1 - y), (1 - x, 1 - y)]

        def copy(a, k, slot, to, src=None):
            return pltpu.make_async_remote_copy(
                src_ref=outs[a].at[slot] if src is None else src, dst_ref=outs[a].at[slot],
                send_sem=send_sems.at[a * 7 + k], recv_sem=recv_sems.at[a * 7 + k],
                device_id=to, device_id_type=pl.DeviceIdType.MESH)

        local, started = [], []
        for a in range(n):
            cp = pltpu.make_async_copy(ins[a], outs[a].at[me], local_sems.at[a])
            cp.start()
            local.append(cp)
            first = [copy(a, 0, me, (x, y, 1 - c), src=ins[a])]
            first += [copy(a, 1 + j, me, (px, py, c), src=ins[a]) for j, (px, py) in enumerate(chips)]
            for cp in first:
                cp.start()
            started += first
        for a in range(n):
            for j, (px, py) in enumerate(chips):
                slot = 4 * px + 2 * py + c
                copy(a, 1 + j, slot, (x, y, c)).wait_recv()
                fwd = copy(a, 4 + j, slot, (x, y, 1 - c))
                fwd.start()
                started.append(fwd)
        for a in range(n):
            copy(a, 0, sib_slot, (x, y, c)).wait_recv()
            for j, (px, py) in enumerate(chips):
                copy(a, 4 + j, 4 * px + 2 * py + (1 - c), (x, y, c)).wait_recv()
        for cp in started:
            cp.wait_send()
        for cp in local:
            cp.wait()

    out_shape = [jax.ShapeDtypeStruct((N_DEV,) + a.shape, a.dtype) for a in arrays]
    any_spec = pl.BlockSpec(memory_space=pl.ANY)
    return pl.pallas_call(
        body, name=name, in_specs=[any_spec] * n, out_specs=[any_spec] * n, out_shape=out_shape,
        scratch_shapes=[pltpu.SemaphoreType.DMA((7 * n,)), pltpu.SemaphoreType.DMA((7 * n,)), pltpu.SemaphoreType.DMA((n,))],
        compiler_params=pltpu.CompilerParams(has_side_effects=True),
    )(*arrays)


def _grad_slot(d):
    return (d % 2) * (N_DEV // 2) + d // 2


def sibling_exchange(gbuf, name):
    half = N_DEV // 2

    def body(g_ref, o_ref, send_sem, recv_sem):
        x, y, c = lax.axis_index("x"), lax.axis_index("y"), lax.axis_index("c")
        cp = pltpu.make_async_remote_copy(
            src_ref=g_ref.at[pl.ds((1 - c) * half, half)], dst_ref=o_ref, send_sem=send_sem, recv_sem=recv_sem,
            device_id=(x, y, 1 - c), device_id_type=pl.DeviceIdType.MESH)
        cp.start()
        cp.wait()

    any_spec = pl.BlockSpec(memory_space=pl.ANY)
    return pl.pallas_call(
        body, name=name, in_specs=[any_spec], out_specs=any_spec,
        out_shape=jax.ShapeDtypeStruct((half,) + gbuf.shape[1:], gbuf.dtype),
        scratch_shapes=[pltpu.SemaphoreType.DMA, pltpu.SemaphoreType.DMA],
        compiler_params=pltpu.CompilerParams(has_side_effects=True),
    )(gbuf)


def pair_sum(a, b, name):
    n, rows, cols = a.shape
    tr = 256

    def body(a_ref, b_ref, o_ref):
        o_ref[...] = (a_ref[...].astype(F32) + b_ref[...].astype(F32)).astype(o_ref.dtype)

    blk = pl.BlockSpec((None, tr, cols), lambda i, j: (i, j, 0))
    return pl.pallas_call(
        body, name=name, grid=(n, rows // tr), in_specs=[blk, blk], out_specs=blk,
        out_shape=jax.ShapeDtypeStruct(a.shape, a.dtype), compiler_params=_cparams(("parallel", "parallel")),
    )(a, b)


def chip_exchange(p, name):
    half = N_DEV // 2

    def body(p_ref, o_ref, send_sems, recv_sems, local_sem):
        x, y, c = lax.axis_index("x"), lax.axis_index("y"), lax.axis_index("c")
        q = 2 * x + y
        own = pltpu.make_async_copy(p_ref.at[q], o_ref.at[q], local_sem)
        own.start()
        pending = []
        for j in range(1, half):
            px = 1 - x if j & 2 else x
            py = 1 - y if j & 1 else y
            pq = 2 * px + py

            def copy(dst_slot):
                return pltpu.make_async_remote_copy(
                    src_ref=p_ref.at[pq], dst_ref=o_ref.at[dst_slot], send_sem=send_sems.at[j - 1],
                    recv_sem=recv_sems.at[j - 1], device_id=(px, py, c), device_id_type=pl.DeviceIdType.MESH)

            send = copy(q)
            send.start()
            pending.append((send, copy(pq)))
        for send, recv in pending:
            send.wait_send()
            recv.wait_recv()
        own.wait()

    any_spec = pl.BlockSpec(memory_space=pl.ANY)
    return pl.pallas_call(
        body, name=name, in_specs=[any_spec], out_specs=any_spec, out_shape=jax.ShapeDtypeStruct(p.shape, p.dtype),
        scratch_shapes=[pltpu.SemaphoreType.DMA((half - 1,)), pltpu.SemaphoreType.DMA((half - 1,)), pltpu.SemaphoreType.DMA],
        compiler_params=pltpu.CompilerParams(has_side_effects=True),
    )(p)


BIG = ("ffn1_gate", "ffn1_up", "ffn1_down", "w_in", "w_out", "ffn2_gate", "ffn2_up", "ffn2_down")
SMALL_REPLICATED = (("norm_ffn1", (DEPTH, D_MODEL)), ("norm_mix", (DEPTH, D_MODEL)), ("norm_ffn2", (DEPTH, D_MODEL)),
                    ("norm_final", (D_MODEL,)), ("hg_lb", (DEPTH, HG_W)), ("hg_norm_w", (DEPTH, HG_DK)),
                    ("gd_a_log", (DEPTH, GD_HEADS)), ("gd_dt_bias", (DEPTH, GD_HEADS)), ("gd_norm_w", (DEPTH, GD_DK)),
                    ("rg_conv_b", (DEPTH, RG_W)), ("rg_wr", (DEPTH, RG_BLOCKS, RG_BD, RG_BD)), ("rg_br", (DEPTH, RG_W)),
                    ("rg_wi", (DEPTH, RG_BLOCKS, RG_BD, RG_BD)), ("rg_bi", (DEPTH, RG_W)), ("rg_lambda", (DEPTH, RG_W)))
SMALL_EXTRA = (("gd_conv_w", (DEPTH, CONV_W, 3 * GD_W)), ("rg_conv_w", (DEPTH, CONV_W, RG_W)), ("loss", (1,)))
SMALL_SHARDED = (("gd_conv_w", (DEPTH, CONV_W, GDC_SHARD)), ("rg_conv_w", (DEPTH, CONV_W, RGC_SHARD)))


def _pad_rows(a, rows):
    return jnp.pad(a, ((0, rows - a.shape[0]), (0, 0)))


def pack_big(get):
    pieces = []
    for l in range(DEPTH):
        for which in (1, 2):
            pieces.append(_pad_rows(get(f"ffn{which}_gate")[l].T, FF_PAD))
            pieces.append(_pad_rows(get(f"ffn{which}_up")[l].T, FF_PAD))
            pieces.append(_pad_rows(get(f"ffn{which}_down")[l], FF_PAD))
    for l in range(DEPTH):
        pieces.append(_pad_rows(get("w_in")[l].T, NIN_SHARD_PAD))
    for l in range(DEPTH):
        pieces.append(get("w_out")[l])
    return jnp.concatenate(pieces, axis=0)


def unpack_big(p):
    out = {}
    for which in (1, 2):
        for part, nm in (("g", "gate"), ("u", "up"), ("d", "down")):
            per_layer = []
            for l in range(DEPTH):
                r0 = _ffn_block(l, which, part) * FF_PAD
                blk = p[r0:r0 + FF_SHARD, :]
                per_layer.append(blk if part == "d" else blk.T)
            out[f"ffn{which}_{nm}"] = jnp.stack(per_layer)
    out["w_in"] = jnp.stack([p[OFF_WIN + NIN_SHARD_PAD * l:OFF_WIN + NIN_SHARD_PAD * l + NIN_SHARD, :].T for l in range(DEPTH)])
    out["w_out"] = jnp.stack([p[OFF_WOUT + WOUT_SHARD * l:OFF_WOUT + WOUT_SHARD * (l + 1), :] for l in range(DEPTH)])
    return out


def pack_small(get, spec):
    flat = []
    for name, shape in spec:
        a = get(name)
        flat.append(jnp.zeros((math.prod(shape),), F32) if a is None else a.reshape(-1).astype(F32))
    v = jnp.concatenate(flat)
    total = -(-v.shape[0] // 1024) * 1024
    return jnp.pad(v, (0, total - v.shape[0])).reshape(total // 128, 128)


def unpack_small(p, spec):
    v = p.reshape(-1)
    out, off = {}, 0
    for name, shape in spec:
        size = math.prod(shape)
        out[name] = v[off:off + size].reshape(shape)
        off += size
    return out


def _block_diag(w):
    rows = []
    for i in range(RG_BLOCKS):
        rows.append(jnp.concatenate([w[i] if j == i else jnp.zeros((RG_BD, RG_BD), F32) for j in range(RG_BLOCKS)], axis=1))
    return jnp.concatenate(rows, axis=0)


def _diag_blocks(w):
    return jnp.stack([w[RG_BD * i:RG_BD * (i + 1), RG_BD * i:RG_BD * (i + 1)] for i in range(RG_BLOCKS)])


def _lane_vec(v):
    return jnp.pad(v.astype(F32), (GD_HEADS, 128 - 2 * GD_HEADS))[None]


INPUT_NAMES = ("x", "norm_ffn1", "ffn1_gate", "ffn1_up", "ffn1_down", "norm_mix", "w_in", "hg_lb", "hg_norm_w",
               "gd_conv_w", "gd_a_log", "gd_dt_bias", "gd_norm_w", "rg_conv_w", "rg_conv_b", "rg_wr", "rg_br", "rg_wi",
               "rg_bi", "rg_lambda", "w_out", "norm_ffn2", "ffn2_gate", "ffn2_up", "ffn2_down", "norm_final")
WEIGHT_NAMES = INPUT_NAMES[1:]


def _step(a):
    x = a["x"][0]
    target = a["loss_target"][0]
    me = 4 * lax.axis_index("x") + 2 * lax.axis_index("y") + lax.axis_index("c")

    w_pack = pack_big(lambda nm: a[nm])
    conv_local = pack_small(lambda nm: a[nm], SMALL_SHARDED)
    assert DEPTH == 2
    w_bf16 = w_pack.astype(BF16)
    gw_first, gconv = gather_two_level([w_bf16[:FIRST_BLOCKS * FF_PAD], conv_local], "gather_weights")

    def carried_rows(b0, b1, l):
        return jnp.concatenate([w_bf16[b0 * FF_PAD:b1 * FF_PAD],
                                w_bf16[OFF_WIN + NIN_SHARD_PAD * l:OFF_WIN + NIN_SHARD_PAD * (l + 1)],
                                w_bf16[OFF_WOUT + WOUT_SHARD * l:OFF_WOUT + WOUT_SHARD * (l + 1)]], axis=0)

    carry_a = carried_rows(FIRST_BLOCKS, 6, 0)
    carry_b = carried_rows(6, N_FFN_BLOCKS, 1)

    conv_parts = [unpack_small(gconv[d], SMALL_SHARDED) for d in range(N_DEV)]
    gd_conv_w = jnp.concatenate([p["gd_conv_w"] for p in conv_parts], axis=-1)
    rg_conv_w = jnp.concatenate([p["rg_conv_w"] for p in conv_parts], axis=-1)

    def mixer_params(l):
        return (a["hg_lb"], a["hg_norm_w"][l][None], gd_conv_w[l], _lane_vec(a["gd_a_log"][l]), _lane_vec(a["gd_dt_bias"][l]),
                a["gd_norm_w"][l][None], rg_conv_w[l], a["rg_conv_b"][l][None], _block_diag(a["rg_wr"][l]), a["rg_br"][l][None],
                _block_diag(a["rg_wi"][l]), a["rg_bi"][l][None], a["rg_lambda"][l][None])

    saved = []
    xs = x
    for l in range(DEPTH):
        if l == 0:
            x1, h1, g1, u1, a1, gw_a = ffn_fwd(xs, a["norm_ffn1"][l][None], gw_first, l, 1, carry=carry_a)
            packs = {1: (gw_first, 0), 2: (gw_a, FIRST_BLOCKS)}
            late, r = gw_a, (6 - FIRST_BLOCKS) * FF_PAD
        else:
            packs = {1: (gw_b, 6), 2: (gw_b, 6)}
            late, r = gw_b, (N_FFN_BLOCKS - 6) * FF_PAD
            x1, h1, g1, u1, a1 = ffn_fwd(xs, a["norm_ffn1"][l][None], gw_b, l, 1, blk0=6)
        wt = win_to_padded(late[:, r:r + NIN_SHARD_PAD, :], l)
        proj, h2 = in_proj_fwd(x1, a["norm_mix"][l][None], wt, l)
        prm = mixer_params(l)
        y, o_mix, st, s, h = mixer_fwd(l, proj, prm)
        wo = late[:, r + NIN_SHARD_PAD:r + NIN_SHARD_PAD + WOUT_SHARD, :].reshape(D_MODEL, D_MODEL)
        x2 = out_proj_fwd(x1, y, wo, l)
        if l == 0:
            x3, h3, g3, u3, a3, gw_b = ffn_fwd(x2, a["norm_ffn2"][l][None], gw_a, l, 2, blk0=FIRST_BLOCKS, carry=carry_b)
        else:
            x3, h3, g3, u3, a3 = ffn_fwd(x2, a["norm_ffn2"][l][None], gw_b, l, 2, blk0=6)
        saved.append(dict(x0=xs, x1=x1, x2=x2, h1=h1, g1=g1, u1=u1, a1=a1, wt=wt, proj=proj, h2=h2, prm=prm, y=y, o_mix=o_mix,
                          states=(st, s, h), wo=wo, h3=h3, g3=g3, u3=u3, a3=a3, packs=packs))
        xs = x3
    loss_row, dx, d_norm_final = loss_head(xs, a["norm_final"][None], target)

    gbuf = jnp.zeros((N_DEV, PACK_ROWS, D_MODEL), BF16)
    sg = {"norm_final": d_norm_final[0], "loss": loss_row[0, 0:1]}
    per_layer = {nm: [None] * DEPTH for nm in ("norm_ffn1", "norm_mix", "norm_ffn2", "hg_norm_w", "gd_conv_w", "gd_a_log",
                                                "gd_dt_bias", "gd_norm_w", "rg_conv_w", "rg_conv_b", "rg_wr", "rg_br", "rg_wi",
                                                "rg_bi", "rg_lambda")}
    d_hg_lb = jnp.zeros((DEPTH, HG_W), F32)
    for l in reversed(range(DEPTH)):
        sv = saved[l]
        pack2, blk2 = sv["packs"][2]
        dx2, dg, du, dn2 = ffn_bwd(dx, sv["x2"], a["norm_ffn2"][l][None], sv["g3"], sv["u3"], pack2, l, 2, blk2)
        gbuf = tn_matmul(dg, sv["h3"], 1.0, FF_PAD, f"dw_gate_l{l}_2", gbuf, _ffn_block(l, 2, "g"))
        gbuf = tn_matmul(du, sv["h3"], 1.0, FF_PAD, f"dw_up_l{l}_2", gbuf, _ffn_block(l, 2, "u"))
        gbuf = tn_matmul(sv["a3"], dx, 0.5, FF_PAD, f"dw_down_l{l}_2", gbuf, _ffn_block(l, 2, "d"))
        dy = out_proj_bwd(dx2, sv["wo"], l)
        gbuf = tn_matmul(sv["y"], dx2, 1.0, WOUT_SHARD, f"dw_out_l{l}", gbuf, (OFF_WOUT + WOUT_SHARD * l) // WOUT_SHARD)
        mb = mixer_bwd(l, sv["proj"], dy, sv["o_mix"], sv["states"], sv["prm"])
        dproj, dprm = mb[0], mb[1:]
        dwt = tn_matmul(dproj, sv["h2"], 1.0, FF_PAD, f"dw_in_l{l}")
        gbuf = win_grad_to_pack(dwt, gbuf, l)
        dx1, dnm = in_proj_bwd(dproj, sv["x1"], a["norm_mix"][l][None], dx2, sv["wt"], l)
        pack1, blk1 = sv["packs"][1]
        dx, dg, du, dn1 = ffn_bwd(dx1, sv["x0"], a["norm_ffn1"][l][None], sv["g1"], sv["u1"], pack1, l, 1, blk1)
        gbuf = tn_matmul(dg, sv["h1"], 1.0, FF_PAD, f"dw_gate_l{l}_1", gbuf, _ffn_block(l, 1, "g"))
        gbuf = tn_matmul(du, sv["h1"], 1.0, FF_PAD, f"dw_up_l{l}_1", gbuf, _ffn_block(l, 1, "u"))
        gbuf = tn_matmul(sv["a1"], dx1, 0.5, FF_PAD, f"dw_down_l{l}_1", gbuf, _ffn_block(l, 1, "d"))
        (g_lb, g_hnw, g_gcw, g_alog, g_dt, g_gnw, g_rcw, g_rcb, g_wr, g_br, g_wi, g_bi, g_lam) = dprm
        d_hg_lb = d_hg_lb + g_lb
        for nm, val in (("norm_ffn1", dn1[0]), ("norm_mix", dnm[0]), ("norm_ffn2", dn2[0]), ("hg_norm_w", g_hnw[0]),
                        ("gd_conv_w", g_gcw), ("gd_a_log", g_alog[0, GD_HEADS:2 * GD_HEADS]),
                        ("gd_dt_bias", g_dt[0, GD_HEADS:2 * GD_HEADS]), ("gd_norm_w", g_gnw[0]), ("rg_conv_w", g_rcw),
                        ("rg_conv_b", g_rcb[0]), ("rg_wr", _diag_blocks(g_wr)), ("rg_br", g_br[0]),
                        ("rg_wi", _diag_blocks(g_wi)), ("rg_bi", g_bi[0]), ("rg_lambda", g_lam[0])):
            per_layer[nm][l] = val
    grad_x = dx
    for nm, vals in per_layer.items():
        sg[nm] = jnp.stack(vals)
    sg["hg_lb"] = d_hg_lb

    small_spec = SMALL_REPLICATED + SMALL_EXTRA
    from_sibling = sibling_exchange(gbuf, "grads_to_sibling")
    mine = lax.dynamic_slice_in_dim(gbuf, lax.axis_index("c") * (N_DEV // 2), N_DEV // 2, axis=0)
    parts_big = chip_exchange(pair_sum(mine, from_sibling, "grads_pair_sum"), "grads_to_chips")
    g_big, d_big, m_big, v_big = sum_adamw(parts_big, w_pack, pack_big(lambda nm: a["m_" + nm]),
                                           pack_big(lambda nm: a["v_" + nm]), "adamw_big", 128)
    (parts_small,) = gather_two_level([pack_small(lambda nm: sg[nm], small_spec)], "gather_small_grads")
    repl = dict(SMALL_REPLICATED)
    small_in = lambda pre: pack_small(lambda nm: a[pre + nm] if nm in repl else None, small_spec)
    rows_small = parts_small.shape[1]
    g_sm, d_sm, m_sm, v_sm = sum_adamw(parts_small, small_in(""), small_in("m_"), small_in("v_"), "adamw_small", rows_small)
    g_small = unpack_small(g_sm, small_spec)
    conv_grads = {"gd_conv_w": lax.dynamic_slice_in_dim(g_small["gd_conv_w"], me * GDC_SHARD, GDC_SHARD, axis=2),
                  "rg_conv_w": lax.dynamic_slice_in_dim(g_small["rg_conv_w"], me * RGC_SHARD, RGC_SHARD, axis=2)}
    conv_in = lambda pre: pack_small(lambda nm: a[pre + nm], SMALL_SHARDED)
    g_cv, d_cv, m_cv, v_cv = sum_adamw(pack_small(lambda nm: conv_grads[nm], SMALL_SHARDED)[None], conv_in(""),
                                       conv_in("m_"), conv_in("v_"), "adamw_conv", conv_local.shape[0])

    results = []
    for big, small, conv in ((g_big, g_sm, g_cv), (d_big, d_sm, d_cv), (m_big, m_sm, m_cv), (v_big, v_sm, v_cv)):
        vals = unpack_big(big)
        vals.update({k: v for k, v in unpack_small(small, small_spec).items() if k in repl})
        vals.update(unpack_small(conv, SMALL_SHARDED))
        results.append(vals)
    loss = g_small["loss"][0]
    out = [loss, grad_x[None]]
    for vals in results:
        out.extend(vals[nm] for nm in WEIGHT_NAMES)
    return tuple(out)


def kernel(x, norm_ffn1, ffn1_gate, ffn1_up, ffn1_down, norm_mix, w_in, hg_lb, hg_norm_w, gd_conv_w, gd_a_log, gd_dt_bias, gd_norm_w, rg_conv_w, rg_conv_b, rg_wr, rg_br, rg_wi, rg_bi, rg_lambda, w_out, norm_ffn2, ffn2_gate, ffn2_up, ffn2_down, norm_final, loss_target, m_norm_ffn1, m_ffn1_gate, m_ffn1_up, m_ffn1_down, m_norm_mix, m_w_in, m_hg_lb, m_hg_norm_w, m_gd_conv_w, m_gd_a_log, m_gd_dt_bias, m_gd_norm_w, m_rg_conv_w, m_rg_conv_b, m_rg_wr, m_rg_br, m_rg_wi, m_rg_bi, m_rg_lambda, m_w_out, m_norm_ffn2, m_ffn2_gate, m_ffn2_up, m_ffn2_down, m_norm_final, v_norm_ffn1, v_ffn1_gate, v_ffn1_up, v_ffn1_down, v_norm_mix, v_w_in, v_hg_lb, v_hg_norm_w, v_gd_conv_w, v_gd_a_log, v_gd_dt_bias, v_gd_norm_w, v_rg_conv_w, v_rg_conv_b, v_rg_wr, v_rg_br, v_rg_wi, v_rg_bi, v_rg_lambda, v_w_out, v_norm_ffn2, v_ffn2_gate, v_ffn2_up, v_ffn2_down, v_norm_final):
    args = locals()
    return _step(dict(args))
```

```python
import functools
import math

import jax
import jax.numpy as jnp
from jax import lax
from jax.experimental import pallas as pl
from jax.experimental.pallas import tpu as pltpu

F32 = jnp.float32
BF16 = jnp.bfloat16

D_MODEL = 1024
DEPTH = 2
D_FF = 2816
HG_HEADS, HG_DK = 4, 64
HG_W = 256
GD_HEADS, GD_DK = 4, 128
GD_W = 512
RG_W = 256
RG_BLOCKS, RG_BD = 4, 64
RG_C = 8.0
CONV_W = 4
CHUNK = 64
EPS = 1e-6
N_IN = 3592
ADAM_LR, ADAM_B1, ADAM_B2, ADAM_EPS, ADAM_WD, ADAM_STEP = 0.001, 0.9, 0.999, 1e-08, 0.01, 10

N_DEV = 8
FF_SHARD = D_FF // N_DEV
FF_PAD = 384
NIN_SHARD = N_IN // N_DEV
NIN_SHARD_PAD = 512
WOUT_SHARD = D_MODEL // N_DEV
GDC_SHARD = (2 * 512 + 512) // N_DEV
RGC_SHARD = RG_W // N_DEV

O_HQ, O_HF, O_HI, O_HG = 0, 256, 512, 768
O_GQ, O_GK, O_GV, O_GZ = 1024, 1536, 2048, 2560
O_GBA = 3072
O_RX, O_RGATE = 3200, 3456
NP = 3840
GBA_SPLIT = 3080

N_FFN_BLOCKS = DEPTH * 2 * 3
OFF_WIN = N_FFN_BLOCKS * FF_PAD
OFF_WOUT = OFF_WIN + DEPTH * NIN_SHARD_PAD
PACK_ROWS = OFF_WOUT + DEPTH * WOUT_SHARD
FIRST_BLOCKS = 3

VMEM_LIMIT = 56 * 1024 * 1024
FFN_GROUP = 2
FFN_STEPS = N_DEV // FFN_GROUP
FFN_COLS = FFN_GROUP * FF_PAD
MIXER_CHUNKS_FWD = 8
MIXER_CHUNKS_BWD = 4
MIXER_BWD_VMEM_LIMIT = 63 * 1024 * 1024


def _ffn_block(layer, which, part):
    return layer * 6 + (which - 1) * 3 + {"g": 0, "u": 1, "d": 2}[part]


def _cparams(sem, **kw):
    return pltpu.CompilerParams(dimension_semantics=sem, vmem_limit_bytes=VMEM_LIMIT, **kw)


def _dot_nn(a, b):
    return lax.dot_general(a, b, (((1,), (0,)), ((), ())), preferred_element_type=F32)


def _dot_nt(a, b):
    return lax.dot_general(a, b, (((1,), (1,)), ((), ())), preferred_element_type=F32)


def _dot_tn(a, b):
    return lax.dot_general(a, b, (((0,), (0,)), ((), ())), preferred_element_type=F32)


def _split3(x):
    hi = x.astype(BF16)
    r = x - hi.astype(F32)
    mid = r.astype(BF16)
    lo = (r - mid.astype(F32)).astype(BF16)
    return hi, mid, lo


def _split2(x):
    hi = x.astype(BF16)
    return hi, (x - hi.astype(F32)).astype(BF16)


def _iota(shape, dim):
    return lax.broadcasted_iota(jnp.int32, shape, dim)


def _sigmoid(x):
    return 0.5 * (jnp.tanh(0.5 * x) + 1.0)


def _logistic(x):
    return jax.nn.sigmoid(x)


def _silu(x):
    return x * _sigmoid(x)


def _softplus(x):
    return jnp.maximum(x, 0.0) + jnp.log(1.0 + jnp.exp(-jnp.abs(x)))


def _gelu_tanh(x):
    return 0.5 * x * (1.0 + jnp.tanh(math.sqrt(2.0 / math.pi) * (x + 0.044715 * x * x * x)))


def _rms_fwd(x, w):
    rstd = lax.rsqrt(jnp.mean(x * x, axis=-1, keepdims=True) + EPS)
    return x * rstd * w


def _rms_bwd(dh, x, w):
    rstd = lax.rsqrt(jnp.mean(x * x, axis=-1, keepdims=True) + EPS)
    xhat = x * rstd
    dxhat = dh * w
    dx = rstd * (dxhat - xhat * jnp.mean(dxhat * xhat, axis=-1, keepdims=True))
    return dx, jnp.sum(dh * xhat, axis=0, keepdims=True)


def _rank_dims(kind, ndim):
    if ndim == 2:
        return {"nn": (((1,), (0,)), ((), ())), "nt": (((1,), (1,)), ((), ())), "tn": (((0,), (0,)), ((), ()))}[kind]
    return {"nn": (((2,), (1,)), ((0,), (0,))), "nt": (((2,), (2,)), ((0,), (0,))), "tn": (((1,), (1,)), ((0,), (0,)))}[kind]


def _rdot(kind, a, b):
    return lax.dot_general(a, b, _rank_dims(kind, a.ndim), preferred_element_type=F32)


def _make_mm(kind, kind_da, kind_db, swap_da, swap_db):
    @jax.custom_vjp
    def mm(a, b):
        return _rdot(kind, a.astype(BF16), b.astype(BF16))

    def fwd(a, b):
        a, b = a.astype(BF16), b.astype(BF16)
        return _rdot(kind, a, b), (a, b)

    def bwd(res, g):
        a, b = res
        g = g.astype(BF16)
        da = _rdot(kind_da, b, g) if swap_da else _rdot(kind_da, g, b)
        db = _rdot(kind_db, g, a) if swap_db else _rdot(kind_db, a, g)
        return da, db

    mm.defvjp(fwd, bwd)
    return mm


mm_nn = _make_mm("nn", "nt", "tn", False, False)
mm_nt = _make_mm("nt", "nn", "tn", False, True)
mm_tn = _make_mm("tn", "nt", "nn", True, False)


def _bcast_const(mat, like):
    return mat if like.ndim == 2 else jnp.broadcast_to(mat, (like.shape[0],) + mat.shape)


def _sel_apply(kind, sel, x):
    hi, mid, lo = _split3(x)
    s = _bcast_const(sel, x)
    return _rdot(kind, s, hi) + _rdot(kind, s, mid) + _rdot(kind, s, lo)


@jax.custom_vjp
def sel_mm(sel, x):
    return _sel_apply("nn", sel, x)


def _sel_mm_fwd(sel, x):
    return _sel_apply("nn", sel, x), sel


def _sel_mm_bwd(sel, g):
    return jnp.zeros_like(sel), _sel_apply("tn", sel, g)


sel_mm.defvjp(_sel_mm_fwd, _sel_mm_bwd)


def _transpose_apply(a):
    n = a.shape[-1]
    eye = (_iota((n, n), 0) == _iota((n, n), 1)).astype(BF16)
    hi, mid, lo = _split3(a)
    e = _bcast_const(eye, a)
    return _rdot("nt", e, hi) + _rdot("nt", e, mid) + _rdot("nt", e, lo)


@jax.custom_vjp
def exact_transpose(a):
    return _transpose_apply(a)


exact_transpose.defvjp(lambda a: (_transpose_apply(a), None), lambda _, g: (_transpose_apply(g),))


def _dot3(kind, a, b):
    ah, al = _split2(a)
    bh, bl = _split2(b)
    return _rdot(kind, ah, bh) + _rdot(kind, ah, bl) + _rdot(kind, al, bh)


def _inverse_levels(lower, n):
    nd = lower.ndim
    r, c = _iota(lower.shape, nd - 2), _iota(lower.shape, nd - 1)
    tinv = (r == c).astype(F32)
    b = 1
    while b < n:
        off = (r // (2 * b) == c // (2 * b)) & (r % (2 * b) >= b) & (c % (2 * b) < b)
        tinv = tinv - _dot3("nn", tinv, _dot3("nn", jnp.where(off, lower, 0.0), tinv))
        b *= 2
    return tinv


@functools.partial(jax.custom_vjp, nondiff_argnums=(1,))
def unit_lower_inverse(lower, n):
    return _inverse_levels(lower, n)


def _unit_lower_inverse_fwd(lower, n):
    tinv = _inverse_levels(lower, n)
    return tinv, tinv


def _unit_lower_inverse_bwd(n, tinv, g):
    return (-_dot3("nt", _dot3("tn", tinv, g), tinv),)


unit_lower_inverse.defvjp(_unit_lower_inverse_fwd, _unit_lower_inverse_bwd)


def mm3_nn(a, b):
    ah = a.astype(BF16).astype(F32)
    al = a - ah
    bh = b.astype(BF16).astype(F32)
    bl = b - bh
    return mm_nn(ah, bh) + mm_nn(ah, bl) + mm_nn(al, bh)


def _seg_apply(x, b):
    hi, mid, lo = _split3(x)
    return _rdot("nn", hi, b) + _rdot("nn", mid, b) + _rdot("nn", lo, b)


@jax.custom_vjp
def _seg_sum(x, b):
    return _seg_apply(x, b)


_seg_sum.defvjp(lambda x, b: (_seg_apply(x, b), b), lambda b, g: (_seg_apply(g, b), jnp.zeros_like(b)))


def _shift_rows(x, halo, k):
    n = x.shape[0]
    ext = jnp.concatenate([jnp.zeros((n - halo.shape[0], x.shape[1]), x.dtype), halo], axis=0)
    t = _iota(x.shape, 0)
    return jnp.where(t >= k, jnp.roll(x, k, axis=0), jnp.roll(ext, k, axis=0))


def _causal_conv(x, halo, w):
    y = x * w[3:4, :]
    for k in range(1, CONV_W):
        y = y + _shift_rows(x, halo, k) * w[3 - k:4 - k, :]
    return y


def _lb_of_layer(hg_lb, layer):
    if layer == 0:
        return jnp.zeros((1, HG_W), F32)
    e = jnp.exp(hg_lb - jnp.max(hg_lb, axis=0, keepdims=True))
    sm = e / jnp.sum(e, axis=0, keepdims=True)
    lb = sm[1:2, :]
    for l in range(2, layer + 1):
        lb = lb + sm[l:l + 1, :]
    return lb


def _chunks(x):
    return x.reshape(x.shape[0] // CHUNK, CHUNK, x.shape[1])


def _hgrn2_pre(hq, hf, hi, lb):
    n, w = CHUNK, HG_W
    q = _chunks(_silu(hq) * (HG_DK ** -0.5))
    f = lb + (1.0 - lb) * _logistic(hf)
    k = _chunks(1.0 - f)
    v = _chunks(hi)
    nb = q.shape[0]
    row, col = _iota((n, n), 0), _iota((n, n), 1)
    levels = [n >> j for j in range(1, n.bit_length())]
    picks = [col <= row] + [col <= (row // (2 * b)) * (2 * b) + b - 1 for b in levels]
    cums = sel_mm(jnp.concatenate(picks, axis=0).astype(BF16), _chunks(jnp.log(f)))
    cum = cums[:, 0:n, :]
    lane_head = _iota((nb, n, w), 2) // HG_DK

    def stack(x):
        return jnp.concatenate([jnp.where(lane_head == h, x, 0.0) for h in range(HG_HEADS)], axis=1)

    t_idx = _iota((nb, n, HG_HEADS * n), 1)
    s_idx = _iota((nb, n, HG_HEADS * n), 2) % n
    p = jnp.where(t_idx == s_idx, mm_nt(q, stack(k)), 0.0)
    for j, b in enumerate(levels):
        ref = cums[:, (j + 1) * n:(j + 2) * n, :]
        qe = q * jnp.exp(jnp.minimum(cum - ref, 0.0))
        ke = k * jnp.exp(jnp.minimum(ref - cum, 0.0))
        mask = (t_idx // (2 * b) == s_idx // (2 * b)) & (t_idx % (2 * b) >= b) & (s_idx % (2 * b) < b)
        p = p + jnp.where(mask, mm_nt(qe, stack(ke)), 0.0)
    last = cum[:, n - 1:n, :]
    return mm_nn(p, stack(v)), q * jnp.exp(cum), k * jnp.exp(last - cum), jnp.exp(last), v


def _gdn_pre(gq, gk, gv, gba, halo_q, halo_k, halo_v, conv_w, alog_vec, dt_vec):
    n = CHUNK
    cq = _silu(_causal_conv(gq, halo_q, conv_w[:, 0:GD_W]))
    ck = _silu(_causal_conv(gk, halo_k, conv_w[:, GD_W:2 * GD_W]))
    cv = _silu(_causal_conv(gv, halo_v, conv_w[:, 2 * GD_W:3 * GD_W]))

    def heads(x):
        return jnp.concatenate([_chunks(x[:, GD_DK * h:GD_DK * (h + 1)]) for h in range(GD_HEADS)], axis=0)

    def l2n(x):
        return x * lax.rsqrt(jnp.sum(x * x, axis=-1, keepdims=True) + EPS)

    q = l2n(heads(cq)) * (GD_DK ** -0.5)
    k = l2n(heads(ck))
    v = heads(cv)
    beta_full = _chunks(_sigmoid(gba))
    g_full = _chunks(-jnp.exp(alog_vec) * _softplus(gba + dt_vec))
    tri = (_iota((n, n), 0) >= _iota((n, n), 1)).astype(BF16)
    cum_full = sel_mm(tri, g_full)
    beta = jnp.concatenate([beta_full[:, :, h:h + 1] for h in range(GD_HEADS)], axis=0)
    cum = jnp.concatenate([cum_full[:, :, GD_HEADS + h:GD_HEADS + h + 1] for h in range(GD_HEADS)], axis=0)
    bsz = cum.shape[0]
    ccol = jnp.broadcast_to(cum, (bsz, n, n))
    diff = ccol - exact_transpose(ccol)
    r_i, c_i = _iota((bsz, n, n), 1), _iota((bsz, n, n), 2)
    decay = jnp.exp(jnp.minimum(diff, 0.0))
    kb = k * beta
    lower = jnp.where(r_i > c_i, mm_nt(kb, k) * decay, 0.0)
    tinv = unit_lower_inverse(lower, n)
    ecum = jnp.exp(cum)
    sol = mm3_nn(tinv, jnp.concatenate([v * beta, kb * ecum], axis=2))
    scores = jnp.where(r_i >= c_i, mm_nt(q, k) * decay, 0.0)
    last = cum[:, n - 1:n, :]
    return sol[:, :, 0:GD_DK], sol[:, :, GD_DK:2 * GD_DK], scores, q * ecum, k * jnp.exp(last - cum), jnp.exp(last)


def _rglru_pre(rx, halo_x, conv_w, conv_b, wr, br, wi, bi, lam):
    xc = _causal_conv(rx, halo_x, conv_w) + conv_b
    r = _logistic(mm_nn(xc, wr) + br)
    ig = _sigmoid(mm_nn(xc, wi) + bi)
    two = 2.0 * (-RG_C * r * _softplus(-lam))
    a = jnp.exp(0.5 * two)
    one_minus = -jnp.tanh(0.5 * two) * (jnp.exp(two) + 1.0)
    acc_a, acc_b = a, jnp.sqrt(one_minus) * (ig * xc)
    t = _iota(acc_a.shape, 0) % CHUNK
    k = 1
    while k < CHUNK:
        sa = jnp.where(t >= k, jnp.roll(acc_a, k, axis=0), 1.0)
        sb = jnp.where(t >= k, jnp.roll(acc_b, k, axis=0), 0.0)
        acc_b = acc_a * sb + acc_b
        acc_a = acc_a * sa
        k *= 2
    return acc_a, acc_b


MIXER_PARAMS = ("hg_lb", "hg_norm_w", "gd_conv_w", "gd_alog_vec", "gd_dt_vec", "gd_norm_w",
                "rg_conv_w", "rg_conv_b", "rg_wr_bd", "rg_br", "rg_wi_bd", "rg_bi", "rg_lambda")


def mixer_pre(layer, proj, halo, params):
    (hg_lb, _, gd_conv_w, alog_vec, dt_vec, _, rg_conv_w, rg_conv_b, wr, br, wi, bi, lam) = params
    hg = _hgrn2_pre(proj[:, O_HQ:O_HQ + 256], proj[:, O_HF:O_HF + 256], proj[:, O_HI:O_HI + 256], _lb_of_layer(hg_lb, layer))
    gd = _gdn_pre(proj[:, O_GQ:O_GQ + 512], proj[:, O_GK:O_GK + 512], proj[:, O_GV:O_GV + 512], proj[:, O_GBA:O_GBA + 128],
                  halo[:, O_GQ:O_GQ + 512], halo[:, O_GK:O_GK + 512], halo[:, O_GV:O_GV + 512], gd_conv_w, alog_vec, dt_vec)
    rg = _rglru_pre(proj[:, O_RX:O_RX + 256], halo[:, O_RX:O_RX + 256], rg_conv_w, rg_conv_b, wr, br, wi, bi, lam)
    return hg, gd, rg


def mixer_pre_chunk(pre, c, nb):
    hg, gd, rg = pre
    pick = lambda x: jnp.concatenate([x[h * nb + c:h * nb + c + 1] for h in range(GD_HEADS)], axis=0)
    return (tuple(x[c] for x in hg), tuple(pick(x) for x in gd), tuple(x[c * CHUNK:(c + 1) * CHUNK, :] for x in rg))


def mixer_pre_unchunk(parts, nb):
    hg = tuple(jnp.concatenate([parts[c][0][i][None] for c in range(nb)], axis=0) for i in range(len(parts[0][0])))
    gd = tuple(jnp.concatenate([parts[c][1][i][h:h + 1] for h in range(GD_HEADS) for c in range(nb)], axis=0)
               for i in range(len(parts[0][1])))
    rg = tuple(jnp.concatenate([parts[c][2][i] for c in range(nb)], axis=0) for i in range(len(parts[0][2])))
    return hg, gd, rg


def mixer_rec(pre_c, st_hg, s_gd, h_rg):
    (o_intra, qe, kd, elast, v), (u, w, scores, gqe, gkd, gel), (acc_a, acc_b) = pre_c
    o_hg = o_intra + mm_nt(qe, st_hg)
    blk = _iota((HG_W, HG_W), 0) // HG_DK == _iota((HG_W, HG_W), 1) // HG_DK
    st_new = st_hg * elast + jnp.where(blk, mm_tn(v, kd), 0.0)
    v_new = u - mm_nn(w, s_gd)
    o_gd = mm_nn(gqe, s_gd) + mm_nn(scores, v_new)
    s_new = s_gd * gel + mm_tn(gkd, v_new)
    hs = acc_b + acc_a * h_rg
    return (o_hg, o_gd, hs), (st_new, s_new, hs[CHUNK - 1:CHUNK, :])


def mixer_post(o_hg, o_gd, hs, proj, params):
    (_, hg_norm_w, _, _, _, gd_norm_w, _, _, _, _, _, _, _) = params
    blk = (_iota((HG_W, HG_W), 0) // HG_DK == _iota((HG_W, HG_W), 1) // HG_DK).astype(BF16)
    ms = _seg_sum(o_hg * o_hg, blk) * (1.0 / HG_DK)
    y_hg = o_hg * lax.rsqrt(ms + EPS) * jnp.concatenate([hg_norm_w] * HG_HEADS, axis=1) * _silu(proj[:, O_HG:O_HG + 256])
    parts = []
    for h in range(GD_HEADS):
        o = o_gd[:, GD_DK * h:GD_DK * (h + 1)]
        parts.append(o * lax.rsqrt(jnp.mean(o * o, axis=-1, keepdims=True) + EPS) * gd_norm_w)
    y_gd = jnp.concatenate(parts, axis=1) * _silu(proj[:, O_GZ:O_GZ + 512])
    y_rg = hs * _gelu_tanh(proj[:, O_RGATE:O_RGATE + 256])
    return jnp.concatenate([y_hg, y_gd, y_rg], axis=1)


def mixer_step_forward(layer, proj, halo, states, params):
    nb = proj.shape[0] // CHUNK
    pre = mixer_pre(layer, proj, halo, params)
    outs, entered = [], []
    for c in range(nb):
        entered.append(states)
        o, states = mixer_rec(mixer_pre_chunk(pre, c, nb), *states)
        outs.append(o)
    o_hg = jnp.concatenate([o[0] for o in outs], axis=0)
    o_gd = jnp.concatenate([jnp.concatenate([o[1][h] for h in range(GD_HEADS)], axis=1) for o in outs], axis=0)
    hs = jnp.concatenate([o[2] for o in outs], axis=0)
    return mixer_post(o_hg, o_gd, hs, proj, params), (o_hg, o_gd, hs), entered, states


def _full(shape):
    return pl.BlockSpec(shape, lambda *_: (0,) * len(shape))


def _token_tile(seq, want):
    return min(want, seq)


def _gather_phases(src, out, send_sems, recv_sems, local_sem):
    x, y, c = lax.axis_index("x"), lax.axis_index("y"), lax.axis_index("c")
    me = 4 * x + 2 * y + c
    sibling = (x, y, 1 - c)
    chips = [(1 - x, y), (x, 1 - y), (1 - x, 1 - y)]

    def copy(k, slot, to, from_src=False):
        return pltpu.make_async_remote_copy(
            src_ref=src if from_src else out.at[slot], dst_ref=out.at[slot], send_sem=send_sems.at[k],
            recv_sem=recv_sems.at[k], device_id=to, device_id_type=pl.DeviceIdType.MESH)

    def first_sends():
        return [copy(0, me, sibling, True)] + [copy(1 + j, me, (px, py, c), True) for j, (px, py) in enumerate(chips)]

    def forward(j):
        px, py = chips[j]
        return copy(4 + j, 4 * px + 2 * py + c, sibling)

    def start():
        pltpu.make_async_copy(src, out.at[me], local_sem).start()
        for cp in first_sends():
            cp.start()

    def middle():
        for j, (px, py) in enumerate(chips):
            copy(1 + j, 4 * px + 2 * py + c, (x, y, c)).wait_recv()
            forward(j).start()

    def finish():
        copy(0, 4 * x + 2 * y + (1 - c), (x, y, c)).wait_recv()
        for j, (px, py) in enumerate(chips):
            copy(4 + j, 4 * px + 2 * py + (1 - c), (x, y, c)).wait_recv()
        for cp in first_sends() + [forward(j) for j in range(len(chips))]:
            cp.wait_send()
        pltpu.make_async_copy(src, out.at[me], local_sem).wait()

    return start, middle, finish


def ffn_fwd(x, nw, pack, layer, which, blk0=0, carry=None):
    seq = x.shape[0]
    t = _token_tile(seq, 1024)
    n_i = seq // t
    bg, bu, bd = (_ffn_block(layer, which, p) - blk0 for p in "gud")

    def body(*refs):
        if carry is None:
            x_ref, nw_ref, wg_ref, wu_ref, wd_ref, xo_ref, h_ref, g_ref, u_ref, a_ref, h_scr, acc = refs
        else:
            (x_ref, nw_ref, wg_ref, wu_ref, wd_ref, c_ref, xo_ref, h_ref, g_ref, u_ref, a_ref, co_ref, h_scr, acc,
             send_sems, recv_sems, local_sem) = refs
            start, middle, finish = _gather_phases(c_ref, co_ref, send_sems, recv_sems, local_sem)
            pl.when((pl.program_id(0) == 0) & (pl.program_id(1) == 0))(start)
        k = pl.program_id(1)

        @pl.when(k == 0)
        def _():
            xv = x_ref[...]
            h = _rms_fwd(xv, nw_ref[...]).astype(BF16)
            h_scr[...] = h
            h_ref[...] = h
            acc[...] = xv

        h = h_scr[...]
        g = _dot_nt(h, wg_ref[...].reshape(FFN_COLS, D_MODEL))
        u = _dot_nt(h, wu_ref[...].reshape(FFN_COLS, D_MODEL))
        a = (g * _sigmoid(g) * u).astype(BF16)
        g_ref[...] = g.astype(BF16)
        u_ref[...] = u.astype(BF16)
        a_ref[...] = a
        acc[...] += 0.5 * _dot_nn(a, wd_ref[...].reshape(FFN_COLS, D_MODEL))

        @pl.when(k == FFN_STEPS - 1)
        def _():
            xo_ref[...] = acc[...]

        if carry is not None:
            @pl.when((pl.program_id(0) == n_i - 1) & (k == FFN_STEPS - 1))
            def _():
                middle()
                finish()

    wspec = lambda blk: pl.BlockSpec((FFN_GROUP, FF_PAD, D_MODEL), lambda i, k: (k, blk, 0))
    act = pl.BlockSpec((t, FFN_COLS), lambda i, k: (i, k))
    row = pl.BlockSpec((t, D_MODEL), lambda i, k: (i, 0))
    ff = N_DEV * FF_PAD
    in_specs = [row, _full((1, D_MODEL)), wspec(bg), wspec(bu), wspec(bd)]
    out_specs = [row, row, act, act, act]
    out_shape = [jax.ShapeDtypeStruct((seq, D_MODEL), F32), jax.ShapeDtypeStruct((seq, D_MODEL), BF16)]
    out_shape += [jax.ShapeDtypeStruct((seq, ff), BF16)] * 3
    scratch = [pltpu.VMEM((t, D_MODEL), BF16), pltpu.VMEM((t, D_MODEL), F32)]
    args = [x, nw, pack, pack, pack]
    if carry is not None:
        any_spec = pl.BlockSpec(memory_space=pl.ANY)
        in_specs.append(any_spec)
        out_specs.append(any_spec)
        out_shape.append(jax.ShapeDtypeStruct((N_DEV,) + carry.shape, carry.dtype))
        scratch += [pltpu.SemaphoreType.DMA((7,)), pltpu.SemaphoreType.DMA((7,)), pltpu.SemaphoreType.DMA]
        args.append(carry)
    return pl.pallas_call(
        body, name=f"ffn_fwd_l{layer}_{which}", grid=(n_i, FFN_STEPS), in_specs=in_specs, out_specs=out_specs,
        out_shape=out_shape, scratch_shapes=scratch,
        compiler_params=_cparams(("parallel", "arbitrary") if carry is None else ("arbitrary", "arbitrary")),
    )(*args)


def ffn_bwd(dout, x, nw, g, u, pack, layer, which, blk0=0):
    seq = x.shape[0]
    t = _token_tile(seq, 1024)
    bg, bu, bd = (_ffn_block(layer, which, p) - blk0 for p in "gud")

    def body(do_ref, x_ref, nw_ref, g_ref, u_ref, wg_ref, wu_ref, wd_ref, dx_ref, dg_ref, du_ref, dnw_ref, doh, dh):
        i, k = pl.program_id(0), pl.program_id(1)

        @pl.when(k == 0)
        def _():
            doh[...] = (0.5 * do_ref[...]).astype(BF16)
            dh[...] = jnp.zeros_like(dh)

        @pl.when((k == 0) & (i == 0))
        def _():
            dnw_ref[...] = jnp.zeros_like(dnw_ref)

        da = _dot_nt(doh[...], wd_ref[...].reshape(FFN_COLS, D_MODEL))
        gv = g_ref[...].astype(F32)
        uv = u_ref[...].astype(F32)
        s = _sigmoid(gv)
        dg = (da * uv * (s * (1.0 + gv * (1.0 - s)))).astype(BF16)
        du = (da * gv * s).astype(BF16)
        dg_ref[...] = dg
        du_ref[...] = du
        w_gu = jnp.concatenate([wg_ref[...].reshape(FFN_COLS, D_MODEL), wu_ref[...].reshape(FFN_COLS, D_MODEL)], axis=0)
        dh[...] += _dot_nn(jnp.concatenate([dg, du], axis=1), w_gu)

        @pl.when(k == FFN_STEPS - 1)
        def _():
            dx, dw = _rms_bwd(dh[...], x_ref[...], nw_ref[...])
            dx_ref[...] = do_ref[...] + dx
            dnw_ref[...] += dw

    wspec = lambda blk: pl.BlockSpec((FFN_GROUP, FF_PAD, D_MODEL), lambda i, k: (k, blk, 0))
    act = pl.BlockSpec((t, FFN_COLS), lambda i, k: (i, k))
    row = pl.BlockSpec((t, D_MODEL), lambda i, k: (i, 0))
    row_once = pl.BlockSpec((t, D_MODEL), lambda i, k: (i, 0), pipeline_mode=pl.Buffered(1))
    ff = N_DEV * FF_PAD
    return pl.pallas_call(
        body, name=f"ffn_bwd_l{layer}_{which}", grid=(seq // t, FFN_STEPS),
        in_specs=[row_once, row_once, _full((1, D_MODEL)), act, act, wspec(bg), wspec(bu), wspec(bd)],
        out_specs=[row, act, act, _full((1, D_MODEL))],
        out_shape=[jax.ShapeDtypeStruct((seq, D_MODEL), F32), jax.ShapeDtypeStruct((seq, ff), BF16),
                   jax.ShapeDtypeStruct((seq, ff), BF16), jax.ShapeDtypeStruct((1, D_MODEL), F32)],
        scratch_shapes=[pltpu.VMEM((t, D_MODEL), BF16), pltpu.VMEM((t, D_MODEL), F32)],
        compiler_params=_cparams(("arbitrary", "arbitrary")),
    )(dout, x, nw, g, u, pack, pack, pack)


def tn_matmul(a, b, scale, tm, name, gbuf=None, blk_of_m=None):
    seq, m_total = a.shape
    tk = _token_tile(seq, 1024 if gbuf is not None else 512)
    nk = seq // tk
    n_m = m_total // tm

    def body(*refs):
        if gbuf is None:
            a_ref, b_ref, o_ref = refs
            acc = o_ref
        else:
            a_ref, b_ref, _, o_ref, acc = refs
        kk = pl.program_id(0)

        @pl.when(kk == 0)
        def _():
            acc[...] = jnp.zeros_like(acc)

        bv = b_ref[...].astype(BF16)
        for m in range(n_m):
            acc[pl.ds(m * tm, tm), :] += _dot_tn(a_ref[:, pl.ds(m * tm, tm)].astype(BF16), bv)

        @pl.when(kk == nk - 1)
        def _():
            if gbuf is None:
                if scale != 1.0:
                    o_ref[...] = o_ref[...] * scale
            else:
                for m in range(n_m):
                    o_ref[_grad_slot(m)] = (acc[pl.ds(m * tm, tm), :] * scale).astype(o_ref.dtype)

    in_specs = [pl.BlockSpec((tk, m_total), lambda k: (k, 0)), pl.BlockSpec((tk, D_MODEL), lambda k: (k, 0))]
    args = [a, b]
    if gbuf is None:
        out_spec = pl.BlockSpec((m_total, D_MODEL), lambda k: (0, 0))
        out_shape = jax.ShapeDtypeStruct((m_total, D_MODEL), F32)
        aliases, scratch = {}, []
    else:
        assert n_m == N_DEV
        in_specs.append(pl.BlockSpec(memory_space=pl.ANY))
        args.append(gbuf)
        out_spec = pl.BlockSpec((N_DEV, tm, D_MODEL), lambda k: (0, blk_of_m, 0))
        out_shape = jax.ShapeDtypeStruct(gbuf.shape, gbuf.dtype)
        aliases, scratch = {2: 0}, [pltpu.VMEM((m_total, D_MODEL), F32)]
    return pl.pallas_call(
        body, name=name, grid=(nk,), in_specs=in_specs, out_specs=out_spec, out_shape=out_shape,
        scratch_shapes=scratch, input_output_aliases=aliases, compiler_params=_cparams(("arbitrary",)),
    )(*args)


def in_proj_fwd(x, nw, wt, layer):
    seq = x.shape[0]
    t = _token_tile(seq, 512)

    def body(x_ref, nw_ref, w_ref, p_ref, h_ref):
        h = _rms_fwd(x_ref[...], nw_ref[...]).astype(BF16)
        h_ref[...] = h
        p_ref[...] = _dot_nt(h, w_ref[...])

    row = pl.BlockSpec((t, D_MODEL), lambda i: (i, 0))
    return pl.pallas_call(
        body, name=f"in_proj_fwd_l{layer}", grid=(seq // t,),
        in_specs=[row, _full((1, D_MODEL)), _full((NP, D_MODEL))],
        out_specs=[pl.BlockSpec((t, NP), lambda i: (i, 0)), row],
        out_shape=[jax.ShapeDtypeStruct((seq, NP), F32), jax.ShapeDtypeStruct((seq, D_MODEL), BF16)],
        compiler_params=_cparams(("parallel",)),
    )(x, nw, wt)


def in_proj_bwd(dproj, x, nw, dres, wt, layer):
    seq = x.shape[0]
    t = _token_tile(seq, 512)

    def body(dp_ref, x_ref, nw_ref, dr_ref, w_ref, dx_ref, dnw_ref):
        @pl.when(pl.program_id(0) == 0)
        def _():
            dnw_ref[...] = jnp.zeros_like(dnw_ref)

        dh = _dot_nn(dp_ref[...], w_ref[...])
        dx, dw = _rms_bwd(dh, x_ref[...], nw_ref[...])
        dx_ref[...] = dr_ref[...] + dx
        dnw_ref[...] += dw

    row = pl.BlockSpec((t, D_MODEL), lambda i: (i, 0))
    return pl.pallas_call(
        body, name=f"in_proj_bwd_l{layer}", grid=(seq // t,),
        in_specs=[pl.BlockSpec((t, NP), lambda i: (i, 0)), row, _full((1, D_MODEL)), row, _full((NP, D_MODEL))],
        out_specs=[row, _full((1, D_MODEL))],
        out_shape=[jax.ShapeDtypeStruct((seq, D_MODEL), F32), jax.ShapeDtypeStruct((1, D_MODEL), F32)],
        compiler_params=_cparams(("arbitrary",)),
    )(dproj, x, nw, dres, wt)


def out_proj_fwd(x, y, w, layer):
    seq = x.shape[0]
    t = _token_tile(seq, 512)

    def body(x_ref, y_ref, w_ref, o_ref):
        o_ref[...] = x_ref[...] + _dot_nn(y_ref[...], w_ref[...])

    row = pl.BlockSpec((t, D_MODEL), lambda i: (i, 0))
    return pl.pallas_call(
        body, name=f"out_proj_fwd_l{layer}", grid=(seq // t,),
        in_specs=[row, row, _full((D_MODEL, D_MODEL))], out_specs=row,
        out_shape=jax.ShapeDtypeStruct((seq, D_MODEL), F32), compiler_params=_cparams(("parallel",)),
    )(x, y, w)


def out_proj_bwd(dx, w, layer):
    seq = dx.shape[0]
    t = _token_tile(seq, 512)

    def body(d_ref, w_ref, o_ref):
        o_ref[...] = _dot_nt(d_ref[...].astype(BF16), w_ref[...])

    row = pl.BlockSpec((t, D_MODEL), lambda i: (i, 0))
    return pl.pallas_call(
        body, name=f"out_proj_bwd_l{layer}", grid=(seq // t,),
        in_specs=[row, _full((D_MODEL, D_MODEL))], out_specs=row,
        out_shape=jax.ShapeDtypeStruct((seq, D_MODEL), F32), compiler_params=_cparams(("parallel",)),
    )(dx, w)


def loss_head(x, nw, target):
    seq = x.shape[0]
    t = _token_tile(seq, 512)

    def body(x_ref, nw_ref, t_ref, loss_ref, dx_ref, dnw_ref):
        @pl.when(pl.program_id(0) == 0)
        def _():
            loss_ref[...] = jnp.zeros_like(loss_ref)
            dnw_ref[...] = jnp.zeros_like(dnw_ref)

        xv, w = x_ref[...], nw_ref[...]
        err = _rms_fwd(xv, w) - t_ref[...]
        loss_ref[...] += 0.5 * jnp.sum(jnp.mean(err * err, axis=-1, keepdims=True), axis=0, keepdims=True)
        dx, dw = _rms_bwd(err * (1.0 / D_MODEL), xv, w)
        dx_ref[...] = dx
        dnw_ref[...] += dw

    row = pl.BlockSpec((t, D_MODEL), lambda i: (i, 0))
    return pl.pallas_call(
        body, name="loss_head", grid=(seq // t,),
        in_specs=[row, _full((1, D_MODEL)), row], out_specs=[_full((1, 128)), row, _full((1, D_MODEL))],
        out_shape=[jax.ShapeDtypeStruct((1, 128), F32), jax.ShapeDtypeStruct((seq, D_MODEL), F32),
                   jax.ShapeDtypeStruct((1, D_MODEL), F32)],
        compiler_params=_cparams(("arbitrary",)),
    )(x, nw, target)


_MIXER_PARAM_SHAPES = ((DEPTH, HG_W), (1, HG_DK), (CONV_W, 3 * GD_W), (1, 128), (1, 128), (1, GD_DK),
                       (CONV_W, RG_W), (1, RG_W), (RG_W, RG_W), (1, RG_W), (RG_W, RG_W), (1, RG_W), (1, RG_W))
_STATE_SHAPES = ((HG_W, HG_W), (GD_HEADS, GD_DK, GD_DK), (1, RG_W))


def _state_spec(cps, sh, index):
    return pl.BlockSpec((cps,) + sh, lambda i: (index(i),) + (0,) * len(sh))


def mixer_fwd(layer, proj, params):
    seq = proj.shape[0]
    cps = min(MIXER_CHUNKS_FWD, seq // CHUNK)
    rows = cps * CHUNK
    n = seq // rows

    def body(*refs):
        p_ref, halo_ref = refs[0], refs[1]
        prm = refs[2:15]
        y_ref, o_ref, st_out, s_out, h_out = refs[15:20]
        st, s, h = refs[20:23]
        i = pl.program_id(0)

        @pl.when(i == 0)
        def _():
            st[...] = jnp.zeros_like(st)
            s[...] = jnp.zeros_like(s)
            h[...] = jnp.zeros_like(h)

        halo = jnp.where(i > 0, halo_ref[...], 0.0)
        y, outs, entered, left = mixer_step_forward(layer, p_ref[...], halo, (st[...], s[...], h[...]),
                                                    tuple(r[...] for r in prm))
        y_ref[...] = y.astype(BF16)
        o_ref[...] = jnp.concatenate(outs, axis=1)
        for c in range(cps):
            st_out[c], s_out[c], h_out[c] = entered[c]
        st[...], s[...], h[...] = left

    in_specs = [pl.BlockSpec((rows, NP), lambda i: (i, 0)),
                pl.BlockSpec((8, NP), lambda i: (jnp.maximum(i * (rows // 8) - 1, 0), 0))]
    in_specs += [_full(sh) for sh in _MIXER_PARAM_SHAPES]
    out_specs = [pl.BlockSpec((rows, D_MODEL), lambda i: (i, 0))] * 2
    out_specs += [_state_spec(cps, sh, lambda i: i) for sh in _STATE_SHAPES]
    out_shape = [jax.ShapeDtypeStruct((seq, D_MODEL), BF16), jax.ShapeDtypeStruct((seq, D_MODEL), F32)]
    out_shape += [jax.ShapeDtypeStruct((seq // CHUNK,) + sh, F32) for sh in _STATE_SHAPES]
    return pl.pallas_call(
        body, name=f"mixer_fwd_l{layer}", grid=(n,), in_specs=in_specs, out_specs=out_specs, out_shape=out_shape,
        scratch_shapes=[pltpu.VMEM(sh, F32) for sh in _STATE_SHAPES],
        compiler_params=_cparams(("arbitrary",)),
    )(proj, proj, *params)


def mixer_bwd(layer, proj, dy, outs, states, params):
    seq = proj.shape[0]
    cps = min(MIXER_CHUNKS_BWD, seq // CHUNK)
    rows = cps * CHUNK
    n = seq // rows

    def body(*refs):
        p_ref, halo_ref, dy_ref, o_ref, st_ref, s_ref, h_ref = refs[0:7]
        prm = refs[7:20]
        dp_ref = refs[20]
        dprm = refs[21:34]
        dst, ds, dh, dhalo = refs[34:38]
        i = pl.program_id(0)
        r = n - 1 - i

        @pl.when(i == 0)
        def _():
            for ref in (dst, ds, dh, dhalo) + tuple(dprm):
                ref[...] = jnp.zeros_like(ref)

        params_v = tuple(q[...] for q in prm)
        proj_v = p_ref[...]
        halo = jnp.where(r > 0, halo_ref[...], 0.0)
        o = o_ref[...]
        _, post_vjp = jax.vjp(mixer_post, o[:, 0:HG_W], o[:, HG_W:HG_W + GD_W], o[:, HG_W + GD_W:D_MODEL], proj_v, params_v)
        d_ohg, d_ogd, d_hs, dproj_post, dparams_post = post_vjp(dy_ref[...])
        pre, pre_vjp = jax.vjp(functools.partial(mixer_pre, layer), proj_v, halo, params_v)
        dstates = (dst[...], ds[...], dh[...])
        parts = [None] * cps
        for c in reversed(range(cps)):
            rs = slice(c * CHUNK, (c + 1) * CHUNK)
            _, rec_vjp = jax.vjp(mixer_rec, mixer_pre_chunk(pre, c, cps), st_ref[c], s_ref[c], h_ref[c])
            d_ogd_c = jnp.concatenate([d_ogd[rs, GD_DK * hd:GD_DK * (hd + 1)][None] for hd in range(GD_HEADS)], axis=0)
            parts[c], dst_c, ds_c, dh_c = rec_vjp(((d_ohg[rs, :], d_ogd_c, d_hs[rs, :]), dstates))
            dstates = (dst_c, ds_c, dh_c)
        dproj_pre, dhalo_n, dparams_pre = pre_vjp(mixer_pre_unchunk(parts, cps))
        carry = jnp.concatenate([jnp.zeros((rows - 8, NP), F32), dhalo[...]], axis=0)
        dp_ref[...] = (dproj_post + dproj_pre + carry).astype(BF16)
        dst[...], ds[...], dh[...] = dstates
        dhalo[...] = dhalo_n
        for ref, a, b in zip(dprm, dparams_post, dparams_pre):
            ref[...] += a + b

    rev = lambda i: n - 1 - i
    once = pl.Buffered(1)
    in_specs = [pl.BlockSpec((rows, NP), lambda i: (rev(i), 0), pipeline_mode=once),
                pl.BlockSpec((8, NP), lambda i: (jnp.maximum(rev(i) * (rows // 8) - 1, 0), 0)),
                pl.BlockSpec((rows, D_MODEL), lambda i: (rev(i), 0), pipeline_mode=once),
                pl.BlockSpec((rows, D_MODEL), lambda i: (rev(i), 0), pipeline_mode=once)]
    in_specs += [_state_spec(cps, sh, rev) for sh in _STATE_SHAPES]
    in_specs += [_full(sh) for sh in _MIXER_PARAM_SHAPES]
    out_specs = [pl.BlockSpec((rows, NP), lambda i: (rev(i), 0))] + [_full(sh) for sh in _MIXER_PARAM_SHAPES]
    out_shape = [jax.ShapeDtypeStruct((seq, NP), BF16)] + [jax.ShapeDtypeStruct(sh, F32) for sh in _MIXER_PARAM_SHAPES]
    return pl.pallas_call(
        body, name=f"mixer_bwd_l{layer}", grid=(n,), in_specs=in_specs, out_specs=out_specs, out_shape=out_shape,
        scratch_shapes=[pltpu.VMEM(sh, F32) for sh in _STATE_SHAPES] + [pltpu.VMEM((8, NP), F32)],
        compiler_params=pltpu.CompilerParams(dimension_semantics=("arbitrary",), vmem_limit_bytes=MIXER_BWD_VMEM_LIMIT),
    )(proj, proj, dy, outs, *states, *params)


def _win_segments():
    out = []
    for k in range(N_DEV):
        a, b = NIN_SHARD * k, NIN_SHARD * (k + 1)
        if a < GBA_SPLIT < b:
            out.append((k, 0, GBA_SPLIT - a, a))
            out.append((k, GBA_SPLIT - a, b - GBA_SPLIT, O_RX))
        elif b <= GBA_SPLIT:
            out.append((k, 0, NIN_SHARD, a))
        else:
            out.append((k, 0, NIN_SHARD, a + O_RX - GBA_SPLIT))
    return out


def win_to_padded(wt_shards, layer):
    lanes = 256

    def body(src, dst, scr):
        scr[...] = jnp.zeros_like(scr)
        for (k, s0, rows, d0) in _win_segments():
            scr[pl.ds(d0, rows), :] = src[k, pl.ds(s0, rows), :].astype(F32)
        dst[...] = scr[...].astype(BF16)

    return pl.pallas_call(
        body, name=f"win_to_padded_l{layer}", grid=(D_MODEL // lanes,),
        in_specs=[pl.BlockSpec((N_DEV, NIN_SHARD_PAD, lanes), lambda j: (0, 0, j))],
        out_specs=pl.BlockSpec((NP, lanes), lambda j: (0, j)),
        out_shape=jax.ShapeDtypeStruct((NP, D_MODEL), BF16),
        scratch_shapes=[pltpu.VMEM((NP, lanes), F32)], compiler_params=_cparams(("parallel",)),
    )(wt_shards)


def win_grad_to_pack(dwt, gbuf, layer):
    lanes = 256
    blk = (OFF_WIN + layer * NIN_SHARD_PAD) // NIN_SHARD_PAD

    def body(src, _, dst, scr):
        scr[...] = jnp.zeros_like(scr)
        for (k, s0, rows, d0) in _win_segments():
            scr[_grad_slot(k), pl.ds(s0, rows), :] = src[pl.ds(d0, rows), :]
        dst[...] = scr[...].astype(dst.dtype)

    return pl.pallas_call(
        body, name=f"win_grad_to_pack_l{layer}", grid=(D_MODEL // lanes,),
        in_specs=[pl.BlockSpec((NP, lanes), lambda j: (0, j)), pl.BlockSpec(memory_space=pl.ANY)],
        out_specs=pl.BlockSpec((N_DEV, NIN_SHARD_PAD, lanes), lambda j: (0, blk, j)),
        out_shape=jax.ShapeDtypeStruct(gbuf.shape, gbuf.dtype), input_output_aliases={1: 0},
        scratch_shapes=[pltpu.VMEM((N_DEV, NIN_SHARD_PAD, lanes), F32)],
        compiler_params=_cparams(("parallel",)),
    )(dwt, gbuf)


def sum_adamw(parts, w, m, v, name, rows_per_step):
    n, rows, cols = parts.shape
    tr = min(rows_per_step, rows)
    c1 = 1.0 - ADAM_B1 ** ADAM_STEP
    c2 = 1.0 - ADAM_B2 ** ADAM_STEP

    def body(p_ref, w_ref, m_ref, v_ref, g_ref, d_ref, mo_ref, vo_ref):
        g = p_ref[0].astype(F32)
        for j in range(1, n):
            g = g + p_ref[j].astype(F32)
        mn = ADAM_B1 * m_ref[...] + (1.0 - ADAM_B1) * g
        vn = ADAM_B2 * v_ref[...] + (1.0 - ADAM_B2) * (g * g)
        g_ref[...] = g
        mo_ref[...] = mn
        vo_ref[...] = vn
        d_ref[...] = -ADAM_LR * ((mn / c1) / (jnp.sqrt(vn / c2) + ADAM_EPS) + ADAM_WD * w_ref[...])

    blk = pl.BlockSpec((tr, cols), lambda i: (i, 0))
    return pl.pallas_call(
        body, name=name, grid=(rows // tr,),
        in_specs=[pl.BlockSpec((n, tr, cols), lambda i: (0, i, 0)), blk, blk, blk], out_specs=[blk] * 4,
        out_shape=[jax.ShapeDtypeStruct((rows, cols), F32)] * 4, compiler_params=_cparams(("parallel",)),
    )(parts, w, m, v)


def gather_two_level(arrays, name):
    n = len(arrays)

    def body(*refs):
        ins, outs = refs[:n], refs[n:2 * n]
        send_sems, recv_sems, local_sems = refs[2 * n:]
        x, y, c = lax.axis_index("x"), lax.axis_index("y"), lax.axis_index("c")
        me = 4 * x + 2 * y + c
        sib_slot = 4 * x + 2 * y + (1 - c)
        chips = [(1 - x, y), (x, 1 - y), (1 - x, 1 - y)]

        def copy(a, k, slot, to, src=None):
            return pltpu.make_async_remote_copy(
                src_ref=outs[a].at[slot] if src is None else src, dst_ref=outs[a].at[slot],
                send_sem=send_sems.at[a * 7 + k], recv_sem=recv_sems.at[a * 7 + k],
                device_id=to, device_id_type=pl.DeviceIdType.MESH)

        local, started = [], []
        for a in range(n):
            cp = pltpu.make_async_copy(ins[a], outs[a].at[me], local_sems.at[a])
            cp.start()
            local.append(cp)
            first = [copy(a, 0, me, (x, y, 1 - c), src=ins[a])]
            first += [copy(a, 1 + j, me, (px, py, c), src=ins[a]) for j, (px, py) in enumerate(chips)]
            for cp in first:
                cp.start()
            started += first
        for a in range(n):
            for j, (px, py) in enumerate(chips):
                slot = 4 * px + 2 * py + c
                copy(a, 1 + j, slot, (x, y, c)).wait_recv()
                fwd = copy(a, 4 + j, slot, (x, y, 1 - c))
                fwd.start()
                started.append(fwd)
        for a in range(n):
            copy(a, 0, sib_slot, (x, y, c)).wait_recv()
            for j, (px, py) in enumerate(chips):
                copy(a, 4 + j, 4 * px + 2 * py + (1 - c), (x, y, c)).wait_recv()
        for cp in started:
            cp.wait_send()
        for cp in local:
            cp.wait()

    out_shape = [jax.ShapeDtypeStruct((N_DEV,) + a.shape, a.dtype) for a in arrays]
    any_spec = pl.BlockSpec(memory_space=pl.ANY)
    return pl.pallas_call(
        body, name=name, in_specs=[any_spec] * n, out_specs=[any_spec] * n, out_shape=out_shape,
        scratch_shapes=[pltpu.SemaphoreType.DMA((7 * n,)), pltpu.SemaphoreType.DMA((7 * n,)), pltpu.SemaphoreType.DMA((n,))],
        compiler_params=pltpu.CompilerParams(has_side_effects=True),
    )(*arrays)


def _grad_slot(d):
    return (d % 2) * (N_DEV // 2) + d // 2


def sibling_exchange(gbuf, name):
    half = N_DEV // 2

    def body(g_ref, o_ref, send_sem, recv_sem):
        x, y, c = lax.axis_index("x"), lax.axis_index("y"), lax.axis_index("c")
        cp = pltpu.make_async_remote_copy(
            src_ref=g_ref.at[pl.ds((1 - c) * half, half)], dst_ref=o_ref, send_sem=send_sem, recv_sem=recv_sem,
            device_id=(x, y, 1 - c), device_id_type=pl.DeviceIdType.MESH)
        cp.start()
        cp.wait()

    any_spec = pl.BlockSpec(memory_space=pl.ANY)
    return pl.pallas_call(
        body, name=name, in_specs=[any_spec], out_specs=any_spec,
        out_shape=jax.ShapeDtypeStruct((half,) + gbuf.shape[1:], gbuf.dtype),
        scratch_shapes=[pltpu.SemaphoreType.DMA, pltpu.SemaphoreType.DMA],
        compiler_params=pltpu.CompilerParams(has_side_effects=True),
    )(gbuf)


def pair_sum(a, b, name):
    n, rows, cols = a.shape
    tr = rows // 4 if rows % 64 == 0 else rows

    def body(a_ref, b_ref, o_ref):
        o_ref[...] = (a_ref[...].astype(F32) + b_ref[...].astype(F32)).astype(o_ref.dtype)

    blk = pl.BlockSpec((None, tr, cols), lambda i, j: (i, j, 0))
    return pl.pallas_call(
        body, name=name, grid=(n, rows // tr), in_specs=[blk, blk], out_specs=blk,
        out_shape=jax.ShapeDtypeStruct(a.shape, a.dtype), compiler_params=_cparams(("parallel", "parallel")),
    )(a, b)


def chip_exchange(p, name):
    half = N_DEV // 2

    def body(p_ref, o_ref, send_sems, recv_sems, local_sem):
        x, y, c = lax.axis_index("x"), lax.axis_index("y"), lax.axis_index("c")
        q = 2 * x + y
        own = pltpu.make_async_copy(p_ref.at[q], o_ref.at[q], local_sem)
        own.start()
        pending = []
        for j in range(1, half):
            px = 1 - x if j & 2 else x
            py = 1 - y if j & 1 else y
            pq = 2 * px + py

            def copy(dst_slot):
                return pltpu.make_async_remote_copy(
                    src_ref=p_ref.at[pq], dst_ref=o_ref.at[dst_slot], send_sem=send_sems.at[j - 1],
                    recv_sem=recv_sems.at[j - 1], device_id=(px, py, c), device_id_type=pl.DeviceIdType.MESH)

            send = copy(q)
            send.start()
            pending.append((send, copy(pq)))
        for send, recv in pending:
            send.wait_send()
            recv.wait_recv()
        own.wait()

    any_spec = pl.BlockSpec(memory_space=pl.ANY)
    return pl.pallas_call(
        body, name=name, in_specs=[any_spec], out_specs=any_spec, out_shape=jax.ShapeDtypeStruct(p.shape, p.dtype),
        scratch_shapes=[pltpu.SemaphoreType.DMA((half - 1,)), pltpu.SemaphoreType.DMA((half - 1,)), pltpu.SemaphoreType.DMA],
        compiler_params=pltpu.CompilerParams(has_side_effects=True),
    )(p)


BIG = ("ffn1_gate", "ffn1_up", "ffn1_down", "w_in", "w_out", "ffn2_gate", "ffn2_up", "ffn2_down")
SMALL_REPLICATED = (("norm_ffn1", (DEPTH, D_MODEL)), ("norm_mix", (DEPTH, D_MODEL)), ("norm_ffn2", (DEPTH, D_MODEL)),
                    ("norm_final", (D_MODEL,)), ("hg_lb", (DEPTH, HG_W)), ("hg_norm_w", (DEPTH, HG_DK)),
                    ("gd_a_log", (DEPTH, GD_HEADS)), ("gd_dt_bias", (DEPTH, GD_HEADS)), ("gd_norm_w", (DEPTH, GD_DK)),
                    ("rg_conv_b", (DEPTH, RG_W)), ("rg_wr", (DEPTH, RG_BLOCKS, RG_BD, RG_BD)), ("rg_br", (DEPTH, RG_W)),
                    ("rg_wi", (DEPTH, RG_BLOCKS, RG_BD, RG_BD)), ("rg_bi", (DEPTH, RG_W)), ("rg_lambda", (DEPTH, RG_W)))
SMALL_EXTRA = (("gd_conv_w", (DEPTH, CONV_W, 3 * GD_W)), ("rg_conv_w", (DEPTH, CONV_W, RG_W)), ("loss", (1,)))
SMALL_SHARDED = (("gd_conv_w", (DEPTH, CONV_W, GDC_SHARD)), ("rg_conv_w", (DEPTH, CONV_W, RGC_SHARD)))


def _pad_rows(a, rows):
    return jnp.pad(a, ((0, rows - a.shape[0]), (0, 0)))


def pack_big(get):
    pieces = []
    for l in range(DEPTH):
        for which in (1, 2):
            pieces.append(_pad_rows(get(f"ffn{which}_gate")[l].T, FF_PAD))
            pieces.append(_pad_rows(get(f"ffn{which}_up")[l].T, FF_PAD))
            pieces.append(_pad_rows(get(f"ffn{which}_down")[l], FF_PAD))
    for l in range(DEPTH):
        pieces.append(_pad_rows(get("w_in")[l].T, NIN_SHARD_PAD))
    for l in range(DEPTH):
        pieces.append(get("w_out")[l])
    return jnp.concatenate(pieces, axis=0)


def unpack_big(p):
    out = {}
    for which in (1, 2):
        for part, nm in (("g", "gate"), ("u", "up"), ("d", "down")):
            per_layer = []
            for l in range(DEPTH):
                r0 = _ffn_block(l, which, part) * FF_PAD
                blk = p[r0:r0 + FF_SHARD, :]
                per_layer.append(blk if part == "d" else blk.T)
            out[f"ffn{which}_{nm}"] = jnp.stack(per_layer)
    out["w_in"] = jnp.stack([p[OFF_WIN + NIN_SHARD_PAD * l:OFF_WIN + NIN_SHARD_PAD * l + NIN_SHARD, :].T for l in range(DEPTH)])
    out["w_out"] = jnp.stack([p[OFF_WOUT + WOUT_SHARD * l:OFF_WOUT + WOUT_SHARD * (l + 1), :] for l in range(DEPTH)])
    return out


def pack_small(get, spec):
    flat = []
    for name, shape in spec:
        a = get(name)
        flat.append(jnp.zeros((math.prod(shape),), F32) if a is None else a.reshape(-1).astype(F32))
    v = jnp.concatenate(flat)
    total = -(-v.shape[0] // 1024) * 1024
    return jnp.pad(v, (0, total - v.shape[0])).reshape(total // 128, 128)


def unpack_small(p, spec):
    v = p.reshape(-1)
    out, off = {}, 0
    for name, shape in spec:
        size = math.prod(shape)
        out[name] = v[off:off + size].reshape(shape)
        off += size
    return out


def _block_diag(w):
    rows = []
    for i in range(RG_BLOCKS):
        rows.append(jnp.concatenate([w[i] if j == i else jnp.zeros((RG_BD, RG_BD), F32) for j in range(RG_BLOCKS)], axis=1))
    return jnp.concatenate(rows, axis=0)


def _diag_blocks(w):
    return jnp.stack([w[RG_BD * i:RG_BD * (i + 1), RG_BD * i:RG_BD * (i + 1)] for i in range(RG_BLOCKS)])


def _lane_vec(v):
    return jnp.pad(v.astype(F32), (GD_HEADS, 128 - 2 * GD_HEADS))[None]


INPUT_NAMES = ("x", "norm_ffn1", "ffn1_gate", "ffn1_up", "ffn1_down", "norm_mix", "w_in", "hg_lb", "hg_norm_w",
               "gd_conv_w", "gd_a_log", "gd_dt_bias", "gd_norm_w", "rg_conv_w", "rg_conv_b", "rg_wr", "rg_br", "rg_wi",
               "rg_bi", "rg_lambda", "w_out", "norm_ffn2", "ffn2_gate", "ffn2_up", "ffn2_down", "norm_final")
WEIGHT_NAMES = INPUT_NAMES[1:]


def _step(a):
    x = a["x"][0]
    target = a["loss_target"][0]
    me = 4 * lax.axis_index("x") + 2 * lax.axis_index("y") + lax.axis_index("c")

    w_pack = pack_big(lambda nm: a[nm])
    conv_local = pack_small(lambda nm: a[nm], SMALL_SHARDED)
    assert DEPTH == 2
    w_bf16 = w_pack.astype(BF16)
    gw_first, gconv = gather_two_level([w_bf16[:FIRST_BLOCKS * FF_PAD], conv_local], "gather_weights")

    def carried_rows(b0, b1, l):
        return jnp.concatenate([w_bf16[b0 * FF_PAD:b1 * FF_PAD],
                                w_bf16[OFF_WIN + NIN_SHARD_PAD * l:OFF_WIN + NIN_SHARD_PAD * (l + 1)],
                                w_bf16[OFF_WOUT + WOUT_SHARD * l:OFF_WOUT + WOUT_SHARD * (l + 1)]], axis=0)

    carry_a = carried_rows(FIRST_BLOCKS, 6, 0)
    carry_b = carried_rows(6, N_FFN_BLOCKS, 1)

    conv_parts = [unpack_small(gconv[d], SMALL_SHARDED) for d in range(N_DEV)]
    gd_conv_w = jnp.concatenate([p["gd_conv_w"] for p in conv_parts], axis=-1)
    rg_conv_w = jnp.concatenate([p["rg_conv_w"] for p in conv_parts], axis=-1)

    def mixer_params(l):
        return (a["hg_lb"], a["hg_norm_w"][l][None], gd_conv_w[l], _lane_vec(a["gd_a_log"][l]), _lane_vec(a["gd_dt_bias"][l]),
                a["gd_norm_w"][l][None], rg_conv_w[l], a["rg_conv_b"][l][None], _block_diag(a["rg_wr"][l]), a["rg_br"][l][None],
                _block_diag(a["rg_wi"][l]), a["rg_bi"][l][None], a["rg_lambda"][l][None])

    saved = []
    xs = x
    for l in range(DEPTH):
        if l == 0:
            x1, h1, g1, u1, a1, gw_a = ffn_fwd(xs, a["norm_ffn1"][l][None], gw_first, l, 1, carry=carry_a)
            packs = {1: (gw_first, 0), 2: (gw_a, FIRST_BLOCKS)}
            late, r = gw_a, (6 - FIRST_BLOCKS) * FF_PAD
        else:
            packs = {1: (gw_b, 6), 2: (gw_b, 6)}
            late, r = gw_b, (N_FFN_BLOCKS - 6) * FF_PAD
            x1, h1, g1, u1, a1 = ffn_fwd(xs, a["norm_ffn1"][l][None], gw_b, l, 1, blk0=6)
        wt = win_to_padded(late[:, r:r + NIN_SHARD_PAD, :], l)
        proj, h2 = in_proj_fwd(x1, a["norm_mix"][l][None], wt, l)
        prm = mixer_params(l)
        y, o_mix, st, s, h = mixer_fwd(l, proj, prm)
        wo = late[:, r + NIN_SHARD_PAD:r + NIN_SHARD_PAD + WOUT_SHARD, :].reshape(D_MODEL, D_MODEL)
        x2 = out_proj_fwd(x1, y, wo, l)
        if l == 0:
            x3, h3, g3, u3, a3, gw_b = ffn_fwd(x2, a["norm_ffn2"][l][None], gw_a, l, 2, blk0=FIRST_BLOCKS, carry=carry_b)
        else:
            x3, h3, g3, u3, a3 = ffn_fwd(x2, a["norm_ffn2"][l][None], gw_b, l, 2, blk0=6)
        saved.append(dict(x0=xs, x1=x1, x2=x2, h1=h1, g1=g1, u1=u1, a1=a1, wt=wt, proj=proj, h2=h2, prm=prm, y=y, o_mix=o_mix,
                          states=(st, s, h), wo=wo, h3=h3, g3=g3, u3=u3, a3=a3, packs=packs))
        xs = x3
    loss_row, dx, d_norm_final = loss_head(xs, a["norm_final"][None], target)

    gbuf = jnp.zeros((N_DEV, PACK_ROWS, D_MODEL), BF16)
    sg = {"norm_final": d_norm_final[0], "loss": loss_row[0, 0:1]}
    per_layer = {nm: [None] * DEPTH for nm in ("norm_ffn1", "norm_mix", "norm_ffn2", "hg_norm_w", "gd_conv_w", "gd_a_log",
                                                "gd_dt_bias", "gd_norm_w", "rg_conv_w", "rg_conv_b", "rg_wr", "rg_br", "rg_wi",
                                                "rg_bi", "rg_lambda")}
    d_hg_lb = jnp.zeros((DEPTH, HG_W), F32)
    for l in reversed(range(DEPTH)):
        sv = saved[l]
        pack2, blk2 = sv["packs"][2]
        dx2, dg, du, dn2 = ffn_bwd(dx, sv["x2"], a["norm_ffn2"][l][None], sv["g3"], sv["u3"], pack2, l, 2, blk2)
        gbuf = tn_matmul(dg, sv["h3"], 1.0, FF_PAD, f"dw_gate_l{l}_2", gbuf, _ffn_block(l, 2, "g"))
        gbuf = tn_matmul(du, sv["h3"], 1.0, FF_PAD, f"dw_up_l{l}_2", gbuf, _ffn_block(l, 2, "u"))
        gbuf = tn_matmul(sv["a3"], dx, 0.5, FF_PAD, f"dw_down_l{l}_2", gbuf, _ffn_block(l, 2, "d"))
        dy = out_proj_bwd(dx2, sv["wo"], l)
        gbuf = tn_matmul(sv["y"], dx2, 1.0, WOUT_SHARD, f"dw_out_l{l}", gbuf, (OFF_WOUT + WOUT_SHARD * l) // WOUT_SHARD)
        mb = mixer_bwd(l, sv["proj"], dy, sv["o_mix"], sv["states"], sv["prm"])
        dproj, dprm = mb[0], mb[1:]
        dwt = tn_matmul(dproj, sv["h2"], 1.0, FF_PAD, f"dw_in_l{l}")
        gbuf = win_grad_to_pack(dwt, gbuf, l)
        dx1, dnm = in_proj_bwd(dproj, sv["x1"], a["norm_mix"][l][None], dx2, sv["wt"], l)
        pack1, blk1 = sv["packs"][1]
        dx, dg, du, dn1 = ffn_bwd(dx1, sv["x0"], a["norm_ffn1"][l][None], sv["g1"], sv["u1"], pack1, l, 1, blk1)
        gbuf = tn_matmul(dg, sv["h1"], 1.0, FF_PAD, f"dw_gate_l{l}_1", gbuf, _ffn_block(l, 1, "g"))
        gbuf = tn_matmul(du, sv["h1"], 1.0, FF_PAD, f"dw_up_l{l}_1", gbuf, _ffn_block(l, 1, "u"))
        gbuf = tn_matmul(sv["a1"], dx1, 0.5, FF_PAD, f"dw_down_l{l}_1", gbuf, _ffn_block(l, 1, "d"))
        (g_lb, g_hnw, g_gcw, g_alog, g_dt, g_gnw, g_rcw, g_rcb, g_wr, g_br, g_wi, g_bi, g_lam) = dprm
        d_hg_lb = d_hg_lb + g_lb
        for nm, val in (("norm_ffn1", dn1[0]), ("norm_mix", dnm[0]), ("norm_ffn2", dn2[0]), ("hg_norm_w", g_hnw[0]),
                        ("gd_conv_w", g_gcw), ("gd_a_log", g_alog[0, GD_HEADS:2 * GD_HEADS]),
                        ("gd_dt_bias", g_dt[0, GD_HEADS:2 * GD_HEADS]), ("gd_norm_w", g_gnw[0]), ("rg_conv_w", g_rcw),
                        ("rg_conv_b", g_rcb[0]), ("rg_wr", _diag_blocks(g_wr)), ("rg_br", g_br[0]),
                        ("rg_wi", _diag_blocks(g_wi)), ("rg_bi", g_bi[0]), ("rg_lambda", g_lam[0])):
            per_layer[nm][l] = val
    grad_x = dx
    for nm, vals in per_layer.items():
        sg[nm] = jnp.stack(vals)
    sg["hg_lb"] = d_hg_lb

    small_spec = SMALL_REPLICATED + SMALL_EXTRA
    from_sibling = sibling_exchange(gbuf, "grads_to_sibling")
    mine = lax.dynamic_slice_in_dim(gbuf, lax.axis_index("c") * (N_DEV // 2), N_DEV // 2, axis=0)
    parts_big = chip_exchange(pair_sum(mine, from_sibling, "grads_pair_sum"), "grads_to_chips")
    g_big, d_big, m_big, v_big = sum_adamw(parts_big, w_pack, pack_big(lambda nm: a["m_" + nm]),
                                           pack_big(lambda nm: a["v_" + nm]), "adamw_big", 128)
    (parts_small,) = gather_two_level([pack_small(lambda nm: sg[nm], small_spec)], "gather_small_grads")
    repl = dict(SMALL_REPLICATED)
    small_in = lambda pre: pack_small(lambda nm: a[pre + nm] if nm in repl else None, small_spec)
    rows_small = parts_small.shape[1]
    g_sm, d_sm, m_sm, v_sm = sum_adamw(parts_small, small_in(""), small_in("m_"), small_in("v_"), "adamw_small", rows_small)
    g_small = unpack_small(g_sm, small_spec)
    conv_grads = {"gd_conv_w": lax.dynamic_slice_in_dim(g_small["gd_conv_w"], me * GDC_SHARD, GDC_SHARD, axis=2),
                  "rg_conv_w": lax.dynamic_slice_in_dim(g_small["rg_conv_w"], me * RGC_SHARD, RGC_SHARD, axis=2)}
    conv_in = lambda pre: pack_small(lambda nm: a[pre + nm], SMALL_SHARDED)
    g_cv, d_cv, m_cv, v_cv = sum_adamw(pack_small(lambda nm: conv_grads[nm], SMALL_SHARDED)[None], conv_in(""),
                                       conv_in("m_"), conv_in("v_"), "adamw_conv", conv_local.shape[0])

    results = []
    for big, small, conv in ((g_big, g_sm, g_cv), (d_big, d_sm, d_cv), (m_big, m_sm, m_cv), (v_big, v_sm, v_cv)):
        vals = unpack_big(big)
        vals.update({k: v for k, v in unpack_small(small, small_spec).items() if k in repl})
        vals.update(unpack_small(conv, SMALL_SHARDED))
        results.append(vals)
    loss = g_small["loss"][0]
    out = [loss, grad_x[None]]
    for vals in results:
        out.extend(vals[nm] for nm in WEIGHT_NAMES)
    return tuple(out)


def kernel(x, norm_ffn1, ffn1_gate, ffn1_up, ffn1_down, norm_mix, w_in, hg_lb, hg_norm_w, gd_conv_w, gd_a_log, gd_dt_bias, gd_norm_w, rg_conv_w, rg_conv_b, rg_wr, rg_br, rg_wi, rg_bi, rg_lambda, w_out, norm_ffn2, ffn2_gate, ffn2_up, ffn2_down, norm_final, loss_target, m_norm_ffn1, m_ffn1_gate, m_ffn1_up, m_ffn1_down, m_norm_mix, m_w_in, m_hg_lb, m_hg_norm_w, m_gd_conv_w, m_gd_a_log, m_gd_dt_bias, m_gd_norm_w, m_rg_conv_w, m_rg_conv_b, m_rg_wr, m_rg_br, m_rg_wi, m_rg_bi, m_rg_lambda, m_w_out, m_norm_ffn2, m_ffn2_gate, m_ffn2_up, m_ffn2_down, m_norm_final, v_norm_ffn1, v_ffn1_gate, v_ffn1_up, v_ffn1_down, v_norm_mix, v_w_in, v_hg_lb, v_hg_norm_w, v_gd_conv_w, v_gd_a_log, v_gd_dt_bias, v_gd_norm_w, v_rg_conv_w, v_rg_conv_b, v_rg_wr, v_rg_br, v_rg_wi, v_rg_bi, v_rg_lambda, v_w_out, v_norm_ffn2, v_ffn2_gate, v_ffn2_up, v_ffn2_down, v_norm_final):
    args = locals()
    return _step(dict(args))
```

```python
import functools
import math

import jax
import jax.numpy as jnp
from jax import lax
from jax.experimental import pallas as pl
from jax.experimental.pallas import tpu as pltpu

F32 = jnp.float32
BF16 = jnp.bfloat16

D_MODEL = 1024
DEPTH = 2
D_FF = 2816
HG_HEADS, HG_DK = 4, 64
HG_W = 256
GD_HEADS, GD_DK = 4, 128
GD_W = 512
RG_W = 256
RG_BLOCKS, RG_BD = 4, 64
RG_C = 8.0
CONV_W = 4
CHUNK = 64
EPS = 1e-6
N_IN = 3592
ADAM_LR, ADAM_B1, ADAM_B2, ADAM_EPS, ADAM_WD, ADAM_STEP = 0.001, 0.9, 0.999, 1e-08, 0.01, 10

N_DEV = 8
FF_SHARD = D_FF // N_DEV
FF_PAD = 384
NIN_SHARD = N_IN // N_DEV
NIN_SHARD_PAD = 512
WOUT_SHARD = D_MODEL // N_DEV
GDC_SHARD = (2 * 512 + 512) // N_DEV
RGC_SHARD = RG_W // N_DEV

O_HQ, O_HF, O_HI, O_HG = 0, 256, 512, 768
O_GQ, O_GK, O_GV, O_GZ = 1024, 1536, 2048, 2560
O_GBA = 3072
O_RX, O_RGATE = 3200, 3456
NP = 3840
GBA_SPLIT = 3080

N_FFN_BLOCKS = DEPTH * 2 * 3
OFF_WIN = N_FFN_BLOCKS * FF_PAD
OFF_WOUT = OFF_WIN + DEPTH * NIN_SHARD_PAD
PACK_ROWS = OFF_WOUT + DEPTH * WOUT_SHARD
FIRST_BLOCKS = 3

VMEM_LIMIT = 56 * 1024 * 1024
FFN_GROUP = 2
FFN_STEPS = N_DEV // FFN_GROUP
FFN_COLS = FFN_GROUP * FF_PAD
MIXER_CHUNKS_FWD = 8
MIXER_CHUNKS_BWD = 4
MIXER_BWD_VMEM_LIMIT = 63 * 1024 * 1024


def _ffn_block(layer, which, part):
    return layer * 6 + (which - 1) * 3 + {"g": 0, "u": 1, "d": 2}[part]


def _cparams(sem, **kw):
    return pltpu.CompilerParams(dimension_semantics=sem, vmem_limit_bytes=VMEM_LIMIT, **kw)


def _dot_nn(a, b):
    return lax.dot_general(a, b, (((1,), (0,)), ((), ())), preferred_element_type=F32)


def _dot_nt(a, b):
    return lax.dot_general(a, b, (((1,), (1,)), ((), ())), preferred_element_type=F32)


def _dot_tn(a, b):
    return lax.dot_general(a, b, (((0,), (0,)), ((), ())), preferred_element_type=F32)


def _split3(x):
    hi = x.astype(BF16)
    r = x - hi.astype(F32)
    mid = r.astype(BF16)
    lo = (r - mid.astype(F32)).astype(BF16)
    return hi, mid, lo


def _split2(x):
    hi = x.astype(BF16)
    return hi, (x - hi.astype(F32)).astype(BF16)


def _iota(shape, dim):
    return lax.broadcasted_iota(jnp.int32, shape, dim)


def _sigmoid(x):
    return 0.5 * (jnp.tanh(0.5 * x) + 1.0)


def _logistic(x):
    return jax.nn.sigmoid(x)


def _silu(x):
    return x * _sigmoid(x)


def _softplus(x):
    return jnp.maximum(x, 0.0) + jnp.log(1.0 + jnp.exp(-jnp.abs(x)))


def _gelu_tanh(x):
    return 0.5 * x * (1.0 + jnp.tanh(math.sqrt(2.0 / math.pi) * (x + 0.044715 * x * x * x)))


def _rms_fwd(x, w):
    rstd = lax.rsqrt(jnp.mean(x * x, axis=-1, keepdims=True) + EPS)
    return x * rstd * w


def _rms_bwd(dh, x, w):
    rstd = lax.rsqrt(jnp.mean(x * x, axis=-1, keepdims=True) + EPS)
    xhat = x * rstd
    dxhat = dh * w
    dx = rstd * (dxhat - xhat * jnp.mean(dxhat * xhat, axis=-1, keepdims=True))
    return dx, jnp.sum(dh * xhat, axis=0, keepdims=True)


def _rank_dims(kind, ndim):
    if ndim == 2:
        return {"nn": (((1,), (0,)), ((), ())), "nt": (((1,), (1,)), ((), ())), "tn": (((0,), (0,)), ((), ()))}[kind]
    return {"nn": (((2,), (1,)), ((0,), (0,))), "nt": (((2,), (2,)), ((0,), (0,))), "tn": (((1,), (1,)), ((0,), (0,)))}[kind]


def _rdot(kind, a, b):
    return lax.dot_general(a, b, _rank_dims(kind, a.ndim), preferred_element_type=F32)


def _make_mm(kind, kind_da, kind_db, swap_da, swap_db):
    @jax.custom_vjp
    def mm(a, b):
        return _rdot(kind, a.astype(BF16), b.astype(BF16))

    def fwd(a, b):
        a, b = a.astype(BF16), b.astype(BF16)
        return _rdot(kind, a, b), (a, b)

    def bwd(res, g):
        a, b = res
        g = g.astype(BF16)
        da = _rdot(kind_da, b, g) if swap_da else _rdot(kind_da, g, b)
        db = _rdot(kind_db, g, a) if swap_db else _rdot(kind_db, a, g)
        return da, db

    mm.defvjp(fwd, bwd)
    return mm


mm_nn = _make_mm("nn", "nt", "tn", False, False)
mm_nt = _make_mm("nt", "nn", "tn", False, True)
mm_tn = _make_mm("tn", "nt", "nn", True, False)


def _bcast_const(mat, like):
    return mat if like.ndim == 2 else jnp.broadcast_to(mat, (like.shape[0],) + mat.shape)


def _sel_apply(kind, sel, x):
    hi, mid, lo = _split3(x)
    s = _bcast_const(sel, x)
    return _rdot(kind, s, hi) + _rdot(kind, s, mid) + _rdot(kind, s, lo)


@jax.custom_vjp
def sel_mm(sel, x):
    return _sel_apply("nn", sel, x)


def _sel_mm_fwd(sel, x):
    return _sel_apply("nn", sel, x), sel


def _sel_mm_bwd(sel, g):
    return jnp.zeros_like(sel), _sel_apply("tn", sel, g)


sel_mm.defvjp(_sel_mm_fwd, _sel_mm_bwd)


def _transpose_apply(a):
    n = a.shape[-1]
    eye = (_iota((n, n), 0) == _iota((n, n), 1)).astype(BF16)
    hi, mid, lo = _split3(a)
    e = _bcast_const(eye, a)
    return _rdot("nt", e, hi) + _rdot("nt", e, mid) + _rdot("nt", e, lo)


@jax.custom_vjp
def exact_transpose(a):
    return _transpose_apply(a)


exact_transpose.defvjp(lambda a: (_transpose_apply(a), None), lambda _, g: (_transpose_apply(g),))


def _dot3(kind, a, b):
    ah, al = _split2(a)
    bh, bl = _split2(b)
    return _rdot(kind, ah, bh) + _rdot(kind, ah, bl) + _rdot(kind, al, bh)


def _inverse_levels(lower, n):
    nd = lower.ndim
    r, c = _iota(lower.shape, nd - 2), _iota(lower.shape, nd - 1)
    tinv = (r == c).astype(F32)
    b = 1
    while b < n:
        off = (r // (2 * b) == c // (2 * b)) & (r % (2 * b) >= b) & (c % (2 * b) < b)
        tinv = tinv - _dot3("nn", tinv, _dot3("nn", jnp.where(off, lower, 0.0), tinv))
        b *= 2
    return tinv


@functools.partial(jax.custom_vjp, nondiff_argnums=(1,))
def unit_lower_inverse(lower, n):
    return _inverse_levels(lower, n)


def _unit_lower_inverse_fwd(lower, n):
    tinv = _inverse_levels(lower, n)
    return tinv, tinv


def _unit_lower_inverse_bwd(n, tinv, g):
    return (-_dot3("nt", _dot3("tn", tinv, g), tinv),)


unit_lower_inverse.defvjp(_unit_lower_inverse_fwd, _unit_lower_inverse_bwd)


def mm3_nn(a, b):
    ah = a.astype(BF16).astype(F32)
    al = a - ah
    bh = b.astype(BF16).astype(F32)
    bl = b - bh
    return mm_nn(ah, bh) + mm_nn(ah, bl) + mm_nn(al, bh)


def _seg_apply(x, b):
    hi, mid, lo = _split3(x)
    return _rdot("nn", hi, b) + _rdot("nn", mid, b) + _rdot("nn", lo, b)


@jax.custom_vjp
def _seg_sum(x, b):
    return _seg_apply(x, b)


_seg_sum.defvjp(lambda x, b: (_seg_apply(x, b), b), lambda b, g: (_seg_apply(g, b), jnp.zeros_like(b)))


def _shift_rows(x, halo, k):
    n = x.shape[0]
    ext = jnp.concatenate([jnp.zeros((n - halo.shape[0], x.shape[1]), x.dtype), halo], axis=0)
    t = _iota(x.shape, 0)
    return jnp.where(t >= k, jnp.roll(x, k, axis=0), jnp.roll(ext, k, axis=0))


def _causal_conv(x, halo, w):
    y = x * w[3:4, :]
    for k in range(1, CONV_W):
        y = y + _shift_rows(x, halo, k) * w[3 - k:4 - k, :]
    return y


def _lb_of_layer(hg_lb, layer):
    if layer == 0:
        return jnp.zeros((1, HG_W), F32)
    e = jnp.exp(hg_lb - jnp.max(hg_lb, axis=0, keepdims=True))
    sm = e / jnp.sum(e, axis=0, keepdims=True)
    lb = sm[1:2, :]
    for l in range(2, layer + 1):
        lb = lb + sm[l:l + 1, :]
    return lb


def _chunks(x):
    return x.reshape(x.shape[0] // CHUNK, CHUNK, x.shape[1])


def _hgrn2_pre(hq, hf, hi, lb):
    n, w = CHUNK, HG_W
    q = _chunks(_silu(hq) * (HG_DK ** -0.5))
    f = lb + (1.0 - lb) * _logistic(hf)
    k = _chunks(1.0 - f)
    v = _chunks(hi)
    nb = q.shape[0]
    row, col = _iota((n, n), 0), _iota((n, n), 1)
    levels = [n >> j for j in range(1, n.bit_length())]
    picks = [col <= row] + [col <= (row // (2 * b)) * (2 * b) + b - 1 for b in levels]
    cums = sel_mm(jnp.concatenate(picks, axis=0).astype(BF16), _chunks(jnp.log(f)))
    cum = cums[:, 0:n, :]
    lane_head = _iota((nb, n, w), 2) // HG_DK

    def stack(x):
        return jnp.concatenate([jnp.where(lane_head == h, x, 0.0) for h in range(HG_HEADS)], axis=1)

    t_idx = _iota((nb, n, HG_HEADS * n), 1)
    s_idx = _iota((nb, n, HG_HEADS * n), 2) % n
    p = jnp.where(t_idx == s_idx, mm_nt(q, stack(k)), 0.0)
    for j, b in enumerate(levels):
        ref = cums[:, (j + 1) * n:(j + 2) * n, :]
        qe = q * jnp.exp(jnp.minimum(cum - ref, 0.0))
        ke = k * jnp.exp(jnp.minimum(ref - cum, 0.0))
        mask = (t_idx // (2 * b) == s_idx // (2 * b)) & (t_idx % (2 * b) >= b) & (s_idx % (2 * b) < b)
        p = p + jnp.where(mask, mm_nt(qe, stack(ke)), 0.0)
    last = cum[:, n - 1:n, :]
    return mm_nn(p, stack(v)), q * jnp.exp(cum), k * jnp.exp(last - cum), jnp.exp(last), v


def _gdn_pre(gq, gk, gv, gba, halo_q, halo_k, halo_v, conv_w, alog_vec, dt_vec):
    n = CHUNK
    cq = _silu(_causal_conv(gq, halo_q, conv_w[:, 0:GD_W]))
    ck = _silu(_causal_conv(gk, halo_k, conv_w[:, GD_W:2 * GD_W]))
    cv = _silu(_causal_conv(gv, halo_v, conv_w[:, 2 * GD_W:3 * GD_W]))

    def heads(x):
        return jnp.concatenate([_chunks(x[:, GD_DK * h:GD_DK * (h + 1)]) for h in range(GD_HEADS)], axis=0)

    def l2n(x):
        return x * lax.rsqrt(jnp.sum(x * x, axis=-1, keepdims=True) + EPS)

    q = l2n(heads(cq)) * (GD_DK ** -0.5)
    k = l2n(heads(ck))
    v = heads(cv)
    beta_full = _chunks(_sigmoid(gba))
    g_full = _chunks(-jnp.exp(alog_vec) * _softplus(gba + dt_vec))
    tri = (_iota((n, n), 0) >= _iota((n, n), 1)).astype(BF16)
    cum_full = sel_mm(tri, g_full)
    beta = jnp.concatenate([beta_full[:, :, h:h + 1] for h in range(GD_HEADS)], axis=0)
    cum = jnp.concatenate([cum_full[:, :, GD_HEADS + h:GD_HEADS + h + 1] for h in range(GD_HEADS)], axis=0)
    bsz = cum.shape[0]
    ccol = jnp.broadcast_to(cum, (bsz, n, n))
    diff = ccol - exact_transpose(ccol)
    r_i, c_i = _iota((bsz, n, n), 1), _iota((bsz, n, n), 2)
    decay = jnp.exp(jnp.minimum(diff, 0.0))
    kb = k * beta
    lower = jnp.where(r_i > c_i, mm_nt(kb, k) * decay, 0.0)
    tinv = unit_lower_inverse(lower, n)
    ecum = jnp.exp(cum)
    sol = mm3_nn(tinv, jnp.concatenate([v * beta, kb * ecum], axis=2))
    scores = jnp.where(r_i >= c_i, mm_nt(q, k) * decay, 0.0)
    last = cum[:, n - 1:n, :]
    return sol[:, :, 0:GD_DK], sol[:, :, GD_DK:2 * GD_DK], scores, q * ecum, k * jnp.exp(last - cum), jnp.exp(last)


def _rglru_pre(rx, halo_x, conv_w, conv_b, wr, br, wi, bi, lam):
    xc = _causal_conv(rx, halo_x, conv_w) + conv_b
    r = _logistic(mm_nn(xc, wr) + br)
    ig = _sigmoid(mm_nn(xc, wi) + bi)
    two = 2.0 * (-RG_C * r * _softplus(-lam))
    a = jnp.exp(0.5 * two)
    one_minus = -jnp.tanh(0.5 * two) * (jnp.exp(two) + 1.0)
    acc_a, acc_b = a, jnp.sqrt(one_minus) * (ig * xc)
    t = _iota(acc_a.shape, 0) % CHUNK
    k = 1
    while k < CHUNK:
        sa = jnp.where(t >= k, jnp.roll(acc_a, k, axis=0), 1.0)
        sb = jnp.where(t >= k, jnp.roll(acc_b, k, axis=0), 0.0)
        acc_b = acc_a * sb + acc_b
        acc_a = acc_a * sa
        k *= 2
    return acc_a, acc_b


MIXER_PARAMS = ("hg_lb", "hg_norm_w", "gd_conv_w", "gd_alog_vec", "gd_dt_vec", "gd_norm_w",
                "rg_conv_w", "rg_conv_b", "rg_wr_bd", "rg_br", "rg_wi_bd", "rg_bi", "rg_lambda")


def mixer_pre(layer, proj, halo, params):
    (hg_lb, _, gd_conv_w, alog_vec, dt_vec, _, rg_conv_w, rg_conv_b, wr, br, wi, bi, lam) = params
    hg = _hgrn2_pre(proj[:, O_HQ:O_HQ + 256], proj[:, O_HF:O_HF + 256], proj[:, O_HI:O_HI + 256], _lb_of_layer(hg_lb, layer))
    gd = _gdn_pre(proj[:, O_GQ:O_GQ + 512], proj[:, O_GK:O_GK + 512], proj[:, O_GV:O_GV + 512], proj[:, O_GBA:O_GBA + 128],
                  halo[:, O_GQ:O_GQ + 512], halo[:, O_GK:O_GK + 512], halo[:, O_GV:O_GV + 512], gd_conv_w, alog_vec, dt_vec)
    rg = _rglru_pre(proj[:, O_RX:O_RX + 256], halo[:, O_RX:O_RX + 256], rg_conv_w, rg_conv_b, wr, br, wi, bi, lam)
    return hg, gd, rg


def mixer_pre_chunk(pre, c, nb):
    hg, gd, rg = pre
    pick = lambda x: jnp.concatenate([x[h * nb + c:h * nb + c + 1] for h in range(GD_HEADS)], axis=0)
    return (tuple(x[c] for x in hg), tuple(pick(x) for x in gd), tuple(x[c * CHUNK:(c + 1) * CHUNK, :] for x in rg))


def mixer_pre_unchunk(parts, nb):
    hg = tuple(jnp.concatenate([parts[c][0][i][None] for c in range(nb)], axis=0) for i in range(len(parts[0][0])))
    gd = tuple(jnp.concatenate([parts[c][1][i][h:h + 1] for h in range(GD_HEADS) for c in range(nb)], axis=0)
               for i in range(len(parts[0][1])))
    rg = tuple(jnp.concatenate([parts[c][2][i] for c in range(nb)], axis=0) for i in range(len(parts[0][2])))
    return hg, gd, rg


def mixer_rec(pre_c, st_hg, s_gd, h_rg):
    (o_intra, qe, kd, elast, v), (u, w, scores, gqe, gkd, gel), (acc_a, acc_b) = pre_c
    o_hg = o_intra + mm_nt(qe, st_hg)
    blk = _iota((HG_W, HG_W), 0) // HG_DK == _iota((HG_W, HG_W), 1) // HG_DK
    st_new = st_hg * elast + jnp.where(blk, mm_tn(v, kd), 0.0)
    v_new = u - mm_nn(w, s_gd)
    o_gd = mm_nn(gqe, s_gd) + mm_nn(scores, v_new)
    s_new = s_gd * gel + mm_tn(gkd, v_new)
    hs = acc_b + acc_a * h_rg
    return (o_hg, o_gd, hs), (st_new, s_new, hs[CHUNK - 1:CHUNK, :])


def mixer_post(o_hg, o_gd, hs, proj, params):
    (_, hg_norm_w, _, _, _, gd_norm_w, _, _, _, _, _, _, _) = params
    blk = (_iota((HG_W, HG_W), 0) // HG_DK == _iota((HG_W, HG_W), 1) // HG_DK).astype(BF16)
    ms = _seg_sum(o_hg * o_hg, blk) * (1.0 / HG_DK)
    y_hg = o_hg * lax.rsqrt(ms + EPS) * jnp.concatenate([hg_norm_w] * HG_HEADS, axis=1) * _silu(proj[:, O_HG:O_HG + 256])
    parts = []
    for h in range(GD_HEADS):
        o = o_gd[:, GD_DK * h:GD_DK * (h + 1)]
        parts.append(o * lax.rsqrt(jnp.mean(o * o, axis=-1, keepdims=True) + EPS) * gd_norm_w)
    y_gd = jnp.concatenate(parts, axis=1) * _silu(proj[:, O_GZ:O_GZ + 512])
    y_rg = hs * _gelu_tanh(proj[:, O_RGATE:O_RGATE + 256])
    return jnp.concatenate([y_hg, y_gd, y_rg], axis=1)


def mixer_step_forward(layer, proj, halo, states, params):
    nb = proj.shape[0] // CHUNK
    pre = mixer_pre(layer, proj, halo, params)
    outs, entered = [], []
    for c in range(nb):
        entered.append(states)
        o, states = mixer_rec(mixer_pre_chunk(pre, c, nb), *states)
        outs.append(o)
    o_hg = jnp.concatenate([o[0] for o in outs], axis=0)
    o_gd = jnp.concatenate([jnp.concatenate([o[1][h] for h in range(GD_HEADS)], axis=1) for o in outs], axis=0)
    hs = jnp.concatenate([o[2] for o in outs], axis=0)
    return mixer_post(o_hg, o_gd, hs, proj, params), (o_hg, o_gd, hs), entered, states


def _full(shape):
    return pl.BlockSpec(shape, lambda *_: (0,) * len(shape))


def _token_tile(seq, want):
    return min(want, seq)


def _gather_phases(src, out, send_sems, recv_sems, local_sem):
    x, y, c = lax.axis_index("x"), lax.axis_index("y"), lax.axis_index("c")
    me = 4 * x + 2 * y + c
    sibling = (x, y, 1 - c)
    chips = [(1 - x, y), (x, 1 - y), (1 - x, 1 - y)]

    def copy(k, slot, to, from_src=False):
        return pltpu.make_async_remote_copy(
            src_ref=src if from_src else out.at[slot], dst_ref=out.at[slot], send_sem=send_sems.at[k],
            recv_sem=recv_sems.at[k], device_id=to, device_id_type=pl.DeviceIdType.MESH)

    def first_sends():
        return [copy(0, me, sibling, True)] + [copy(1 + j, me, (px, py, c), True) for j, (px, py) in enumerate(chips)]

    def forward(j):
        px, py = chips[j]
        return copy(4 + j, 4 * px + 2 * py + c, sibling)

    def start():
        pltpu.make_async_copy(src, out.at[me], local_sem).start()
        for cp in first_sends():
            cp.start()

    def middle():
        for j, (px, py) in enumerate(chips):
            copy(1 + j, 4 * px + 2 * py + c, (x, y, c)).wait_recv()
            forward(j).start()

    def finish():
        copy(0, 4 * x + 2 * y + (1 - c), (x, y, c)).wait_recv()
        for j, (px, py) in enumerate(chips):
            copy(4 + j, 4 * px + 2 * py + (1 - c), (x, y, c)).wait_recv()
        for cp in first_sends() + [forward(j) for j in range(len(chips))]:
            cp.wait_send()
        pltpu.make_async_copy(src, out.at[me], local_sem).wait()

    return start, middle, finish


def ffn_fwd(x, nw, pack, layer, which, blk0=0, carry=None):
    seq = x.shape[0]
    t = _token_tile(seq, 1024)
    n_i = seq // t
    bg, bu, bd = (_ffn_block(layer, which, p) - blk0 for p in "gud")

    def body(*refs):
        if carry is None:
            x_ref, nw_ref, wg_ref, wu_ref, wd_ref, xo_ref, h_ref, g_ref, u_ref, a_ref, h_scr, acc = refs
        else:
            (x_ref, nw_ref, wg_ref, wu_ref, wd_ref, c_ref, xo_ref, h_ref, g_ref, u_ref, a_ref, co_ref, h_scr, acc,
             send_sems, recv_sems, local_sem) = refs
            start, middle, finish = _gather_phases(c_ref, co_ref, send_sems, recv_sems, local_sem)
            pl.when((pl.program_id(0) == 0) & (pl.program_id(1) == 0))(start)
        k = pl.program_id(1)

        @pl.when(k == 0)
        def _():
            xv = x_ref[...]
            h = _rms_fwd(xv, nw_ref[...]).astype(BF16)
            h_scr[...] = h
            h_ref[...] = h
            acc[...] = xv

        h = h_scr[...]
        g = _dot_nt(h, wg_ref[...].reshape(FFN_COLS, D_MODEL))
        u = _dot_nt(h, wu_ref[...].reshape(FFN_COLS, D_MODEL))
        a = (g * _sigmoid(g) * u).astype(BF16)
        g_ref[...] = g.astype(BF16)
        u_ref[...] = u.astype(BF16)
        a_ref[...] = a
        acc[...] += 0.5 * _dot_nn(a, wd_ref[...].reshape(FFN_COLS, D_MODEL))

        @pl.when(k == FFN_STEPS - 1)
        def _():
            xo_ref[...] = acc[...]

        if carry is not None:
            @pl.when((pl.program_id(0) == n_i - 1) & (k == FFN_STEPS - 1))
            def _():
                middle()
                finish()

    wspec = lambda blk: pl.BlockSpec((FFN_GROUP, FF_PAD, D_MODEL), lambda i, k: (k, blk, 0))
    act = pl.BlockSpec((t, FFN_COLS), lambda i, k: (i, k))
    row = pl.BlockSpec((t, D_MODEL), lambda i, k: (i, 0))
    ff = N_DEV * FF_PAD
    in_specs = [row, _full((1, D_MODEL)), wspec(bg), wspec(bu), wspec(bd)]
    out_specs = [row, row, act, act, act]
    out_shape = [jax.ShapeDtypeStruct((seq, D_MODEL), F32), jax.ShapeDtypeStruct((seq, D_MODEL), BF16)]
    out_shape += [jax.ShapeDtypeStruct((seq, ff), BF16)] * 3
    scratch = [pltpu.VMEM((t, D_MODEL), BF16), pltpu.VMEM((t, D_MODEL), F32)]
    args = [x, nw, pack, pack, pack]
    if carry is not None:
        any_spec = pl.BlockSpec(memory_space=pl.ANY)
        in_specs.append(any_spec)
        out_specs.append(any_spec)
        out_shape.append(jax.ShapeDtypeStruct((N_DEV,) + carry.shape, carry.dtype))
        scratch += [pltpu.SemaphoreType.DMA((7,)), pltpu.SemaphoreType.DMA((7,)), pltpu.SemaphoreType.DMA]
        args.append(carry)
    return pl.pallas_call(
        body, name=f"ffn_fwd_l{layer}_{which}", grid=(n_i, FFN_STEPS), in_specs=in_specs, out_specs=out_specs,
        out_shape=out_shape, scratch_shapes=scratch,
        compiler_params=_cparams(("parallel", "arbitrary") if carry is None else ("arbitrary", "arbitrary")),
    )(*args)


def ffn_bwd(dout, x, nw, g, u, pack, layer, which, blk0=0):
    seq = x.shape[0]
    t = _token_tile(seq, 1024)
    bg, bu, bd = (_ffn_block(layer, which, p) - blk0 for p in "gud")

    def body(do_ref, x_ref, nw_ref, g_ref, u_ref, wg_ref, wu_ref, wd_ref, dx_ref, dg_ref, du_ref, dnw_ref, doh, dh):
        i, k = pl.program_id(0), pl.program_id(1)

        @pl.when(k == 0)
        def _():
            doh[...] = (0.5 * do_ref[...]).astype(BF16)
            dh[...] = jnp.zeros_like(dh)

        @pl.when((k == 0) & (i == 0))
        def _():
            dnw_ref[...] = jnp.zeros_like(dnw_ref)

        da = _dot_nt(doh[...], wd_ref[...].reshape(FFN_COLS, D_MODEL))
        gv = g_ref[...].astype(F32)
        uv = u_ref[...].astype(F32)
        s = _sigmoid(gv)
        dg = (da * uv * (s * (1.0 + gv * (1.0 - s)))).astype(BF16)
        du = (da * gv * s).astype(BF16)
        dg_ref[...] = dg
        du_ref[...] = du
        w_gu = jnp.concatenate([wg_ref[...].reshape(FFN_COLS, D_MODEL), wu_ref[...].reshape(FFN_COLS, D_MODEL)], axis=0)
        dh[...] += _dot_nn(jnp.concatenate([dg, du], axis=1), w_gu)

        @pl.when(k == FFN_STEPS - 1)
        def _():
            dx, dw = _rms_bwd(dh[...], x_ref[...], nw_ref[...])
            dx_ref[...] = do_ref[...] + dx
            dnw_ref[...] += dw

    wspec = lambda blk: pl.BlockSpec((FFN_GROUP, FF_PAD, D_MODEL), lambda i, k: (k, blk, 0))
    act = pl.BlockSpec((t, FFN_COLS), lambda i, k: (i, k))
    row = pl.BlockSpec((t, D_MODEL), lambda i, k: (i, 0))
    row_once = pl.BlockSpec((t, D_MODEL), lambda i, k: (i, 0), pipeline_mode=pl.Buffered(1))
    ff = N_DEV * FF_PAD
    return pl.pallas_call(
        body, name=f"ffn_bwd_l{layer}_{which}", grid=(seq // t, FFN_STEPS),
        in_specs=[row_once, row_once, _full((1, D_MODEL)), act, act, wspec(bg), wspec(bu), wspec(bd)],
        out_specs=[row, act, act, _full((1, D_MODEL))],
        out_shape=[jax.ShapeDtypeStruct((seq, D_MODEL), F32), jax.ShapeDtypeStruct((seq, ff), BF16),
                   jax.ShapeDtypeStruct((seq, ff), BF16), jax.ShapeDtypeStruct((1, D_MODEL), F32)],
        scratch_shapes=[pltpu.VMEM((t, D_MODEL), BF16), pltpu.VMEM((t, D_MODEL), F32)],
        compiler_params=_cparams(("arbitrary", "arbitrary")),
    )(dout, x, nw, g, u, pack, pack, pack)


def tn_matmul(a, b, scale, tm, name, gbuf=None, blk_of_m=None):
    seq, m_total = a.shape
    tk = _token_tile(seq, 1024 if gbuf is not None else 512)
    nk = seq // tk
    n_m = m_total // tm

    def body(*refs):
        if gbuf is None:
            a_ref, b_ref, o_ref = refs
            acc = o_ref
        else:
            a_ref, b_ref, _, o_ref, acc = refs
        kk = pl.program_id(0)

        @pl.when(kk == 0)
        def _():
            acc[...] = jnp.zeros_like(acc)

        bv = b_ref[...].astype(BF16)
        for m in range(n_m):
            acc[pl.ds(m * tm, tm), :] += _dot_tn(a_ref[:, pl.ds(m * tm, tm)].astype(BF16), bv)

        @pl.when(kk == nk - 1)
        def _():
            if gbuf is None:
                if scale != 1.0:
                    o_ref[...] = o_ref[...] * scale
            else:
                for m in range(n_m):
                    o_ref[_grad_slot(m)] = (acc[pl.ds(m * tm, tm), :] * scale).astype(o_ref.dtype)

    in_specs = [pl.BlockSpec((tk, m_total), lambda k: (k, 0)), pl.BlockSpec((tk, D_MODEL), lambda k: (k, 0))]
    args = [a, b]
    if gbuf is None:
        out_spec = pl.BlockSpec((m_total, D_MODEL), lambda k: (0, 0))
        out_shape = jax.ShapeDtypeStruct((m_total, D_MODEL), F32)
        aliases, scratch = {}, []
    else:
        assert n_m == N_DEV
        in_specs.append(pl.BlockSpec(memory_space=pl.ANY))
        args.append(gbuf)
        out_spec = pl.BlockSpec((N_DEV, tm, D_MODEL), lambda k: (0, blk_of_m, 0))
        out_shape = jax.ShapeDtypeStruct(gbuf.shape, gbuf.dtype)
        aliases, scratch = {2: 0}, [pltpu.VMEM((m_total, D_MODEL), F32)]
    return pl.pallas_call(
        body, name=name, grid=(nk,), in_specs=in_specs, out_specs=out_spec, out_shape=out_shape,
        scratch_shapes=scratch, input_output_aliases=aliases, compiler_params=_cparams(("arbitrary",)),
    )(*args)


def in_proj_fwd(x, nw, wt, layer):
    seq = x.shape[0]
    t = _token_tile(seq, 512)

    def body(x_ref, nw_ref, w_ref, p_ref, h_ref):
        h = _rms_fwd(x_ref[...], nw_ref[...]).astype(BF16)
        h_ref[...] = h
        p_ref[...] = _dot_nt(h, w_ref[...])

    row = pl.BlockSpec((t, D_MODEL), lambda i: (i, 0))
    return pl.pallas_call(
        body, name=f"in_proj_fwd_l{layer}", grid=(seq // t,),
        in_specs=[row, _full((1, D_MODEL)), _full((NP, D_MODEL))],
        out_specs=[pl.BlockSpec((t, NP), lambda i: (i, 0)), row],
        out_shape=[jax.ShapeDtypeStruct((seq, NP), F32), jax.ShapeDtypeStruct((seq, D_MODEL), BF16)],
        compiler_params=_cparams(("parallel",)),
    )(x, nw, wt)


def in_proj_bwd(dproj, x, nw, dres, wt, layer):
    seq = x.shape[0]
    t = _token_tile(seq, 512)

    def body(dp_ref, x_ref, nw_ref, dr_ref, w_ref, dx_ref, dnw_ref):
        @pl.when(pl.program_id(0) == 0)
        def _():
            dnw_ref[...] = jnp.zeros_like(dnw_ref)

        dh = _dot_nn(dp_ref[...], w_ref[...])
        dx, dw = _rms_bwd(dh, x_ref[...], nw_ref[...])
        dx_ref[...] = dr_ref[...] + dx
        dnw_ref[...] += dw

    row = pl.BlockSpec((t, D_MODEL), lambda i: (i, 0))
    return pl.pallas_call(
        body, name=f"in_proj_bwd_l{layer}", grid=(seq // t,),
        in_specs=[pl.BlockSpec((t, NP), lambda i: (i, 0)), row, _full((1, D_MODEL)), row, _full((NP, D_MODEL))],
        out_specs=[row, _full((1, D_MODEL))],
        out_shape=[jax.ShapeDtypeStruct((seq, D_MODEL), F32), jax.ShapeDtypeStruct((1, D_MODEL), F32)],
        compiler_params=_cparams(("arbitrary",)),
    )(dproj, x, nw, dres, wt)


def out_proj_fwd(x, y, w, layer):
    seq = x.shape[0]
    t = _token_tile(seq, 512)

    def body(x_ref, y_ref, w_ref, o_ref):
        o_ref[...] = x_ref[...] + _dot_nn(y_ref[...], w_ref[...])

    row = pl.BlockSpec((t, D_MODEL), lambda i: (i, 0))
    return pl.pallas_call(
        body, name=f"out_proj_fwd_l{layer}", grid=(seq // t,),
        in_specs=[row, row, _full((D_MODEL, D_MODEL))], out_specs=row,
        out_shape=jax.ShapeDtypeStruct((seq, D_MODEL), F32), compiler_params=_cparams(("parallel",)),
    )(x, y, w)


def out_proj_bwd(dx, w, layer):
    seq = dx.shape[0]
    t = _token_tile(seq, 512)

    def body(d_ref, w_ref, o_ref):
        o_ref[...] = _dot_nt(d_ref[...].astype(BF16), w_ref[...])

    row = pl.BlockSpec((t, D_MODEL), lambda i: (i, 0))
    return pl.pallas_call(
        body, name=f"out_proj_bwd_l{layer}", grid=(seq // t,),
        in_specs=[row, _full((D_MODEL, D_MODEL))], out_specs=row,
        out_shape=jax.ShapeDtypeStruct((seq, D_MODEL), F32), compiler_params=_cparams(("parallel",)),
    )(dx, w)


def loss_head(x, nw, target):
    seq = x.shape[0]
    t = _token_tile(seq, 512)

    def body(x_ref, nw_ref, t_ref, loss_ref, dx_ref, dnw_ref):
        @pl.when(pl.program_id(0) == 0)
        def _():
            loss_ref[...] = jnp.zeros_like(loss_ref)
            dnw_ref[...] = jnp.zeros_like(dnw_ref)

        xv, w = x_ref[...], nw_ref[...]
        err = _rms_fwd(xv, w) - t_ref[...]
        loss_ref[...] += 0.5 * jnp.sum(jnp.mean(err * err, axis=-1, keepdims=True), axis=0, keepdims=True)
        dx, dw = _rms_bwd(err * (1.0 / D_MODEL), xv, w)
        dx_ref[...] = dx
        dnw_ref[...] += dw

    row = pl.BlockSpec((t, D_MODEL), lambda i: (i, 0))
    return pl.pallas_call(
        body, name="loss_head", grid=(seq // t,),
        in_specs=[row, _full((1, D_MODEL)), row], out_specs=[_full((1, 128)), row, _full((1, D_MODEL))],
        out_shape=[jax.ShapeDtypeStruct((1, 128), F32), jax.ShapeDtypeStruct((seq, D_MODEL), F32),
                   jax.ShapeDtypeStruct((1, D_MODEL), F32)],
        compiler_params=_cparams(("arbitrary",)),
    )(x, nw, target)


_MIXER_PARAM_SHAPES = ((DEPTH, HG_W), (1, HG_DK), (CONV_W, 3 * GD_W), (1, 128), (1, 128), (1, GD_DK),
                       (CONV_W, RG_W), (1, RG_W), (RG_W, RG_W), (1, RG_W), (RG_W, RG_W), (1, RG_W), (1, RG_W))
_STATE_SHAPES = ((HG_W, HG_W), (GD_HEADS, GD_DK, GD_DK), (1, RG_W))


def _state_spec(cps, sh, index):
    return pl.BlockSpec((cps,) + sh, lambda i: (index(i),) + (0,) * len(sh))


def mixer_fwd(layer, proj, params):
    seq = proj.shape[0]
    cps = min(MIXER_CHUNKS_FWD, seq // CHUNK)
    rows = cps * CHUNK
    n = seq // rows

    def body(*refs):
        p_ref, halo_ref = refs[0], refs[1]
        prm = refs[2:15]
        y_ref, o_ref, st_out, s_out, h_out = refs[15:20]
        st, s, h = refs[20:23]
        i = pl.program_id(0)

        @pl.when(i == 0)
        def _():
            st[...] = jnp.zeros_like(st)
            s[...] = jnp.zeros_like(s)
            h[...] = jnp.zeros_like(h)

        halo = jnp.where(i > 0, halo_ref[...], 0.0)
        y, outs, entered, left = mixer_step_forward(layer, p_ref[...], halo, (st[...], s[...], h[...]),
                                                    tuple(r[...] for r in prm))
        y_ref[...] = y.astype(BF16)
        o_ref[...] = jnp.concatenate(outs, axis=1)
        for c in range(cps):
            st_out[c], s_out[c], h_out[c] = entered[c]
        st[...], s[...], h[...] = left

    in_specs = [pl.BlockSpec((rows, NP), lambda i: (i, 0)),
                pl.BlockSpec((8, NP), lambda i: (jnp.maximum(i * (rows // 8) - 1, 0), 0))]
    in_specs += [_full(sh) for sh in _MIXER_PARAM_SHAPES]
    out_specs = [pl.BlockSpec((rows, D_MODEL), lambda i: (i, 0))] * 2
    out_specs += [_state_spec(cps, sh, lambda i: i) for sh in _STATE_SHAPES]
    out_shape = [jax.ShapeDtypeStruct((seq, D_MODEL), BF16), jax.ShapeDtypeStruct((seq, D_MODEL), F32)]
    out_shape += [jax.ShapeDtypeStruct((seq // CHUNK,) + sh, F32) for sh in _STATE_SHAPES]
    return pl.pallas_call(
        body, name=f"mixer_fwd_l{layer}", grid=(n,), in_specs=in_specs, out_specs=out_specs, out_shape=out_shape,
        scratch_shapes=[pltpu.VMEM(sh, F32) for sh in _STATE_SHAPES],
        compiler_params=_cparams(("arbitrary",)),
    )(proj, proj, *params)


def mixer_bwd(layer, proj, dy, outs, states, params):
    seq = proj.shape[0]
    cps = min(MIXER_CHUNKS_BWD, seq // CHUNK)
    rows = cps * CHUNK
    n = seq // rows

    def body(*refs):
        p_ref, halo_ref, dy_ref, o_ref, st_ref, s_ref, h_ref = refs[0:7]
        prm = refs[7:20]
        dp_ref = refs[20]
        dprm = refs[21:34]
        dst, ds, dh, dhalo = refs[34:38]
        i = pl.program_id(0)
        r = n - 1 - i

        @pl.when(i == 0)
        def _():
            for ref in (dst, ds, dh, dhalo) + tuple(dprm):
                ref[...] = jnp.zeros_like(ref)

        params_v = tuple(q[...] for q in prm)
        proj_v = p_ref[...]
        halo = jnp.where(r > 0, halo_ref[...], 0.0)
        o = o_ref[...]
        _, post_vjp = jax.vjp(mixer_post, o[:, 0:HG_W], o[:, HG_W:HG_W + GD_W], o[:, HG_W + GD_W:D_MODEL], proj_v, params_v)
        d_ohg, d_ogd, d_hs, dproj_post, dparams_post = post_vjp(dy_ref[...])
        pre, pre_vjp = jax.vjp(functools.partial(mixer_pre, layer), proj_v, halo, params_v)
        dstates = (dst[...], ds[...], dh[...])
        parts = [None] * cps
        for c in reversed(range(cps)):
            rs = slice(c * CHUNK, (c + 1) * CHUNK)
            _, rec_vjp = jax.vjp(mixer_rec, mixer_pre_chunk(pre, c, cps), st_ref[c], s_ref[c], h_ref[c])
            d_ogd_c = jnp.concatenate([d_ogd[rs, GD_DK * hd:GD_DK * (hd + 1)][None] for hd in range(GD_HEADS)], axis=0)
            parts[c], dst_c, ds_c, dh_c = rec_vjp(((d_ohg[rs, :], d_ogd_c, d_hs[rs, :]), dstates))
            dstates = (dst_c, ds_c, dh_c)
        dproj_pre, dhalo_n, dparams_pre = pre_vjp(mixer_pre_unchunk(parts, cps))
        carry = jnp.concatenate([jnp.zeros((rows - 8, NP), F32), dhalo[...]], axis=0)
        dp_ref[...] = (dproj_post + dproj_pre + carry).astype(BF16)
        dst[...], ds[...], dh[...] = dstates
        dhalo[...] = dhalo_n
        for ref, a, b in zip(dprm, dparams_post, dparams_pre):
            ref[...] += a + b

    rev = lambda i: n - 1 - i
    once = pl.Buffered(1)
    in_specs = [pl.BlockSpec((rows, NP), lambda i: (rev(i), 0), pipeline_mode=once),
                pl.BlockSpec((8, NP), lambda i: (jnp.maximum(rev(i) * (rows // 8) - 1, 0), 0)),
                pl.BlockSpec((rows, D_MODEL), lambda i: (rev(i), 0), pipeline_mode=once),
                pl.BlockSpec((rows, D_MODEL), lambda i: (rev(i), 0), pipeline_mode=once)]
    in_specs += [_state_spec(cps, sh, rev) for sh in _STATE_SHAPES]
    in_specs += [_full(sh) for sh in _MIXER_PARAM_SHAPES]
    out_specs = [pl.BlockSpec((rows, NP), lambda i: (rev(i), 0))] + [_full(sh) for sh in _MIXER_PARAM_SHAPES]
    out_shape = [jax.ShapeDtypeStruct((seq, NP), BF16)] + [jax.ShapeDtypeStruct(sh, F32) for sh in _MIXER_PARAM_SHAPES]
    return pl.pallas_call(
        body, name=f"mixer_bwd_l{layer}", grid=(n,), in_specs=in_specs, out_specs=out_specs, out_shape=out_shape,
        scratch_shapes=[pltpu.VMEM(sh, F32) for sh in _STATE_SHAPES] + [pltpu.VMEM((8, NP), F32)],
        compiler_params=pltpu.CompilerParams(dimension_semantics=("arbitrary",), vmem_limit_bytes=MIXER_BWD_VMEM_LIMIT),
    )(proj, proj, dy, outs, *states, *params)


def _win_segments():
    out = []
    for k in range(N_DEV):
        a, b = NIN_SHARD * k, NIN_SHARD * (k + 1)
        if a < GBA_SPLIT < b:
            out.append((k, 0, GBA_SPLIT - a, a))
            out.append((k, GBA_SPLIT - a, b - GBA_SPLIT, O_RX))
        elif b <= GBA_SPLIT:
            out.append((k, 0, NIN_SHARD, a))
        else:
            out.append((k, 0, NIN_SHARD, a + O_RX - GBA_SPLIT))
    return out


def win_to_padded(wt_shards, layer):
    lanes = 256

    def body(src, dst, scr):
        scr[...] = jnp.zeros_like(scr)
        for (k, s0, rows, d0) in _win_segments():
            scr[pl.ds(d0, rows), :] = src[k, pl.ds(s0, rows), :].astype(F32)
        dst[...] = scr[...].astype(BF16)

    return pl.pallas_call(
        body, name=f"win_to_padded_l{layer}", grid=(D_MODEL // lanes,),
        in_specs=[pl.BlockSpec((N_DEV, NIN_SHARD_PAD, lanes), lambda j: (0, 0, j))],
        out_specs=pl.BlockSpec((NP, lanes), lambda j: (0, j)),
        out_shape=jax.ShapeDtypeStruct((NP, D_MODEL), BF16),
        scratch_shapes=[pltpu.VMEM((NP, lanes), F32)], compiler_params=_cparams(("parallel",)),
    )(wt_shards)


def win_grad_to_pack(dwt, gbuf, layer):
    lanes = 256
    blk = (OFF_WIN + layer * NIN_SHARD_PAD) // NIN_SHARD_PAD

    def body(src, _, dst, scr):
        scr[...] = jnp.zeros_like(scr)
        for (k, s0, rows, d0) in _win_segments():
            scr[_grad_slot(k), pl.ds(s0, rows), :] = src[pl.ds(d0, rows), :]
        dst[...] = scr[...].astype(dst.dtype)

    return pl.pallas_call(
        body, name=f"win_grad_to_pack_l{layer}", grid=(D_MODEL // lanes,),
        in_specs=[pl.BlockSpec((NP, lanes), lambda j: (0, j)), pl.BlockSpec(memory_space=pl.ANY)],
        out_specs=pl.BlockSpec((N_DEV, NIN_SHARD_PAD, lanes), lambda j: (0, blk, j)),
        out_shape=jax.ShapeDtypeStruct(gbuf.shape, gbuf.dtype), input_output_aliases={1: 0},
        scratch_shapes=[pltpu.VMEM((N_DEV, NIN_SHARD_PAD, lanes), F32)],
        compiler_params=_cparams(("parallel",)),
    )(dwt, gbuf)


def sum_adamw(parts, w, m, v, name, rows_per_step):
    n, rows, cols = parts.shape
    tr = min(rows_per_step, rows)
    c1 = 1.0 - ADAM_B1 ** ADAM_STEP
    c2 = 1.0 - ADAM_B2 ** ADAM_STEP

    def body(p_ref, w_ref, m_ref, v_ref, g_ref, d_ref, mo_ref, vo_ref):
        g = p_ref[0].astype(F32)
        for j in range(1, n):
            g = g + p_ref[j].astype(F32)
        mn = ADAM_B1 * m_ref[...] + (1.0 - ADAM_B1) * g
        vn = ADAM_B2 * v_ref[...] + (1.0 - ADAM_B2) * (g * g)
        g_ref[...] = g
        mo_ref[...] = mn
        vo_ref[...] = vn
        d_ref[...] = -ADAM_LR * ((mn / c1) / (jnp.sqrt(vn / c2) + ADAM_EPS) + ADAM_WD * w_ref[...])

    blk = pl.BlockSpec((tr, cols), lambda i: (i, 0))
    return pl.pallas_call(
        body, name=name, grid=(rows // tr,),
        in_specs=[pl.BlockSpec((n, tr, cols), lambda i: (0, i, 0)), blk, blk, blk], out_specs=[blk] * 4,
        out_shape=[jax.ShapeDtypeStruct((rows, cols), F32)] * 4, compiler_params=_cparams(("parallel",)),
    )(parts, w, m, v)


def gather_two_level(arrays, name):
    n = len(arrays)

    def body(*refs):
        ins, outs = refs[:n], refs[n:2 * n]
        send_sems, recv_sems, local_sems = refs[2 * n:]
        x, y, c = lax.axis_index("x"), lax.axis_index("y"), lax.axis_index("c")
        me = 4 * x + 2 * y + c
        sib_slot = 4 * x + 2 * y + (1 - c)
        chips = [(1 - x, y), (x, 1 - y), (1 - x, 1 - y)]

        def copy(a, k, slot, to, src=None):
            return pltpu.make_async_remote_copy(
                src_ref=outs[a].at[slot] if src is None else src, dst_ref=outs[a].at[slot],
                send_sem=send_sems.at[a * 7 + k], recv_sem=recv_sems.at[a * 7 + k],
                device_id=to, device_id_type=pl.DeviceIdType.MESH)

        local, started = [], []
        for a in range(n):
            cp = pltpu.make_async_copy(ins[a], outs[a].at[me], local_sems.at[a])
            cp.start()
            local.append(cp)
            first = [copy(a, 0, me, (x, y, 1 - c), src=ins[a])]
            first += [copy(a, 1 + j, me, (px, py, c), src=ins[a]) for j, (px, py) in enumerate(chips)]
            for cp in first:
                cp.start()
            started += first
        for a in range(n):
            for j, (px, py) in enumerate(chips):
                slot = 4 * px + 2 * py + c
                copy(a, 1 + j, slot, (x, y, c)).wait_recv()
                fwd = copy(a, 4 + j, slot, (x, y, 1 - c))
                fwd.start()
                started.append(fwd)
        for a in range(n):
            copy(a, 0, sib_slot, (x, y, c)).wait_recv()
            for j, (px, py) in enumerate(chips):
                copy(a, 4 + j, 4 * px + 2 * py + (1 - c), (x, y, c)).wait_recv()
        for cp in started:
            cp.wait_send()
        for cp in local:
            cp.wait()

    out_shape = [jax.ShapeDtypeStruct((N_DEV,) + a.shape, a.dtype) for a in arrays]
    any_spec = pl.BlockSpec(memory_space=pl.ANY)
    return pl.pallas_call(
        body, name=name, in_specs=[any_spec] * n, out_specs=[any_spec] * n, out_shape=out_shape,
        scratch_shapes=[pltpu.SemaphoreType.DMA((7 * n,)), pltpu.SemaphoreType.DMA((7 * n,)), pltpu.SemaphoreType.DMA((n,))],
        compiler_params=pltpu.CompilerParams(has_side_effects=True),
    )(*arrays)


def _grad_slot(d):
    return (d % 2) * (N_DEV // 2) + d // 2


def sibling_exchange(gbuf, name):
    half = N_DEV // 2

    def body(g_ref, o_ref, send_sem, recv_sem):
        x, y, c = lax.axis_index("x"), lax.axis_index("y"), lax.axis_index("c")
        cp = pltpu.make_async_remote_copy(
            src_ref=g_ref.at[pl.ds((1 - c) * half, half)], dst_ref=o_ref, send_sem=send_sem, recv_sem=recv_sem,
            device_id=(x, y, 1 - c), device_id_type=pl.DeviceIdType.MESH)
        cp.start()
        cp.wait()

    any_spec = pl.BlockSpec(memory_space=pl.ANY)
    return pl.pallas_call(
        body, name=name, in_specs=[any_spec], out_specs=any_spec,
        out_shape=jax.ShapeDtypeStruct((half,) + gbuf.shape[1:], gbuf.dtype),
        scratch_shapes=[pltpu.SemaphoreType.DMA, pltpu.SemaphoreType.DMA],
        compiler_params=pltpu.CompilerParams(has_side_effects=True),
    )(gbuf)


def pair_sum(gbuf, b, half_index, name):
    n, rows, cols = b.shape
    tr = rows // 4 if rows % 64 == 0 else rows

    def body(half_ref, a_ref, b_ref, o_ref):
        o_ref[...] = (a_ref[...].astype(F32) + b_ref[...].astype(F32)).astype(o_ref.dtype)

    blk = pl.BlockSpec((None, tr, cols), lambda i, j, half_ref: (i, j, 0))
    blk_a = pl.BlockSpec((None, tr, cols), lambda i, j, half_ref: (half_ref[0] * n + i, j, 0))
    return pl.pallas_call(
        body, name=name,
        grid_spec=pltpu.PrefetchScalarGridSpec(num_scalar_prefetch=1, grid=(n, rows // tr), in_specs=[blk_a, blk],
                                               out_specs=blk),
        out_shape=jax.ShapeDtypeStruct(b.shape, b.dtype), compiler_params=_cparams(("parallel", "parallel")),
    )(jnp.reshape(half_index, (1,)).astype(jnp.int32), gbuf, b)


def chip_exchange(p, name):
    half = N_DEV // 2

    def body(p_ref, o_ref, send_sems, recv_sems, local_sem):
        x, y, c = lax.axis_index("x"), lax.axis_index("y"), lax.axis_index("c")
        q = 2 * x + y
        own = pltpu.make_async_copy(p_ref.at[q], o_ref.at[q], local_sem)
        own.start()
        pending = []
        for j in range(1, half):
            px = 1 - x if j & 2 else x
            py = 1 - y if j & 1 else y
            pq = 2 * px + py

            def copy(dst_slot):
                return pltpu.make_async_remote_copy(
                    src_ref=p_ref.at[pq], dst_ref=o_ref.at[dst_slot], send_sem=send_sems.at[j - 1],
                    recv_sem=recv_sems.at[j - 1], device_id=(px, py, c), device_id_type=pl.DeviceIdType.MESH)

            send = copy(q)
            send.start()
            pending.append((send, copy(pq)))
        for send, recv in pending:
            send.wait_send()
            recv.wait_recv()
        own.wait()

    any_spec = pl.BlockSpec(memory_space=pl.ANY)
    return pl.pallas_call(
        body, name=name, in_specs=[any_spec], out_specs=any_spec, out_shape=jax.ShapeDtypeStruct(p.shape, p.dtype),
        scratch_shapes=[pltpu.SemaphoreType.DMA((half - 1,)), pltpu.SemaphoreType.DMA((half - 1,)), pltpu.SemaphoreType.DMA],
        compiler_params=pltpu.CompilerParams(has_side_effects=True),
    )(p)


BIG = ("ffn1_gate", "ffn1_up", "ffn1_down", "w_in", "w_out", "ffn2_gate", "ffn2_up", "ffn2_down")
SMALL_REPLICATED = (("norm_ffn1", (DEPTH, D_MODEL)), ("norm_mix", (DEPTH, D_MODEL)), ("norm_ffn2", (DEPTH, D_MODEL)),
                    ("norm_final", (D_MODEL,)), ("hg_lb", (DEPTH, HG_W)), ("hg_norm_w", (DEPTH, HG_DK)),
                    ("gd_a_log", (DEPTH, GD_HEADS)), ("gd_dt_bias", (DEPTH, GD_HEADS)), ("gd_norm_w", (DEPTH, GD_DK)),
                    ("rg_conv_b", (DEPTH, RG_W)), ("rg_wr", (DEPTH, RG_BLOCKS, RG_BD, RG_BD)), ("rg_br", (DEPTH, RG_W)),
                    ("rg_wi", (DEPTH, RG_BLOCKS, RG_BD, RG_BD)), ("rg_bi", (DEPTH, RG_W)), ("rg_lambda", (DEPTH, RG_W)))
SMALL_EXTRA = (("gd_conv_w", (DEPTH, CONV_W, 3 * GD_W)), ("rg_conv_w", (DEPTH, CONV_W, RG_W)), ("loss", (1,)))
SMALL_SHARDED = (("gd_conv_w", (DEPTH, CONV_W, GDC_SHARD)), ("rg_conv_w", (DEPTH, CONV_W, RGC_SHARD)))


def _pad_rows(a, rows):
    return jnp.pad(a, ((0, rows - a.shape[0]), (0, 0)))


def pack_big(get):
    pieces = []
    for l in range(DEPTH):
        for which in (1, 2):
            pieces.append(_pad_rows(get(f"ffn{which}_gate")[l].T, FF_PAD))
            pieces.append(_pad_rows(get(f"ffn{which}_up")[l].T, FF_PAD))
            pieces.append(_pad_rows(get(f"ffn{which}_down")[l], FF_PAD))
    for l in range(DEPTH):
        pieces.append(_pad_rows(get("w_in")[l].T, NIN_SHARD_PAD))
    for l in range(DEPTH):
        pieces.append(get("w_out")[l])
    return jnp.concatenate(pieces, axis=0)


def unpack_big(p):
    out = {}
    for which in (1, 2):
        for part, nm in (("g", "gate"), ("u", "up"), ("d", "down")):
            per_layer = []
            for l in range(DEPTH):
                r0 = _ffn_block(l, which, part) * FF_PAD
                blk = p[r0:r0 + FF_SHARD, :]
                per_layer.append(blk if part == "d" else blk.T)
            out[f"ffn{which}_{nm}"] = jnp.stack(per_layer)
    out["w_in"] = jnp.stack([p[OFF_WIN + NIN_SHARD_PAD * l:OFF_WIN + NIN_SHARD_PAD * l + NIN_SHARD, :].T for l in range(DEPTH)])
    out["w_out"] = jnp.stack([p[OFF_WOUT + WOUT_SHARD * l:OFF_WOUT + WOUT_SHARD * (l + 1), :] for l in range(DEPTH)])
    return out


def pack_small(get, spec):
    flat = []
    for name, shape in spec:
        a = get(name)
        flat.append(jnp.zeros((math.prod(shape),), F32) if a is None else a.reshape(-1).astype(F32))
    v = jnp.concatenate(flat)
    total = -(-v.shape[0] // 1024) * 1024
    return jnp.pad(v, (0, total - v.shape[0])).reshape(total // 128, 128)


def unpack_small(p, spec):
    v = p.reshape(-1)
    out, off = {}, 0
    for name, shape in spec:
        size = math.prod(shape)
        out[name] = v[off:off + size].reshape(shape)
        off += size
    return out


def _block_diag(w):
    rows = []
    for i in range(RG_BLOCKS):
        rows.append(jnp.concatenate([w[i] if j == i else jnp.zeros((RG_BD, RG_BD), F32) for j in range(RG_BLOCKS)], axis=1))
    return jnp.concatenate(rows, axis=0)


def _diag_blocks(w):
    return jnp.stack([w[RG_BD * i:RG_BD * (i + 1), RG_BD * i:RG_BD * (i + 1)] for i in range(RG_BLOCKS)])


def _lane_vec(v):
    return jnp.pad(v.astype(F32), (GD_HEADS, 128 - 2 * GD_HEADS))[None]


INPUT_NAMES = ("x", "norm_ffn1", "ffn1_gate", "ffn1_up", "ffn1_down", "norm_mix", "w_in", "hg_lb", "hg_norm_w",
               "gd_conv_w", "gd_a_log", "gd_dt_bias", "gd_norm_w", "rg_conv_w", "rg_conv_b", "rg_wr", "rg_br", "rg_wi",
               "rg_bi", "rg_lambda", "w_out", "norm_ffn2", "ffn2_gate", "ffn2_up", "ffn2_down", "norm_final")
WEIGHT_NAMES = INPUT_NAMES[1:]


def _step(a):
    x = a["x"][0]
    target = a["loss_target"][0]
    me = 4 * lax.axis_index("x") + 2 * lax.axis_index("y") + lax.axis_index("c")

    w_pack = pack_big(lambda nm: a[nm])
    conv_local = pack_small(lambda nm: a[nm], SMALL_SHARDED)
    assert DEPTH == 2
    w_bf16 = w_pack.astype(BF16)
    gw_first, gconv = gather_two_level([w_bf16[:FIRST_BLOCKS * FF_PAD], conv_local], "gather_weights")

    def carried_rows(b0, b1, l):
        return jnp.concatenate([w_bf16[b0 * FF_PAD:b1 * FF_PAD],
                                w_bf16[OFF_WIN + NIN_SHARD_PAD * l:OFF_WIN + NIN_SHARD_PAD * (l + 1)],
                                w_bf16[OFF_WOUT + WOUT_SHARD * l:OFF_WOUT + WOUT_SHARD * (l + 1)]], axis=0)

    carry_a = carried_rows(FIRST_BLOCKS, 6, 0)
    carry_b = carried_rows(6, N_FFN_BLOCKS, 1)

    conv_parts = [unpack_small(gconv[d], SMALL_SHARDED) for d in range(N_DEV)]
    gd_conv_w = jnp.concatenate([p["gd_conv_w"] for p in conv_parts], axis=-1)
    rg_conv_w = jnp.concatenate([p["rg_conv_w"] for p in conv_parts], axis=-1)

    def mixer_params(l):
        return (a["hg_lb"], a["hg_norm_w"][l][None], gd_conv_w[l], _lane_vec(a["gd_a_log"][l]), _lane_vec(a["gd_dt_bias"][l]),
                a["gd_norm_w"][l][None], rg_conv_w[l], a["rg_conv_b"][l][None], _block_diag(a["rg_wr"][l]), a["rg_br"][l][None],
                _block_diag(a["rg_wi"][l]), a["rg_bi"][l][None], a["rg_lambda"][l][None])

    saved = []
    xs = x
    for l in range(DEPTH):
        if l == 0:
            x1, h1, g1, u1, a1, gw_a = ffn_fwd(xs, a["norm_ffn1"][l][None], gw_first, l, 1, carry=carry_a)
            packs = {1: (gw_first, 0), 2: (gw_a, FIRST_BLOCKS)}
            late, r = gw_a, (6 - FIRST_BLOCKS) * FF_PAD
        else:
            packs = {1: (gw_b, 6), 2: (gw_b, 6)}
            late, r = gw_b, (N_FFN_BLOCKS - 6) * FF_PAD
            x1, h1, g1, u1, a1 = ffn_fwd(xs, a["norm_ffn1"][l][None], gw_b, l, 1, blk0=6)
        wt = win_to_padded(late[:, r:r + NIN_SHARD_PAD, :], l)
        proj, h2 = in_proj_fwd(x1, a["norm_mix"][l][None], wt, l)
        prm = mixer_params(l)
        y, o_mix, st, s, h = mixer_fwd(l, proj, prm)
        wo = late[:, r + NIN_SHARD_PAD:r + NIN_SHARD_PAD + WOUT_SHARD, :].reshape(D_MODEL, D_MODEL)
        x2 = out_proj_fwd(x1, y, wo, l)
        if l == 0:
            x3, h3, g3, u3, a3, gw_b = ffn_fwd(x2, a["norm_ffn2"][l][None], gw_a, l, 2, blk0=FIRST_BLOCKS, carry=carry_b)
        else:
            x3, h3, g3, u3, a3 = ffn_fwd(x2, a["norm_ffn2"][l][None], gw_b, l, 2, blk0=6)
        saved.append(dict(x0=xs, x1=x1, x2=x2, h1=h1, g1=g1, u1=u1, a1=a1, wt=wt, proj=proj, h2=h2, prm=prm, y=y, o_mix=o_mix,
                          states=(st, s, h), wo=wo, h3=h3, g3=g3, u3=u3, a3=a3, packs=packs))
        xs = x3
    loss_row, dx, d_norm_final = loss_head(xs, a["norm_final"][None], target)

    gbuf = jnp.zeros((N_DEV, PACK_ROWS, D_MODEL), BF16)
    sg = {"norm_final": d_norm_final[0], "loss": loss_row[0, 0:1]}
    per_layer = {nm: [None] * DEPTH for nm in ("norm_ffn1", "norm_mix", "norm_ffn2", "hg_norm_w", "gd_conv_w", "gd_a_log",
                                                "gd_dt_bias", "gd_norm_w", "rg_conv_w", "rg_conv_b", "rg_wr", "rg_br", "rg_wi",
                                                "rg_bi", "rg_lambda")}
    d_hg_lb = jnp.zeros((DEPTH, HG_W), F32)
    for l in reversed(range(DEPTH)):
        sv = saved[l]
        pack2, blk2 = sv["packs"][2]
        dx2, dg, du, dn2 = ffn_bwd(dx, sv["x2"], a["norm_ffn2"][l][None], sv["g3"], sv["u3"], pack2, l, 2, blk2)
        gbuf = tn_matmul(dg, sv["h3"], 1.0, FF_PAD, f"dw_gate_l{l}_2", gbuf, _ffn_block(l, 2, "g"))
        gbuf = tn_matmul(du, sv["h3"], 1.0, FF_PAD, f"dw_up_l{l}_2", gbuf, _ffn_block(l, 2, "u"))
        gbuf = tn_matmul(sv["a3"], dx, 0.5, FF_PAD, f"dw_down_l{l}_2", gbuf, _ffn_block(l, 2, "d"))
        dy = out_proj_bwd(dx2, sv["wo"], l)
        gbuf = tn_matmul(sv["y"], dx2, 1.0, WOUT_SHARD, f"dw_out_l{l}", gbuf, (OFF_WOUT + WOUT_SHARD * l) // WOUT_SHARD)
        mb = mixer_bwd(l, sv["proj"], dy, sv["o_mix"], sv["states"], sv["prm"])
        dproj, dprm = mb[0], mb[1:]
        dwt = tn_matmul(dproj, sv["h2"], 1.0, FF_PAD, f"dw_in_l{l}")
        gbuf = win_grad_to_pack(dwt, gbuf, l)
        dx1, dnm = in_proj_bwd(dproj, sv["x1"], a["norm_mix"][l][None], dx2, sv["wt"], l)
        pack1, blk1 = sv["packs"][1]
        dx, dg, du, dn1 = ffn_bwd(dx1, sv["x0"], a["norm_ffn1"][l][None], sv["g1"], sv["u1"], pack1, l, 1, blk1)
        gbuf = tn_matmul(dg, sv["h1"], 1.0, FF_PAD, f"dw_gate_l{l}_1", gbuf, _ffn_block(l, 1, "g"))
        gbuf = tn_matmul(du, sv["h1"], 1.0, FF_PAD, f"dw_up_l{l}_1", gbuf, _ffn_block(l, 1, "u"))
        gbuf = tn_matmul(sv["a1"], dx1, 0.5, FF_PAD, f"dw_down_l{l}_1", gbuf, _ffn_block(l, 1, "d"))
        (g_lb, g_hnw, g_gcw, g_alog, g_dt, g_gnw, g_rcw, g_rcb, g_wr, g_br, g_wi, g_bi, g_lam) = dprm
        d_hg_lb = d_hg_lb + g_lb
        for nm, val in (("norm_ffn1", dn1[0]), ("norm_mix", dnm[0]), ("norm_ffn2", dn2[0]), ("hg_norm_w", g_hnw[0]),
                        ("gd_conv_w", g_gcw), ("gd_a_log", g_alog[0, GD_HEADS:2 * GD_HEADS]),
                        ("gd_dt_bias", g_dt[0, GD_HEADS:2 * GD_HEADS]), ("gd_norm_w", g_gnw[0]), ("rg_conv_w", g_rcw),
                        ("rg_conv_b", g_rcb[0]), ("rg_wr", _diag_blocks(g_wr)), ("rg_br", g_br[0]),
                        ("rg_wi", _diag_blocks(g_wi)), ("rg_bi", g_bi[0]), ("rg_lambda", g_lam[0])):
            per_layer[nm][l] = val
    grad_x = dx
    for nm, vals in per_layer.items():
        sg[nm] = jnp.stack(vals)
    sg["hg_lb"] = d_hg_lb

    small_spec = SMALL_REPLICATED + SMALL_EXTRA
    from_sibling = sibling_exchange(gbuf, "grads_to_sibling")
    parts_big = chip_exchange(pair_sum(gbuf, from_sibling, lax.axis_index("c"), "grads_pair_sum"), "grads_to_chips")
    g_big, d_big, m_big, v_big = sum_adamw(parts_big, w_pack, pack_big(lambda nm: a["m_" + nm]),
                                           pack_big(lambda nm: a["v_" + nm]), "adamw_big", 128)
    (parts_small,) = gather_two_level([pack_small(lambda nm: sg[nm], small_spec)], "gather_small_grads")
    repl = dict(SMALL_REPLICATED)
    small_in = lambda pre: pack_small(lambda nm: a[pre + nm] if nm in repl else None, small_spec)
    rows_small = parts_small.shape[1]
    g_sm, d_sm, m_sm, v_sm = sum_adamw(parts_small, small_in(""), small_in("m_"), small_in("v_"), "adamw_small", rows_small)
    g_small = unpack_small(g_sm, small_spec)
    conv_grads = {"gd_conv_w": lax.dynamic_slice_in_dim(g_small["gd_conv_w"], me * GDC_SHARD, GDC_SHARD, axis=2),
                  "rg_conv_w": lax.dynamic_slice_in_dim(g_small["rg_conv_w"], me * RGC_SHARD, RGC_SHARD, axis=2)}
    conv_in = lambda pre: pack_small(lambda nm: a[pre + nm], SMALL_SHARDED)
    g_cv, d_cv, m_cv, v_cv = sum_adamw(pack_small(lambda nm: conv_grads[nm], SMALL_SHARDED)[None], conv_in(""),
                                       conv_in("m_"), conv_in("v_"), "adamw_conv", conv_local.shape[0])

    results = []
    for big, small, conv in ((g_big, g_sm, g_cv), (d_big, d_sm, d_cv), (m_big, m_sm, m_cv), (v_big, v_sm, v_cv)):
        vals = unpack_big(big)
        vals.update({k: v for k, v in unpack_small(small, small_spec).items() if k in repl})
        vals.update(unpack_small(conv, SMALL_SHARDED))
        results.append(vals)
    loss = g_small["loss"][0]
    out = [loss, grad_x[None]]
    for vals in results:
        out.extend(vals[nm] for nm in WEIGHT_NAMES)
    return tuple(out)


def kernel(x, norm_ffn1, ffn1_gate, ffn1_up, ffn1_down, norm_mix, w_in, hg_lb, hg_norm_w, gd_conv_w, gd_a_log, gd_dt_bias, gd_norm_w, rg_conv_w, rg_conv_b, rg_wr, rg_br, rg_wi, rg_bi, rg_lambda, w_out, norm_ffn2, ffn2_gate, ffn2_up, ffn2_down, norm_final, loss_target, m_norm_ffn1, m_ffn1_gate, m_ffn1_up, m_ffn1_down, m_norm_mix, m_w_in, m_hg_lb, m_hg_norm_w, m_gd_conv_w, m_gd_a_log, m_gd_dt_bias, m_gd_norm_w, m_rg_conv_w, m_rg_conv_b, m_rg_wr, m_rg_br, m_rg_wi, m_rg_bi, m_rg_lambda, m_w_out, m_norm_ffn2, m_ffn2_gate, m_ffn2_up, m_ffn2_down, m_norm_final, v_norm_ffn1, v_ffn1_gate, v_ffn1_up, v_ffn1_down, v_norm_mix, v_w_in, v_hg_lb, v_hg_norm_w, v_gd_conv_w, v_gd_a_log, v_gd_dt_bias, v_gd_norm_w, v_rg_conv_w, v_rg_conv_b, v_rg_wr, v_rg_br, v_rg_wi, v_rg_bi, v_rg_lambda, v_w_out, v_norm_ffn2, v_ffn2_gate, v_ffn2_up, v_ffn2_down, v_norm_final):
    args = locals()
    return _step(dict(args))
```
